```python
import jax
import jax.numpy as jnp
from jax import lax
import numpy as np

D_MODEL = 1024
BATCH = 8
SEQ = 2048
DEPTH = 1

N_MEM = 256
GRID_W = 64
EPS = 1e-6

ATT_HEADS = 8
ATT_KV_HEADS = 2
ATT_HEAD_DIM = 64
ATT_Q_DIM = ATT_HEADS * ATT_HEAD_DIM
ATT_KV_DIM = ATT_KV_HEADS * ATT_HEAD_DIM
ATT_BLOCK = 128
ROPE_THETA = 10000.0
ROPE_PAIRS_PER_AXIS = ATT_HEAD_DIM // 4

HG_HEADS = 4
HG_HEAD_K = 128
HG_HEAD_V = 128
HG_KEY_DIM = HG_HEADS * HG_HEAD_K
HG_VAL_DIM = HG_HEADS * HG_HEAD_V
HG_CHUNK = 32

MIX_WIDTH = ATT_Q_DIM + HG_VAL_DIM
IN_SPLITS = (ATT_Q_DIM, ATT_KV_DIM, ATT_KV_DIM, HG_KEY_DIM, HG_KEY_DIM, HG_KEY_DIM, HG_VAL_DIM, HG_VAL_DIM)
N_IN = ATT_Q_DIM + 2 * ATT_KV_DIM + 3 * HG_KEY_DIM + 2 * HG_VAL_DIM

X_HEADS = 4
X_HEAD_DIM = D_MODEL // X_HEADS

D_FF = 2816
CONV_W = 3

kernel_name = "hymba_axial_gqa_hgrn2_sandwich_convffn"


def rmsnorm(x, g):
    xf = x.astype(jnp.float32)
    y = xf * lax.rsqrt(jnp.mean(xf * xf, axis=-1, keepdims=True) + EPS)
    return (y * g.astype(jnp.float32)).astype(x.dtype)


def split_columns(p, sizes):
    idx = np.cumsum(np.array(sizes))[:-1].tolist()
    return jnp.split(p, idx, axis=-1)


def axial_rope_tables(n):
    rows = n // GRID_W
    r, c = jnp.meshgrid(jnp.arange(rows), jnp.arange(GRID_W), indexing="ij")
    inv = jnp.power(ROPE_THETA, -jnp.arange(ROPE_PAIRS_PER_AXIS, dtype=jnp.float32) / ROPE_PAIRS_PER_AXIS)
    ang = jnp.concatenate([r.reshape(-1, 1).astype(jnp.float32) * inv,
                           c.reshape(-1, 1).astype(jnp.float32) * inv], axis=-1)
    return jnp.cos(ang), jnp.sin(ang)


def apply_rope(x, cos, sin):
    b, n, h, d = x.shape
    xp = x.reshape(b, n, h, d // 2, 2)
    x0, x1 = xp[..., 0], xp[..., 1]
    c = cos[None, :, None, :].astype(x.dtype)
    s = sin[None, :, None, :].astype(x.dtype)
    return jnp.stack([x0 * c - x1 * s, x0 * s + x1 * c], axis=-1).reshape(b, n, h, d)


def axial_gqa_attention(q, k, v):
    b, n = q.shape[0], q.shape[1]
    grp = ATT_HEADS // ATT_KV_HEADS
    nb = n // ATT_BLOCK
    qb = jnp.moveaxis(q.reshape(b, nb, ATT_BLOCK, ATT_KV_HEADS, grp, ATT_HEAD_DIM), 1, 0)
    scale = ATT_HEAD_DIM ** -0.5

    def one_block(qblk):
        s = jnp.einsum("bqhgd,bkhd->bhgqk", qblk, k).astype(jnp.float32) * scale
        p = jax.nn.softmax(s, axis=-1).astype(v.dtype)
        return jnp.einsum("bhgqk,bkhd->bqhgd", p, v)

    o = lax.map(one_block, qb)
    return jnp.moveaxis(o, 0, 1).reshape(b, n, ATT_Q_DIM)


def bidirectional_gated_scan(q, logf_fwd, logf_bwd, v):
    b, n, h, kd = q.shape
    vd = v.shape[-1]
    nc = n // HG_CHUNK

    def flip(a):
        return a[:, ::-1]

    qs = jnp.stack([q, flip(q)])
    lf = jnp.stack([logf_fwd, flip(logf_bwd)])
    vs = jnp.stack([v, flip(v)])
    ks = -jnp.expm1(lf)

    def to_chunks(a):
        a = a.reshape(2, b, nc, HG_CHUNK, h, a.shape[-1])
        return jnp.transpose(a, (2, 0, 1, 4, 3, 5))

    tri = jnp.tril(jnp.ones((HG_CHUNK, HG_CHUNK), dtype=bool))[:, :, None]

    def step(S, xs):
        qx, kx, vx, lx = xs
        bcum = jnp.cumsum(lx, axis=-2)
        diff = bcum[..., :, None, :] - bcum[..., None, :, :]
        dec = jnp.exp(jnp.where(tri, diff, -jnp.inf))
        a = jnp.einsum("...tk,...sk,...tsk->...ts", qx, kx, dec)
        o_intra = jnp.einsum("...ts,...sv->...tv", a, vx)
        o_inter = jnp.einsum("...tk,...kv->...tv", qx * jnp.exp(bcum), S)
        b_last = bcum[..., -1:, :]
        k_dec = kx * jnp.exp(b_last - bcum)
        S_new = jnp.exp(b_last[..., 0, :])[..., :, None] * S + jnp.einsum("...sk,...sv->...kv", k_dec, vx)
        return S_new, o_intra + o_inter

    S0 = jnp.zeros((2, b, h, kd, vd), jnp.float32)
    _, out = lax.scan(step, S0, (to_chunks(qs), to_chunks(ks), to_chunks(vs), to_chunks(lf)))
    out = jnp.transpose(out, (1, 2, 0, 4, 3, 5)).reshape(2, b, n, h, vd)
    return out[0] + flip(out[1])


def hgrn2_group(q, zf_fwd, zf_bwd, i_in, g, lb, out_g):
    b, n = q.shape[0], q.shape[1]

    def heads(a, d):
        return a.reshape(b, n, HG_HEADS, d)

    def log_forget(z, lb_dir):
        f = lb_dir + (1.0 - lb_dir) * jax.nn.sigmoid(z.astype(jnp.float32))
        return heads(jnp.log(f), HG_HEAD_K)

    qf = heads(jax.nn.silu(q.astype(jnp.float32)), HG_HEAD_K)
    o = bidirectional_gated_scan(qf, log_forget(zf_fwd, lb[0]), log_forget(zf_bwd, lb[1]),
                                 heads(i_in.astype(jnp.float32), HG_HEAD_V))
    o = rmsnorm(o, out_g).reshape(b, n, HG_VAL_DIM)
    return (o * jax.nn.silu(g.astype(jnp.float32))).astype(g.dtype)


def memory_cross_attention(h, m, wq, wkv, wo):
    b, n = h.shape[0], h.shape[1]
    nm = m.shape[1]
    q = (h @ wq).reshape(b, n, X_HEADS, X_HEAD_DIM)
    kv = (m @ wkv).reshape(b, nm, 2, X_HEADS, X_HEAD_DIM)
    k, v = kv[:, :, 0], kv[:, :, 1]
    s = jnp.einsum("bqhd,bkhd->bhqk", q, k).astype(jnp.float32) * (X_HEAD_DIM ** -0.5)
    p = jax.nn.softmax(s, axis=-1).astype(v.dtype)
    o = jnp.einsum("bhqk,bkhd->bqhd", p, v).reshape(b, n, X_HEADS * X_HEAD_DIM)
    return o @ wo


def conv_ffn(h, w_up, conv_w, conv_b, w_down):
    n = h.shape[1]
    u = h @ w_up
    half = CONV_W // 2
    up = jnp.pad(u, ((0, 0), (half, half), (0, 0)))
    acc = conv_b
    for j in range(CONV_W):
        acc = acc + up[:, j:j + n] * conv_w[j]
    gate, val = jnp.split(acc, 2, axis=-1)
    return (jax.nn.silu(gate) * val) @ w_down


def setup_inputs(seed: int = 0) -> dict:
    key = jax.random.key(seed)
    ks = jax.random.split(key, 22)
    f32 = jnp.float32
    L = DEPTH

    def nrm(k, shape, scale):
        return jax.random.normal(k, shape, f32) * scale

    def gain(k, shape):
        return 1.0 + nrm(k, shape, 0.05)

    return {
        "x": nrm(ks[0], (BATCH, SEQ, D_MODEL), 1.0),
        "mem": nrm(ks[1], (BATCH, N_MEM, D_MODEL), 1.0),
        "pre_mix_g": gain(ks[2], (L, D_MODEL)),
        "w_in": nrm(ks[3], (L, D_MODEL, N_IN), D_MODEL ** -0.5),
        "q_norm_g": gain(ks[4], (L, ATT_HEAD_DIM)),
        "k_norm_g": gain(ks[5], (L, ATT_HEAD_DIM)),
        "hg_lb": nrm(ks[6], (2, L + 1, HG_KEY_DIM), 0.5),
        "hg_out_norm_g": gain(ks[7], (L, HG_HEAD_V)),
        "w_out": nrm(ks[8], (L, MIX_WIDTH, D_MODEL), MIX_WIDTH ** -0.5),
        "post_mix_g": gain(ks[9], (L, D_MODEL)),
        "pre_x_g": gain(ks[10], (L, D_MODEL)),
        "mem_norm_g": gain(ks[11], (L, D_MODEL)),
        "w_xq": nrm(ks[12], (L, D_MODEL, D_MODEL), D_MODEL ** -0.5),
        "w_xkv": nrm(ks[13], (L, D_MODEL, 2 * D_MODEL), D_MODEL ** -0.5),
        "w_xo": nrm(ks[14], (L, D_MODEL, D_MODEL), D_MODEL ** -0.5),
        "post_x_g": gain(ks[15], (L, D_MODEL)),
        "pre_ffn_g": gain(ks[16], (L, D_MODEL)),
        "w_up": nrm(ks[17], (L, D_MODEL, 2 * D_FF), D_MODEL ** -0.5),
        "conv_w": nrm(ks[18], (L, CONV_W, 2 * D_FF), CONV_W ** -0.5),
        "conv_b": nrm(ks[19], (L, 2 * D_FF), 0.02),
        "w_down": nrm(ks[20], (L, D_FF, D_MODEL), D_FF ** -0.5),
        "post_ffn_g": gain(ks[21], (L, D_MODEL)),
    }


def reference(x, mem, pre_mix_g, w_in, q_norm_g, k_norm_g, hg_lb, hg_out_norm_g, w_out, post_mix_g,
              pre_x_g, mem_norm_g, w_xq, w_xkv, w_xo, post_x_g, pre_ffn_g, w_up, conv_w, conv_b,
              w_down, post_ffn_g):
    b, n = x.shape[0], x.shape[1]
    cos, sin = axial_rope_tables(n)
    lb_all = jnp.cumsum(jax.nn.softmax(hg_lb.astype(jnp.float32), axis=1), axis=1)
    for l in range(DEPTH):
        h = rmsnorm(x, pre_mix_g[l])
        aq, ak, av, hq, hf_fwd, hf_bwd, hi, hg = split_columns(h @ w_in[l], IN_SPLITS)
        aq = rmsnorm(aq.reshape(b, n, ATT_HEADS, ATT_HEAD_DIM), q_norm_g[l])
        ak = rmsnorm(ak.reshape(b, n, ATT_KV_HEADS, ATT_HEAD_DIM), k_norm_g[l])
        av = av.reshape(b, n, ATT_KV_HEADS, ATT_HEAD_DIM)
        att = axial_gqa_attention(apply_rope(aq, cos, sin), apply_rope(ak, cos, sin), av)
        rec = hgrn2_group(hq, hf_fwd, hf_bwd, hi, hg, lb_all[:, l], hg_out_norm_g[l])
        mixed = jnp.concatenate([att, rec], axis=-1) @ w_out[l]
        x = x + rmsnorm(mixed, post_mix_g[l])
        h = rmsnorm(x, pre_x_g[l])
        m = rmsnorm(mem, mem_norm_g[l])
        x = x + rmsnorm(memory_cross_attention(h, m, w_xq[l], w_xkv[l], w_xo[l]), post_x_g[l])
        h = rmsnorm(x, pre_ffn_g[l])
        x = x + rmsnorm(conv_ffn(h, w_up[l], conv_w[l], conv_b[l], w_down[l]), post_ffn_g[l])
    return x
```

```python
import functools

import jax
import jax.numpy as jnp
from jax import lax
from jax.experimental import pallas as pl
from jax.experimental.pallas import tpu as pltpu

F32 = jnp.float32
BF16 = jnp.bfloat16

EPS = 1e-6
GRID_W = 64
ROPE_THETA = 10000.0

ATT_HEADS = 8
ATT_KV_HEADS = 2
ATT_GROUP = ATT_HEADS // ATT_KV_HEADS
ATT_HEAD_DIM = 64
ATT_Q_DIM = ATT_HEADS * ATT_HEAD_DIM
ATT_KV_DIM = ATT_KV_HEADS * ATT_HEAD_DIM

HG_HEADS = 4
HG_DIM = 128
HG_WIDTH = HG_HEADS * HG_DIM
HG_CHUNK = 128

X_HEADS = 4
CONV_W = 3
FF_CHUNK = 256
HALO = 16

VMEM_LIMIT = 56 * 1024 * 1024


def _params(*sem):
    return pltpu.CompilerParams(dimension_semantics=sem, vmem_limit_bytes=VMEM_LIMIT)


def _rms(x, g):
    ms = jnp.mean(x * x, axis=-1, keepdims=True)
    return x * lax.rsqrt(ms + EPS) * g


def _silu(x):
    return x / (1.0 + jnp.exp(-x))


def _dot(a, b):
    return jnp.dot(a, b, preferred_element_type=F32)


def _dot_nt(a, b):
    return lax.dot_general(a, b, (((1,), (1,)), ((), ())), preferred_element_type=F32)


def _dot_tn(a, b):
    return lax.dot_general(a, b, (((0,), (0,)), ((), ())), preferred_element_type=F32)


def _split3(x):
    hi = x.astype(BF16)
    r = x - hi.astype(F32)
    mid = r.astype(BF16)
    lo = (r - mid.astype(F32)).astype(BF16)
    return hi, mid, lo


def _const_spec(shape):
    return pl.BlockSpec(shape, lambda *_: (0,) * len(shape))


def _mem_proj_kernel(m_ref, g_ref, w_ref, k_ref, v_ref):
    d = m_ref.shape[-1]
    m = _rms(m_ref[0], g_ref[...]).astype(BF16)
    kv = _dot(m, w_ref[...])
    k_ref[0] = kv[:, :d].astype(BF16)
    v_ref[0] = kv[:, d:].astype(BF16)


def _mem_proj(mem, g, w_xkv):
    b, nm, d = mem.shape
    out = jax.ShapeDtypeStruct((b, nm, d), BF16)
    return pl.pallas_call(
        _mem_proj_kernel,
        grid=(b,),
        in_specs=[pl.BlockSpec((1, nm, d), lambda i: (i, 0, 0)),
                  _const_spec((1, d)), _const_spec((d, 2 * d))],
        out_specs=[pl.BlockSpec((1, nm, d), lambda i: (i, 0, 0))] * 2,
        out_shape=[out, out],
        compiler_params=_params("parallel"),
        name="mem_proj",
    )(mem, g, w_xkv)


def _head_norm_rope(a, seg, gain, cos, sin_signed):
    sq = a * a
    hi = sq.astype(BF16)
    lo = (sq - hi.astype(F32)).astype(BF16)
    ss = _dot(hi, seg) + _dot(lo, seg)
    an = a * lax.rsqrt(ss * (1.0 / ATT_HEAD_DIM) + EPS) * gain
    w = a.shape[-1]
    lane = lax.broadcasted_iota(jnp.int32, a.shape, 1)
    partner = jnp.where(lane % 2 == 0, pltpu.roll(an, w - 1, 1), pltpu.roll(an, 1, 1))
    return an * cos + partner * sin_signed


def _in_proj_kernel(x_ref, g_ref, w_ref, qg_ref, kg_ref, cos_ref, sin_ref, lb_ref, segq_ref, segk_ref,
                    q_ref, k_ref, v_ref, hq_ref, lff_ref, lfb_ref, hi_ref, hg_ref):
    h = _rms(x_ref[0], g_ref[...]).astype(BF16)

    def proj(lo, width):
        return _dot(h, w_ref[:, lo:lo + width])

    cos = cos_ref[...]
    sin = sin_ref[...]
    c0 = 0
    q = _head_norm_rope(proj(c0, ATT_Q_DIM), segq_ref[...], qg_ref[...],
                        jnp.concatenate([cos] * (ATT_Q_DIM // 128), axis=-1),
                        jnp.concatenate([sin] * (ATT_Q_DIM // 128), axis=-1))
    for i in range(ATT_HEADS):
        q_ref[0, i] = q[:, i * ATT_HEAD_DIM:(i + 1) * ATT_HEAD_DIM].astype(BF16)
    c0 += ATT_Q_DIM
    k = _head_norm_rope(proj(c0, ATT_KV_DIM), segk_ref[...], kg_ref[...], cos, sin)
    c0 += ATT_KV_DIM
    v = proj(c0, ATT_KV_DIM)
    c0 += ATT_KV_DIM
    for i in range(ATT_KV_HEADS):
        k_ref[0, i] = k[:, i * ATT_HEAD_DIM:(i + 1) * ATT_HEAD_DIM].astype(BF16)
        v_ref[0, i] = v[:, i * ATT_HEAD_DIM:(i + 1) * ATT_HEAD_DIM].astype(BF16)

    def heads_out(ref, val):
        for i in range(HG_HEADS):
            ref[0, i] = val[:, i * HG_DIM:(i + 1) * HG_DIM]

    heads_out(hq_ref, _silu(proj(c0, HG_WIDTH)))
    c0 += HG_WIDTH
    for d, ref in enumerate((lff_ref, lfb_ref)):
        lb = lb_ref[d:d + 1, :]
        z = proj(c0, HG_WIDTH)
        heads_out(ref, jnp.log(lb + (1.0 - lb) / (1.0 + jnp.exp(-z))))
        c0 += HG_WIDTH
    heads_out(hi_ref, proj(c0, HG_WIDTH))
    c0 += HG_WIDTH
    hg_ref[0] = _silu(proj(c0, HG_WIDTH))


def _in_proj(x, g, w_in, qg, kg, cos, sin, lb, segq, segk, tm):
    b, n, d = x.shape
    nt = n // tm
    n_in = w_in.shape[1]
    tok = lambda i, j: (i, j, 0)
    head = lambda i, j: (i, 0, j, 0)
    pos = lambda i, j: (j, 0)
    return pl.pallas_call(
        _in_proj_kernel,
        grid=(b, nt),
        in_specs=[pl.BlockSpec((1, tm, d), tok), _const_spec((1, d)), _const_spec((d, n_in)),
                  _const_spec((1, ATT_Q_DIM)), _const_spec((1, ATT_KV_DIM)),
                  pl.BlockSpec((tm, 128), pos), pl.BlockSpec((tm, 128), pos),
                  _const_spec((2, HG_WIDTH)),
                  _const_spec((ATT_Q_DIM, ATT_Q_DIM)), _const_spec((ATT_KV_DIM, ATT_KV_DIM))],
        out_specs=[pl.BlockSpec((1, ATT_HEADS, tm, ATT_HEAD_DIM), head),
                   pl.BlockSpec((1, ATT_KV_HEADS, tm, ATT_HEAD_DIM), head),
                   pl.BlockSpec((1, ATT_KV_HEADS, tm, ATT_HEAD_DIM), head),
                   pl.BlockSpec((1, HG_HEADS, tm, HG_DIM), head),
                   pl.BlockSpec((1, HG_HEADS, tm, HG_DIM), head),
                   pl.BlockSpec((1, HG_HEADS, tm, HG_DIM), head),
                   pl.BlockSpec((1, HG_HEADS, tm, HG_DIM), head),
                   pl.BlockSpec((1, tm, HG_WIDTH), tok)],
        out_shape=[jax.ShapeDtypeStruct((b, ATT_HEADS, n, ATT_HEAD_DIM), BF16),
                   jax.ShapeDtypeStruct((b, ATT_KV_HEADS, n, ATT_HEAD_DIM), BF16),
                   jax.ShapeDtypeStruct((b, ATT_KV_HEADS, n, ATT_HEAD_DIM), BF16),
                   jax.ShapeDtypeStruct((b, HG_HEADS, n, HG_DIM), F32),
                   jax.ShapeDtypeStruct((b, HG_HEADS, n, HG_DIM), F32),
                   jax.ShapeDtypeStruct((b, HG_HEADS, n, HG_DIM), F32),
                   jax.ShapeDtypeStruct((b, HG_HEADS, n, HG_DIM), F32),
                   jax.ShapeDtypeStruct((b, n, HG_WIDTH), F32)],
        compiler_params=_params("parallel", "parallel"),
        name="in_proj",
    )(x, g, w_in, qg, kg, cos, sin, lb, segq, segk)


def _attention_kernel(q_ref, k_ref, v_ref, o_ref):
    for i in range(ATT_HEADS):
        kv = i // ATT_GROUP
        s = _dot_nt(q_ref[0, i], k_ref[0, kv])
        p = jnp.exp(s - jnp.max(s, axis=-1, keepdims=True))
        l = jnp.sum(p, axis=-1, keepdims=True)
        o = _dot(p.astype(BF16), v_ref[0, kv]) / l
        o_ref[0, :, i * ATT_HEAD_DIM:(i + 1) * ATT_HEAD_DIM] = o.astype(BF16)


def _attention(q, k, v, tq):
    b, _, n, dh = q.shape
    return pl.pallas_call(
        _attention_kernel,
        grid=(b, n // tq),
        in_specs=[pl.BlockSpec((1, ATT_HEADS, tq, dh), lambda i, j: (i, 0, j, 0)),
                  pl.BlockSpec((1, ATT_KV_HEADS, n, dh), lambda i, j: (i, 0, 0, 0)),
                  pl.BlockSpec((1, ATT_KV_HEADS, n, dh), lambda i, j: (i, 0, 0, 0))],
        out_specs=pl.BlockSpec((1, tq, ATT_Q_DIM), lambda i, j: (i, j, 0)),
        out_shape=jax.ShapeDtypeStruct((b, n, ATT_Q_DIM), BF16),
        compiler_params=_params("parallel", "parallel"),
        name="attention",
    )(q, k, v)


def _hgrn_levels():
    m, out = 1, []
    while m < HG_CHUNK:
        out.append(m)
        m *= 2
    return out


def _hgrn_chunk(q, lf, v, state, tri, b_scr, reverse):
    c = HG_CHUNK
    kk = 1.0 - jnp.exp(lf)
    b = sum(_dot(tri, part) for part in _split3(lf))
    b_scr[...] = b
    row = lax.broadcasted_iota(jnp.int32, (c, HG_DIM), 0)
    r2 = lax.broadcasted_iota(jnp.int32, (c, c), 0)
    c2 = lax.broadcasted_iota(jnp.int32, (c, c), 1)
    qb = q.astype(BF16)
    kb = kk.astype(BF16)
    a = jnp.where(r2 == c2, _dot_nt(qb, kb), 0.0)
    for m in _hgrn_levels():
        later = ((row // m) % 2 == 1) != reverse
        if m >= 8:
            parts = []
            for j in range(c // (2 * m)):
                r = 2 * m * j + (m if reverse else m - 1)
                parts.append(jnp.broadcast_to(b_scr[r:r + 1, :], (2 * m, HG_DIM)))
            bref = jnp.concatenate(parts, axis=0) if len(parts) > 1 else parts[0]
        else:
            p = row % (2 * m)
            bref = b
            for delta in range(-(m - 1), m + 1):
                if delta == 0:
                    continue
                shift = -delta if reverse else delta
                off = (m - 1 + delta) if not reverse else (m - delta)
                bref = jnp.where(p == off, pltpu.roll(b, shift % c, 0), bref)
        fac = jnp.exp(-jnp.abs(b - bref))
        scaled = jnp.where(later, q, kk) * fac
        qs = jnp.where(later, scaled, 0.0).astype(BF16)
        ks = jnp.where(later, 0.0, scaled).astype(BF16)
        same = (r2 // (2 * m)) == (c2 // (2 * m))
        a = a + jnp.where(same, _dot_nt(qs, ks), 0.0)
    vb = v.astype(BF16)
    last = 0 if reverse else c - 1
    b_last = b_scr[last:last + 1, :]
    o = _dot(a.astype(BF16), vb) + _dot_nt((q * jnp.exp(b)).astype(BF16), state.astype(BF16))
    k_dec = (kk * jnp.exp(b_last - b)).astype(BF16)
    new_state = state * jnp.exp(b_last) + _dot_tn(vb, k_dec)
    return o, new_state


def _hgrn_kernel(qf_ref, lff_ref, vf_ref, qb_ref, lfb_ref, vb_ref, of_ref, ob_ref, s_ref, b_scr):
    @pl.when(pl.program_id(1) == 0)
    def _():
        s_ref[...] = jnp.zeros_like(s_ref)

    c = HG_CHUNK
    r2 = lax.broadcasted_iota(jnp.int32, (c, c), 0)
    c2 = lax.broadcasted_iota(jnp.int32, (c, c), 1)
    tri_f = jnp.where(c2 <= r2, 1.0, 0.0).astype(BF16)
    tri_b = jnp.where(c2 >= r2, 1.0, 0.0).astype(BF16)

    def body(h, carry):
        o, s = _hgrn_chunk(qf_ref[0, h], lff_ref[0, h], vf_ref[0, h], s_ref[0, h], tri_f, b_scr, False)
        of_ref[0, h] = o
        s_ref[0, h] = s
        o, s = _hgrn_chunk(qb_ref[0, h], lfb_ref[0, h], vb_ref[0, h], s_ref[1, h], tri_b, b_scr, True)
        ob_ref[0, h] = o
        s_ref[1, h] = s
        return carry

    lax.fori_loop(0, HG_HEADS, body, 0)


def _hgrn(hq, lf_f, lf_b, hi):
    b, nh, n, dk = hq.shape
    nc = n // HG_CHUNK
    blk = (1, nh, HG_CHUNK, dk)
    fwd = pl.BlockSpec(blk, lambda i, j: (i, 0, j, 0))
    bwd = pl.BlockSpec(blk, lambda i, j: (i, 0, nc - 1 - j, 0))
    out = jax.ShapeDtypeStruct((b, nh, n, dk), F32)
    return pl.pallas_call(
        _hgrn_kernel,
        grid=(b, nc),
        in_specs=[fwd, fwd, fwd, bwd, bwd, bwd],
        out_specs=[fwd, bwd],
        out_shape=[out, out],
        scratch_shapes=[pltpu.VMEM((2, nh, dk, dk), F32), pltpu.VMEM((HG_CHUNK, dk), F32)],
        compiler_params=_params("parallel", "arbitrary"),
        name="hgrn",
    )(hq, lf_f, hi, hq, lf_b, hi)


def _mix_cross_kernel(x_ref, att_ref, of_ref, ob_ref, sg_ref, og_ref, wo_ref, pmg_ref, pxg_ref,
                      wq_ref, km_ref, vm_ref, wxo_ref, poxg_ref, pfg_ref, x2_ref, h3_ref):
    rec = []
    for i in range(HG_HEADS):
        o = _rms(of_ref[0, i] + ob_ref[0, i], og_ref[...])
        rec.append((o * sg_ref[0, :, i * HG_DIM:(i + 1) * HG_DIM]).astype(BF16))
    mix_in = jnp.concatenate([att_ref[0]] + rec, axis=-1)
    x1 = x_ref[0] + _rms(_dot(mix_in, wo_ref[...]), pmg_ref[...])

    h2 = _rms(x1, pxg_ref[...]).astype(BF16)
    d = h2.shape[-1]
    dh = d // X_HEADS
    q = _dot(h2, wq_ref[...]) * (dh ** -0.5)
    heads = []
    for i in range(X_HEADS):
        sl = slice(i * dh, (i + 1) * dh)
        s = _dot_nt(q[:, sl].astype(BF16), km_ref[0, :, sl])
        p = jnp.exp(s - jnp.max(s, axis=-1, keepdims=True))
        l = jnp.sum(p, axis=-1, keepdims=True)
        heads.append((_dot(p.astype(BF16), vm_ref[0, :, sl]) / l).astype(BF16))
    xo = _dot(jnp.concatenate(heads, axis=-1), wxo_ref[...])
    x2 = x1 + _rms(xo, poxg_ref[...])
    x2_ref[0] = x2
    h3_ref[0] = _rms(x2, pfg_ref[...]).astype(BF16)


def _mix_cross(x, att, o_f, o_b, sg, og, w_out, pmg, pxg, w_xq, k_mem, v_mem, w_xo, poxg, pfg, tm):
    b, n, d = x.shape
    nm = k_mem.shape[1]
    tok = lambda i, j: (i, j, 0)
    head = lambda i, j: (i, 0, j, 0)
    batch = lambda i, j: (i, 0, 0)
    vec = _const_spec((1, d))
    mat = _const_spec((d, d))
    return pl.pallas_call(
        _mix_cross_kernel,
        grid=(b, n // tm),
        in_specs=[pl.BlockSpec((1, tm, d), tok), pl.BlockSpec((1, tm, ATT_Q_DIM), tok),
                  pl.BlockSpec((1, HG_HEADS, tm, HG_DIM), head), pl.BlockSpec((1, HG_HEADS, tm, HG_DIM), head),
                  pl.BlockSpec((1, tm, HG_WIDTH), tok), _const_spec((1, HG_DIM)),
                  mat, vec, vec, mat,
                  pl.BlockSpec((1, nm, d), batch), pl.BlockSpec((1, nm, d), batch),
                  mat, vec, vec],
        out_specs=[pl.BlockSpec((1, tm, d), tok), pl.BlockSpec((1, tm, d), tok)],
        out_shape=[jax.ShapeDtypeStruct((b, n, d), F32), jax.ShapeDtypeStruct((b, n, d), BF16)],
        compiler_params=_params("parallel", "parallel"),
        name="mix_cross",
    )(x, att, o_f, o_b, sg, og, w_out, pmg, pxg, w_xq, k_mem, v_mem, w_xo, poxg, pfg)


def _conv_ffn_kernel(x_ref, h_ref, hp_ref, hn_ref, wg_ref, wv_ref, cwg_ref, cwv_ref, cbg_ref, cbv_ref,
                     wd_ref, g_ref, o_ref, ug_scr, uv_scr, acc_ref):
    j = pl.program_id(1)
    tm = h_ref.shape[1]
    prev = jnp.where(j > 0, hp_ref[0], jnp.zeros_like(hp_ref[0]))
    nxt = jnp.where(j < pl.num_programs(1) - 1, hn_ref[0], jnp.zeros_like(hn_ref[0]))
    hext = jnp.concatenate([prev, h_ref[0], nxt], axis=0)
    acc_ref[...] = jnp.zeros_like(acc_ref)

    def conv(scr, cw, cb):
        out = cb
        for t in range(CONV_W):
            out = out + scr[pl.ds(HALO - CONV_W // 2 + t, tm), :] * cw[t:t + 1, :]
        return out

    def body(c, carry):
        ug_scr[...] = _dot(hext, wg_ref[c])
        uv_scr[...] = _dot(hext, wv_ref[c])
        gate = conv(ug_scr, cwg_ref[c], cbg_ref[c])
        val = conv(uv_scr, cwv_ref[c], cbv_ref[c])
        act = (_silu(gate) * val).astype(BF16)
        acc_ref[...] += _dot(act, wd_ref[c])
        return carry

    lax.fori_loop(0, wg_ref.shape[0], body, 0)
    o_ref[0] = x_ref[0] + _rms(acc_ref[...], g_ref[...])


def _conv_ffn(x2, h3, wg, wv, cwg, cwv, cbg, cbv, wd, g, tm):
    b, n, d = x2.shape
    nck, _, ck = wg.shape
    hb = tm // HALO
    last = n // HALO - 1
    tok = lambda i, j: (i, j, 0)
    full = lambda shape: pl.BlockSpec(shape, lambda *_: (0,) * len(shape), pipeline_mode=pl.Buffered(1))
    return pl.pallas_call(
        _conv_ffn_kernel,
        grid=(b, n // tm),
        in_specs=[pl.BlockSpec((1, tm, d), tok), pl.BlockSpec((1, tm, d), tok),
                  pl.BlockSpec((1, HALO, d), lambda i, j: (i, jnp.maximum(j * hb - 1, 0), 0)),
                  pl.BlockSpec((1, HALO, d), lambda i, j: (i, jnp.minimum((j + 1) * hb, last), 0)),
                  full((nck, d, ck)), full((nck, d, ck)),
                  full((nck, CONV_W, ck)), full((nck, CONV_W, ck)),
                  full((nck, 1, ck)), full((nck, 1, ck)),
                  full((nck, ck, d)), _const_spec((1, d))],
        out_specs=pl.BlockSpec((1, tm, d), tok),
        out_shape=jax.ShapeDtypeStruct((b, n, d), F32),
        scratch_shapes=[pltpu.VMEM((tm + 2 * HALO, ck), F32), pltpu.VMEM((tm + 2 * HALO, ck), F32),
                        pltpu.VMEM((tm, d), F32)],
        compiler_params=_params("parallel", "parallel"),
        name="conv_ffn",
    )(x2, h3, h3, h3, wg, wv, cwg, cwv, cbg, cbv, wd, g)


def _rope_tables(n):
    pairs = ATT_HEAD_DIM // 4
    pos = jnp.arange(n)
    inv = jnp.power(ROPE_THETA, -jnp.arange(pairs, dtype=F32) / pairs)
    ang = jnp.concatenate([(pos // GRID_W).astype(F32)[:, None] * inv,
                           (pos % GRID_W).astype(F32)[:, None] * inv], axis=-1)
    cos = jnp.repeat(jnp.cos(ang), 2, axis=-1)
    sin = jnp.repeat(jnp.sin(ang), 2, axis=-1) * jnp.tile(jnp.array([-1.0, 1.0], F32), ATT_HEAD_DIM // 2)
    return jnp.tile(cos, (1, 2)), jnp.tile(sin, (1, 2))


def _segment_ones(width):
    i = jnp.arange(width) // ATT_HEAD_DIM
    return (i[:, None] == i[None, :]).astype(BF16)


def _layer(x, mem, lb, pre_mix_g, w_in, q_norm_g, k_norm_g, hg_out_norm_g, w_out, post_mix_g, pre_x_g,
           mem_norm_g, w_xq, w_xkv, w_xo, post_x_g, pre_ffn_g, w_up, conv_w, conv_b, w_down, post_ffn_g):
    b, n, d = x.shape
    d_ff = w_down.shape[0]
    assert n % HG_CHUNK == 0 and n % GRID_W == 0 and d_ff % FF_CHUNK == 0
    tm = min(256, n)
    tf = min(512, n)
    row = lambda g: g.reshape(1, -1).astype(F32)

    cos, sin = _rope_tables(n)
    qg = jnp.tile(q_norm_g, ATT_HEADS).reshape(1, -1) * (ATT_HEAD_DIM ** -0.5)
    kg = jnp.tile(k_norm_g, ATT_KV_HEADS).reshape(1, -1)

    k_mem, v_mem = _mem_proj(mem, row(mem_norm_g), w_xkv.astype(BF16))
    q, k, v, hq, lf_f, lf_b, hi, sg = _in_proj(
        x, row(pre_mix_g), w_in.astype(BF16), qg, kg, cos, sin, lb,
        _segment_ones(ATT_Q_DIM), _segment_ones(ATT_KV_DIM), tm)
    att = _attention(q, k, v, tm)
    o_f, o_b = _hgrn(hq, lf_f, lf_b, hi)
    x2, h3 = _mix_cross(x, att, o_f, o_b, sg, row(hg_out_norm_g), w_out.astype(BF16), row(post_mix_g),
                        row(pre_x_g), w_xq.astype(BF16), k_mem, v_mem, w_xo.astype(BF16),
                        row(post_x_g), row(pre_ffn_g), tm)

    nck = d_ff // FF_CHUNK
    chunks = lambda w: jnp.transpose(w.reshape(w.shape[0], nck, FF_CHUNK), (1, 0, 2))
    w_up = w_up.astype(BF16)
    return _conv_ffn(x2, h3, chunks(w_up[:, :d_ff]), chunks(w_up[:, d_ff:]),
                     chunks(conv_w[:, :d_ff]), chunks(conv_w[:, d_ff:]),
                     chunks(conv_b[None, :d_ff]), chunks(conv_b[None, d_ff:]),
                     w_down.astype(BF16).reshape(nck, FF_CHUNK, d), row(post_ffn_g), tf)


def kernel(x, mem, pre_mix_g, w_in, q_norm_g, k_norm_g, hg_lb, hg_out_norm_g, w_out, post_mix_g, pre_x_g,
           mem_norm_g, w_xq, w_xkv, w_xo, post_x_g, pre_ffn_g, w_up, conv_w, conv_b, w_down, post_ffn_g):
    lb_all = jnp.cumsum(jax.nn.softmax(hg_lb.astype(F32), axis=1), axis=1)
    for l in range(w_in.shape[0]):
        x = _layer(x, mem, lb_all[:, l], pre_mix_g[l], w_in[l], q_norm_g[l], k_norm_g[l], hg_out_norm_g[l],
                   w_out[l], post_mix_g[l], pre_x_g[l], mem_norm_g[l], w_xq[l], w_xkv[l], w_xo[l],
                   post_x_g[l], pre_ffn_g[l], w_up[l], conv_w[l], conv_b[l], w_down[l], post_ffn_g[l])
    return x
```

```python
import math

import numpy as np
import jax
import jax.numpy as jnp
from jax import lax
from jax.experimental import pallas as pl
from jax.experimental.pallas import tpu as pltpu

F32 = jnp.float32
BF16 = jnp.bfloat16

EPS = 1e-6
LOG2E = math.log2(math.e)
GRID_W = 64
ROPE_THETA = 10000.0

ATT_HEADS = 8
ATT_KV_HEADS = 2
ATT_GROUP = ATT_HEADS // ATT_KV_HEADS
ATT_HEAD_DIM = 64
ATT_Q_DIM = ATT_HEADS * ATT_HEAD_DIM
ATT_KV_DIM = ATT_KV_HEADS * ATT_HEAD_DIM

HG_HEADS = 4
HG_DIM = 128
HG_WIDTH = HG_HEADS * HG_DIM
HG_CHUNK = 128

X_HEADS = 4
CONV_W = 3
FF_CHUNK = 256
HALO = 16

VMEM_LIMIT = 56 * 1024 * 1024
SUBLANES = 8


def _params(*sem):
    return pltpu.CompilerParams(dimension_semantics=sem, vmem_limit_bytes=VMEM_LIMIT)


def _rms(x, g):
    ms = jnp.mean(x * x, axis=-1, keepdims=True)
    return x * lax.rsqrt(ms + EPS) * g


def _silu(x):
    return x / (1.0 + jnp.exp(-x))


def _dot(a, b):
    return jnp.dot(a, b, preferred_element_type=F32)


def _dot_nt(a, b):
    return lax.dot_general(a, b, (((1,), (1,)), ((), ())), preferred_element_type=F32)


def _dot_tn(a, b):
    return lax.dot_general(a, b, (((0,), (0,)), ((), ())), preferred_element_type=F32)


def _const_spec(shape):
    return pl.BlockSpec(shape, lambda *_: (0,) * len(shape))


def _mem_proj_kernel(m_ref, g_ref, w_ref, k_ref, v_ref):
    d = m_ref.shape[-1]
    m = _rms(m_ref[0], g_ref[...]).astype(BF16)
    kv = _dot(m, w_ref[...])
    k_ref[0] = kv[:, :d].astype(BF16)
    v_ref[0] = kv[:, d:].astype(BF16)


def _mem_proj(mem, g, w_xkv):
    b, nm, d = mem.shape
    out = jax.ShapeDtypeStruct((b, nm, d), BF16)
    return pl.pallas_call(
        _mem_proj_kernel,
        grid=(b,),
        in_specs=[pl.BlockSpec((1, nm, d), lambda i: (i, 0, 0)),
                  _const_spec((1, d)), _const_spec((d, 2 * d))],
        out_specs=[pl.BlockSpec((1, nm, d), lambda i: (i, 0, 0))] * 2,
        out_shape=[out, out],
        compiler_params=_params("parallel"),
        name="mem_proj",
    )(mem, g, w_xkv)


def _head_norm_rope(a, seg, gain, cos, sin_signed):
    sq = a * a
    hi = sq.astype(BF16)
    lo = (sq - hi.astype(F32)).astype(BF16)
    ss = _dot(hi, seg) + _dot(lo, seg)
    an = a * lax.rsqrt(ss * (1.0 / ATT_HEAD_DIM) + EPS) * gain
    w = a.shape[-1]
    lane = lax.broadcasted_iota(jnp.int32, a.shape, 1)
    partner = jnp.where(lane % 2 == 0, pltpu.roll(an, w - 1, 1), pltpu.roll(an, 1, 1))
    return an * cos + partner * sin_signed


def _in_proj_kernel(x_ref, g_ref, w_ref, qg_ref, kg_ref, cos_ref, sin_ref, lb_ref, segq_ref, segk_ref,
                    q_ref, k_ref, v_ref, hq_ref, lff_ref, lfb_ref, hi_ref, hg_ref):
    h = _rms(x_ref[0], g_ref[...]).astype(BF16)

    def proj(lo, width):
        return _dot(h, w_ref[:, lo:lo + width])

    cos = cos_ref[...]
    sin = sin_ref[...]
    c0 = 0
    q = _head_norm_rope(proj(c0, ATT_Q_DIM), segq_ref[...], qg_ref[...],
                        jnp.concatenate([cos] * (ATT_Q_DIM // 128), axis=-1),
                        jnp.concatenate([sin] * (ATT_Q_DIM // 128), axis=-1))
    for i in range(ATT_HEADS):
        q_ref[0, i] = q[:, i * ATT_HEAD_DIM:(i + 1) * ATT_HEAD_DIM].astype(BF16)
    c0 += ATT_Q_DIM
    k = _head_norm_rope(proj(c0, ATT_KV_DIM), segk_ref[...], kg_ref[...], cos, sin)
    c0 += ATT_KV_DIM
    v = proj(c0, ATT_KV_DIM)
    c0 += ATT_KV_DIM
    for i in range(ATT_KV_HEADS):
        k_ref[0, i] = k[:, i * ATT_HEAD_DIM:(i + 1) * ATT_HEAD_DIM].astype(BF16)
        v_ref[0, i] = v[:, i * ATT_HEAD_DIM:(i + 1) * ATT_HEAD_DIM].astype(BF16)

    def heads_out(ref, val):
        for i in range(HG_HEADS):
            ref[0, i] = val[:, i * HG_DIM:(i + 1) * HG_DIM]

    heads_out(hq_ref, _silu(proj(c0, HG_WIDTH)))
    c0 += HG_WIDTH
    for d, ref in enumerate((lff_ref, lfb_ref)):
        lb = lb_ref[d:d + 1, :]
        z = proj(c0, HG_WIDTH)
        heads_out(ref, jnp.log(lb + (1.0 - lb) / (1.0 + jnp.exp(-z))) * LOG2E)
        c0 += HG_WIDTH
    heads_out(hi_ref, proj(c0, HG_WIDTH))
    c0 += HG_WIDTH
    hg_ref[0] = _silu(proj(c0, HG_WIDTH))


def _in_proj(x, g, w_in, qg, kg, cos, sin, lb, segq, segk, tm):
    b, n, d = x.shape
    nt = n // tm
    n_in = w_in.shape[1]
    tok = lambda i, j: (i, j, 0)
    head = lambda i, j: (i, 0, j, 0)
    pos = lambda i, j: (j, 0)
    return pl.pallas_call(
        _in_proj_kernel,
        grid=(b, nt),
        in_specs=[pl.BlockSpec((1, tm, d), tok), _const_spec((1, d)), _const_spec((d, n_in)),
                  _const_spec((1, ATT_Q_DIM)), _const_spec((1, ATT_KV_DIM)),
                  pl.BlockSpec((tm, 128), pos), pl.BlockSpec((tm, 128), pos),
                  _const_spec((2, HG_WIDTH)),
                  _const_spec((ATT_Q_DIM, ATT_Q_DIM)), _const_spec((ATT_KV_DIM, ATT_KV_DIM))],
        out_specs=[pl.BlockSpec((1, ATT_HEADS, tm, ATT_HEAD_DIM), head),
                   pl.BlockSpec((1, ATT_KV_HEADS, tm, ATT_HEAD_DIM), head),
                   pl.BlockSpec((1, ATT_KV_HEADS, tm, ATT_HEAD_DIM), head),
                   pl.BlockSpec((1, HG_HEADS, tm, HG_DIM), head),
                   pl.BlockSpec((1, HG_HEADS, tm, HG_DIM), head),
                   pl.BlockSpec((1, HG_HEADS, tm, HG_DIM), head),
                   pl.BlockSpec((1, HG_HEADS, tm, HG_DIM), head),
                   pl.BlockSpec((1, tm, HG_WIDTH), tok)],
        out_shape=[jax.ShapeDtypeStruct((b, ATT_HEADS, n, ATT_HEAD_DIM), BF16),
                   jax.ShapeDtypeStruct((b, ATT_KV_HEADS, n, ATT_HEAD_DIM), BF16),
                   jax.ShapeDtypeStruct((b, ATT_KV_HEADS, n, ATT_HEAD_DIM), BF16),
                   jax.ShapeDtypeStruct((b, HG_HEADS, n, HG_DIM), F32),
                   jax.ShapeDtypeStruct((b, HG_HEADS, n, HG_DIM), F32),
                   jax.ShapeDtypeStruct((b, HG_HEADS, n, HG_DIM), F32),
                   jax.ShapeDtypeStruct((b, HG_HEADS, n, HG_DIM), F32),
                   jax.ShapeDtypeStruct((b, n, HG_WIDTH), F32)],
        compiler_params=_params("parallel", "parallel"),
        name="in_proj",
    )(x, g, w_in, qg, kg, cos, sin, lb, segq, segk)


def _attention_kernel(q_ref, k_ref, v_ref, o_ref):
    for i in range(ATT_HEADS):
        kv = i // ATT_GROUP
        s = _dot_nt(q_ref[0, i], k_ref[0, kv])
        p = jnp.exp(s - jnp.max(s, axis=-1, keepdims=True))
        l = jnp.sum(p, axis=-1, keepdims=True)
        o = _dot(p.astype(BF16), v_ref[0, kv]) / l
        o_ref[0, :, i * ATT_HEAD_DIM:(i + 1) * ATT_HEAD_DIM] = o.astype(BF16)


def _attention(q, k, v, tq):
    b, _, n, dh = q.shape
    return pl.pallas_call(
        _attention_kernel,
        grid=(b, n // tq),
        in_specs=[pl.BlockSpec((1, ATT_HEADS, tq, dh), lambda i, j: (i, 0, j, 0)),
                  pl.BlockSpec((1, ATT_KV_HEADS, n, dh), lambda i, j: (i, 0, 0, 0)),
                  pl.BlockSpec((1, ATT_KV_HEADS, n, dh), lambda i, j: (i, 0, 0, 0))],
        out_specs=pl.BlockSpec((1, tq, ATT_Q_DIM), lambda i, j: (i, j, 0)),
        out_shape=jax.ShapeDtypeStruct((b, n, ATT_Q_DIM), BF16),
        compiler_params=_params("parallel", "parallel"),
        name="attention",
    )(q, k, v)


def _hgrn_levels():
    m, out = 1, []
    while m < HG_CHUNK:
        out.append(m)
        m *= 2
    return out


def _hgrn_tables():
    c = HG_CHUNK
    t = np.arange(c)[:, None]
    u = np.arange(c)[None, :]
    mats, masks = [], []
    for reverse in (False, True):
        blocks, pairs = [], [t == u]
        for m in _hgrn_levels():
            base = (t // (2 * m)) * (2 * m)
            if not reverse:
                r = base + m - 1
                block = np.where(t > r, (u > r) & (u <= t), (u > t) & (u <= r))
                roles = ((t // m) % 2 == 1) & ((u // m) % 2 == 0)
            else:
                r = base + m
                block = np.where(t < r, (u >= t) & (u < r), (u >= r) & (u < t))
                roles = ((t // m) % 2 == 0) & ((u // m) % 2 == 1)
            if m < SUBLANES:
                blocks.append(block)
            pairs.append(roles & (t // (2 * m) == u // (2 * m)))
        blocks.append(u >= t if reverse else u <= t)
        mats.append(np.tile(np.concatenate(blocks, axis=0), (1, 2)))
        masks.append(np.stack(pairs))
    return np.stack(mats).astype(np.float32), np.stack(masks).astype(np.float32)


def _hgrn_chunk(q, lf2, v, state, sum_mat, pair_mask, b_scr, reverse):
    c = HG_CHUNK
    levels = _hgrn_levels()
    n_small = sum(m < SUBLANES for m in levels)
    hi = lf2.astype(BF16)
    mid = (lf2 - hi.astype(F32)).astype(BF16)
    sums = _dot(sum_mat, jnp.concatenate([hi, mid], axis=0))
    b = sums[n_small * c:]
    b_scr[...] = b

    qb = q.astype(BF16)
    kb = (1.0 - jnp.exp2(lf2)).astype(BF16)
    a = _dot_nt(qb, kb) * pair_mask[0]
    for j, m in enumerate(levels):
        if m < SUBLANES:
            e = sums[j * c:(j + 1) * c]
        else:
            parts = []
            for g in range(c // (2 * m)):
                lo, mid_row, up = 2 * m * g, 2 * m * g + m, 2 * m * (g + 1)
                r = mid_row if reverse else mid_row - 1
                b_r = b_scr[r:r + 1, :]
                parts += [b[lo:mid_row] - b_r, b_r - b[mid_row:up]] if reverse else \
                         [b_r - b[lo:mid_row], b[mid_row:up] - b_r]
            e = jnp.concatenate(parts, axis=0)
        fac = jnp.exp2(e).astype(BF16)
        a = a + _dot_nt(qb * fac, kb * fac) * pair_mask[j + 1]
    vb = v.astype(BF16)
    last = 0 if reverse else c - 1
    total = b_scr[last:last + 1, :]
    o = _dot(a.astype(BF16), vb) + _dot_nt(qb * jnp.exp2(b).astype(BF16), state.astype(BF16))
    k_dec = kb * jnp.exp2(total - b).astype(BF16)
    new_state = state * jnp.exp2(total) + _dot_tn(vb, k_dec)
    return o, new_state


def _hgrn_kernel(qf_ref, lff_ref, vf_ref, qb_ref, lfb_ref, vb_ref, sm_ref, pm_ref, of_ref, ob_ref,
                 s_ref, b_scr):
    @pl.when(pl.program_id(1) == 0)
    def _():
        s_ref[...] = jnp.zeros_like(s_ref)

    for h in range(HG_HEADS):
        o, s = _hgrn_chunk(qf_ref[0, h], lff_ref[0, h], vf_ref[0, h], s_ref[0, h], sm_ref[0], pm_ref.at[0],
                           b_scr.at[0, h], False)
        of_ref[0, h] = o
        s_ref[0, h] = s
        o, s = _hgrn_chunk(qb_ref[0, h], lfb_ref[0, h], vb_ref[0, h], s_ref[1, h], sm_ref[1], pm_ref.at[1],
                           b_scr.at[1, h], True)
        ob_ref[0, h] = o
        s_ref[1, h] = s


def _hgrn(hq, lf_f, lf_b, hi):
    b, nh, n, dk = hq.shape
    nc = n // HG_CHUNK
    blk = (1, nh, HG_CHUNK, dk)
    fwd = pl.BlockSpec(blk, lambda i, j: (i, 0, j, 0))
    bwd = pl.BlockSpec(blk, lambda i, j: (i, 0, nc - 1 - j, 0))
    out = jax.ShapeDtypeStruct((b, nh, n, dk), F32)
    sum_mat, pair_mask = _hgrn_tables()
    return pl.pallas_call(
        _hgrn_kernel,
        grid=(b, nc),
        in_specs=[fwd, fwd, fwd, bwd, bwd, bwd, _const_spec(sum_mat.shape), _const_spec(pair_mask.shape)],
        out_specs=[fwd, bwd],
        out_shape=[out, out],
        scratch_shapes=[pltpu.VMEM((2, nh, dk, dk), F32), pltpu.VMEM((2, nh, HG_CHUNK, dk), F32)],
        compiler_params=_params("parallel", "arbitrary"),
        name="hgrn",
    )(hq, lf_f, hi, hq, lf_b, hi, jnp.asarray(sum_mat, BF16), jnp.asarray(pair_mask, F32))


def _mix_cross_kernel(x_ref, att_ref, of_ref, ob_ref, sg_ref, og_ref, wo_ref, pmg_ref, pxg_ref,
                      wq_ref, km_ref, vm_ref, wxo_ref, poxg_ref, pfg_ref, x2_ref, h3_ref):
    rec = []
    for i in range(HG_HEADS):
        o = _rms(of_ref[0, i] + ob_ref[0, i], og_ref[...])
        rec.append((o * sg_ref[0, :, i * HG_DIM:(i + 1) * HG_DIM]).astype(BF16))
    mix_in = jnp.concatenate([att_ref[0]] + rec, axis=-1)
    x1 = x_ref[0] + _rms(_dot(mix_in, wo_ref[...]), pmg_ref[...])

    h2 = _rms(x1, pxg_ref[...]).astype(BF16)
    d = h2.shape[-1]
    dh = d // X_HEADS
    q = _dot(h2, wq_ref[...]) * (dh ** -0.5)
    heads = []
    for i in range(X_HEADS):
        sl = slice(i * dh, (i + 1) * dh)
        s = _dot_nt(q[:, sl].astype(BF16), km_ref[0, :, sl])
        p = jnp.exp(s - jnp.max(s, axis=-1, keepdims=True))
        l = jnp.sum(p, axis=-1, keepdims=True)
        heads.append((_dot(p.astype(BF16), vm_ref[0, :, sl]) / l).astype(BF16))
    xo = _dot(jnp.concatenate(heads, axis=-1), wxo_ref[...])
    x2 = x1 + _rms(xo, poxg_ref[...])
    x2_ref[0] = x2
    h3_ref[0] = _rms(x2, pfg_ref[...]).astype(BF16)


def _mix_cross(x, att, o_f, o_b, sg, og, w_out, pmg, pxg, w_xq, k_mem, v_mem, w_xo, poxg, pfg, tm):
    b, n, d = x.shape
    nm = k_mem.shape[1]
    tok = lambda i, j: (i, j, 0)
    head = lambda i, j: (i, 0, j, 0)
    batch = lambda i, j: (i, 0, 0)
    vec = _const_spec((1, d))
    mat = _const_spec((d, d))
    return pl.pallas_call(
        _mix_cross_kernel,
        grid=(b, n // tm),
        in_specs=[pl.BlockSpec((1, tm, d), tok), pl.BlockSpec((1, tm, ATT_Q_DIM), tok),
                  pl.BlockSpec((1, HG_HEADS, tm, HG_DIM), head), pl.BlockSpec((1, HG_HEADS, tm, HG_DIM), head),
                  pl.BlockSpec((1, tm, HG_WIDTH), tok), _const_spec((1, HG_DIM)),
                  mat, vec, vec, mat,
                  pl.BlockSpec((1, nm, d), batch), pl.BlockSpec((1, nm, d), batch),
                  mat, vec, vec],
        out_specs=[pl.BlockSpec((1, tm, d), tok), pl.BlockSpec((1, tm, d), tok)],
        out_shape=[jax.ShapeDtypeStruct((b, n, d), F32), jax.ShapeDtypeStruct((b, n, d), BF16)],
        compiler_params=_params("parallel", "parallel"),
        name="mix_cross",
    )(x, att, o_f, o_b, sg, og, w_out, pmg, pxg, w_xq, k_mem, v_mem, w_xo, poxg, pfg)


def _conv_ffn_kernel(x_ref, h_ref, hp_ref, hn_ref, wg_ref, wv_ref, cwg_ref, cwv_ref, cbg_ref, cbv_ref,
                     wd_ref, g_ref, o_ref, ug_scr, uv_scr, act_scr):
    j = pl.program_id(1)
    tm = h_ref.shape[1]
    nck, _, ck = wg_ref.shape
    prev = jnp.where(j > 0, hp_ref[0], jnp.zeros_like(hp_ref[0]))
    nxt = jnp.where(j < pl.num_programs(1) - 1, hn_ref[0], jnp.zeros_like(hn_ref[0]))
    hext = jnp.concatenate([prev, h_ref[0], nxt], axis=0)

    def conv(scr, cw, cb):
        out = cb
        for t in range(CONV_W):
            out = out + scr[pl.ds(HALO - CONV_W // 2 + t, tm), :] * cw[t:t + 1, :]
        return out

    for c in range(nck):
        slot = c % 2
        ug_scr[slot] = _dot(hext, wg_ref[c])
        uv_scr[slot] = _dot(hext, wv_ref[c])
        gate = conv(ug_scr.at[slot], cwg_ref[c], cbg_ref[c])
        val = conv(uv_scr.at[slot], cwv_ref[c], cbv_ref[c])
        act_scr[:, c * ck:(c + 1) * ck] = (_silu(gate) * val).astype(BF16)
    o_ref[0] = x_ref[0] + _rms(_dot(act_scr[...], wd_ref[...]), g_ref[...])


def _conv_ffn(x2, h3, wg, wv, cwg, cwv, cbg, cbv, wd, g, tm):
    b, n, d = x2.shape
    nck, _, ck = wg.shape
    hb = tm // HALO
    last = n // HALO - 1
    tok = lambda i, j: (i, j, 0)
    full = lambda shape: pl.BlockSpec(shape, lambda *_: (0,) * len(shape), pipeline_mode=pl.Buffered(1))
    return pl.pallas_call(
        _conv_ffn_kernel,
        grid=(b, n // tm),
        in_specs=[pl.BlockSpec((1, tm, d), tok), pl.BlockSpec((1, tm, d), tok),
                  pl.BlockSpec((1, HALO, d), lambda i, j: (i, jnp.maximum(j * hb - 1, 0), 0)),
                  pl.BlockSpec((1, HALO, d), lambda i, j: (i, jnp.minimum((j + 1) * hb, last), 0)),
                  full((nck, d, ck)), full((nck, d, ck)),
                  full((nck, CONV_W, ck)), full((nck, CONV_W, ck)),
                  full((nck, 1, ck)), full((nck, 1, ck)),
                  full((nck * ck, d)), _const_spec((1, d))],
        out_specs=pl.BlockSpec((1, tm, d), tok),
        out_shape=jax.ShapeDtypeStruct((b, n, d), F32),
        scratch_shapes=[pltpu.VMEM((2, tm + 2 * HALO, ck), F32), pltpu.VMEM((2, tm + 2 * HALO, ck), F32),
                        pltpu.VMEM((tm, nck * ck), BF16)],
        compiler_params=_params("parallel", "parallel"),
        name="conv_ffn",
    )(x2, h3, h3, h3, wg, wv, cwg, cwv, cbg, cbv, wd, g)


def _rope_tables(n):
    pairs = ATT_HEAD_DIM // 4
    pos = jnp.arange(n)
    inv = jnp.power(ROPE_THETA, -jnp.arange(pairs, dtype=F32) / pairs)
    ang = jnp.concatenate([(pos // GRID_W).astype(F32)[:, None] * inv,
                           (pos % GRID_W).astype(F32)[:, None] * inv], axis=-1)
    cos = jnp.repeat(jnp.cos(ang), 2, axis=-1)
    sin = jnp.repeat(jnp.sin(ang), 2, axis=-1) * jnp.tile(jnp.array([-1.0, 1.0], F32), ATT_HEAD_DIM // 2)
    return jnp.tile(cos, (1, 2)), jnp.tile(sin, (1, 2))


def _segment_ones(width):
    i = jnp.arange(width) // ATT_HEAD_DIM
    return (i[:, None] == i[None, :]).astype(BF16)


def _layer(x, mem, lb, pre_mix_g, w_in, q_norm_g, k_norm_g, hg_out_norm_g, w_out, post_mix_g, pre_x_g,
           mem_norm_g, w_xq, w_xkv, w_xo, post_x_g, pre_ffn_g, w_up, conv_w, conv_b, w_down, post_ffn_g):
    b, n, d = x.shape
    d_ff = w_down.shape[0]
    assert n % HG_CHUNK == 0 and n % GRID_W == 0 and d_ff % FF_CHUNK == 0
    tm = min(256, n)
    tf = min(512, n)
    row = lambda g: g.reshape(1, -1).astype(F32)

    cos, sin = _rope_tables(n)
    qg = jnp.tile(q_norm_g, ATT_HEADS).reshape(1, -1) * (ATT_HEAD_DIM ** -0.5)
    kg = jnp.tile(k_norm_g, ATT_KV_HEADS).reshape(1, -1)

    k_mem, v_mem = _mem_proj(mem, row(mem_norm_g), w_xkv.astype(BF16))
    q, k, v, hq, lf_f, lf_b, hi, sg = _in_proj(
        x, row(pre_mix_g), w_in.astype(BF16), qg, kg, cos, sin, lb,
        _segment_ones(ATT_Q_DIM), _segment_ones(ATT_KV_DIM), tm)
    att = _attention(q, k, v, tm)
    o_f, o_b = _hgrn(hq, lf_f, lf_b, hi)
    x2, h3 = _mix_cross(x, att, o_f, o_b, sg, row(hg_out_norm_g), w_out.astype(BF16), row(post_mix_g),
                        row(pre_x_g), w_xq.astype(BF16), k_mem, v_mem, w_xo.astype(BF16),
                        row(post_x_g), row(pre_ffn_g), tm)

    nck = d_ff // FF_CHUNK
    chunks = lambda w: jnp.transpose(w.reshape(w.shape[0], nck, FF_CHUNK), (1, 0, 2))
    w_up = w_up.astype(BF16)
    return _conv_ffn(x2, h3, chunks(w_up[:, :d_ff]), chunks(w_up[:, d_ff:]),
                     chunks(conv_w[:, :d_ff]), chunks(conv_w[:, d_ff:]),
                     chunks(conv_b[None, :d_ff]), chunks(conv_b[None, d_ff:]),
                     w_down.astype(BF16), row(post_ffn_g), tf)


def kernel(x, mem, pre_mix_g, w_in, q_norm_g, k_norm_g, hg_lb, hg_out_norm_g, w_out, post_mix_g, pre_x_g,
           mem_norm_g, w_xq, w_xkv, w_xo, post_x_g, pre_ffn_g, w_up, conv_w, conv_b, w_down, post_ffn_g):
    lb_all = jnp.cumsum(jax.nn.softmax(hg_lb.astype(F32), axis=1), axis=1)
    for l in range(w_in.shape[0]):
        x = _layer(x, mem, lb_all[:, l], pre_mix_g[l], w_in[l], q_norm_g[l], k_norm_g[l], hg_out_norm_g[l],
                   w_out[l], post_mix_g[l], pre_x_g[l], mem_norm_g[l], w_xq[l], w_xkv[l], w_xo[l],
                   post_x_g[l], pre_ffn_g[l], w_up[l], conv_w[l], conv_b[l], w_down[l], post_ffn_g[l])
    return x
```

```python
import math

import numpy as np
import jax
import jax.numpy as jnp
from jax import lax
from jax.experimental import pallas as pl
from jax.experimental.pallas import tpu as pltpu

F32 = jnp.float32
BF16 = jnp.bfloat16

EPS = 1e-6
LOG2E = math.log2(math.e)
GRID_W = 64
ROPE_THETA = 10000.0

ATT_HEADS = 8
ATT_KV_HEADS = 2
ATT_GROUP = ATT_HEADS // ATT_KV_HEADS
ATT_HEAD_DIM = 64
ATT_Q_DIM = ATT_HEADS * ATT_HEAD_DIM
ATT_KV_DIM = ATT_KV_HEADS * ATT_HEAD_DIM

HG_HEADS = 4
HG_DIM = 128
HG_WIDTH = HG_HEADS * HG_DIM
HG_CHUNK = 128
HG_FAST_BLOCK = 32
HG_FAST_MAX_LOG2 = 96.0

X_HEADS = 4
CONV_W = 3
FF_CHUNK = 256
HALO = 16

VMEM_LIMIT = 56 * 1024 * 1024
SUBLANES = 8


def _params(*sem):
    return pltpu.CompilerParams(dimension_semantics=sem, vmem_limit_bytes=VMEM_LIMIT)


def _rms(x, g):
    ms = jnp.mean(x * x, axis=-1, keepdims=True)
    return x * lax.rsqrt(ms + EPS) * g


def _silu(x):
    return x / (1.0 + jnp.exp(-x))


def _dot(a, b):
    return jnp.dot(a, b, preferred_element_type=F32)


def _dot_nt(a, b):
    return lax.dot_general(a, b, (((1,), (1,)), ((), ())), preferred_element_type=F32)


def _dot_tn(a, b):
    return lax.dot_general(a, b, (((0,), (0,)), ((), ())), preferred_element_type=F32)


def _const_spec(shape):
    return pl.BlockSpec(shape, lambda *_: (0,) * len(shape))


def _mem_proj_kernel(m_ref, g_ref, w_ref, k_ref, v_ref):
    d = m_ref.shape[-1]
    m = _rms(m_ref[0], g_ref[...]).astype(BF16)
    kv = _dot(m, w_ref[...])
    k_ref[0] = kv[:, :d].astype(BF16)
    v_ref[0] = kv[:, d:].astype(BF16)


def _mem_proj(mem, g, w_xkv):
    b, nm, d = mem.shape
    out = jax.ShapeDtypeStruct((b, nm, d), BF16)
    return pl.pallas_call(
        _mem_proj_kernel,
        grid=(b,),
        in_specs=[pl.BlockSpec((1, nm, d), lambda i: (i, 0, 0)),
                  _const_spec((1, d)), _const_spec((d, 2 * d))],
        out_specs=[pl.BlockSpec((1, nm, d), lambda i: (i, 0, 0))] * 2,
        out_shape=[out, out],
        compiler_params=_params("parallel"),
        name="mem_proj",
    )(mem, g, w_xkv)


def _head_norm_rope(a, seg, gain, cos, sin_signed):
    sq = a * a
    hi = sq.astype(BF16)
    lo = (sq - hi.astype(F32)).astype(BF16)
    ss = _dot(hi, seg) + _dot(lo, seg)
    an = a * lax.rsqrt(ss * (1.0 / ATT_HEAD_DIM) + EPS) * gain
    w = a.shape[-1]
    lane = lax.broadcasted_iota(jnp.int32, a.shape, 1)
    partner = jnp.where(lane % 2 == 0, pltpu.roll(an, w - 1, 1), pltpu.roll(an, 1, 1))
    return an * cos + partner * sin_signed


def _in_proj_kernel(x_ref, g_ref, w_ref, qg_ref, kg_ref, cos_ref, sin_ref, lb_ref, segq_ref, segk_ref,
                    q_ref, k_ref, v_ref, hq_ref, lff_ref, lfb_ref, hi_ref, hg_ref):
    h = _rms(x_ref[0], g_ref[...]).astype(BF16)

    def proj(lo, width):
        return _dot(h, w_ref[:, lo:lo + width])

    cos = cos_ref[...]
    sin = sin_ref[...]
    c0 = 0
    q = _head_norm_rope(proj(c0, ATT_Q_DIM), segq_ref[...], qg_ref[...],
                        jnp.concatenate([cos] * (ATT_Q_DIM // 128), axis=-1),
                        jnp.concatenate([sin] * (ATT_Q_DIM // 128), axis=-1))
    for i in range(ATT_HEADS):
        q_ref[0, i] = q[:, i * ATT_HEAD_DIM:(i + 1) * ATT_HEAD_DIM].astype(BF16)
    c0 += ATT_Q_DIM
    k = _head_norm_rope(proj(c0, ATT_KV_DIM), segk_ref[...], kg_ref[...], cos, sin)
    c0 += ATT_KV_DIM
    v = proj(c0, ATT_KV_DIM)
    c0 += ATT_KV_DIM
    for i in range(ATT_KV_HEADS):
        k_ref[0, i] = k[:, i * ATT_HEAD_DIM:(i + 1) * ATT_HEAD_DIM].astype(BF16)
        v_ref[0, i] = v[:, i * ATT_HEAD_DIM:(i + 1) * ATT_HEAD_DIM].astype(BF16)

    def heads_out(ref, val):
        for i in range(HG_HEADS):
            ref[0, i] = val[:, i * HG_DIM:(i + 1) * HG_DIM]

    heads_out(hq_ref, _silu(proj(c0, HG_WIDTH)))
    c0 += HG_WIDTH
    for d, ref in enumerate((lff_ref, lfb_ref)):
        lb = lb_ref[d:d + 1, :]
        z = proj(c0, HG_WIDTH)
        heads_out(ref, jnp.log(lb + (1.0 - lb) / (1.0 + jnp.exp(-z))) * LOG2E)
        c0 += HG_WIDTH
    heads_out(hi_ref, proj(c0, HG_WIDTH))
    c0 += HG_WIDTH
    hg_ref[0] = _silu(proj(c0, HG_WIDTH))


def _in_proj(x, g, w_in, qg, kg, cos, sin, lb, segq, segk, tm):
    b, n, d = x.shape
    nt = n // tm
    n_in = w_in.shape[1]
    tok = lambda i, j: (i, j, 0)
    head = lambda i, j: (i, 0, j, 0)
    pos = lambda i, j: (j, 0)
    return pl.pallas_call(
        _in_proj_kernel,
        grid=(b, nt),
        in_specs=[pl.BlockSpec((1, tm, d), tok), _const_spec((1, d)), _const_spec((d, n_in)),
                  _const_spec((1, ATT_Q_DIM)), _const_spec((1, ATT_KV_DIM)),
                  pl.BlockSpec((tm, 128), pos), pl.BlockSpec((tm, 128), pos),
                  _const_spec((2, HG_WIDTH)),
                  _const_spec((ATT_Q_DIM, ATT_Q_DIM)), _const_spec((ATT_KV_DIM, ATT_KV_DIM))],
        out_specs=[pl.BlockSpec((1, ATT_HEADS, tm, ATT_HEAD_DIM), head),
                   pl.BlockSpec((1, ATT_KV_HEADS, tm, ATT_HEAD_DIM), head),
                   pl.BlockSpec((1, ATT_KV_HEADS, tm, ATT_HEAD_DIM), head),
                   pl.BlockSpec((1, HG_HEADS, tm, HG_DIM), head),
                   pl.BlockSpec((1, HG_HEADS, tm, HG_DIM), head),
                   pl.BlockSpec((1, HG_HEADS, tm, HG_DIM), head),
                   pl.BlockSpec((1, HG_HEADS, tm, HG_DIM), head),
                   pl.BlockSpec((1, tm, HG_WIDTH), tok)],
        out_shape=[jax.ShapeDtypeStruct((b, ATT_HEADS, n, ATT_HEAD_DIM), BF16),
                   jax.ShapeDtypeStruct((b, ATT_KV_HEADS, n, ATT_HEAD_DIM), BF16),
                   jax.ShapeDtypeStruct((b, ATT_KV_HEADS, n, ATT_HEAD_DIM), BF16),
                   jax.ShapeDtypeStruct((b, HG_HEADS, n, HG_DIM), F32),
                   jax.ShapeDtypeStruct((b, HG_HEADS, n, HG_DIM), F32),
                   jax.ShapeDtypeStruct((b, HG_HEADS, n, HG_DIM), F32),
                   jax.ShapeDtypeStruct((b, HG_HEADS, n, HG_DIM), F32),
                   jax.ShapeDtypeStruct((b, n, HG_WIDTH), F32)],
        compiler_params=_params("parallel", "parallel"),
        name="in_proj",
    )(x, g, w_in, qg, kg, cos, sin, lb, segq, segk)


def _attention_kernel(q_ref, k_ref, v_ref, o_ref):
    for i in range(ATT_HEADS):
        kv = i // ATT_GROUP
        s = _dot_nt(q_ref[0, i], k_ref[0, kv])
        p = jnp.exp2(s - jnp.max(s, axis=-1, keepdims=True))
        l = jnp.sum(p, axis=-1, keepdims=True)
        o = _dot(p.astype(BF16), v_ref[0, kv]) / l
        o_ref[0, :, i * ATT_HEAD_DIM:(i + 1) * ATT_HEAD_DIM] = o.astype(BF16)


def _attention(q, k, v, tq):
    b, _, n, dh = q.shape
    return pl.pallas_call(
        _attention_kernel,
        grid=(b, n // tq),
        in_specs=[pl.BlockSpec((1, ATT_HEADS, tq, dh), lambda i, j: (i, 0, j, 0)),
                  pl.BlockSpec((1, ATT_KV_HEADS, n, dh), lambda i, j: (i, 0, 0, 0)),
                  pl.BlockSpec((1, ATT_KV_HEADS, n, dh), lambda i, j: (i, 0, 0, 0))],
        out_specs=pl.BlockSpec((1, tq, ATT_Q_DIM), lambda i, j: (i, j, 0)),
        out_shape=jax.ShapeDtypeStruct((b, n, ATT_Q_DIM), BF16),
        compiler_params=_params("parallel", "parallel"),
        name="attention",
    )(q, k, v)


def _hgrn_levels():
    m, out = 1, []
    while m < HG_CHUNK:
        out.append(m)
        m *= 2
    return out


def _hgrn_tables():
    c = HG_CHUNK
    t = np.arange(c)[:, None]
    u = np.arange(c)[None, :]
    mats, masks, fast = [], [], []
    for reverse in (False, True):
        blocks, pairs = [], [t == u]
        for m in _hgrn_levels():
            base = (t // (2 * m)) * (2 * m)
            if not reverse:
                r = base + m - 1
                block = np.where(t > r, (u > r) & (u <= t), (u > t) & (u <= r))
                roles = ((t // m) % 2 == 1) & ((u // m) % 2 == 0)
            else:
                r = base + m
                block = np.where(t < r, (u >= t) & (u < r), (u >= r) & (u < t))
                roles = ((t // m) % 2 == 0) & ((u // m) % 2 == 1)
            if m < SUBLANES:
                blocks.append(block)
            pairs.append(roles & (t // (2 * m) == u // (2 * m)))
        blocks.append(u >= t if reverse else u <= t)
        mats.append(np.tile(np.concatenate(blocks, axis=0), (1, 2)))
        masks.append(np.stack(pairs))
        fast.append((t // HG_FAST_BLOCK == u // HG_FAST_BLOCK) & (u >= t if reverse else u <= t))
    return tuple(np.stack(x).astype(np.float32) for x in (mats, masks, fast))


def _hgrn_sums(lf2, sum_mat):
    hi = lf2.astype(BF16)
    mid = (lf2 - hi.astype(F32)).astype(BF16)
    return _dot(sum_mat, jnp.concatenate([hi, mid], axis=0))


def _hgrn_chunk(q, lf2, v, state, sum_mat, pair_mask, fast_mask, b_scr, reverse, fast):
    c = HG_CHUNK
    levels = list(enumerate(_hgrn_levels()))
    n_small = sum(m < SUBLANES for _, m in levels)
    qb = q.astype(BF16)
    kb = (1.0 - jnp.exp2(lf2)).astype(BF16)
    if fast:
        b = b_scr[...]
        parts = []
        for g in range(c // HG_FAST_BLOCK):
            lo, up = g * HG_FAST_BLOCK, (g + 1) * HG_FAST_BLOCK
            before = up if reverse else lo - 1
            parts.append(b[lo:up] - b_scr[before:before + 1, :] if 0 <= before < c else b[lo:up])
        loc = jnp.concatenate(parts, axis=0)
        a = _dot_nt(qb * jnp.exp2(loc).astype(BF16), kb * jnp.exp2(-loc).astype(BF16)) * fast_mask[...]
        levels = [(j, m) for j, m in levels if m >= HG_FAST_BLOCK]
    else:
        sums = _hgrn_sums(lf2, sum_mat)
        b = sums[n_small * c:]
        b_scr[...] = b
        a = _dot_nt(qb, kb) * pair_mask[0]
    for j, m in levels:
        if m < SUBLANES:
            e = sums[j * c:(j + 1) * c]
        else:
            parts = []
            for g in range(c // (2 * m)):
                lo, mid_row, up = 2 * m * g, 2 * m * g + m, 2 * m * (g + 1)
                r = mid_row if reverse else mid_row - 1
                b_r = b_scr[r:r + 1, :]
                parts += [b[lo:mid_row] - b_r, b_r - b[mid_row:up]] if reverse else \
                         [b_r - b[lo:mid_row], b[mid_row:up] - b_r]
            e = jnp.concatenate(parts, axis=0)
        fac = jnp.exp2(e).astype(BF16)
        a = a + _dot_nt(qb * fac, kb * fac) * pair_mask[j + 1]
    vb = v.astype(BF16)
    last = 0 if reverse else c - 1
    total = b_scr[last:last + 1, :]
    o = _dot(a.astype(BF16), vb) + _dot_nt(qb * jnp.exp2(b).astype(BF16), state.astype(BF16))
    k_dec = kb * jnp.exp2(total - b).astype(BF16)
    new_state = state * jnp.exp2(total) + _dot_tn(vb, k_dec)
    return o, new_state


def _hgrn_kernel(qf_ref, lff_ref, vf_ref, qb_ref, lfb_ref, vb_ref, sm_ref, pm_ref, fm_ref, of_ref, ob_ref,
                 s_ref, b_scr):
    @pl.when(pl.program_id(1) == 0)
    def _():
        s_ref[...] = jnp.zeros_like(s_ref)

    c = HG_CHUNK
    n_small = sum(m < SUBLANES for m in _hgrn_levels())
    chains = [(d, h) for h in range(HG_HEADS) for d in (0, 1)]
    q_refs, lf_refs, v_refs, o_refs = (qf_ref, qb_ref), (lff_ref, lfb_ref), (vf_ref, vb_ref), (of_ref, ob_ref)

    worst = jnp.zeros((1, HG_DIM), F32)
    for d, h in chains:
        b_scr[d, h] = _hgrn_sums(lf_refs[d][0, h], sm_ref[d, n_small * c:, :])
        for g in range(c // HG_FAST_BLOCK):
            lo, up = g * HG_FAST_BLOCK, (g + 1) * HG_FAST_BLOCK
            first, last = (up, lo) if d else (lo - 1, up - 1)
            block = b_scr[d, h, last:last + 1, :]
            if 0 <= first < c:
                block = block - b_scr[d, h, first:first + 1, :]
            worst = jnp.minimum(worst, block)
    mild = jnp.min(worst) >= -HG_FAST_MAX_LOG2

    def run(fast):
        for d, h in chains:
            o, s = _hgrn_chunk(q_refs[d][0, h], lf_refs[d][0, h], v_refs[d][0, h], s_ref[d, h], sm_ref[d],
                               pm_ref.at[d], fm_ref.at[d], b_scr.at[d, h], bool(d), fast)
            o_refs[d][0, h] = o
            s_ref[d, h] = s

    pl.when(mild)(lambda: run(True))
    pl.when(jnp.logical_not(mild))(lambda: run(False))


def _hgrn(hq, lf_f, lf_b, hi):
    b, nh, n, dk = hq.shape
    nc = n // HG_CHUNK
    blk = (1, nh, HG_CHUNK, dk)
    fwd = pl.BlockSpec(blk, lambda i, j: (i, 0, j, 0))
    bwd = pl.BlockSpec(blk, lambda i, j: (i, 0, nc - 1 - j, 0))
    out = jax.ShapeDtypeStruct((b, nh, n, dk), F32)
    tables = _hgrn_tables()
    return pl.pallas_call(
        _hgrn_kernel,
        grid=(b, nc),
        in_specs=[fwd, fwd, fwd, bwd, bwd, bwd] + [_const_spec(t.shape) for t in tables],
        out_specs=[fwd, bwd],
        out_shape=[out, out],
        scratch_shapes=[pltpu.VMEM((2, nh, dk, dk), F32), pltpu.VMEM((2, nh, HG_CHUNK, dk), F32)],
        compiler_params=_params("parallel", "arbitrary"),
        name="hgrn",
    )(hq, lf_f, hi, hq, lf_b, hi, jnp.asarray(tables[0], BF16), jnp.asarray(tables[1]), jnp.asarray(tables[2]))


def _mix_cross_kernel(x_ref, att_ref, of_ref, ob_ref, sg_ref, og_ref, wo_ref, pmg_ref, pxg_ref,
                      wq_ref, km_ref, vm_ref, wxo_ref, poxg_ref, pfg_ref, x2_ref, h3_ref):
    rec = []
    for i in range(HG_HEADS):
        o = _rms(of_ref[0, i] + ob_ref[0, i], og_ref[...])
        rec.append((o * sg_ref[0, :, i * HG_DIM:(i + 1) * HG_DIM]).astype(BF16))
    mix_in = jnp.concatenate([att_ref[0]] + rec, axis=-1)
    x1 = x_ref[0] + _rms(_dot(mix_in, wo_ref[...]), pmg_ref[...])

    h2 = _rms(x1, pxg_ref[...]).astype(BF16)
    d = h2.shape[-1]
    dh = d // X_HEADS
    q = _dot(h2, wq_ref[...]) * (dh ** -0.5)
    heads = []
    for i in range(X_HEADS):
        sl = slice(i * dh, (i + 1) * dh)
        s = _dot_nt(q[:, sl].astype(BF16), km_ref[0, :, sl])
        p = jnp.exp(s - jnp.max(s, axis=-1, keepdims=True))
        l = jnp.sum(p, axis=-1, keepdims=True)
        heads.append((_dot(p.astype(BF16), vm_ref[0, :, sl]) / l).astype(BF16))
    xo = _dot(jnp.concatenate(heads, axis=-1), wxo_ref[...])
    x2 = x1 + _rms(xo, poxg_ref[...])
    x2_ref[0] = x2
    h3_ref[0] = _rms(x2, pfg_ref[...]).astype(BF16)


def _mix_cross(x, att, o_f, o_b, sg, og, w_out, pmg, pxg, w_xq, k_mem, v_mem, w_xo, poxg, pfg, tm):
    b, n, d = x.shape
    nm = k_mem.shape[1]
    tok = lambda i, j: (i, j, 0)
    head = lambda i, j: (i, 0, j, 0)
    batch = lambda i, j: (i, 0, 0)
    vec = _const_spec((1, d))
    mat = _const_spec((d, d))
    return pl.pallas_call(
        _mix_cross_kernel,
        grid=(b, n // tm),
        in_specs=[pl.BlockSpec((1, tm, d), tok), pl.BlockSpec((1, tm, ATT_Q_DIM), tok),
                  pl.BlockSpec((1, HG_HEADS, tm, HG_DIM), head), pl.BlockSpec((1, HG_HEADS, tm, HG_DIM), head),
                  pl.BlockSpec((1, tm, HG_WIDTH), tok), _const_spec((1, HG_DIM)),
                  mat, vec, vec, mat,
                  pl.BlockSpec((1, nm, d), batch), pl.BlockSpec((1, nm, d), batch),
                  mat, vec, vec],
        out_specs=[pl.BlockSpec((1, tm, d), tok), pl.BlockSpec((1, tm, d), tok)],
        out_shape=[jax.ShapeDtypeStruct((b, n, d), F32), jax.ShapeDtypeStruct((b, n, d), BF16)],
        compiler_params=_params("parallel", "parallel"),
        name="mix_cross",
    )(x, att, o_f, o_b, sg, og, w_out, pmg, pxg, w_xq, k_mem, v_mem, w_xo, poxg, pfg)


def _conv_ffn_kernel(x_ref, h_ref, hp_ref, hn_ref, wu_ref, cw_ref, cb_ref, wd_ref, g_ref, o_ref,
                     ug_scr, uv_scr, act_scr):
    j = pl.program_id(1)
    tm = h_ref.shape[1]
    d_ff = wd_ref.shape[0]
    ck = FF_CHUNK
    prev = jnp.where(j > 0, hp_ref[0], jnp.zeros_like(hp_ref[0]))
    nxt = jnp.where(j < pl.num_programs(1) - 1, hn_ref[0], jnp.zeros_like(hn_ref[0]))
    hext = jnp.concatenate([prev, h_ref[0], nxt], axis=0)

    def conv(scr, col):
        out = cb_ref[:, col:col + ck]
        for t in range(CONV_W):
            out = out + scr[pl.ds(HALO - CONV_W // 2 + t, tm), :] * cw_ref[t:t + 1, col:col + ck]
        return out

    for c in range(d_ff // ck):
        slot = c % 2
        ug_scr[slot] = _dot(hext, wu_ref[:, c * ck:(c + 1) * ck])
        uv_scr[slot] = _dot(hext, wu_ref[:, d_ff + c * ck:d_ff + (c + 1) * ck])
        gate = conv(ug_scr.at[slot], c * ck)
        val = conv(uv_scr.at[slot], d_ff + c * ck)
        act_scr[:, c * ck:(c + 1) * ck] = (_silu(gate) * val).astype(BF16)
    o_ref[0] = x_ref[0] + _rms(_dot(act_scr[...], wd_ref[...]), g_ref[...])


def _conv_ffn(x2, h3, w_up, conv_w, conv_b, wd, g, tm):
    b, n, d = x2.shape
    d_ff = wd.shape[0]
    ck = FF_CHUNK
    hb = tm // HALO
    last = n // HALO - 1
    tok = lambda i, j: (i, j, 0)
    full = lambda shape: pl.BlockSpec(shape, lambda *_: (0,) * len(shape), pipeline_mode=pl.Buffered(1))
    return pl.pallas_call(
        _conv_ffn_kernel,
        grid=(b, n // tm),
        in_specs=[pl.BlockSpec((1, tm, d), tok), pl.BlockSpec((1, tm, d), tok),
                  pl.BlockSpec((1, HALO, d), lambda i, j: (i, jnp.maximum(j * hb - 1, 0), 0)),
                  pl.BlockSpec((1, HALO, d), lambda i, j: (i, jnp.minimum((j + 1) * hb, last), 0)),
                  full((d, 2 * d_ff)), full((CONV_W, 2 * d_ff)), full((1, 2 * d_ff)),
                  full((d_ff, d)), _const_spec((1, d))],
        out_specs=pl.BlockSpec((1, tm, d), tok),
        out_shape=jax.ShapeDtypeStruct((b, n, d), F32),
        scratch_shapes=[pltpu.VMEM((2, tm + 2 * HALO, ck), F32), pltpu.VMEM((2, tm + 2 * HALO, ck), F32),
                        pltpu.VMEM((tm, d_ff), BF16)],
        compiler_params=_params("parallel", "parallel"),
        name="conv_ffn",
    )(x2, h3, h3, h3, w_up, conv_w, conv_b, wd, g)


def _rope_tables(n):
    pairs = ATT_HEAD_DIM // 4
    pos = jnp.arange(n)
    inv = jnp.power(ROPE_THETA, -jnp.arange(pairs, dtype=F32) / pairs)
    ang = jnp.concatenate([(pos // GRID_W).astype(F32)[:, None] * inv,
                           (pos % GRID_W).astype(F32)[:, None] * inv], axis=-1)
    cos = jnp.repeat(jnp.cos(ang), 2, axis=-1)
    sin = jnp.repeat(jnp.sin(ang), 2, axis=-1) * jnp.tile(jnp.array([-1.0, 1.0], F32), ATT_HEAD_DIM // 2)
    return jnp.tile(cos, (1, 2)), jnp.tile(sin, (1, 2))


def _segment_ones(width):
    i = jnp.arange(width) // ATT_HEAD_DIM
    return (i[:, None] == i[None, :]).astype(BF16)


def _layer(x, mem, lb, pre_mix_g, w_in, q_norm_g, k_norm_g, hg_out_norm_g, w_out, post_mix_g, pre_x_g,
           mem_norm_g, w_xq, w_xkv, w_xo, post_x_g, pre_ffn_g, w_up, conv_w, conv_b, w_down, post_ffn_g):
    b, n, d = x.shape
    d_ff = w_down.shape[0]
    assert n % HG_CHUNK == 0 and n % GRID_W == 0 and d_ff % FF_CHUNK == 0
    tm = min(256, n)
    tf = min(512, n)
    row = lambda g: g.reshape(1, -1).astype(F32)

    cos, sin = _rope_tables(n)
    qg = jnp.tile(q_norm_g, ATT_HEADS).reshape(1, -1) * (ATT_HEAD_DIM ** -0.5 * LOG2E)
    kg = jnp.tile(k_norm_g, ATT_KV_HEADS).reshape(1, -1)

    k_mem, v_mem = _mem_proj(mem, row(mem_norm_g), w_xkv.astype(BF16))
    q, k, v, hq, lf_f, lf_b, hi, sg = _in_proj(
        x, row(pre_mix_g), w_in.astype(BF16), qg, kg, cos, sin, lb,
        _segment_ones(ATT_Q_DIM), _segment_ones(ATT_KV_DIM), tm)
    att = _attention(q, k, v, tm)
    o_f, o_b = _hgrn(hq, lf_f, lf_b, hi)
    x2, h3 = _mix_cross(x, att, o_f, o_b, sg, row(hg_out_norm_g), w_out.astype(BF16), row(post_mix_g),
                        row(pre_x_g), w_xq.astype(BF16), k_mem, v_mem, w_xo.astype(BF16),
                        row(post_x_g), row(pre_ffn_g), tm)

    return _conv_ffn(x2, h3, w_up.astype(BF16), conv_w, row(conv_b), w_down.astype(BF16), row(post_ffn_g), tf)


def kernel(x, mem, pre_mix_g, w_in, q_norm_g, k_norm_g, hg_lb, hg_out_norm_g, w_out, post_mix_g, pre_x_g,
           mem_norm_g, w_xq, w_xkv, w_xo, post_x_g, pre_ffn_g, w_up, conv_w, conv_b, w_down, post_ffn_g):
    lb_all = jnp.cumsum(jax.nn.softmax(hg_lb.astype(F32), axis=1), axis=1)
    for l in range(w_in.shape[0]):
        x = _layer(x, mem, lb_all[:, l], pre_mix_g[l], w_in[l], q_norm_g[l], k_norm_g[l], hg_out_norm_g[l],
                   w_out[l], post_mix_g[l], pre_x_g[l], mem_norm_g[l], w_xq[l], w_xkv[l], w_xo[l],
                   post_x_g[l], pre_ffn_g[l], w_up[l], conv_w[l], conv_b[l], w_down[l], post_ffn_g[l])
    return x
```

```python
import functools
import math

import numpy as np
import jax
import jax.numpy as jnp
from jax import lax
from jax.experimental import pallas as pl
from jax.experimental.pallas import tpu as pltpu

F32 = jnp.float32
BF16 = jnp.bfloat16

EPS = 1e-6
LOG2E = math.log2(math.e)
GRID_W = 64
ROPE_THETA = 10000.0

ATT_HEADS = 8
ATT_KV_HEADS = 2
ATT_GROUP = ATT_HEADS // ATT_KV_HEADS
ATT_HEAD_DIM = 64
ATT_Q_DIM = ATT_HEADS * ATT_HEAD_DIM
ATT_KV_DIM = ATT_KV_HEADS * ATT_HEAD_DIM
ATT_KEY_SPLITS = 2

HG_HEADS = 4
HG_DIM = 128
HG_WIDTH = HG_HEADS * HG_DIM
HG_CHUNK = 128
HG_FAST_BLOCK = 32
HG_FAST_MAX_LOG2 = 96.0

X_HEADS = 4
CONV_W = 3
FF_CHUNK = 256
HALO = 16

VMEM_LIMIT = 56 * 1024 * 1024
SUBLANES = 8


def _params(*sem):
    return pltpu.CompilerParams(dimension_semantics=sem, vmem_limit_bytes=VMEM_LIMIT)


def _rms(x, g):
    ms = jnp.mean(x * x, axis=-1, keepdims=True)
    return x * lax.rsqrt(ms + EPS) * g


def _silu(x):
    return x / (1.0 + jnp.exp(-x))


def _dot(a, b):
    return jnp.dot(a, b, preferred_element_type=F32)


def _dot_nt(a, b):
    return lax.dot_general(a, b, (((1,), (1,)), ((), ())), preferred_element_type=F32)


def _dot_tn(a, b):
    return lax.dot_general(a, b, (((0,), (0,)), ((), ())), preferred_element_type=F32)


def _const_spec(shape):
    return pl.BlockSpec(shape, lambda *_: (0,) * len(shape))


def _mem_proj_kernel(m_ref, g_ref, w_ref, k_ref, v_ref):
    d = m_ref.shape[-1]
    m = _rms(m_ref[0], g_ref[...]).astype(BF16)
    kv = _dot(m, w_ref[...])
    k_ref[0] = kv[:, :d].astype(BF16)
    v_ref[0] = kv[:, d:].astype(BF16)


def _mem_proj(mem, g, w_xkv):
    b, nm, d = mem.shape
    out = jax.ShapeDtypeStruct((b, nm, d), BF16)
    return pl.pallas_call(
        _mem_proj_kernel,
        grid=(b,),
        in_specs=[pl.BlockSpec((1, nm, d), lambda i: (i, 0, 0)),
                  _const_spec((1, d)), _const_spec((d, 2 * d))],
        out_specs=[pl.BlockSpec((1, nm, d), lambda i: (i, 0, 0))] * 2,
        out_shape=[out, out],
        compiler_params=_params("parallel"),
        name="mem_proj",
    )(mem, g, w_xkv)


def _head_norm_rope(a, seg, gain, cos, sin_signed):
    sq = a * a
    hi = sq.astype(BF16)
    lo = (sq - hi.astype(F32)).astype(BF16)
    ss = _dot(hi, seg) + _dot(lo, seg)
    an = a * lax.rsqrt(ss * (1.0 / ATT_HEAD_DIM) + EPS) * gain
    w = a.shape[-1]
    lane = lax.broadcasted_iota(jnp.int32, a.shape, 1)
    partner = jnp.where(lane % 2 == 0, pltpu.roll(an, w - 1, 1), pltpu.roll(an, 1, 1))
    return an * cos + partner * sin_signed


def _in_proj_kernel(x_ref, g_ref, w_ref, qg_ref, kg_ref, cos_ref, sin_ref, lb_ref, segq_ref, segk_ref,
                    q_ref, k_ref, vt_ref, hq_ref, lff_ref, lfb_ref, hi_ref, hg_ref):
    h = _rms(x_ref[0], g_ref[...]).astype(BF16)

    def proj(lo, width):
        return _dot(h, w_ref[:, lo:lo + width])

    cos = cos_ref[...]
    sin = sin_ref[...]
    c0 = 0
    q = _head_norm_rope(proj(c0, ATT_Q_DIM), segq_ref[...], qg_ref[...],
                        jnp.concatenate([cos] * (ATT_Q_DIM // 128), axis=-1),
                        jnp.concatenate([sin] * (ATT_Q_DIM // 128), axis=-1))
    for i in range(ATT_HEADS):
        q_ref[0, i] = q[:, i * ATT_HEAD_DIM:(i + 1) * ATT_HEAD_DIM].astype(BF16)
    c0 += ATT_Q_DIM
    k = _head_norm_rope(proj(c0, ATT_KV_DIM), segk_ref[...], kg_ref[...], cos, sin)
    c0 += ATT_KV_DIM
    v_t = jnp.transpose(proj(c0, ATT_KV_DIM))
    c0 += ATT_KV_DIM
    for i in range(ATT_KV_HEADS):
        k_ref[0, i] = k[:, i * ATT_HEAD_DIM:(i + 1) * ATT_HEAD_DIM].astype(BF16)
        vt_ref[0, i] = v_t[i * ATT_HEAD_DIM:(i + 1) * ATT_HEAD_DIM, :].astype(BF16)

    def heads_out(ref, val):
        for i in range(HG_HEADS):
            ref[0, i] = val[:, i * HG_DIM:(i + 1) * HG_DIM]

    heads_out(hq_ref, _silu(proj(c0, HG_WIDTH)))
    c0 += HG_WIDTH
    for d, ref in enumerate((lff_ref, lfb_ref)):
        lb = lb_ref[d:d + 1, :]
        z = proj(c0, HG_WIDTH)
        heads_out(ref, jnp.log(lb + (1.0 - lb) / (1.0 + jnp.exp(-z))) * LOG2E)
        c0 += HG_WIDTH
    heads_out(hi_ref, proj(c0, HG_WIDTH))
    c0 += HG_WIDTH
    hg_ref[0] = _silu(proj(c0, HG_WIDTH))


def _in_proj(x, g, w_in, qg, kg, cos, sin, lb, segq, segk, tm):
    b, n, d = x.shape
    nt = n // tm
    n_in = w_in.shape[1]
    tok = lambda i, j: (i, j, 0)
    head = lambda i, j: (i, 0, j, 0)
    pos = lambda i, j: (j, 0)
    return pl.pallas_call(
        _in_proj_kernel,
        grid=(b, nt),
        in_specs=[pl.BlockSpec((1, tm, d), tok), _const_spec((1, d)), _const_spec((d, n_in)),
                  _const_spec((1, ATT_Q_DIM)), _const_spec((1, ATT_KV_DIM)),
                  pl.BlockSpec((tm, 128), pos), pl.BlockSpec((tm, 128), pos),
                  _const_spec((2, HG_WIDTH)),
                  _const_spec((ATT_Q_DIM, ATT_Q_DIM)), _const_spec((ATT_KV_DIM, ATT_KV_DIM))],
        out_specs=[pl.BlockSpec((1, ATT_HEADS, tm, ATT_HEAD_DIM), head),
                   pl.BlockSpec((1, ATT_KV_HEADS, tm, ATT_HEAD_DIM), head),
                   pl.BlockSpec((1, ATT_KV_HEADS, ATT_HEAD_DIM, tm), lambda i, j: (i, 0, 0, j)),
                   pl.BlockSpec((1, HG_HEADS, tm, HG_DIM), head),
                   pl.BlockSpec((1, HG_HEADS, tm, HG_DIM), head),
                   pl.BlockSpec((1, HG_HEADS, tm, HG_DIM), head),
                   pl.BlockSpec((1, HG_HEADS, tm, HG_DIM), head),
                   pl.BlockSpec((1, tm, HG_WIDTH), tok)],
        out_shape=[jax.ShapeDtypeStruct((b, ATT_HEADS, n, ATT_HEAD_DIM), BF16),
                   jax.ShapeDtypeStruct((b, ATT_KV_HEADS, n, ATT_HEAD_DIM), BF16),
                   jax.ShapeDtypeStruct((b, ATT_KV_HEADS, ATT_HEAD_DIM, n), BF16),
                   jax.ShapeDtypeStruct((b, HG_HEADS, n, HG_DIM), F32),
                   jax.ShapeDtypeStruct((b, HG_HEADS, n, HG_DIM), F32),
                   jax.ShapeDtypeStruct((b, HG_HEADS, n, HG_DIM), F32),
                   jax.ShapeDtypeStruct((b, HG_HEADS, n, HG_DIM), F32),
                   jax.ShapeDtypeStruct((b, n, HG_WIDTH), F32)],
        compiler_params=_params("parallel", "parallel"),
        name="in_proj",
    )(x, g, w_in, qg, kg, cos, sin, lb, segq, segk)


def _attention_kernel(q_ref, k_ref, vt_ref, o_ref, s_even, s_odd):
    n = k_ref.shape[2]
    part = n // ATT_KEY_SPLITS
    parts = [slice(j * part, (j + 1) * part) for j in range(ATT_KEY_SPLITS)]

    def scores(i, s_scr):
        for rows in parts:
            s_scr[rows, :] = _dot_nt(k_ref[0, i // ATT_GROUP, rows, :], q_ref[0, i])

    def finish(i, s_scr):
        s = [s_scr[rows, :] for rows in parts]
        m = functools.reduce(jnp.maximum, [jnp.max(x, axis=0, keepdims=True) for x in s])
        p = [jnp.exp2(x - m) for x in s]
        l = sum(jnp.sum(x, axis=0, keepdims=True) for x in p)
        o_t = sum(_dot(vt_ref[0, i // ATT_GROUP, :, rows], x.astype(BF16)) for rows, x in zip(parts, p))
        o_ref[0, i] = (o_t / l).astype(BF16)

    scores(0, s_even)

    def pair(j, carry):
        scores(2 * j + 1, s_odd)
        finish(2 * j, s_even)
        scores(2 * j + 2, s_even)
        finish(2 * j + 1, s_odd)
        return carry

    lax.fori_loop(0, ATT_HEADS // 2 - 1, pair, 0)
    scores(ATT_HEADS - 1, s_odd)
    finish(ATT_HEADS - 2, s_even)
    finish(ATT_HEADS - 1, s_odd)


def _attention(q, k, v_t, tq):
    b, _, n, dh = q.shape
    return pl.pallas_call(
        _attention_kernel,
        grid=(b, n // tq),
        in_specs=[pl.BlockSpec((1, ATT_HEADS, tq, dh), lambda i, j: (i, 0, j, 0)),
                  pl.BlockSpec((1, ATT_KV_HEADS, n, dh), lambda i, j: (i, 0, 0, 0)),
                  pl.BlockSpec((1, ATT_KV_HEADS, dh, n), lambda i, j: (i, 0, 0, 0))],
        out_specs=pl.BlockSpec((1, ATT_HEADS, dh, tq), lambda i, j: (i, 0, 0, j)),
        out_shape=jax.ShapeDtypeStruct((b, ATT_HEADS, dh, n), BF16),
        scratch_shapes=[pltpu.VMEM((n, tq), F32), pltpu.VMEM((n, tq), F32)],
        compiler_params=_params("parallel", "parallel"),
        name="attention",
    )(q, k, v_t)


def _hgrn_levels():
    m, out = 1, []
    while m < HG_CHUNK:
        out.append(m)
        m *= 2
    return out


def _hgrn_tables():
    c = HG_CHUNK
    t = np.arange(c)[:, None]
    u = np.arange(c)[None, :]
    mats, masks, fast = [], [], []
    for reverse in (False, True):
        blocks, pairs = [], [t == u]
        for m in _hgrn_levels():
            base = (t // (2 * m)) * (2 * m)
            if not reverse:
                r = base + m - 1
                block = np.where(t > r, (u > r) & (u <= t), (u > t) & (u <= r))
                roles = ((t // m) % 2 == 1) & ((u // m) % 2 == 0)
            else:
                r = base + m
                block = np.where(t < r, (u >= t) & (u < r), (u >= r) & (u < t))
                roles = ((t // m) % 2 == 0) & ((u // m) % 2 == 1)
            if m < SUBLANES:
                blocks.append(block)
            pairs.append(roles & (t // (2 * m) == u // (2 * m)))
        blocks.append(u >= t if reverse else u <= t)
        mats.append(np.tile(np.concatenate(blocks, axis=0), (1, 2)))
        masks.append(np.stack(pairs))
        fast.append((t // HG_FAST_BLOCK == u // HG_FAST_BLOCK) & (u >= t if reverse else u <= t))
    return tuple(np.stack(x).astype(np.float32) for x in (mats, masks, fast))


def _hgrn_sums(lf2, sum_mat):
    hi = lf2.astype(BF16)
    mid = (lf2 - hi.astype(F32)).astype(BF16)
    return _dot(sum_mat, jnp.concatenate([hi, mid], axis=0))


def _hgrn_chunk(q, lf2, v, state, sum_mat, pair_mask, fast_mask, b_scr, reverse, fast):
    c = HG_CHUNK
    levels = list(enumerate(_hgrn_levels()))
    n_small = sum(m < SUBLANES for _, m in levels)
    qb = q.astype(BF16)
    kb = (1.0 - jnp.exp2(lf2)).astype(BF16)
    if fast:
        b = b_scr[...]
        parts = []
        for g in range(c // HG_FAST_BLOCK):
            lo, up = g * HG_FAST_BLOCK, (g + 1) * HG_FAST_BLOCK
            before = up if reverse else lo - 1
            parts.append(b[lo:up] - b_scr[before:before + 1, :] if 0 <= before < c else b[lo:up])
        loc = jnp.concatenate(parts, axis=0)
        a = _dot_nt(qb * jnp.exp2(loc).astype(BF16), kb * jnp.exp2(-loc).astype(BF16)) * fast_mask[...]
        levels = [(j, m) for j, m in levels if m >= HG_FAST_BLOCK]
    else:
        sums = _hgrn_sums(lf2, sum_mat)
        b = sums[n_small * c:]
        b_scr[...] = b
        a = _dot_nt(qb, kb) * pair_mask[0]
    for j, m in levels:
        if m < SUBLANES:
            e = sums[j * c:(j + 1) * c]
        else:
            parts = []
            for g in range(c // (2 * m)):
                lo, mid_row, up = 2 * m * g, 2 * m * g + m, 2 * m * (g + 1)
                r = mid_row if reverse else mid_row - 1
                b_r = b_scr[r:r + 1, :]
                parts += [b[lo:mid_row] - b_r, b_r - b[mid_row:up]] if reverse else \
                         [b_r - b[lo:mid_row], b[mid_row:up] - b_r]
            e = jnp.concatenate(parts, axis=0)
        fac = jnp.exp2(e).astype(BF16)
        a = a + _dot_nt(qb * fac, kb * fac) * pair_mask[j + 1]
    vb = v.astype(BF16)
    last = 0 if reverse else c - 1
    total = b_scr[last:last + 1, :]
    o = _dot(a.astype(BF16), vb) + _dot_nt(qb * jnp.exp2(b).astype(BF16), state.astype(BF16))
    k_dec = kb * jnp.exp2(total - b).astype(BF16)
    new_state = state * jnp.exp2(total) + _dot_tn(vb, k_dec)
    return o, new_state


def _hgrn_kernel(qf_ref, lff_ref, vf_ref, qb_ref, lfb_ref, vb_ref, sm_ref, pm_ref, fm_ref, of_ref, ob_ref,
                 s_ref, b_scr):
    @pl.when(pl.program_id(1) == 0)
    def _():
        s_ref[...] = jnp.zeros_like(s_ref)

    c = HG_CHUNK
    n_small = sum(m < SUBLANES for m in _hgrn_levels())
    chains = [(d, h) for h in range(HG_HEADS) for d in (0, 1)]
    q_refs, lf_refs, v_refs, o_refs = (qf_ref, qb_ref), (lff_ref, lfb_ref), (vf_ref, vb_ref), (of_ref, ob_ref)

    worst = jnp.zeros((1, HG_DIM), F32)
    for d, h in chains:
        b_scr[d, h] = _hgrn_sums(lf_refs[d][0, h], sm_ref[d, n_small * c:, :])
        for g in range(c // HG_FAST_BLOCK):
            lo, up = g * HG_FAST_BLOCK, (g + 1) * HG_FAST_BLOCK
            first, last = (up, lo) if d else (lo - 1, up - 1)
            block = b_scr[d, h, last:last + 1, :]
            if 0 <= first < c:
                block = block - b_scr[d, h, first:first + 1, :]
            worst = jnp.minimum(worst, block)
    mild = jnp.min(worst) >= -HG_FAST_MAX_LOG2

    def run(fast):
        for d, h in chains:
            o, s = _hgrn_chunk(q_refs[d][0, h], lf_refs[d][0, h], v_refs[d][0, h], s_ref[d, h], sm_ref[d],
                               pm_ref.at[d], fm_ref.at[d], b_scr.at[d, h], bool(d), fast)
            o_refs[d][0, h] = o
            s_ref[d, h] = s

    pl.when(mild)(lambda: run(True))
    pl.when(jnp.logical_not(mild))(lambda: run(False))


def _hgrn(hq, lf_f, lf_b, hi):
    b, nh, n, dk = hq.shape
    nc = n // HG_CHUNK
    blk = (1, nh, HG_CHUNK, dk)
    fwd = pl.BlockSpec(blk, lambda i, j: (i, 0, j, 0))
    bwd = pl.BlockSpec(blk, lambda i, j: (i, 0, nc - 1 - j, 0))
    out = jax.ShapeDtypeStruct((b, nh, n, dk), F32)
    tables = _hgrn_tables()
    return pl.pallas_call(
        _hgrn_kernel,
        grid=(b, nc),
        in_specs=[fwd, fwd, fwd, bwd, bwd, bwd] + [_const_spec(t.shape) for t in tables],
        out_specs=[fwd, bwd],
        out_shape=[out, out],
        scratch_shapes=[pltpu.VMEM((2, nh, dk, dk), F32), pltpu.VMEM((2, nh, HG_CHUNK, dk), F32)],
        compiler_params=_params("parallel", "arbitrary"),
        name="hgrn",
    )(hq, lf_f, hi, hq, lf_b, hi, jnp.asarray(tables[0], BF16), jnp.asarray(tables[1]), jnp.asarray(tables[2]))


def _mix_cross_kernel(x_ref, att_ref, of_ref, ob_ref, sg_ref, og_ref, wo_ref, pmg_ref, pxg_ref,
                      wq_ref, km_ref, vm_ref, wxo_ref, poxg_ref, pfg_ref, x2_ref, h3_ref):
    rec = []
    for i in range(HG_HEADS):
        o = _rms(of_ref[0, i] + ob_ref[0, i], og_ref[...])
        rec.append((o * sg_ref[0, :, i * HG_DIM:(i + 1) * HG_DIM]).astype(BF16))
    att_t = att_ref[0].reshape(ATT_Q_DIM, att_ref.shape[-1])
    mixed = _dot_tn(att_t, wo_ref[:ATT_Q_DIM, :]) + _dot(jnp.concatenate(rec, axis=-1), wo_ref[ATT_Q_DIM:, :])
    x1 = x_ref[0] + _rms(mixed, pmg_ref[...])

    h2 = _rms(x1, pxg_ref[...]).astype(BF16)
    d = h2.shape[-1]
    dh = d // X_HEADS
    q = _dot(h2, wq_ref[...]) * (dh ** -0.5)
    heads = []
    for i in range(X_HEADS):
        sl = slice(i * dh, (i + 1) * dh)
        s = _dot_nt(q[:, sl].astype(BF16), km_ref[0, :, sl])
        p = jnp.exp(s - jnp.max(s, axis=-1, keepdims=True))
        l = jnp.sum(p, axis=-1, keepdims=True)
        heads.append((_dot(p.astype(BF16), vm_ref[0, :, sl]) / l).astype(BF16))
    xo = _dot(jnp.concatenate(heads, axis=-1), wxo_ref[...])
    x2 = x1 + _rms(xo, poxg_ref[...])
    x2_ref[0] = x2
    h3_ref[0] = _rms(x2, pfg_ref[...]).astype(BF16)


def _mix_cross(x, att, o_f, o_b, sg, og, w_out, pmg, pxg, w_xq, k_mem, v_mem, w_xo, poxg, pfg, tm):
    b, n, d = x.shape
    nm = k_mem.shape[1]
    tok = lambda i, j: (i, j, 0)
    head = lambda i, j: (i, 0, j, 0)
    batch = lambda i, j: (i, 0, 0)
    vec = _const_spec((1, d))
    mat = _const_spec((d, d))
    return pl.pallas_call(
        _mix_cross_kernel,
        grid=(b, n // tm),
        in_specs=[pl.BlockSpec((1, tm, d), tok),
                  pl.BlockSpec((1, ATT_HEADS, ATT_HEAD_DIM, tm), lambda i, j: (i, 0, 0, j)),
                  pl.BlockSpec((1, HG_HEADS, tm, HG_DIM), head), pl.BlockSpec((1, HG_HEADS, tm, HG_DIM), head),
                  pl.BlockSpec((1, tm, HG_WIDTH), tok), _const_spec((1, HG_DIM)),
                  mat, vec, vec, mat,
                  pl.BlockSpec((1, nm, d), batch), pl.BlockSpec((1, nm, d), batch),
                  mat, vec, vec],
        out_specs=[pl.BlockSpec((1, tm, d), tok), pl.BlockSpec((1, tm, d), tok)],
        out_shape=[jax.ShapeDtypeStruct((b, n, d), F32), jax.ShapeDtypeStruct((b, n, d), BF16)],
        compiler_params=_params("parallel", "parallel"),
        name="mix_cross",
    )(x, att, o_f, o_b, sg, og, w_out, pmg, pxg, w_xq, k_mem, v_mem, w_xo, poxg, pfg)


def _conv_ffn_kernel(x_ref, h_ref, hp_ref, hn_ref, wu_ref, cw_ref, cb_ref, wd_ref, g_ref, o_ref,
                     ug_scr, uv_scr, act_scr):
    j = pl.program_id(1)
    tm = h_ref.shape[1]
    d_ff = wd_ref.shape[0]
    ck = FF_CHUNK
    prev = jnp.where(j > 0, hp_ref[0], jnp.zeros_like(hp_ref[0]))
    nxt = jnp.where(j < pl.num_programs(1) - 1, hn_ref[0], jnp.zeros_like(hn_ref[0]))
    hext = jnp.concatenate([prev, h_ref[0], nxt], axis=0)

    def conv(scr, col):
        out = cb_ref[:, col:col + ck]
        for t in range(CONV_W):
            out = out + scr[pl.ds(HALO - CONV_W // 2 + t, tm), :] * cw_ref[t:t + 1, col:col + ck]
        return out

    for c in range(d_ff // ck):
        slot = c % 2
        ug_scr[slot] = _dot(hext, wu_ref[:, c * ck:(c + 1) * ck])
        uv_scr[slot] = _dot(hext, wu_ref[:, d_ff + c * ck:d_ff + (c + 1) * ck])
        gate = conv(ug_scr.at[slot], c * ck)
        val = conv(uv_scr.at[slot], d_ff + c * ck)
        act_scr[:, c * ck:(c + 1) * ck] = (_silu(gate) * val).astype(BF16)
    o_ref[0] = x_ref[0] + _rms(_dot(act_scr[...], wd_ref[...]), g_ref[...])


def _conv_ffn(x2, h3, w_up, conv_w, conv_b, wd, g, tm):
    b, n, d = x2.shape
    d_ff = wd.shape[0]
    ck = FF_CHUNK
    hb = tm // HALO
    last = n // HALO - 1
    tok = lambda i, j: (i, j, 0)
    full = lambda shape: pl.BlockSpec(shape, lambda *_: (0,) * len(shape), pipeline_mode=pl.Buffered(1))
    return pl.pallas_call(
        _conv_ffn_kernel,
        grid=(b, n // tm),
        in_specs=[pl.BlockSpec((1, tm, d), tok), pl.BlockSpec((1, tm, d), tok),
                  pl.BlockSpec((1, HALO, d), lambda i, j: (i, jnp.maximum(j * hb - 1, 0), 0)),
                  pl.BlockSpec((1, HALO, d), lambda i, j: (i, jnp.minimum((j + 1) * hb, last), 0)),
                  full((d, 2 * d_ff)), full((CONV_W, 2 * d_ff)), full((1, 2 * d_ff)),
                  full((d_ff, d)), _const_spec((1, d))],
        out_specs=pl.BlockSpec((1, tm, d), tok),
        out_shape=jax.ShapeDtypeStruct((b, n, d), F32),
        scratch_shapes=[pltpu.VMEM((2, tm + 2 * HALO, ck), F32), pltpu.VMEM((2, tm + 2 * HALO, ck), F32),
                        pltpu.VMEM((tm, d_ff), BF16)],
        compiler_params=_params("parallel", "parallel"),
        name="conv_ffn",
    )(x2, h3, h3, h3, w_up, conv_w, conv_b, wd, g)


def _rope_tables(n):
    pairs = ATT_HEAD_DIM // 4
    pos = jnp.arange(n)
    inv = jnp.power(ROPE_THETA, -jnp.arange(pairs, dtype=F32) / pairs)
    ang = jnp.concatenate([(pos // GRID_W).astype(F32)[:, None] * inv,
                           (pos % GRID_W).astype(F32)[:, None] * inv], axis=-1)
    cos = jnp.repeat(jnp.cos(ang), 2, axis=-1)
    sin = jnp.repeat(jnp.sin(ang), 2, axis=-1) * jnp.tile(jnp.array([-1.0, 1.0], F32), ATT_HEAD_DIM // 2)
    return jnp.tile(cos, (1, 2)), jnp.tile(sin, (1, 2))


def _segment_ones(width):
    i = jnp.arange(width) // ATT_HEAD_DIM
    return (i[:, None] == i[None, :]).astype(BF16)


def _layer(x, mem, lb, pre_mix_g, w_in, q_norm_g, k_norm_g, hg_out_norm_g, w_out, post_mix_g, pre_x_g,
           mem_norm_g, w_xq, w_xkv, w_xo, post_x_g, pre_ffn_g, w_up, conv_w, conv_b, w_down, post_ffn_g):
    b, n, d = x.shape
    d_ff = w_down.shape[0]
    assert n % HG_CHUNK == 0 and n % GRID_W == 0 and d_ff % FF_CHUNK == 0
    tm = min(256, n)
    tf = min(512, n)
    row = lambda g: g.reshape(1, -1).astype(F32)

    cos, sin = _rope_tables(n)
    qg = jnp.tile(q_norm_g, ATT_HEADS).reshape(1, -1) * (ATT_HEAD_DIM ** -0.5 * LOG2E)
    kg = jnp.tile(k_norm_g, ATT_KV_HEADS).reshape(1, -1)

    k_mem, v_mem = _mem_proj(mem, row(mem_norm_g), w_xkv.astype(BF16))
    q, k, v_t, hq, lf_f, lf_b, hi, sg = _in_proj(
        x, row(pre_mix_g), w_in.astype(BF16), qg, kg, cos, sin, lb,
        _segment_ones(ATT_Q_DIM), _segment_ones(ATT_KV_DIM), tm)
    att = _attention(q, k, v_t, tm)
    o_f, o_b = _hgrn(hq, lf_f, lf_b, hi)
    x2, h3 = _mix_cross(x, att, o_f, o_b, sg, row(hg_out_norm_g), w_out.astype(BF16), row(post_mix_g),
                        row(pre_x_g), w_xq.astype(BF16), k_mem, v_mem, w_xo.astype(BF16),
                        row(post_x_g), row(pre_ffn_g), tm)

    return _conv_ffn(x2, h3, w_up.astype(BF16), conv_w, row(conv_b), w_down.astype(BF16), row(post_ffn_g), tf)


def kernel(x, mem, pre_mix_g, w_in, q_norm_g, k_norm_g, hg_lb, hg_out_norm_g, w_out, post_mix_g, pre_x_g,
           mem_norm_g, w_xq, w_xkv, w_xo, post_x_g, pre_ffn_g, w_up, conv_w, conv_b, w_down, post_ffn_g):
    lb_all = jnp.cumsum(jax.nn.softmax(hg_lb.astype(F32), axis=1), axis=1)
    for l in range(w_in.shape[0]):
        x = _layer(x, mem, lb_all[:, l], pre_mix_g[l], w_in[l], q_norm_g[l], k_norm_g[l], hg_out_norm_g[l],
                   w_out[l], post_mix_g[l], pre_x_g[l], mem_norm_g[l], w_xq[l], w_xkv[l], w_xo[l],
                   post_x_g[l], pre_ffn_g[l], w_up[l], conv_w[l], conv_b[l], w_down[l], post_ffn_g[l])
    return x
```

```python
import functools
import itertools
import math

import numpy as np
import jax
import jax.numpy as jnp
from jax import lax
from jax.experimental import pallas as pl
from jax.experimental.pallas import tpu as pltpu

F32 = jnp.float32
BF16 = jnp.bfloat16

EPS = 1e-6
LOG2E = math.log2(math.e)
GRID_W = 64
ROPE_THETA = 10000.0

ATT_HEADS = 8
ATT_KV_HEADS = 2
ATT_GROUP = ATT_HEADS // ATT_KV_HEADS
ATT_HEAD_DIM = 64
ATT_Q_DIM = ATT_HEADS * ATT_HEAD_DIM
ATT_KV_DIM = ATT_KV_HEADS * ATT_HEAD_DIM
ATT_KEY_SPLITS = 2

HG_HEADS = 4
HG_DIM = 128
HG_WIDTH = HG_HEADS * HG_DIM
HG_CHUNK = 128
HG_FAST_BLOCK = 32
HG_FAST_MAX_LOG2 = 96.0

X_HEADS = 4
MIX_ROW_PARTS = 2
CONV_W = 3
FF_CHUNK = 256
HALO = 16

VMEM_LIMIT = 56 * 1024 * 1024
SUBLANES = 8


def _params(*sem):
    return pltpu.CompilerParams(dimension_semantics=sem, vmem_limit_bytes=VMEM_LIMIT)


def _rms(x, g):
    ms = jnp.mean(x * x, axis=-1, keepdims=True)
    return x * lax.rsqrt(ms + EPS) * g


def _silu(x):
    return x / (1.0 + jnp.exp(-x))


def _dot(a, b):
    return jnp.dot(a, b, preferred_element_type=F32)


def _dot_nt(a, b):
    return lax.dot_general(a, b, (((1,), (1,)), ((), ())), preferred_element_type=F32)


def _dot_tn(a, b):
    return lax.dot_general(a, b, (((0,), (0,)), ((), ())), preferred_element_type=F32)


def _const_spec(shape):
    return pl.BlockSpec(shape, lambda *_: (0,) * len(shape))


def _mem_proj_kernel(m_ref, g_ref, w_ref, k_ref, v_ref):
    d = m_ref.shape[-1]
    m = _rms(m_ref[0], g_ref[...]).astype(BF16)
    kv = _dot(m, w_ref[...])
    k_ref[0] = kv[:, :d].astype(BF16)
    v_ref[0] = kv[:, d:].astype(BF16)


def _mem_proj(mem, g, w_xkv):
    b, nm, d = mem.shape
    out = jax.ShapeDtypeStruct((b, nm, d), BF16)
    return pl.pallas_call(
        _mem_proj_kernel,
        grid=(b,),
        in_specs=[pl.BlockSpec((1, nm, d), lambda i: (i, 0, 0)),
                  _const_spec((1, d)), _const_spec((d, 2 * d))],
        out_specs=[pl.BlockSpec((1, nm, d), lambda i: (i, 0, 0))] * 2,
        out_shape=[out, out],
        compiler_params=_params("parallel"),
        name="mem_proj",
    )(mem, g, w_xkv)


def _head_norm_rope(a, seg, gain, cos, sin_signed):
    sq = a * a
    hi = sq.astype(BF16)
    lo = (sq - hi.astype(F32)).astype(BF16)
    ss = _dot(hi, seg) + _dot(lo, seg)
    an = a * lax.rsqrt(ss * (1.0 / ATT_HEAD_DIM) + EPS) * gain
    w = a.shape[-1]
    lane = lax.broadcasted_iota(jnp.int32, a.shape, 1)
    partner = jnp.where(lane % 2 == 0, pltpu.roll(an, w - 1, 1), pltpu.roll(an, 1, 1))
    return an * cos + partner * sin_signed


def _in_proj_kernel(x_ref, g_ref, w_ref, qg_ref, kg_ref, cos_ref, sin_ref, lb_ref, segq_ref, segk_ref,
                    q_ref, k_ref, vt_ref, hq_ref, lff_ref, lfb_ref, hi_ref, hg_ref):
    h = _rms(x_ref[0], g_ref[...]).astype(BF16)

    def proj(lo, width):
        return _dot(h, w_ref[:, lo:lo + width])

    cos = cos_ref[...]
    sin = sin_ref[...]
    c0 = 0
    q = _head_norm_rope(proj(c0, ATT_Q_DIM), segq_ref[...], qg_ref[...],
                        jnp.concatenate([cos] * (ATT_Q_DIM // 128), axis=-1),
                        jnp.concatenate([sin] * (ATT_Q_DIM // 128), axis=-1))
    for i in range(ATT_HEADS):
        q_ref[0, i] = q[:, i * ATT_HEAD_DIM:(i + 1) * ATT_HEAD_DIM].astype(BF16)
    c0 += ATT_Q_DIM
    k = _head_norm_rope(proj(c0, ATT_KV_DIM), segk_ref[...], kg_ref[...], cos, sin)
    c0 += ATT_KV_DIM
    v_t = jnp.transpose(proj(c0, ATT_KV_DIM))
    c0 += ATT_KV_DIM
    for i in range(ATT_KV_HEADS):
        k_ref[0, i] = k[:, i * ATT_HEAD_DIM:(i + 1) * ATT_HEAD_DIM].astype(BF16)
        vt_ref[0, i] = v_t[i * ATT_HEAD_DIM:(i + 1) * ATT_HEAD_DIM, :].astype(BF16)

    def heads_out(ref, val):
        for i in range(HG_HEADS):
            ref[0, i] = val[:, i * HG_DIM:(i + 1) * HG_DIM]

    heads_out(hq_ref, _silu(proj(c0, HG_WIDTH)))
    c0 += HG_WIDTH
    for d, ref in enumerate((lff_ref, lfb_ref)):
        lb = lb_ref[d:d + 1, :]
        z = proj(c0, HG_WIDTH)
        heads_out(ref, jnp.log(lb + (1.0 - lb) / (1.0 + jnp.exp(-z))) * LOG2E)
        c0 += HG_WIDTH
    heads_out(hi_ref, proj(c0, HG_WIDTH))
    c0 += HG_WIDTH
    hg_ref[0] = _silu(proj(c0, HG_WIDTH))


def _in_proj(x, g, w_in, qg, kg, cos, sin, lb, segq, segk, tm):
    b, n, d = x.shape
    nt = n // tm
    n_in = w_in.shape[1]
    tok = lambda i, j: (i, j, 0)
    head = lambda i, j: (i, 0, j, 0)
    pos = lambda i, j: (j, 0)
    return pl.pallas_call(
        _in_proj_kernel,
        grid=(b, nt),
        in_specs=[pl.BlockSpec((1, tm, d), tok), _const_spec((1, d)), _const_spec((d, n_in)),
                  _const_spec((1, ATT_Q_DIM)), _const_spec((1, ATT_KV_DIM)),
                  pl.BlockSpec((tm, 128), pos), pl.BlockSpec((tm, 128), pos),
                  _const_spec((2, HG_WIDTH)),
                  _const_spec((ATT_Q_DIM, ATT_Q_DIM)), _const_spec((ATT_KV_DIM, ATT_KV_DIM))],
        out_specs=[pl.BlockSpec((1, ATT_HEADS, tm, ATT_HEAD_DIM), head),
                   pl.BlockSpec((1, ATT_KV_HEADS, tm, ATT_HEAD_DIM), head),
                   pl.BlockSpec((1, ATT_KV_HEADS, ATT_HEAD_DIM, tm), lambda i, j: (i, 0, 0, j)),
                   pl.BlockSpec((1, HG_HEADS, tm, HG_DIM), head),
                   pl.BlockSpec((1, HG_HEADS, tm, HG_DIM), head),
                   pl.BlockSpec((1, HG_HEADS, tm, HG_DIM), head),
                   pl.BlockSpec((1, HG_HEADS, tm, HG_DIM), head),
                   pl.BlockSpec((1, tm, HG_WIDTH), tok)],
        out_shape=[jax.ShapeDtypeStruct((b, ATT_HEADS, n, ATT_HEAD_DIM), BF16),
                   jax.ShapeDtypeStruct((b, ATT_KV_HEADS, n, ATT_HEAD_DIM), BF16),
                   jax.ShapeDtypeStruct((b, ATT_KV_HEADS, ATT_HEAD_DIM, n), BF16),
                   jax.ShapeDtypeStruct((b, HG_HEADS, n, HG_DIM), F32),
                   jax.ShapeDtypeStruct((b, HG_HEADS, n, HG_DIM), F32),
                   jax.ShapeDtypeStruct((b, HG_HEADS, n, HG_DIM), F32),
                   jax.ShapeDtypeStruct((b, HG_HEADS, n, HG_DIM), F32),
                   jax.ShapeDtypeStruct((b, n, HG_WIDTH), F32)],
        compiler_params=_params("parallel", "parallel"),
        name="in_proj",
    )(x, g, w_in, qg, kg, cos, sin, lb, segq, segk)


def _attention_kernel(q_ref, k_ref, vt_ref, o_ref, s_even, s_odd):
    n = k_ref.shape[2]
    part = n // ATT_KEY_SPLITS
    parts = [slice(j * part, (j + 1) * part) for j in range(ATT_KEY_SPLITS)]

    def scores(i, s_scr):
        for rows in parts:
            s_scr[rows, :] = _dot_nt(k_ref[0, i // ATT_GROUP, rows, :], q_ref[0, i])

    def finish(i, s_scr):
        s = [s_scr[rows, :] for rows in parts]
        m = functools.reduce(jnp.maximum, [jnp.max(x, axis=0, keepdims=True) for x in s])
        p = [jnp.exp2(x - m) for x in s]
        l = sum(jnp.sum(x, axis=0, keepdims=True) for x in p)
        o_t = sum(_dot(vt_ref[0, i // ATT_GROUP, :, rows], x.astype(BF16)) for rows, x in zip(parts, p))
        o_ref[0, i] = (o_t / l).astype(BF16)

    scores(0, s_even)

    def pair(j, carry):
        scores(2 * j + 1, s_odd)
        finish(2 * j, s_even)
        scores(2 * j + 2, s_even)
        finish(2 * j + 1, s_odd)
        return carry

    lax.fori_loop(0, ATT_HEADS // 2 - 1, pair, 0)
    scores(ATT_HEADS - 1, s_odd)
    finish(ATT_HEADS - 2, s_even)
    finish(ATT_HEADS - 1, s_odd)


def _attention(q, k, v_t, tq):
    b, _, n, dh = q.shape
    return pl.pallas_call(
        _attention_kernel,
        grid=(b, n // tq),
        in_specs=[pl.BlockSpec((1, ATT_HEADS, tq, dh), lambda i, j: (i, 0, j, 0)),
                  pl.BlockSpec((1, ATT_KV_HEADS, n, dh), lambda i, j: (i, 0, 0, 0)),
                  pl.BlockSpec((1, ATT_KV_HEADS, dh, n), lambda i, j: (i, 0, 0, 0))],
        out_specs=pl.BlockSpec((1, ATT_HEADS, dh, tq), lambda i, j: (i, 0, 0, j)),
        out_shape=jax.ShapeDtypeStruct((b, ATT_HEADS, dh, n), BF16),
        scratch_shapes=[pltpu.VMEM((n, tq), F32), pltpu.VMEM((n, tq), F32)],
        compiler_params=_params("parallel", "parallel"),
        name="attention",
    )(q, k, v_t)


def _hgrn_levels():
    m, out = 1, []
    while m < HG_CHUNK:
        out.append(m)
        m *= 2
    return out


def _hgrn_tables():
    c = HG_CHUNK
    t = np.arange(c)[:, None]
    u = np.arange(c)[None, :]
    mats, masks, fast = [], [], []
    for reverse in (False, True):
        blocks, pairs = [], [t == u]
        for m in _hgrn_levels():
            base = (t // (2 * m)) * (2 * m)
            if not reverse:
                r = base + m - 1
                block = np.where(t > r, (u > r) & (u <= t), (u > t) & (u <= r))
                roles = ((t // m) % 2 == 1) & ((u // m) % 2 == 0)
            else:
                r = base + m
                block = np.where(t < r, (u >= t) & (u < r), (u >= r) & (u < t))
                roles = ((t // m) % 2 == 0) & ((u // m) % 2 == 1)
            if m < SUBLANES:
                blocks.append(block)
            pairs.append(roles & (t // (2 * m) == u // (2 * m)))
        blocks.append(u >= t if reverse else u <= t)
        mats.append(np.tile(np.concatenate(blocks, axis=0), (1, 2)))
        masks.append(np.stack(pairs))
        fast.append((t // HG_FAST_BLOCK == u // HG_FAST_BLOCK) & (u >= t if reverse else u <= t))
    return tuple(np.stack(x).astype(np.float32) for x in (mats, masks, fast))


def _hgrn_sums(lf2, sum_mat):
    hi = lf2.astype(BF16)
    mid = (lf2 - hi.astype(F32)).astype(BF16)
    return _dot(sum_mat, jnp.concatenate([hi, mid], axis=0))


def _hgrn_chunk(q_ref, lf_ref, v_ref, o_ref, state_ref, sum_mat, pair_mask, fast_mask, b_scr, reverse, fast):
    c = HG_CHUNK
    levels = list(enumerate(_hgrn_levels()))
    n_small = sum(m < SUBLANES for _, m in levels)
    q, lf2, v, state = q_ref[...], lf_ref[...], v_ref[...], state_ref[...]
    qb = q.astype(BF16)
    kb = (1.0 - jnp.exp2(lf2)).astype(BF16)
    if fast:
        b = _hgrn_sums(lf2, sum_mat[n_small * c:, :])
        yield
        b_scr[...] = b
        parts = []
        for g in range(c // HG_FAST_BLOCK):
            lo, up = g * HG_FAST_BLOCK, (g + 1) * HG_FAST_BLOCK
            before = up if reverse else lo - 1
            parts.append(b[lo:up] - b_scr[before:before + 1, :] if 0 <= before < c else b[lo:up])
        loc = jnp.concatenate(parts, axis=0)
        a = _dot_nt(qb * jnp.exp2(loc).astype(BF16), kb * jnp.exp2(-loc).astype(BF16)) * fast_mask[...]
        yield
        levels = [(j, m) for j, m in levels if m >= HG_FAST_BLOCK]
    else:
        sums = _hgrn_sums(lf2, sum_mat[...])
        yield
        b = sums[n_small * c:]
        b_scr[...] = b
        a = _dot_nt(qb, kb) * pair_mask[0]
        yield
    for j, m in levels:
        if m < SUBLANES:
            e = sums[j * c:(j + 1) * c]
        else:
            parts = []
            for g in range(c // (2 * m)):
                lo, mid_row, up = 2 * m * g, 2 * m * g + m, 2 * m * (g + 1)
                r = mid_row if reverse else mid_row - 1
                b_r = b_scr[r:r + 1, :]
                parts += [b[lo:mid_row] - b_r, b_r - b[mid_row:up]] if reverse else \
                         [b_r - b[lo:mid_row], b[mid_row:up] - b_r]
            e = jnp.concatenate(parts, axis=0)
        fac = jnp.exp2(e).astype(BF16)
        a = a + _dot_nt(qb * fac, kb * fac) * pair_mask[j + 1]
        yield
    vb = v.astype(BF16)
    last = 0 if reverse else c - 1
    total = b_scr[last:last + 1, :]
    o_ref[...] = _dot(a.astype(BF16), vb) + _dot_nt(qb * jnp.exp2(b).astype(BF16), state.astype(BF16))
    yield
    k_dec = kb * jnp.exp2(total - b).astype(BF16)
    state_ref[...] = state * jnp.exp2(total) + _dot_tn(vb, k_dec)


def _hgrn_kernel(qf_ref, lff_ref, vf_ref, qb_ref, lfb_ref, vb_ref, sm_ref, pm_ref, fm_ref, of_ref, ob_ref,
                 s_ref, b_scr):
    @pl.when(pl.program_id(1) == 0)
    def _():
        s_ref[...] = jnp.zeros_like(s_ref)

    c = HG_CHUNK
    chains = [(d, h) for h in range(HG_HEADS) for d in (0, 1)]
    q_refs, lf_refs, v_refs, o_refs = (qf_ref, qb_ref), (lff_ref, lfb_ref), (vf_ref, vb_ref), (of_ref, ob_ref)

    worst = jnp.zeros((1, HG_DIM), F32)
    for d, h in chains:
        for g in range(c // HG_FAST_BLOCK):
            block = lf_refs[d][0, h, g * HG_FAST_BLOCK:(g + 1) * HG_FAST_BLOCK, :]
            worst = jnp.minimum(worst, jnp.sum(block, axis=0, keepdims=True))
    mild = jnp.min(worst) >= -HG_FAST_MAX_LOG2

    def run(fast):
        work = [_hgrn_chunk(q_refs[d].at[0, h], lf_refs[d].at[0, h], v_refs[d].at[0, h], o_refs[d].at[0, h],
                            s_ref.at[d, h], sm_ref.at[d], pm_ref.at[d], fm_ref.at[d], b_scr.at[d, h],
                            bool(d), fast) for d, h in chains]
        for _ in itertools.zip_longest(*work):
            pass

    pl.when(mild)(lambda: run(True))
    pl.when(jnp.logical_not(mild))(lambda: run(False))


def _hgrn(hq, lf_f, lf_b, hi):
    b, nh, n, dk = hq.shape
    nc = n // HG_CHUNK
    blk = (1, nh, HG_CHUNK, dk)
    fwd = pl.BlockSpec(blk, lambda i, j: (i, 0, j, 0))
    bwd = pl.BlockSpec(blk, lambda i, j: (i, 0, nc - 1 - j, 0))
    out = jax.ShapeDtypeStruct((b, nh, n, dk), F32)
    tables = _hgrn_tables()
    return pl.pallas_call(
        _hgrn_kernel,
        grid=(b, nc),
        in_specs=[fwd, fwd, fwd, bwd, bwd, bwd] + [_const_spec(t.shape) for t in tables],
        out_specs=[fwd, bwd],
        out_shape=[out, out],
        scratch_shapes=[pltpu.VMEM((2, nh, dk, dk), F32), pltpu.VMEM((2, nh, HG_CHUNK, dk), F32)],
        compiler_params=_params("parallel", "arbitrary"),
        name="hgrn",
    )(hq, lf_f, hi, hq, lf_b, hi, jnp.asarray(tables[0], BF16), jnp.asarray(tables[1]), jnp.asarray(tables[2]))


def _mix_cross_kernel(x_ref, att_ref, of_ref, ob_ref, sg_ref, og_ref, wo_ref, pmg_ref, pxg_ref,
                      wq_ref, km_ref, vm_ref, wxo_ref, poxg_ref, pfg_ref, x2_ref, h3_ref):
    tm = x_ref.shape[1]
    part = tm // MIX_ROW_PARTS

    def rows_of(lo):
        rows = slice(lo, lo + part)
        rec = []
        for i in range(HG_HEADS):
            o = _rms(of_ref[0, i, rows, :] + ob_ref[0, i, rows, :], og_ref[...])
            rec.append((o * sg_ref[0, rows, i * HG_DIM:(i + 1) * HG_DIM]).astype(BF16))
        att_t = att_ref[0, :, :, rows].reshape(ATT_Q_DIM, part)
        mixed = _dot_tn(att_t, wo_ref[:ATT_Q_DIM, :]) + _dot(jnp.concatenate(rec, axis=-1), wo_ref[ATT_Q_DIM:, :])
        yield
        x1 = x_ref[0, rows, :] + _rms(mixed, pmg_ref[...])
        h2 = _rms(x1, pxg_ref[...]).astype(BF16)
        d = h2.shape[-1]
        dh = d // X_HEADS
        q = _dot(h2, wq_ref[...]) * (dh ** -0.5)
        yield
        heads = []
        for i in range(X_HEADS):
            sl = slice(i * dh, (i + 1) * dh)
            s = _dot_nt(q[:, sl].astype(BF16), km_ref[0, :, sl])
            p = jnp.exp(s - jnp.max(s, axis=-1, keepdims=True))
            l = jnp.sum(p, axis=-1, keepdims=True)
            heads.append((_dot(p.astype(BF16), vm_ref[0, :, sl]) / l).astype(BF16))
        yield
        xo = _dot(jnp.concatenate(heads, axis=-1), wxo_ref[...])
        yield
        x2 = x1 + _rms(xo, poxg_ref[...])
        x2_ref[0, rows, :] = x2
        h3_ref[0, rows, :] = _rms(x2, pfg_ref[...]).astype(BF16)

    for _ in itertools.zip_longest(*[rows_of(j * part) for j in range(MIX_ROW_PARTS)]):
        pass


def _mix_cross(x, att, o_f, o_b, sg, og, w_out, pmg, pxg, w_xq, k_mem, v_mem, w_xo, poxg, pfg, tm):
    b, n, d = x.shape
    nm = k_mem.shape[1]
    tok = lambda i, j: (i, j, 0)
    head = lambda i, j: (i, 0, j, 0)
    batch = lambda i, j: (i, 0, 0)
    vec = _const_spec((1, d))
    mat = _const_spec((d, d))
    return pl.pallas_call(
        _mix_cross_kernel,
        grid=(b, n // tm),
        in_specs=[pl.BlockSpec((1, tm, d), tok),
                  pl.BlockSpec((1, ATT_HEADS, ATT_HEAD_DIM, tm), lambda i, j: (i, 0, 0, j)),
                  pl.BlockSpec((1, HG_HEADS, tm, HG_DIM), head), pl.BlockSpec((1, HG_HEADS, tm, HG_DIM), head),
                  pl.BlockSpec((1, tm, HG_WIDTH), tok), _const_spec((1, HG_DIM)),
                  mat, vec, vec, mat,
                  pl.BlockSpec((1, nm, d), batch), pl.BlockSpec((1, nm, d), batch),
                  mat, vec, vec],
        out_specs=[pl.BlockSpec((1, tm, d), tok), pl.BlockSpec((1, tm, d), tok)],
        out_shape=[jax.ShapeDtypeStruct((b, n, d), F32), jax.ShapeDtypeStruct((b, n, d), BF16)],
        compiler_params=_params("parallel", "parallel"),
        name="mix_cross",
    )(x, att, o_f, o_b, sg, og, w_out, pmg, pxg, w_xq, k_mem, v_mem, w_xo, poxg, pfg)


def _conv_ffn_kernel(x_ref, h_ref, hp_ref, hn_ref, wu_ref, cw_ref, cb_ref, wd_ref, g_ref, o_ref,
                     ug_scr, uv_scr, act_scr):
    j = pl.program_id(1)
    tm = h_ref.shape[1]
    d_ff = wd_ref.shape[0]
    ck = FF_CHUNK
    prev = jnp.where(j > 0, hp_ref[0], jnp.zeros_like(hp_ref[0]))
    nxt = jnp.where(j < pl.num_programs(1) - 1, hn_ref[0], jnp.zeros_like(hn_ref[0]))
    hext = jnp.concatenate([prev, h_ref[0], nxt], axis=0)

    def conv(scr, col):
        out = cb_ref[:, col:col + ck]
        for t in range(CONV_W):
            out = out + scr[pl.ds(HALO - CONV_W // 2 + t, tm), :] * cw_ref[t:t + 1, col:col + ck]
        return out

    for c in range(d_ff // ck):
        slot = c % 2
        ug_scr[slot] = _dot(hext, wu_ref[:, c * ck:(c + 1) * ck])
        uv_scr[slot] = _dot(hext, wu_ref[:, d_ff + c * ck:d_ff + (c + 1) * ck])
        gate = conv(ug_scr.at[slot], c * ck)
        val = conv(uv_scr.at[slot], d_ff + c * ck)
        act_scr[:, c * ck:(c + 1) * ck] = (_silu(gate) * val).astype(BF16)
    o_ref[0] = x_ref[0] + _rms(_dot(act_scr[...], wd_ref[...]), g_ref[...])


def _conv_ffn(x2, h3, w_up, conv_w, conv_b, wd, g, tm):
    b, n, d = x2.shape
    d_ff = wd.shape[0]
    ck = FF_CHUNK
    hb = tm // HALO
    last = n // HALO - 1
    tok = lambda i, j: (i, j, 0)
    full = lambda shape: pl.BlockSpec(shape, lambda *_: (0,) * len(shape), pipeline_mode=pl.Buffered(1))
    return pl.pallas_call(
        _conv_ffn_kernel,
        grid=(b, n // tm),
        in_specs=[pl.BlockSpec((1, tm, d), tok), pl.BlockSpec((1, tm, d), tok),
                  pl.BlockSpec((1, HALO, d), lambda i, j: (i, jnp.maximum(j * hb - 1, 0), 0)),
                  pl.BlockSpec((1, HALO, d), lambda i, j: (i, jnp.minimum((j + 1) * hb, last), 0)),
                  full((d, 2 * d_ff)), full((CONV_W, 2 * d_ff)), full((1, 2 * d_ff)),
                  full((d_ff, d)), _const_spec((1, d))],
        out_specs=pl.BlockSpec((1, tm, d), tok),
        out_shape=jax.ShapeDtypeStruct((b, n, d), F32),
        scratch_shapes=[pltpu.VMEM((2, tm + 2 * HALO, ck), F32), pltpu.VMEM((2, tm + 2 * HALO, ck), F32),
                        pltpu.VMEM((tm, d_ff), BF16)],
        compiler_params=_params("parallel", "parallel"),
        name="conv_ffn",
    )(x2, h3, h3, h3, w_up, conv_w, conv_b, wd, g)


def _rope_tables(n):
    pairs = ATT_HEAD_DIM // 4
    pos = jnp.arange(n)
    inv = jnp.power(ROPE_THETA, -jnp.arange(pairs, dtype=F32) / pairs)
    ang = jnp.concatenate([(pos // GRID_W).astype(F32)[:, None] * inv,
                           (pos % GRID_W).astype(F32)[:, None] * inv], axis=-1)
    cos = jnp.repeat(jnp.cos(ang), 2, axis=-1)
    sin = jnp.repeat(jnp.sin(ang), 2, axis=-1) * jnp.tile(jnp.array([-1.0, 1.0], F32), ATT_HEAD_DIM // 2)
    return jnp.tile(cos, (1, 2)), jnp.tile(sin, (1, 2))


def _segment_ones(width):
    i = jnp.arange(width) // ATT_HEAD_DIM
    return (i[:, None] == i[None, :]).astype(BF16)


def _layer(x, mem, lb, pre_mix_g, w_in, q_norm_g, k_norm_g, hg_out_norm_g, w_out, post_mix_g, pre_x_g,
           mem_norm_g, w_xq, w_xkv, w_xo, post_x_g, pre_ffn_g, w_up, conv_w, conv_b, w_down, post_ffn_g):
    b, n, d = x.shape
    d_ff = w_down.shape[0]
    assert n % HG_CHUNK == 0 and n % GRID_W == 0 and d_ff % FF_CHUNK == 0
    tm = min(256, n)
    tf = min(512, n)
    row = lambda g: g.reshape(1, -1).astype(F32)

    cos, sin = _rope_tables(n)
    qg = jnp.tile(q_norm_g, ATT_HEADS).reshape(1, -1) * (ATT_HEAD_DIM ** -0.5 * LOG2E)
    kg = jnp.tile(k_norm_g, ATT_KV_HEADS).reshape(1, -1)

    k_mem, v_mem = _mem_proj(mem, row(mem_norm_g), w_xkv.astype(BF16))
    q, k, v_t, hq, lf_f, lf_b, hi, sg = _in_proj(
        x, row(pre_mix_g), w_in.astype(BF16), qg, kg, cos, sin, lb,
        _segment_ones(ATT_Q_DIM), _segment_ones(ATT_KV_DIM), tf)
    att = _attention(q, k, v_t, tm)
    o_f, o_b = _hgrn(hq, lf_f, lf_b, hi)
    x2, h3 = _mix_cross(x, att, o_f, o_b, sg, row(hg_out_norm_g), w_out.astype(BF16), row(post_mix_g),
                        row(pre_x_g), w_xq.astype(BF16), k_mem, v_mem, w_xo.astype(BF16),
                        row(post_x_g), row(pre_ffn_g), tf)

    return _conv_ffn(x2, h3, w_up.astype(BF16), conv_w, row(conv_b), w_down.astype(BF16), row(post_ffn_g), tf)


def kernel(x, mem, pre_mix_g, w_in, q_norm_g, k_norm_g, hg_lb, hg_out_norm_g, w_out, post_mix_g, pre_x_g,
           mem_norm_g, w_xq, w_xkv, w_xo, post_x_g, pre_ffn_g, w_up, conv_w, conv_b, w_down, post_ffn_g):
    lb_all = jnp.cumsum(jax.nn.softmax(hg_lb.astype(F32), axis=1), axis=1)
    for l in range(w_in.shape[0]):
        x = _layer(x, mem, lb_all[:, l], pre_mix_g[l], w_in[l], q_norm_g[l], k_norm_g[l], hg_out_norm_g[l],
                   w_out[l], post_mix_g[l], pre_x_g[l], mem_norm_g[l], w_xq[l], w_xkv[l], w_xo[l],
                   post_x_g[l], pre_ffn_g[l], w_up[l], conv_w[l], conv_b[l], w_down[l], post_ffn_g[l])
    return x
```

```python
import functools
import itertools
import math

import numpy as np
import jax
import jax.numpy as jnp
from jax import lax
from jax.experimental import pallas as pl
from jax.experimental.pallas import tpu as pltpu

F32 = jnp.float32
BF16 = jnp.bfloat16

EPS = 1e-6
LOG2E = math.log2(math.e)
GRID_W = 64
ROPE_THETA = 10000.0

ATT_HEADS = 8
ATT_KV_HEADS = 2
ATT_GROUP = ATT_HEADS // ATT_KV_HEADS
ATT_HEAD_DIM = 64
ATT_Q_DIM = ATT_HEADS * ATT_HEAD_DIM
ATT_KV_DIM = ATT_KV_HEADS * ATT_HEAD_DIM
ATT_KEY_SPLITS = 2

HG_HEADS = 4
HG_DIM = 128
HG_WIDTH = HG_HEADS * HG_DIM
HG_CHUNK = 128
HG_FAST_BLOCK = 32
HG_FAST_MAX_LOG2 = 96.0

X_HEADS = 4
IN_ROW_PARTS = 2
MIX_ROW_PARTS = 2
CONV_W = 3
FF_CHUNK = 256
HALO = 16

VMEM_LIMIT = 56 * 1024 * 1024
SUBLANES = 8


def _params(*sem):
    return pltpu.CompilerParams(dimension_semantics=sem, vmem_limit_bytes=VMEM_LIMIT)


def _rms(x, g):
    ms = jnp.mean(x * x, axis=-1, keepdims=True)
    return x * lax.rsqrt(ms + EPS) * g


def _silu(x):
    return x / (1.0 + jnp.exp(-x))


def _dot(a, b):
    return jnp.dot(a, b, preferred_element_type=F32)


def _dot_nt(a, b):
    return lax.dot_general(a, b, (((1,), (1,)), ((), ())), preferred_element_type=F32)


def _dot_tn(a, b):
    return lax.dot_general(a, b, (((0,), (0,)), ((), ())), preferred_element_type=F32)


def _const_spec(shape):
    return pl.BlockSpec(shape, lambda *_: (0,) * len(shape))


def _mem_proj_kernel(m_ref, g_ref, w_ref, k_ref, v_ref):
    d = m_ref.shape[-1]
    m = _rms(m_ref[0], g_ref[...]).astype(BF16)
    kv = _dot(m, w_ref[...])
    k_ref[0] = kv[:, :d].astype(BF16)
    v_ref[0] = kv[:, d:].astype(BF16)


def _mem_proj(mem, g, w_xkv):
    b, nm, d = mem.shape
    out = jax.ShapeDtypeStruct((b, nm, d), BF16)
    return pl.pallas_call(
        _mem_proj_kernel,
        grid=(b,),
        in_specs=[pl.BlockSpec((1, nm, d), lambda i: (i, 0, 0)),
                  _const_spec((1, d)), _const_spec((d, 2 * d))],
        out_specs=[pl.BlockSpec((1, nm, d), lambda i: (i, 0, 0))] * 2,
        out_shape=[out, out],
        compiler_params=_params("parallel"),
        name="mem_proj",
    )(mem, g, w_xkv)


def _head_norm_rope(a, seg, gain, cos, sin_signed):
    sq = a * a
    hi = sq.astype(BF16)
    lo = (sq - hi.astype(F32)).astype(BF16)
    ss = _dot(hi, seg) + _dot(lo, seg)
    an = a * lax.rsqrt(ss * (1.0 / ATT_HEAD_DIM) + EPS) * gain
    w = a.shape[-1]
    lane = lax.broadcasted_iota(jnp.int32, a.shape, 1)
    partner = jnp.where(lane % 2 == 0, pltpu.roll(an, w - 1, 1), pltpu.roll(an, 1, 1))
    return an * cos + partner * sin_signed


def _in_proj_kernel(x_ref, g_ref, w_ref, qg_ref, kg_ref, cos_ref, sin_ref, lb_ref, segq_ref, segk_ref,
                    q_ref, k_ref, vt_ref, hq_ref, lff_ref, lfb_ref, hi_ref, hg_ref):
    part = x_ref.shape[1] // IN_ROW_PARTS

    def rows_of(r0):
        rows = slice(r0, r0 + part)
        h = _rms(x_ref[0, rows, :], g_ref[...]).astype(BF16)

        def proj(lo, width):
            return _dot(h, w_ref[:, lo:lo + width])

        def heads_out(ref, val):
            for i in range(HG_HEADS):
                ref[0, i, rows, :] = val[:, i * HG_DIM:(i + 1) * HG_DIM]

        cos = cos_ref[rows, :]
        sin = sin_ref[rows, :]
        c0 = 0
        q = _head_norm_rope(proj(c0, ATT_Q_DIM), segq_ref[...], qg_ref[...],
                            jnp.concatenate([cos] * (ATT_Q_DIM // 128), axis=-1),
                            jnp.concatenate([sin] * (ATT_Q_DIM // 128), axis=-1))
        for i in range(ATT_HEADS):
            q_ref[0, i, rows, :] = q[:, i * ATT_HEAD_DIM:(i + 1) * ATT_HEAD_DIM].astype(BF16)
        c0 += ATT_Q_DIM
        yield
        k = _head_norm_rope(proj(c0, ATT_KV_DIM), segk_ref[...], kg_ref[...], cos, sin)
        c0 += ATT_KV_DIM
        v_t = jnp.transpose(proj(c0, ATT_KV_DIM))
        c0 += ATT_KV_DIM
        for i in range(ATT_KV_HEADS):
            k_ref[0, i, rows, :] = k[:, i * ATT_HEAD_DIM:(i + 1) * ATT_HEAD_DIM].astype(BF16)
            vt_ref[0, i, :, rows] = v_t[i * ATT_HEAD_DIM:(i + 1) * ATT_HEAD_DIM, :].astype(BF16)
        yield
        heads_out(hq_ref, _silu(proj(c0, HG_WIDTH)))
        c0 += HG_WIDTH
        yield
        for d, ref in enumerate((lff_ref, lfb_ref)):
            lb = lb_ref[d:d + 1, :]
            z = proj(c0, HG_WIDTH)
            heads_out(ref, jnp.log(lb + (1.0 - lb) / (1.0 + jnp.exp(-z))) * LOG2E)
            c0 += HG_WIDTH
            yield
        heads_out(hi_ref, proj(c0, HG_WIDTH))
        c0 += HG_WIDTH
        yield
        hg_ref[0, rows, :] = _silu(proj(c0, HG_WIDTH))

    for _ in itertools.zip_longest(*[rows_of(j * part) for j in range(IN_ROW_PARTS)]):
        pass


def _in_proj(x, g, w_in, qg, kg, cos, sin, lb, segq, segk, tm):
    b, n, d = x.shape
    nt = n // tm
    n_in = w_in.shape[1]
    tok = lambda i, j: (i, j, 0)
    head = lambda i, j: (i, 0, j, 0)
    pos = lambda i, j: (j, 0)
    return pl.pallas_call(
        _in_proj_kernel,
        grid=(b, nt),
        in_specs=[pl.BlockSpec((1, tm, d), tok), _const_spec((1, d)), _const_spec((d, n_in)),
                  _const_spec((1, ATT_Q_DIM)), _const_spec((1, ATT_KV_DIM)),
                  pl.BlockSpec((tm, 128), pos), pl.BlockSpec((tm, 128), pos),
                  _const_spec((2, HG_WIDTH)),
                  _const_spec((ATT_Q_DIM, ATT_Q_DIM)), _const_spec((ATT_KV_DIM, ATT_KV_DIM))],
        out_specs=[pl.BlockSpec((1, ATT_HEADS, tm, ATT_HEAD_DIM), head),
                   pl.BlockSpec((1, ATT_KV_HEADS, tm, ATT_HEAD_DIM), head),
                   pl.BlockSpec((1, ATT_KV_HEADS, ATT_HEAD_DIM, tm), lambda i, j: (i, 0, 0, j)),
                   pl.BlockSpec((1, HG_HEADS, tm, HG_DIM), head),
                   pl.BlockSpec((1, HG_HEADS, tm, HG_DIM), head),
                   pl.BlockSpec((1, HG_HEADS, tm, HG_DIM), head),
                   pl.BlockSpec((1, HG_HEADS, tm, HG_DIM), head),
                   pl.BlockSpec((1, tm, HG_WIDTH), tok)],
        out_shape=[jax.ShapeDtypeStruct((b, ATT_HEADS, n, ATT_HEAD_DIM), BF16),
                   jax.ShapeDtypeStruct((b, ATT_KV_HEADS, n, ATT_HEAD_DIM), BF16),
                   jax.ShapeDtypeStruct((b, ATT_KV_HEADS, ATT_HEAD_DIM, n), BF16),
                   jax.ShapeDtypeStruct((b, HG_HEADS, n, HG_DIM), F32),
                   jax.ShapeDtypeStruct((b, HG_HEADS, n, HG_DIM), F32),
                   jax.ShapeDtypeStruct((b, HG_HEADS, n, HG_DIM), F32),
                   jax.ShapeDtypeStruct((b, HG_HEADS, n, HG_DIM), F32),
                   jax.ShapeDtypeStruct((b, n, HG_WIDTH), F32)],
        compiler_params=_params("parallel", "parallel"),
        name="in_proj",
    )(x, g, w_in, qg, kg, cos, sin, lb, segq, segk)


def _attention_kernel(q_ref, k_ref, vt_ref, o_ref, s_even, s_odd):
    n = k_ref.shape[2]
    part = n // ATT_KEY_SPLITS
    parts = [slice(j * part, (j + 1) * part) for j in range(ATT_KEY_SPLITS)]

    def scores(i, s_scr):
        for rows in parts:
            s_scr[rows, :] = _dot_nt(k_ref[0, i // ATT_GROUP, rows, :], q_ref[0, i])

    def finish(i, s_scr):
        s = [s_scr[rows, :] for rows in parts]
        m = functools.reduce(jnp.maximum, [jnp.max(x, axis=0, keepdims=True) for x in s])
        p = [jnp.exp2(x - m) for x in s]
        l = sum(jnp.sum(x, axis=0, keepdims=True) for x in p)
        o_t = sum(_dot(vt_ref[0, i // ATT_GROUP, :, rows], x.astype(BF16)) for rows, x in zip(parts, p))
        o_ref[0, i] = (o_t / l).astype(BF16)

    scores(0, s_even)

    def pair(j, carry):
        scores(2 * j + 1, s_odd)
        finish(2 * j, s_even)
        scores(2 * j + 2, s_even)
        finish(2 * j + 1, s_odd)
        return carry

    lax.fori_loop(0, ATT_HEADS // 2 - 1, pair, 0)
    scores(ATT_HEADS - 1, s_odd)
    finish(ATT_HEADS - 2, s_even)
    finish(ATT_HEADS - 1, s_odd)


def _attention(q, k, v_t, tq):
    b, _, n, dh = q.shape
    return pl.pallas_call(
        _attention_kernel,
        grid=(b, n // tq),
        in_specs=[pl.BlockSpec((1, ATT_HEADS, tq, dh), lambda i, j: (i, 0, j, 0)),
                  pl.BlockSpec((1, ATT_KV_HEADS, n, dh), lambda i, j: (i, 0, 0, 0)),
                  pl.BlockSpec((1, ATT_KV_HEADS, dh, n), lambda i, j: (i, 0, 0, 0))],
        out_specs=pl.BlockSpec((1, ATT_HEADS, dh, tq), lambda i, j: (i, 0, 0, j)),
        out_shape=jax.ShapeDtypeStruct((b, ATT_HEADS, dh, n), BF16),
        scratch_shapes=[pltpu.VMEM((n, tq), F32), pltpu.VMEM((n, tq), F32)],
        compiler_params=_params("parallel", "parallel"),
        name="attention",
    )(q, k, v_t)


def _hgrn_levels():
    m, out = 1, []
    while m < HG_CHUNK:
        out.append(m)
        m *= 2
    return out


def _hgrn_tables():
    c = HG_CHUNK
    t = np.arange(c)[:, None]
    u = np.arange(c)[None, :]
    mats, masks, fast = [], [], []
    for reverse in (False, True):
        blocks, pairs = [], [t == u]
        for m in _hgrn_levels():
            base = (t // (2 * m)) * (2 * m)
            if not reverse:
                r = base + m - 1
                block = np.where(t > r, (u > r) & (u <= t), (u > t) & (u <= r))
                roles = ((t // m) % 2 == 1) & ((u // m) % 2 == 0)
            else:
                r = base + m
                block = np.where(t < r, (u >= t) & (u < r), (u >= r) & (u < t))
                roles = ((t // m) % 2 == 0) & ((u // m) % 2 == 1)
            if m < SUBLANES:
                blocks.append(block)
            pairs.append(roles & (t // (2 * m) == u // (2 * m)))
        blocks.append(u >= t if reverse else u <= t)
        mats.append(np.tile(np.concatenate(blocks, axis=0), (1, 2)))
        masks.append(np.stack(pairs))
        fast.append((t // HG_FAST_BLOCK == u // HG_FAST_BLOCK) & (u >= t if reverse else u <= t))
    return tuple(np.stack(x).astype(np.float32) for x in (mats, masks, fast))


def _hgrn_sums(lf2, sum_mat):
    hi = lf2.astype(BF16)
    mid = (lf2 - hi.astype(F32)).astype(BF16)
    return _dot(sum_mat, jnp.concatenate([hi, mid], axis=0))


def _hgrn_chunk(q_ref, lf_ref, v_ref, o_ref, state_ref, sum_mat, pair_mask, fast_mask, b_scr, reverse, fast):
    c = HG_CHUNK
    levels = list(enumerate(_hgrn_levels()))
    n_small = sum(m < SUBLANES for _, m in levels)
    q, lf2, v, state = q_ref[...], lf_ref[...], v_ref[...], state_ref[...]
    qb = q.astype(BF16)
    kb = (1.0 - jnp.exp2(lf2)).astype(BF16)
    if fast:
        b = _hgrn_sums(lf2, sum_mat[n_small * c:, :])
        yield
        b_scr[...] = b
        parts = []
        for g in range(c // HG_FAST_BLOCK):
            lo, up = g * HG_FAST_BLOCK, (g + 1) * HG_FAST_BLOCK
            before = up if reverse else lo - 1
            parts.append(b[lo:up] - b_scr[before:before + 1, :] if 0 <= before < c else b[lo:up])
        loc = jnp.concatenate(parts, axis=0)
        a = _dot_nt(qb * jnp.exp2(loc).astype(BF16), kb * jnp.exp2(-loc).astype(BF16)) * fast_mask[...]
        yield
        levels = [(j, m) for j, m in levels if m >= HG_FAST_BLOCK]
    else:
        sums = _hgrn_sums(lf2, sum_mat[...])
        yield
        b = sums[n_small * c:]
        b_scr[...] = b
        a = _dot_nt(qb, kb) * pair_mask[0]
        yield
    for j, m in levels:
        if m < SUBLANES:
            e = sums[j * c:(j + 1) * c]
        else:
            parts = []
            for g in range(c // (2 * m)):
                lo, mid_row, up = 2 * m * g, 2 * m * g + m, 2 * m * (g + 1)
                r = mid_row if reverse else mid_row - 1
                b_r = b_scr[r:r + 1, :]
                parts += [b[lo:mid_row] - b_r, b_r - b[mid_row:up]] if reverse else \
                         [b_r - b[lo:mid_row], b[mid_row:up] - b_r]
            e = jnp.concatenate(parts, axis=0)
        fac = jnp.exp2(e).astype(BF16)
        a = a + _dot_nt(qb * fac, kb * fac) * pair_mask[j + 1]
        yield
    vb = v.astype(BF16)
    last = 0 if reverse else c - 1
    total = b_scr[last:last + 1, :]
    o_ref[...] = _dot(a.astype(BF16), vb) + _dot_nt(qb * jnp.exp2(b).astype(BF16), state.astype(BF16))
    yield
    k_dec = kb * jnp.exp2(total - b).astype(BF16)
    state_ref[...] = state * jnp.exp2(total) + _dot_tn(vb, k_dec)


def _hgrn_kernel(qf_ref, lff_ref, vf_ref, qb_ref, lfb_ref, vb_ref, sm_ref, pm_ref, fm_ref, of_ref, ob_ref,
                 s_ref, b_scr):
    @pl.when(pl.program_id(1) == 0)
    def _():
        s_ref[...] = jnp.zeros_like(s_ref)

    c = HG_CHUNK
    chains = [(d, h) for h in range(HG_HEADS) for d in (0, 1)]
    q_refs, lf_refs, v_refs, o_refs = (qf_ref, qb_ref), (lff_ref, lfb_ref), (vf_ref, vb_ref), (of_ref, ob_ref)

    worst = jnp.zeros((1, HG_DIM), F32)
    for d, h in chains:
        for g in range(c // HG_FAST_BLOCK):
            block = lf_refs[d][0, h, g * HG_FAST_BLOCK:(g + 1) * HG_FAST_BLOCK, :]
            worst = jnp.minimum(worst, jnp.sum(block, axis=0, keepdims=True))
    mild = jnp.min(worst) >= -HG_FAST_MAX_LOG2

    def run(fast):
        work = [_hgrn_chunk(q_refs[d].at[0, h], lf_refs[d].at[0, h], v_refs[d].at[0, h], o_refs[d].at[0, h],
                            s_ref.at[d, h], sm_ref.at[d], pm_ref.at[d], fm_ref.at[d], b_scr.at[d, h],
                            bool(d), fast) for d, h in chains]
        for _ in itertools.zip_longest(*work):
            pass

    pl.when(mild)(lambda: run(True))
    pl.when(jnp.logical_not(mild))(lambda: run(False))


def _hgrn(hq, lf_f, lf_b, hi):
    b, nh, n, dk = hq.shape
    nc = n // HG_CHUNK
    blk = (1, nh, HG_CHUNK, dk)
    fwd = pl.BlockSpec(blk, lambda i, j: (i, 0, j, 0))
    bwd = pl.BlockSpec(blk, lambda i, j: (i, 0, nc - 1 - j, 0))
    out = jax.ShapeDtypeStruct((b, nh, n, dk), F32)
    tables = _hgrn_tables()
    return pl.pallas_call(
        _hgrn_kernel,
        grid=(b, nc),
        in_specs=[fwd, fwd, fwd, bwd, bwd, bwd] + [_const_spec(t.shape) for t in tables],
        out_specs=[fwd, bwd],
        out_shape=[out, out],
        scratch_shapes=[pltpu.VMEM((2, nh, dk, dk), F32), pltpu.VMEM((2, nh, HG_CHUNK, dk), F32)],
        compiler_params=_params("parallel", "arbitrary"),
        name="hgrn",
    )(hq, lf_f, hi, hq, lf_b, hi, jnp.asarray(tables[0], BF16), jnp.asarray(tables[1]), jnp.asarray(tables[2]))


def _mix_cross_kernel(x_ref, att_ref, of_ref, ob_ref, sg_ref, og_ref, wo_ref, pmg_ref, pxg_ref,
                      wq_ref, km_ref, vm_ref, wxo_ref, poxg_ref, pfg_ref, x2_ref, h3_ref):
    tm = x_ref.shape[1]
    part = tm // MIX_ROW_PARTS

    def rows_of(lo):
        rows = slice(lo, lo + part)
        rec = []
        for i in range(HG_HEADS):
            o = _rms(of_ref[0, i, rows, :] + ob_ref[0, i, rows, :], og_ref[...])
            rec.append((o * sg_ref[0, rows, i * HG_DIM:(i + 1) * HG_DIM]).astype(BF16))
        att_t = att_ref[0, :, :, rows].reshape(ATT_Q_DIM, part)
        mixed = _dot_tn(att_t, wo_ref[:ATT_Q_DIM, :]) + _dot(jnp.concatenate(rec, axis=-1), wo_ref[ATT_Q_DIM:, :])
        yield
        x1 = x_ref[0, rows, :] + _rms(mixed, pmg_ref[...])
        h2 = _rms(x1, pxg_ref[...]).astype(BF16)
        d = h2.shape[-1]
        dh = d // X_HEADS
        q = _dot(h2, wq_ref[...]) * (dh ** -0.5)
        yield
        heads = []
        for i in range(X_HEADS):
            sl = slice(i * dh, (i + 1) * dh)
            s = _dot_nt(q[:, sl].astype(BF16), km_ref[0, :, sl])
            p = jnp.exp(s - jnp.max(s, axis=-1, keepdims=True))
            l = jnp.sum(p, axis=-1, keepdims=True)
            heads.append((_dot(p.astype(BF16), vm_ref[0, :, sl]) / l).astype(BF16))
        yield
        xo = _dot(jnp.concatenate(heads, axis=-1), wxo_ref[...])
        yield
        x2 = x1 + _rms(xo, poxg_ref[...])
        x2_ref[0, rows, :] = x2
        h3_ref[0, rows, :] = _rms(x2, pfg_ref[...]).astype(BF16)

    for _ in itertools.zip_longest(*[rows_of(j * part) for j in range(MIX_ROW_PARTS)]):
        pass


def _mix_cross(x, att, o_f, o_b, sg, og, w_out, pmg, pxg, w_xq, k_mem, v_mem, w_xo, poxg, pfg, tm):
    b, n, d = x.shape
    nm = k_mem.shape[1]
    tok = lambda i, j: (i, j, 0)
    head = lambda i, j: (i, 0, j, 0)
    batch = lambda i, j: (i, 0, 0)
    vec = _const_spec((1, d))
    mat = _const_spec((d, d))
    return pl.pallas_call(
        _mix_cross_kernel,
        grid=(b, n // tm),
        in_specs=[pl.BlockSpec((1, tm, d), tok),
                  pl.BlockSpec((1, ATT_HEADS, ATT_HEAD_DIM, tm), lambda i, j: (i, 0, 0, j)),
                  pl.BlockSpec((1, HG_HEADS, tm, HG_DIM), head), pl.BlockSpec((1, HG_HEADS, tm, HG_DIM), head),
                  pl.BlockSpec((1, tm, HG_WIDTH), tok), _const_spec((1, HG_DIM)),
                  mat, vec, vec, mat,
                  pl.BlockSpec((1, nm, d), batch), pl.BlockSpec((1, nm, d), batch),
                  mat, vec, vec],
        out_specs=[pl.BlockSpec((1, tm, d), tok), pl.BlockSpec((1, tm, d), tok)],
        out_shape=[jax.ShapeDtypeStruct((b, n, d), F32), jax.ShapeDtypeStruct((b, n, d), BF16)],
        compiler_params=_params("parallel", "parallel"),
        name="mix_cross",
    )(x, att, o_f, o_b, sg, og, w_out, pmg, pxg, w_xq, k_mem, v_mem, w_xo, poxg, pfg)


def _conv_ffn_kernel(x_ref, h_ref, hp_ref, hn_ref, wu_ref, cw_ref, cb_ref, wd_ref, g_ref, o_ref, act_scr):
    j = pl.program_id(1)
    tm = h_ref.shape[1]
    d_ff = wd_ref.shape[0]
    ck = FF_CHUNK
    prev = jnp.where(j > 0, hp_ref[0], jnp.zeros_like(hp_ref[0]))
    nxt = jnp.where(j < pl.num_programs(1) - 1, hn_ref[0], jnp.zeros_like(hn_ref[0]))
    hext = jnp.concatenate([prev, h_ref[0], nxt], axis=0)

    def conv(col):
        u = _dot(hext, wu_ref[:, col:col + ck])
        taps = (pltpu.roll(u, 1, 0), u, pltpu.roll(u, u.shape[0] - 1, 0))
        out = cb_ref[:, col:col + ck]
        for t in range(CONV_W):
            out = out + taps[t][HALO:HALO + tm] * cw_ref[t:t + 1, col:col + ck]
        return out

    for c in range(d_ff // ck):
        act_scr[:, c * ck:(c + 1) * ck] = (_silu(conv(c * ck)) * conv(d_ff + c * ck)).astype(BF16)
    o_ref[0] = x_ref[0] + _rms(_dot(act_scr[...], wd_ref[...]), g_ref[...])


def _conv_ffn(x2, h3, w_up, conv_w, conv_b, wd, g, tm):
    b, n, d = x2.shape
    d_ff = wd.shape[0]
    ck = FF_CHUNK
    hb = tm // HALO
    last = n // HALO - 1
    tok = lambda i, j: (i, j, 0)
    full = lambda shape: pl.BlockSpec(shape, lambda *_: (0,) * len(shape), pipeline_mode=pl.Buffered(1))
    return pl.pallas_call(
        _conv_ffn_kernel,
        grid=(b, n // tm),
        in_specs=[pl.BlockSpec((1, tm, d), tok), pl.BlockSpec((1, tm, d), tok),
                  pl.BlockSpec((1, HALO, d), lambda i, j: (i, jnp.maximum(j * hb - 1, 0), 0)),
                  pl.BlockSpec((1, HALO, d), lambda i, j: (i, jnp.minimum((j + 1) * hb, last), 0)),
                  full((d, 2 * d_ff)), full((CONV_W, 2 * d_ff)), full((1, 2 * d_ff)),
                  full((d_ff, d)), _const_spec((1, d))],
        out_specs=pl.BlockSpec((1, tm, d), tok),
        out_shape=jax.ShapeDtypeStruct((b, n, d), F32),
        scratch_shapes=[pltpu.VMEM((tm, d_ff), BF16)],
        compiler_params=_params("parallel", "parallel"),
        name="conv_ffn",
    )(x2, h3, h3, h3, w_up, conv_w, conv_b, wd, g)


def _rope_tables(n):
    pairs = ATT_HEAD_DIM // 4
    pos = jnp.arange(n)
    inv = jnp.power(ROPE_THETA, -jnp.arange(pairs, dtype=F32) / pairs)
    ang = jnp.concatenate([(pos // GRID_W).astype(F32)[:, None] * inv,
                           (pos % GRID_W).astype(F32)[:, None] * inv], axis=-1)
    cos = jnp.repeat(jnp.cos(ang), 2, axis=-1)
    sin = jnp.repeat(jnp.sin(ang), 2, axis=-1) * jnp.tile(jnp.array([-1.0, 1.0], F32), ATT_HEAD_DIM // 2)
    return jnp.tile(cos, (1, 2)), jnp.tile(sin, (1, 2))


def _segment_ones(width):
    i = jnp.arange(width) // ATT_HEAD_DIM
    return (i[:, None] == i[None, :]).astype(BF16)


def _layer(x, mem, lb, pre_mix_g, w_in, q_norm_g, k_norm_g, hg_out_norm_g, w_out, post_mix_g, pre_x_g,
           mem_norm_g, w_xq, w_xkv, w_xo, post_x_g, pre_ffn_g, w_up, conv_w, conv_b, w_down, post_ffn_g):
    b, n, d = x.shape
    d_ff = w_down.shape[0]
    assert n % HG_CHUNK == 0 and n % GRID_W == 0 and d_ff % FF_CHUNK == 0
    tm = min(256, n)
    tf = min(512, n)
    row = lambda g: g.reshape(1, -1).astype(F32)

    cos, sin = _rope_tables(n)
    qg = jnp.tile(q_norm_g, ATT_HEADS).reshape(1, -1) * (ATT_HEAD_DIM ** -0.5 * LOG2E)
    kg = jnp.tile(k_norm_g, ATT_KV_HEADS).reshape(1, -1)

    k_mem, v_mem = _mem_proj(mem, row(mem_norm_g), w_xkv.astype(BF16))
    q, k, v_t, hq, lf_f, lf_b, hi, sg = _in_proj(
        x, row(pre_mix_g), w_in.astype(BF16), qg, kg, cos, sin, lb,
        _segment_ones(ATT_Q_DIM), _segment_ones(ATT_KV_DIM), tf)
    att = _attention(q, k, v_t, tm)
    o_f, o_b = _hgrn(hq, lf_f, lf_b, hi)
    x2, h3 = _mix_cross(x, att, o_f, o_b, sg, row(hg_out_norm_g), w_out.astype(BF16), row(post_mix_g),
                        row(pre_x_g), w_xq.astype(BF16), k_mem, v_mem, w_xo.astype(BF16),
                        row(post_x_g), row(pre_ffn_g), tf)

    return _conv_ffn(x2, h3, w_up.astype(BF16), conv_w, row(conv_b), w_down.astype(BF16), row(post_ffn_g), tf)


def kernel(x, mem, pre_mix_g, w_in, q_norm_g, k_norm_g, hg_lb, hg_out_norm_g, w_out, post_mix_g, pre_x_g,
           mem_norm_g, w_xq, w_xkv, w_xo, post_x_g, pre_ffn_g, w_up, conv_w, conv_b, w_down, post_ffn_g):
    lb_all = jnp.cumsum(jax.nn.softmax(hg_lb.astype(F32), axis=1), axis=1)
    for l in range(w_in.shape[0]):
        x = _layer(x, mem, lb_all[:, l], pre_mix_g[l], w_in[l], q_norm_g[l], k_norm_g[l], hg_out_norm_g[l],
                   w_out[l], post_mix_g[l], pre_x_g[l], mem_norm_g[l], w_xq[l], w_xkv[l], w_xo[l],
                   post_x_g[l], pre_ffn_g[l], w_up[l], conv_w[l], conv_b[l], w_down[l], post_ffn_g[l])
    return x
```

```python
import functools
import itertools
import math

import numpy as np
import jax
import jax.numpy as jnp
from jax import lax
from jax.experimental import pallas as pl
from jax.experimental.pallas import tpu as pltpu

F32 = jnp.float32
BF16 = jnp.bfloat16

EPS = 1e-6
LOG2E = math.log2(math.e)
GRID_W = 64
ROPE_THETA = 10000.0

ATT_HEADS = 8
ATT_KV_HEADS = 2
ATT_GROUP = ATT_HEADS // ATT_KV_HEADS
ATT_HEAD_DIM = 64
ATT_Q_DIM = ATT_HEADS * ATT_HEAD_DIM
ATT_KV_DIM = ATT_KV_HEADS * ATT_HEAD_DIM
ATT_KEY_SPLITS = 2

HG_HEADS = 4
HG_DIM = 128
HG_WIDTH = HG_HEADS * HG_DIM
HG_CHUNK = 128
HG_STEP_CHUNKS = 4
HG_FAST_BLOCK = 32
HG_FAST_MAX_LOG2 = 96.0

X_HEADS = 4
IN_ROW_PARTS = 2
MIX_ROW_PARTS = 2
CONV_W = 3
FF_CHUNK = 256
HALO = 16

VMEM_LIMIT = 56 * 1024 * 1024
SUBLANES = 8


def _params(*sem):
    return pltpu.CompilerParams(dimension_semantics=sem, vmem_limit_bytes=VMEM_LIMIT)


def _rms(x, g):
    ms = jnp.mean(x * x, axis=-1, keepdims=True)
    return x * lax.rsqrt(ms + EPS) * g


def _silu(x):
    return x / (1.0 + jnp.exp(-x))


def _dot(a, b):
    return jnp.dot(a, b, preferred_element_type=F32)


def _dot_nt(a, b):
    return lax.dot_general(a, b, (((1,), (1,)), ((), ())), preferred_element_type=F32)


def _dot_tn(a, b):
    return lax.dot_general(a, b, (((0,), (0,)), ((), ())), preferred_element_type=F32)


def _const_spec(shape):
    return pl.BlockSpec(shape, lambda *_: (0,) * len(shape))


def _mem_proj_kernel(m_ref, g_ref, w_ref, k_ref, v_ref):
    d = m_ref.shape[-1]
    m = _rms(m_ref[0], g_ref[...]).astype(BF16)
    kv = _dot(m, w_ref[...])
    k_ref[0] = kv[:, :d].astype(BF16)
    v_ref[0] = kv[:, d:].astype(BF16)


def _mem_proj(mem, g, w_xkv):
    b, nm, d = mem.shape
    out = jax.ShapeDtypeStruct((b, nm, d), BF16)
    return pl.pallas_call(
        _mem_proj_kernel,
        grid=(b,),
        in_specs=[pl.BlockSpec((1, nm, d), lambda i: (i, 0, 0)),
                  _const_spec((1, d)), _const_spec((d, 2 * d))],
        out_specs=[pl.BlockSpec((1, nm, d), lambda i: (i, 0, 0))] * 2,
        out_shape=[out, out],
        compiler_params=_params("parallel"),
        name="mem_proj",
    )(mem, g, w_xkv)


def _head_norm_rope(a, seg, gain, cos, sin_signed):
    sq = a * a
    hi = sq.astype(BF16)
    lo = (sq - hi.astype(F32)).astype(BF16)
    ss = _dot(hi, seg) + _dot(lo, seg)
    an = a * lax.rsqrt(ss * (1.0 / ATT_HEAD_DIM) + EPS) * gain
    w = a.shape[-1]
    lane = lax.broadcasted_iota(jnp.int32, a.shape, 1)
    partner = jnp.where(lane % 2 == 0, pltpu.roll(an, w - 1, 1), pltpu.roll(an, 1, 1))
    return an * cos + partner * sin_signed


def _in_proj_kernel(x_ref, g_ref, w_ref, qg_ref, kg_ref, cos_ref, sin_ref, lb_ref, segq_ref, segk_ref,
                    q_ref, k_ref, vt_ref, hq_ref, lff_ref, lfb_ref, hi_ref, hg_ref):
    part = x_ref.shape[1] // IN_ROW_PARTS

    def rows_of(r0):
        rows = slice(r0, r0 + part)
        h = _rms(x_ref[0, rows, :], g_ref[...]).astype(BF16)

        def proj(lo, width):
            return _dot(h, w_ref[:, lo:lo + width])

        def heads_out(ref, val):
            for i in range(HG_HEADS):
                ref[0, i, rows, :] = val[:, i * HG_DIM:(i + 1) * HG_DIM].astype(ref.dtype)

        cos = cos_ref[rows, :]
        sin = sin_ref[rows, :]
        c0 = 0
        q = _head_norm_rope(proj(c0, ATT_Q_DIM), segq_ref[...], qg_ref[...],
                            jnp.concatenate([cos] * (ATT_Q_DIM // 128), axis=-1),
                            jnp.concatenate([sin] * (ATT_Q_DIM // 128), axis=-1))
        for i in range(ATT_HEADS):
            q_ref[0, i, rows, :] = q[:, i * ATT_HEAD_DIM:(i + 1) * ATT_HEAD_DIM].astype(BF16)
        c0 += ATT_Q_DIM
        yield
        k = _head_norm_rope(proj(c0, ATT_KV_DIM), segk_ref[...], kg_ref[...], cos, sin)
        c0 += ATT_KV_DIM
        v_t = jnp.transpose(proj(c0, ATT_KV_DIM))
        c0 += ATT_KV_DIM
        for i in range(ATT_KV_HEADS):
            k_ref[0, i, rows, :] = k[:, i * ATT_HEAD_DIM:(i + 1) * ATT_HEAD_DIM].astype(BF16)
            vt_ref[0, i, :, rows] = v_t[i * ATT_HEAD_DIM:(i + 1) * ATT_HEAD_DIM, :].astype(BF16)
        yield
        heads_out(hq_ref, _silu(proj(c0, HG_WIDTH)))
        c0 += HG_WIDTH
        yield
        for d, ref in enumerate((lff_ref, lfb_ref)):
            lb = lb_ref[d:d + 1, :]
            z = proj(c0, HG_WIDTH)
            heads_out(ref, jnp.log(lb + (1.0 - lb) / (1.0 + jnp.exp(-z))) * LOG2E)
            c0 += HG_WIDTH
            yield
        heads_out(hi_ref, proj(c0, HG_WIDTH))
        c0 += HG_WIDTH
        yield
        hg_ref[0, rows, :] = _silu(proj(c0, HG_WIDTH))

    for _ in itertools.zip_longest(*[rows_of(j * part) for j in range(IN_ROW_PARTS)]):
        pass


def _in_proj(x, g, w_in, qg, kg, cos, sin, lb, segq, segk, tm):
    b, n, d = x.shape
    nt = n // tm
    n_in = w_in.shape[1]
    tok = lambda i, j: (i, j, 0)
    head = lambda i, j: (i, 0, j, 0)
    pos = lambda i, j: (j, 0)
    return pl.pallas_call(
        _in_proj_kernel,
        grid=(b, nt),
        in_specs=[pl.BlockSpec((1, tm, d), tok), _const_spec((1, d)), _const_spec((d, n_in)),
                  _const_spec((1, ATT_Q_DIM)), _const_spec((1, ATT_KV_DIM)),
                  pl.BlockSpec((tm, 128), pos), pl.BlockSpec((tm, 128), pos),
                  _const_spec((2, HG_WIDTH)),
                  _const_spec((ATT_Q_DIM, ATT_Q_DIM)), _const_spec((ATT_KV_DIM, ATT_KV_DIM))],
        out_specs=[pl.BlockSpec((1, ATT_HEADS, tm, ATT_HEAD_DIM), head),
                   pl.BlockSpec((1, ATT_KV_HEADS, tm, ATT_HEAD_DIM), head),
                   pl.BlockSpec((1, ATT_KV_HEADS, ATT_HEAD_DIM, tm), lambda i, j: (i, 0, 0, j)),
                   pl.BlockSpec((1, HG_HEADS, tm, HG_DIM), head),
                   pl.BlockSpec((1, HG_HEADS, tm, HG_DIM), head),
                   pl.BlockSpec((1, HG_HEADS, tm, HG_DIM), head),
                   pl.BlockSpec((1, HG_HEADS, tm, HG_DIM), head),
                   pl.BlockSpec((1, tm, HG_WIDTH), tok)],
        out_shape=[jax.ShapeDtypeStruct((b, ATT_HEADS, n, ATT_HEAD_DIM), BF16),
                   jax.ShapeDtypeStruct((b, ATT_KV_HEADS, n, ATT_HEAD_DIM), BF16),
                   jax.ShapeDtypeStruct((b, ATT_KV_HEADS, ATT_HEAD_DIM, n), BF16),
                   jax.ShapeDtypeStruct((b, HG_HEADS, n, HG_DIM), BF16),
                   jax.ShapeDtypeStruct((b, HG_HEADS, n, HG_DIM), F32),
                   jax.ShapeDtypeStruct((b, HG_HEADS, n, HG_DIM), F32),
                   jax.ShapeDtypeStruct((b, HG_HEADS, n, HG_DIM), BF16),
                   jax.ShapeDtypeStruct((b, n, HG_WIDTH), F32)],
        compiler_params=_params("parallel", "parallel"),
        name="in_proj",
    )(x, g, w_in, qg, kg, cos, sin, lb, segq, segk)


def _attention_kernel(q_ref, k_ref, vt_ref, o_ref, s_even, s_odd):
    n = k_ref.shape[2]
    part = n // ATT_KEY_SPLITS
    parts = [slice(j * part, (j + 1) * part) for j in range(ATT_KEY_SPLITS)]

    def scores(i, s_scr):
        for rows in parts:
            s_scr[rows, :] = _dot_nt(k_ref[0, i // ATT_GROUP, rows, :], q_ref[0, i])

    def finish(i, s_scr):
        s = [s_scr[rows, :] for rows in parts]
        m = functools.reduce(jnp.maximum, [jnp.max(x, axis=0, keepdims=True) for x in s])
        p = [jnp.exp2(x - m) for x in s]
        l = sum(jnp.sum(x, axis=0, keepdims=True) for x in p)
        o_t = sum(_dot(vt_ref[0, i // ATT_GROUP, :, rows], x.astype(BF16)) for rows, x in zip(parts, p))
        o_ref[0, i] = (o_t / l).astype(BF16)

    scores(0, s_even)

    def pair(j, carry):
        scores(2 * j + 1, s_odd)
        finish(2 * j, s_even)
        scores(2 * j + 2, s_even)
        finish(2 * j + 1, s_odd)
        return carry

    lax.fori_loop(0, ATT_HEADS // 2 - 1, pair, 0)
    scores(ATT_HEADS - 1, s_odd)
    finish(ATT_HEADS - 2, s_even)
    finish(ATT_HEADS - 1, s_odd)


def _attention(q, k, v_t, tq):
    b, _, n, dh = q.shape
    return pl.pallas_call(
        _attention_kernel,
        grid=(b, n // tq),
        in_specs=[pl.BlockSpec((1, ATT_HEADS, tq, dh), lambda i, j: (i, 0, j, 0)),
                  pl.BlockSpec((1, ATT_KV_HEADS, n, dh), lambda i, j: (i, 0, 0, 0)),
                  pl.BlockSpec((1, ATT_KV_HEADS, dh, n), lambda i, j: (i, 0, 0, 0))],
        out_specs=pl.BlockSpec((1, ATT_HEADS, dh, tq), lambda i, j: (i, 0, 0, j)),
        out_shape=jax.ShapeDtypeStruct((b, ATT_HEADS, dh, n), BF16),
        scratch_shapes=[pltpu.VMEM((n, tq), F32), pltpu.VMEM((n, tq), F32)],
        compiler_params=_params("parallel", "parallel"),
        name="attention",
    )(q, k, v_t)


def _hgrn_levels():
    m, out = 1, []
    while m < HG_CHUNK:
        out.append(m)
        m *= 2
    return out


def _hgrn_tables():
    c = HG_CHUNK
    t = np.arange(c)[:, None]
    u = np.arange(c)[None, :]
    mats, masks, fast = [], [], []
    for reverse in (False, True):
        blocks, pairs = [], [t == u]
        for m in _hgrn_levels():
            base = (t // (2 * m)) * (2 * m)
            if not reverse:
                r = base + m - 1
                block = np.where(t > r, (u > r) & (u <= t), (u > t) & (u <= r))
                roles = ((t // m) % 2 == 1) & ((u // m) % 2 == 0)
            else:
                r = base + m
                block = np.where(t < r, (u >= t) & (u < r), (u >= r) & (u < t))
                roles = ((t // m) % 2 == 0) & ((u // m) % 2 == 1)
            if m < SUBLANES:
                blocks.append(block)
            pairs.append(roles & (t // (2 * m) == u // (2 * m)))
        blocks.append(u >= t if reverse else u <= t)
        mats.append(np.tile(np.concatenate(blocks, axis=0), (1, 2)))
        masks.append(np.stack(pairs))
        fast.append((t // HG_FAST_BLOCK == u // HG_FAST_BLOCK) & (u >= t if reverse else u <= t))
    return tuple(np.stack(x).astype(np.float32) for x in (mats, masks, fast))


def _hgrn_sums(lf2, sum_mat):
    hi = lf2.astype(BF16)
    mid = (lf2 - hi.astype(F32)).astype(BF16)
    return _dot(sum_mat, jnp.concatenate([hi, mid], axis=0))


def _hgrn_chunk(q_ref, lf_ref, v_ref, o_ref, state_ref, sum_mat, pair_mask, fast_mask, b_scr, reverse, fast):
    c = HG_CHUNK
    levels = list(enumerate(_hgrn_levels()))
    n_small = sum(m < SUBLANES for _, m in levels)
    q, lf2, v, state = q_ref[...], lf_ref[...], v_ref[...], state_ref[...]
    qb = q.astype(BF16)
    kb = (1.0 - jnp.exp2(lf2)).astype(BF16)
    if fast:
        b = _hgrn_sums(lf2, sum_mat[n_small * c:, :])
        yield
        b_scr[...] = b
        parts = []
        for g in range(c // HG_FAST_BLOCK):
            lo, up = g * HG_FAST_BLOCK, (g + 1) * HG_FAST_BLOCK
            before = up if reverse else lo - 1
            parts.append(b[lo:up] - b_scr[before:before + 1, :] if 0 <= before < c else b[lo:up])
        loc = jnp.concatenate(parts, axis=0)
        a = _dot_nt(qb * jnp.exp2(loc).astype(BF16), kb * jnp.exp2(-loc).astype(BF16)) * fast_mask[...]
        yield
        levels = [(j, m) for j, m in levels if m >= HG_FAST_BLOCK]
    else:
        sums = _hgrn_sums(lf2, sum_mat[...])
        yield
        b = sums[n_small * c:]
        b_scr[...] = b
        a = _dot_nt(qb, kb) * pair_mask[0]
        yield
    for j, m in levels:
        if m < SUBLANES:
            e = sums[j * c:(j + 1) * c]
        else:
            parts = []
            for g in range(c // (2 * m)):
                lo, mid_row, up = 2 * m * g, 2 * m * g + m, 2 * m * (g + 1)
                r = mid_row if reverse else mid_row - 1
                b_r = b_scr[r:r + 1, :]
                parts += [b[lo:mid_row] - b_r, b_r - b[mid_row:up]] if reverse else \
                         [b_r - b[lo:mid_row], b[mid_row:up] - b_r]
            e = jnp.concatenate(parts, axis=0)
        fac = jnp.exp2(e).astype(BF16)
        a = a + _dot_nt(qb * fac, kb * fac) * pair_mask[j + 1]
        yield
    vb = v.astype(BF16)
    last = 0 if reverse else c - 1
    total = b_scr[last:last + 1, :]
    o_ref[...] = _dot(a.astype(BF16), vb) + _dot_nt(qb * jnp.exp2(b).astype(BF16), state.astype(BF16))
    yield
    k_dec = kb * jnp.exp2(total - b).astype(BF16)
    state_ref[...] = state * jnp.exp2(total) + _dot_tn(vb, k_dec)


def _hgrn_kernel(qf_ref, lff_ref, vf_ref, qb_ref, lfb_ref, vb_ref, sm_ref, pm_ref, fm_ref, of_ref, ob_ref,
                 s_ref, b_scr):
    @pl.when(pl.program_id(1) == 0)
    def _():
        s_ref[...] = jnp.zeros_like(s_ref)

    c = HG_CHUNK
    rows_per_step = qf_ref.shape[2]
    chains = [(d, h) for h in range(HG_HEADS) for d in (0, 1)]
    q_refs, lf_refs, v_refs, o_refs = (qf_ref, qb_ref), (lff_ref, lfb_ref), (vf_ref, vb_ref), (of_ref, ob_ref)

    worst = jnp.zeros((1, HG_DIM), F32)
    for d, h in chains:
        for g in range(rows_per_step // HG_FAST_BLOCK):
            block = lf_refs[d][0, h, g * HG_FAST_BLOCK:(g + 1) * HG_FAST_BLOCK, :]
            worst = jnp.minimum(worst, jnp.sum(block, axis=0, keepdims=True))
    mild = jnp.min(worst) >= -HG_FAST_MAX_LOG2

    def run(fast):
        n_sub = rows_per_step // c
        for sub in range(n_sub):
            work = []
            for d, h in chains:
                k = n_sub - 1 - sub if d else sub
                rows = slice(k * c, (k + 1) * c)
                work.append(_hgrn_chunk(q_refs[d].at[0, h, rows, :], lf_refs[d].at[0, h, rows, :],
                                        v_refs[d].at[0, h, rows, :], o_refs[d].at[0, h, rows, :],
                                        s_ref.at[d, h], sm_ref.at[d], pm_ref.at[d], fm_ref.at[d],
                                        b_scr.at[d, h], bool(d), fast))
            for _ in itertools.zip_longest(*work):
                pass

    pl.when(mild)(lambda: run(True))
    pl.when(jnp.logical_not(mild))(lambda: run(False))


def _hgrn(hq, lf_f, lf_b, hi):
    b, nh, n, dk = hq.shape
    rows_per_step = HG_CHUNK * HG_STEP_CHUNKS if n % (HG_CHUNK * HG_STEP_CHUNKS) == 0 else HG_CHUNK
    nc = n // rows_per_step
    blk = (1, nh, rows_per_step, dk)
    fwd = pl.BlockSpec(blk, lambda i, j: (i, 0, j, 0))
    bwd = pl.BlockSpec(blk, lambda i, j: (i, 0, nc - 1 - j, 0))
    out = jax.ShapeDtypeStruct((b, nh, n, dk), F32)
    tables = _hgrn_tables()
    return pl.pallas_call(
        _hgrn_kernel,
        grid=(b, nc),
        in_specs=[fwd, fwd, fwd, bwd, bwd, bwd] + [_const_spec(t.shape) for t in tables],
        out_specs=[fwd, bwd],
        out_shape=[out, out],
        scratch_shapes=[pltpu.VMEM((2, nh, dk, dk), F32), pltpu.VMEM((2, nh, HG_CHUNK, dk), F32)],
        compiler_params=_params("parallel", "arbitrary"),
        name="hgrn",
    )(hq, lf_f, hi, hq, lf_b, hi, jnp.asarray(tables[0], BF16), jnp.asarray(tables[1]), jnp.asarray(tables[2]))


def _mix_cross_kernel(x_ref, att_ref, of_ref, ob_ref, sg_ref, og_ref, wo_ref, pmg_ref, pxg_ref,
                      wq_ref, km_ref, vm_ref, wxo_ref, poxg_ref, pfg_ref, x2_ref, h3_ref):
    tm = x_ref.shape[1]
    part = tm // MIX_ROW_PARTS

    def rows_of(lo):
        rows = slice(lo, lo + part)
        rec = []
        for i in range(HG_HEADS):
            o = _rms(of_ref[0, i, rows, :] + ob_ref[0, i, rows, :], og_ref[...])
            rec.append((o * sg_ref[0, rows, i * HG_DIM:(i + 1) * HG_DIM]).astype(BF16))
        att_t = att_ref[0, :, :, rows].reshape(ATT_Q_DIM, part)
        mixed = _dot_tn(att_t, wo_ref[:ATT_Q_DIM, :]) + _dot(jnp.concatenate(rec, axis=-1), wo_ref[ATT_Q_DIM:, :])
        yield
        x1 = x_ref[0, rows, :] + _rms(mixed, pmg_ref[...])
        h2 = _rms(x1, pxg_ref[...]).astype(BF16)
        d = h2.shape[-1]
        dh = d // X_HEADS
        q = _dot(h2, wq_ref[...]) * (dh ** -0.5)
        yield
        heads = []
        for i in range(X_HEADS):
            sl = slice(i * dh, (i + 1) * dh)
            s = _dot_nt(q[:, sl].astype(BF16), km_ref[0, :, sl])
            p = jnp.exp(s - jnp.max(s, axis=-1, keepdims=True))
            l = jnp.sum(p, axis=-1, keepdims=True)
            heads.append((_dot(p.astype(BF16), vm_ref[0, :, sl]) / l).astype(BF16))
        yield
        xo = _dot(jnp.concatenate(heads, axis=-1), wxo_ref[...])
        yield
        x2 = x1 + _rms(xo, poxg_ref[...])
        x2_ref[0, rows, :] = x2
        h3_ref[0, rows, :] = _rms(x2, pfg_ref[...]).astype(BF16)

    for _ in itertools.zip_longest(*[rows_of(j * part) for j in range(MIX_ROW_PARTS)]):
        pass


def _mix_cross(x, att, o_f, o_b, sg, og, w_out, pmg, pxg, w_xq, k_mem, v_mem, w_xo, poxg, pfg, tm):
    b, n, d = x.shape
    nm = k_mem.shape[1]
    tok = lambda i, j: (i, j, 0)
    head = lambda i, j: (i, 0, j, 0)
    batch = lambda i, j: (i, 0, 0)
    vec = _const_spec((1, d))
    mat = _const_spec((d, d))
    return pl.pallas_call(
        _mix_cross_kernel,
        grid=(b, n // tm),
        in_specs=[pl.BlockSpec((1, tm, d), tok),
                  pl.BlockSpec((1, ATT_HEADS, ATT_HEAD_DIM, tm), lambda i, j: (i, 0, 0, j)),
                  pl.BlockSpec((1, HG_HEADS, tm, HG_DIM), head), pl.BlockSpec((1, HG_HEADS, tm, HG_DIM), head),
                  pl.BlockSpec((1, tm, HG_WIDTH), tok), _const_spec((1, HG_DIM)),
                  mat, vec, vec, mat,
                  pl.BlockSpec((1, nm, d), batch), pl.BlockSpec((1, nm, d), batch),
                  mat, vec, vec],
        out_specs=[pl.BlockSpec((1, tm, d), tok), pl.BlockSpec((1, tm, d), tok)],
        out_shape=[jax.ShapeDtypeStruct((b, n, d), F32), jax.ShapeDtypeStruct((b, n, d), BF16)],
        compiler_params=_params("parallel", "parallel"),
        name="mix_cross",
    )(x, att, o_f, o_b, sg, og, w_out, pmg, pxg, w_xq, k_mem, v_mem, w_xo, poxg, pfg)


def _conv_ffn_kernel(x_ref, h_ref, hp_ref, hn_ref, wu_ref, cw_ref, cb_ref, wd_ref, g_ref, o_ref, act_scr):
    j = pl.program_id(1)
    tm = h_ref.shape[1]
    d_ff = wd_ref.shape[0]
    ck = FF_CHUNK
    prev = jnp.where(j > 0, hp_ref[0], jnp.zeros_like(hp_ref[0]))
    nxt = jnp.where(j < pl.num_programs(1) - 1, hn_ref[0], jnp.zeros_like(hn_ref[0]))
    hext = jnp.concatenate([prev, h_ref[0], nxt], axis=0)

    def conv(col):
        u = _dot(hext, wu_ref[:, col:col + ck])
        taps = (pltpu.roll(u, 1, 0), u, pltpu.roll(u, u.shape[0] - 1, 0))
        out = cb_ref[:, col:col + ck]
        for t in range(CONV_W):
            out = out + taps[t][HALO:HALO + tm] * cw_ref[t:t + 1, col:col + ck]
        return out

    for c in range(d_ff // ck):
        act_scr[:, c * ck:(c + 1) * ck] = (_silu(conv(c * ck)) * conv(d_ff + c * ck)).astype(BF16)
    o_ref[0] = x_ref[0] + _rms(_dot(act_scr[...], wd_ref[...]), g_ref[...])


def _conv_ffn(x2, h3, w_up, conv_w, conv_b, wd, g, tm):
    b, n, d = x2.shape
    d_ff = wd.shape[0]
    ck = FF_CHUNK
    hb = tm // HALO
    last = n // HALO - 1
    tok = lambda i, j: (i, j, 0)
    full = lambda shape: pl.BlockSpec(shape, lambda *_: (0,) * len(shape), pipeline_mode=pl.Buffered(1))
    return pl.pallas_call(
        _conv_ffn_kernel,
        grid=(b, n // tm),
        in_specs=[pl.BlockSpec((1, tm, d), tok), pl.BlockSpec((1, tm, d), tok),
                  pl.BlockSpec((1, HALO, d), lambda i, j: (i, jnp.maximum(j * hb - 1, 0), 0)),
                  pl.BlockSpec((1, HALO, d), lambda i, j: (i, jnp.minimum((j + 1) * hb, last), 0)),
                  full((d, 2 * d_ff)), full((CONV_W, 2 * d_ff)), full((1, 2 * d_ff)),
                  full((d_ff, d)), _const_spec((1, d))],
        out_specs=pl.BlockSpec((1, tm, d), tok),
        out_shape=jax.ShapeDtypeStruct((b, n, d), F32),
        scratch_shapes=[pltpu.VMEM((tm, d_ff), BF16)],
        compiler_params=_params("parallel", "parallel"),
        name="conv_ffn",
    )(x2, h3, h3, h3, w_up, conv_w, conv_b, wd, g)


def _rope_tables(n):
    pairs = ATT_HEAD_DIM // 4
    pos = jnp.arange(n)
    inv = jnp.power(ROPE_THETA, -jnp.arange(pairs, dtype=F32) / pairs)
    ang = jnp.concatenate([(pos // GRID_W).astype(F32)[:, None] * inv,
                           (pos % GRID_W).astype(F32)[:, None] * inv], axis=-1)
    cos = jnp.repeat(jnp.cos(ang), 2, axis=-1)
    sin = jnp.repeat(jnp.sin(ang), 2, axis=-1) * jnp.tile(jnp.array([-1.0, 1.0], F32), ATT_HEAD_DIM // 2)
    return jnp.tile(cos, (1, 2)), jnp.tile(sin, (1, 2))


def _segment_ones(width):
    i = jnp.arange(width) // ATT_HEAD_DIM
    return (i[:, None] == i[None, :]).astype(BF16)


def _layer(x, mem, lb, pre_mix_g, w_in, q_norm_g, k_norm_g, hg_out_norm_g, w_out, post_mix_g, pre_x_g,
           mem_norm_g, w_xq, w_xkv, w_xo, post_x_g, pre_ffn_g, w_up, conv_w, conv_b, w_down, post_ffn_g):
    b, n, d = x.shape
    d_ff = w_down.shape[0]
    assert n % HG_CHUNK == 0 and n % GRID_W == 0 and d_ff % FF_CHUNK == 0
    tm = min(256, n)
    tf = min(512, n)
    row = lambda g: g.reshape(1, -1).astype(F32)

    cos, sin = _rope_tables(n)
    qg = jnp.tile(q_norm_g, ATT_HEADS).reshape(1, -1) * (ATT_HEAD_DIM ** -0.5 * LOG2E)
    kg = jnp.tile(k_norm_g, ATT_KV_HEADS).reshape(1, -1)

    k_mem, v_mem = _mem_proj(mem, row(mem_norm_g), w_xkv.astype(BF16))
    q, k, v_t, hq, lf_f, lf_b, hi, sg = _in_proj(
        x, row(pre_mix_g), w_in.astype(BF16), qg, kg, cos, sin, lb,
        _segment_ones(ATT_Q_DIM), _segment_ones(ATT_KV_DIM), tf)
    att = _attention(q, k, v_t, tm)
    o_f, o_b = _hgrn(hq, lf_f, lf_b, hi)
    x2, h3 = _mix_cross(x, att, o_f, o_b, sg, row(hg_out_norm_g), w_out.astype(BF16), row(post_mix_g),
                        row(pre_x_g), w_xq.astype(BF16), k_mem, v_mem, w_xo.astype(BF16),
                        row(post_x_g), row(pre_ffn_g), tf)

    return _conv_ffn(x2, h3, w_up.astype(BF16), conv_w, row(conv_b), w_down.astype(BF16), row(post_ffn_g), tf)


def kernel(x, mem, pre_mix_g, w_in, q_norm_g, k_norm_g, hg_lb, hg_out_norm_g, w_out, post_mix_g, pre_x_g,
           mem_norm_g, w_xq, w_xkv, w_xo, post_x_g, pre_ffn_g, w_up, conv_w, conv_b, w_down, post_ffn_g):
    lb_all = jnp.cumsum(jax.nn.softmax(hg_lb.astype(F32), axis=1), axis=1)
    for l in range(w_in.shape[0]):
        x = _layer(x, mem, lb_all[:, l], pre_mix_g[l], w_in[l], q_norm_g[l], k_norm_g[l], hg_out_norm_g[l],
                   w_out[l], post_mix_g[l], pre_x_g[l], mem_norm_g[l], w_xq[l], w_xkv[l], w_xo[l],
                   post_x_g[l], pre_ffn_g[l], w_up[l], conv_w[l], conv_b[l], w_down[l], post_ffn_g[l])
    return x
```

```python
import functools
import itertools
import math

import numpy as np
import jax
import jax.numpy as jnp
from jax import lax
from jax.experimental import pallas as pl
from jax.experimental.pallas import tpu as pltpu

F32 = jnp.float32
BF16 = jnp.bfloat16

EPS = 1e-6
LOG2E = math.log2(math.e)
GRID_W = 64
ROPE_THETA = 10000.0

ATT_HEADS = 8
ATT_KV_HEADS = 2
ATT_GROUP = ATT_HEADS // ATT_KV_HEADS
ATT_HEAD_DIM = 64
ATT_Q_DIM = ATT_HEADS * ATT_HEAD_DIM
ATT_KV_DIM = ATT_KV_HEADS * ATT_HEAD_DIM
ATT_KEY_SPLITS = 2

HG_HEADS = 4
HG_DIM = 128
HG_WIDTH = HG_HEADS * HG_DIM
HG_CHUNK = 128
HG_STEP_CHUNKS = 4
HG_FAST_BLOCK = 32
HG_FAST_MAX_LOG2 = 96.0

X_HEADS = 4
IN_ROW_PARTS = 2
MIX_ROW_PARTS = 2
CONV_W = 3
FF_CHUNK = 256
HALO = 16

VMEM_LIMIT = 56 * 1024 * 1024
SUBLANES = 8


def _params(*sem):
    return pltpu.CompilerParams(dimension_semantics=sem, vmem_limit_bytes=VMEM_LIMIT)


def _rms(x, g):
    ms = jnp.mean(x * x, axis=-1, keepdims=True)
    return x * lax.rsqrt(ms + EPS) * g


def _silu(x):
    return x / (1.0 + jnp.exp(-x))


def _dot(a, b):
    return jnp.dot(a, b, preferred_element_type=F32)


def _dot_nt(a, b):
    return lax.dot_general(a, b, (((1,), (1,)), ((), ())), preferred_element_type=F32)


def _dot_tn(a, b):
    return lax.dot_general(a, b, (((0,), (0,)), ((), ())), preferred_element_type=F32)


def _const_spec(shape):
    return pl.BlockSpec(shape, lambda *_: (0,) * len(shape))


def _mem_proj_kernel(m_ref, g_ref, w_ref, k_ref, v_ref):
    d = m_ref.shape[-1]
    m = _rms(m_ref[0], g_ref[...]).astype(BF16)
    kv = _dot(m, w_ref[...])
    k_ref[0] = kv[:, :d].astype(BF16)
    v_ref[0] = kv[:, d:].astype(BF16)


def _mem_proj(mem, g, w_xkv):
    b, nm, d = mem.shape
    out = jax.ShapeDtypeStruct((b, nm, d), BF16)
    return pl.pallas_call(
        _mem_proj_kernel,
        grid=(b,),
        in_specs=[pl.BlockSpec((1, nm, d), lambda i: (i, 0, 0)),
                  _const_spec((1, d)), _const_spec((d, 2 * d))],
        out_specs=[pl.BlockSpec((1, nm, d), lambda i: (i, 0, 0))] * 2,
        out_shape=[out, out],
        compiler_params=_params("parallel"),
        name="mem_proj",
    )(mem, g, w_xkv)


def _head_norm_rope(a, seg, gain, cos, sin_signed):
    sq = a * a
    hi = sq.astype(BF16)
    lo = (sq - hi.astype(F32)).astype(BF16)
    ss = _dot(hi, seg) + _dot(lo, seg)
    an = a * lax.rsqrt(ss * (1.0 / ATT_HEAD_DIM) + EPS) * gain
    w = a.shape[-1]
    lane = lax.broadcasted_iota(jnp.int32, a.shape, 1)
    partner = jnp.where(lane % 2 == 0, pltpu.roll(an, w - 1, 1), pltpu.roll(an, 1, 1))
    return an * cos + partner * sin_signed


def _in_proj_kernel(x_ref, g_ref, w_ref, qg_ref, kg_ref, cos_ref, sin_ref, lb_ref, segq_ref, segk_ref,
                    q_ref, k_ref, vt_ref, hq_ref, lff_ref, lfb_ref, hi_ref, hg_ref):
    part = x_ref.shape[1] // IN_ROW_PARTS

    def rows_of(r0):
        rows = slice(r0, r0 + part)
        h = _rms(x_ref[0, rows, :], g_ref[...]).astype(BF16)

        def proj(lo, width):
            return _dot(h, w_ref[:, lo:lo + width])

        def heads_out(ref, val):
            for i in range(HG_HEADS):
                ref[0, i, rows, :] = val[:, i * HG_DIM:(i + 1) * HG_DIM].astype(ref.dtype)

        cos = cos_ref[rows, :]
        sin = sin_ref[rows, :]
        c0 = 0
        q = _head_norm_rope(proj(c0, ATT_Q_DIM), segq_ref[...], qg_ref[...],
                            jnp.concatenate([cos] * (ATT_Q_DIM // 128), axis=-1),
                            jnp.concatenate([sin] * (ATT_Q_DIM // 128), axis=-1))
        for i in range(ATT_HEADS):
            q_ref[0, i, rows, :] = q[:, i * ATT_HEAD_DIM:(i + 1) * ATT_HEAD_DIM].astype(BF16)
        c0 += ATT_Q_DIM
        yield
        k = _head_norm_rope(proj(c0, ATT_KV_DIM), segk_ref[...], kg_ref[...], cos, sin)
        c0 += ATT_KV_DIM
        v_t = jnp.transpose(proj(c0, ATT_KV_DIM))
        c0 += ATT_KV_DIM
        for i in range(ATT_KV_HEADS):
            k_ref[0, i, rows, :] = k[:, i * ATT_HEAD_DIM:(i + 1) * ATT_HEAD_DIM].astype(BF16)
            vt_ref[0, i, :, rows] = v_t[i * ATT_HEAD_DIM:(i + 1) * ATT_HEAD_DIM, :].astype(BF16)
        yield
        heads_out(hq_ref, _silu(proj(c0, HG_WIDTH)))
        c0 += HG_WIDTH
        yield
        for d, ref in enumerate((lff_ref, lfb_ref)):
            lb = lb_ref[d:d + 1, :]
            z = proj(c0, HG_WIDTH)
            heads_out(ref, jnp.log(lb + (1.0 - lb) / (1.0 + jnp.exp(-z))) * LOG2E)
            c0 += HG_WIDTH
            yield
        heads_out(hi_ref, proj(c0, HG_WIDTH))
        c0 += HG_WIDTH
        yield
        hg_ref[0, rows, :] = _silu(proj(c0, HG_WIDTH))

    for _ in itertools.zip_longest(*[rows_of(j * part) for j in range(IN_ROW_PARTS)]):
        pass


def _in_proj(x, g, w_in, qg, kg, cos, sin, lb, segq, segk, tm):
    b, n, d = x.shape
    nt = n // tm
    n_in = w_in.shape[1]
    tok = lambda i, j: (i, j, 0)
    head = lambda i, j: (i, 0, j, 0)
    pos = lambda i, j: (j, 0)
    return pl.pallas_call(
        _in_proj_kernel,
        grid=(b, nt),
        in_specs=[pl.BlockSpec((1, tm, d), tok), _const_spec((1, d)), _const_spec((d, n_in)),
                  _const_spec((1, ATT_Q_DIM)), _const_spec((1, ATT_KV_DIM)),
                  pl.BlockSpec((tm, 128), pos), pl.BlockSpec((tm, 128), pos),
                  _const_spec((2, HG_WIDTH)),
                  _const_spec((ATT_Q_DIM, ATT_Q_DIM)), _const_spec((ATT_KV_DIM, ATT_KV_DIM))],
        out_specs=[pl.BlockSpec((1, ATT_HEADS, tm, ATT_HEAD_DIM), head),
                   pl.BlockSpec((1, ATT_KV_HEADS, tm, ATT_HEAD_DIM), head),
                   pl.BlockSpec((1, ATT_KV_HEADS, ATT_HEAD_DIM, tm), lambda i, j: (i, 0, 0, j)),
                   pl.BlockSpec((1, HG_HEADS, tm, HG_DIM), head),
                   pl.BlockSpec((1, HG_HEADS, tm, HG_DIM), head),
                   pl.BlockSpec((1, HG_HEADS, tm, HG_DIM), head),
                   pl.BlockSpec((1, HG_HEADS, tm, HG_DIM), head),
                   pl.BlockSpec((1, tm, HG_WIDTH), tok)],
        out_shape=[jax.ShapeDtypeStruct((b, ATT_HEADS, n, ATT_HEAD_DIM), BF16),
                   jax.ShapeDtypeStruct((b, ATT_KV_HEADS, n, ATT_HEAD_DIM), BF16),
                   jax.ShapeDtypeStruct((b, ATT_KV_HEADS, ATT_HEAD_DIM, n), BF16),
                   jax.ShapeDtypeStruct((b, HG_HEADS, n, HG_DIM), BF16),
                   jax.ShapeDtypeStruct((b, HG_HEADS, n, HG_DIM), F32),
                   jax.ShapeDtypeStruct((b, HG_HEADS, n, HG_DIM), F32),
                   jax.ShapeDtypeStruct((b, HG_HEADS, n, HG_DIM), BF16),
                   jax.ShapeDtypeStruct((b, n, HG_WIDTH), F32)],
        compiler_params=_params("parallel", "parallel"),
        name="in_proj",
    )(x, g, w_in, qg, kg, cos, sin, lb, segq, segk)


def _attention_kernel(q_ref, qn_ref, k_ref, vt_ref, o_ref, s_even, s_odd):
    n = k_ref.shape[2]
    part = n // ATT_KEY_SPLITS
    parts = [slice(j * part, (j + 1) * part) for j in range(ATT_KEY_SPLITS)]

    def scores(i, s_scr, queries=q_ref):
        for rows in parts:
            s_scr[rows, :] = _dot_nt(k_ref[0, i // ATT_GROUP, rows, :], queries[0, i])

    def finish(i, s_scr):
        s = [s_scr[rows, :] for rows in parts]
        m = functools.reduce(jnp.maximum, [jnp.max(x, axis=0, keepdims=True) for x in s])
        p = [jnp.exp2(x - m) for x in s]
        l = sum(jnp.sum(x, axis=0, keepdims=True) for x in p)
        o_t = sum(_dot(vt_ref[0, i // ATT_GROUP, :, rows], x.astype(BF16)) for rows, x in zip(parts, p))
        o_ref[0, i] = (o_t / l).astype(BF16)

    @pl.when(pl.program_id(1) == 0)
    def _():
        scores(0, s_even)

    def pair(j, carry):
        scores(2 * j + 1, s_odd)
        finish(2 * j, s_even)
        scores(2 * j + 2, s_even)
        finish(2 * j + 1, s_odd)
        return carry

    lax.fori_loop(0, ATT_HEADS // 2 - 1, pair, 0)
    scores(ATT_HEADS - 1, s_odd)
    finish(ATT_HEADS - 2, s_even)
    scores(0, s_even, qn_ref)
    finish(ATT_HEADS - 1, s_odd)


def _attention(q, k, v_t, tq):
    b, _, n, dh = q.shape
    last = n // tq - 1
    return pl.pallas_call(
        _attention_kernel,
        grid=(b, n // tq),
        in_specs=[pl.BlockSpec((1, ATT_HEADS, tq, dh), lambda i, j: (i, 0, j, 0)),
                  pl.BlockSpec((1, ATT_HEADS, tq, dh), lambda i, j: (i, 0, jnp.minimum(j + 1, last), 0)),
                  pl.BlockSpec((1, ATT_KV_HEADS, n, dh), lambda i, j: (i, 0, 0, 0)),
                  pl.BlockSpec((1, ATT_KV_HEADS, dh, n), lambda i, j: (i, 0, 0, 0))],
        out_specs=pl.BlockSpec((1, ATT_HEADS, dh, tq), lambda i, j: (i, 0, 0, j)),
        out_shape=jax.ShapeDtypeStruct((b, ATT_HEADS, dh, n), BF16),
        scratch_shapes=[pltpu.VMEM((n, tq), F32), pltpu.VMEM((n, tq), F32)],
        compiler_params=_params("parallel", "arbitrary"),
        name="attention",
    )(q, q, k, v_t)


def _hgrn_levels():
    m, out = 1, []
    while m < HG_CHUNK:
        out.append(m)
        m *= 2
    return out


def _hgrn_tables():
    c = HG_CHUNK
    t = np.arange(c)[:, None]
    u = np.arange(c)[None, :]
    mats, masks, fast = [], [], []
    for reverse in (False, True):
        blocks, pairs = [], [t == u]
        for m in _hgrn_levels():
            base = (t // (2 * m)) * (2 * m)
            if not reverse:
                r = base + m - 1
                block = np.where(t > r, (u > r) & (u <= t), (u > t) & (u <= r))
                roles = ((t // m) % 2 == 1) & ((u // m) % 2 == 0)
            else:
                r = base + m
                block = np.where(t < r, (u >= t) & (u < r), (u >= r) & (u < t))
                roles = ((t // m) % 2 == 0) & ((u // m) % 2 == 1)
            if m < SUBLANES:
                blocks.append(block)
            pairs.append(roles & (t // (2 * m) == u // (2 * m)))
        blocks.append(u >= t if reverse else u <= t)
        mats.append(np.tile(np.concatenate(blocks, axis=0), (1, 2)))
        masks.append(np.stack(pairs))
        fast.append((t // HG_FAST_BLOCK == u // HG_FAST_BLOCK) & (u >= t if reverse else u <= t))
    return tuple(np.stack(x).astype(np.float32) for x in (mats, masks, fast))


def _hgrn_sums(lf2, sum_mat):
    hi = lf2.astype(BF16)
    mid = (lf2 - hi.astype(F32)).astype(BF16)
    return _dot(sum_mat, jnp.concatenate([hi, mid], axis=0))


def _hgrn_chunk(q_ref, lf_ref, v_ref, o_ref, state_ref, sum_mat, pair_mask, fast_mask, b_scr, reverse, fast):
    c = HG_CHUNK
    levels = list(enumerate(_hgrn_levels()))
    n_small = sum(m < SUBLANES for _, m in levels)
    q, lf2, v, state = q_ref[...], lf_ref[...], v_ref[...], state_ref[...]
    qb = q.astype(BF16)
    kb = (1.0 - jnp.exp2(lf2)).astype(BF16)
    if fast:
        b = _hgrn_sums(lf2, sum_mat[n_small * c:, :])
        yield
        b_scr[...] = b
        parts = []
        for g in range(c // HG_FAST_BLOCK):
            lo, up = g * HG_FAST_BLOCK, (g + 1) * HG_FAST_BLOCK
            before = up if reverse else lo - 1
            parts.append(b[lo:up] - b_scr[before:before + 1, :] if 0 <= before < c else b[lo:up])
        loc = jnp.concatenate(parts, axis=0)
        a = _dot_nt(qb * jnp.exp2(loc).astype(BF16), kb * jnp.exp2(-loc).astype(BF16)) * fast_mask[...]
        yield
        levels = [(j, m) for j, m in levels if m >= HG_FAST_BLOCK]
    else:
        sums = _hgrn_sums(lf2, sum_mat[...])
        yield
        b = sums[n_small * c:]
        b_scr[...] = b
        a = _dot_nt(qb, kb) * pair_mask[0]
        yield
    for j, m in levels:
        if m < SUBLANES:
            e = sums[j * c:(j + 1) * c]
        else:
            parts = []
            for g in range(c // (2 * m)):
                lo, mid_row, up = 2 * m * g, 2 * m * g + m, 2 * m * (g + 1)
                r = mid_row if reverse else mid_row - 1
                b_r = b_scr[r:r + 1, :]
                parts += [b[lo:mid_row] - b_r, b_r - b[mid_row:up]] if reverse else \
                         [b_r - b[lo:mid_row], b[mid_row:up] - b_r]
            e = jnp.concatenate(parts, axis=0)
        fac = jnp.exp2(e).astype(BF16)
        a = a + _dot_nt(qb * fac, kb * fac) * pair_mask[j + 1]
        yield
    vb = v.astype(BF16)
    last = 0 if reverse else c - 1
    total = b_scr[last:last + 1, :]
    o_ref[...] = _dot(a.astype(BF16), vb) + _dot_nt(qb * jnp.exp2(b).astype(BF16), state.astype(BF16))
    yield
    k_dec = kb * jnp.exp2(total - b).astype(BF16)
    state_ref[...] = state * jnp.exp2(total) + _dot_tn(vb, k_dec)


def _hgrn_kernel(qf_ref, lff_ref, vf_ref, qb_ref, lfb_ref, vb_ref, sm_ref, pm_ref, fm_ref, of_ref, ob_ref,
                 s_ref, b_scr):
    @pl.when(pl.program_id(1) == 0)
    def _():
        s_ref[...] = jnp.zeros_like(s_ref)

    c = HG_CHUNK
    rows_per_step = qf_ref.shape[2]
    chains = [(d, h) for h in range(HG_HEADS) for d in (0, 1)]
    q_refs, lf_refs, v_refs, o_refs = (qf_ref, qb_ref), (lff_ref, lfb_ref), (vf_ref, vb_ref), (of_ref, ob_ref)

    worst = jnp.zeros((1, HG_DIM), F32)
    for d, h in chains:
        for g in range(rows_per_step // HG_FAST_BLOCK):
            block = lf_refs[d][0, h, g * HG_FAST_BLOCK:(g + 1) * HG_FAST_BLOCK, :]
            worst = jnp.minimum(worst, jnp.sum(block, axis=0, keepdims=True))
    mild = jnp.min(worst) >= -HG_FAST_MAX_LOG2

    def run(fast):
        n_sub = rows_per_step // c
        for sub in range(n_sub):
            work = []
            for d, h in chains:
                k = n_sub - 1 - sub if d else sub
                rows = slice(k * c, (k + 1) * c)
                work.append(_hgrn_chunk(q_refs[d].at[0, h, rows, :], lf_refs[d].at[0, h, rows, :],
                                        v_refs[d].at[0, h, rows, :], o_refs[d].at[0, h, rows, :],
                                        s_ref.at[d, h], sm_ref.at[d], pm_ref.at[d], fm_ref.at[d],
                                        b_scr.at[d, h], bool(d), fast))
            for _ in itertools.zip_longest(*work):
                pass

    pl.when(mild)(lambda: run(True))
    pl.when(jnp.logical_not(mild))(lambda: run(False))


def _hgrn(hq, lf_f, lf_b, hi):
    b, nh, n, dk = hq.shape
    rows_per_step = HG_CHUNK * HG_STEP_CHUNKS if n % (HG_CHUNK * HG_STEP_CHUNKS) == 0 else HG_CHUNK
    nc = n // rows_per_step
    blk = (1, nh, rows_per_step, dk)
    fwd = pl.BlockSpec(blk, lambda i, j: (i, 0, j, 0))
    bwd = pl.BlockSpec(blk, lambda i, j: (i, 0, nc - 1 - j, 0))
    out = jax.ShapeDtypeStruct((b, nh, n, dk), F32)
    tables = _hgrn_tables()
    return pl.pallas_call(
        _hgrn_kernel,
        grid=(b, nc),
        in_specs=[fwd, fwd, fwd, bwd, bwd, bwd] + [_const_spec(t.shape) for t in tables],
        out_specs=[fwd, bwd],
        out_shape=[out, out],
        scratch_shapes=[pltpu.VMEM((2, nh, dk, dk), F32), pltpu.VMEM((2, nh, HG_CHUNK, dk), F32)],
        compiler_params=_params("parallel", "arbitrary"),
        name="hgrn",
    )(hq, lf_f, hi, hq, lf_b, hi, jnp.asarray(tables[0], BF16), jnp.asarray(tables[1]), jnp.asarray(tables[2]))


def _mix_cross_kernel(x_ref, att_ref, of_ref, ob_ref, sg_ref, og_ref, wo_ref, pmg_ref, pxg_ref,
                      wq_ref, km_ref, vm_ref, wxo_ref, poxg_ref, pfg_ref, x2_ref, h3_ref):
    tm = x_ref.shape[1]
    part = tm // MIX_ROW_PARTS

    def rows_of(lo):
        rows = slice(lo, lo + part)
        rec = []
        for i in range(HG_HEADS):
            o = _rms(of_ref[0, i, rows, :] + ob_ref[0, i, rows, :], og_ref[...])
            rec.append((o * sg_ref[0, rows, i * HG_DIM:(i + 1) * HG_DIM]).astype(BF16))
        att_t = att_ref[0, :, :, rows].reshape(ATT_Q_DIM, part)
        mixed = _dot_tn(att_t, wo_ref[:ATT_Q_DIM, :]) + _dot(jnp.concatenate(rec, axis=-1), wo_ref[ATT_Q_DIM:, :])
        yield
        x1 = x_ref[0, rows, :] + _rms(mixed, pmg_ref[...])
        h2 = _rms(x1, pxg_ref[...]).astype(BF16)
        d = h2.shape[-1]
        dh = d // X_HEADS
        q = _dot(h2, wq_ref[...]) * (dh ** -0.5)
        yield
        heads = []
        for i in range(X_HEADS):
            sl = slice(i * dh, (i + 1) * dh)
            s = _dot_nt(q[:, sl].astype(BF16), km_ref[0, :, sl])
            p = jnp.exp(s - jnp.max(s, axis=-1, keepdims=True))
            l = jnp.sum(p, axis=-1, keepdims=True)
            heads.append((_dot(p.astype(BF16), vm_ref[0, :, sl]) / l).astype(BF16))
        yield
        xo = _dot(jnp.concatenate(heads, axis=-1), wxo_ref[...])
        yield
        x2 = x1 + _rms(xo, poxg_ref[...])
        x2_ref[0, rows, :] = x2
        h3_ref[0, rows, :] = _rms(x2, pfg_ref[...]).astype(BF16)

    for _ in itertools.zip_longest(*[rows_of(j * part) for j in range(MIX_ROW_PARTS)]):
        pass


def _mix_cross(x, att, o_f, o_b, sg, og, w_out, pmg, pxg, w_xq, k_mem, v_mem, w_xo, poxg, pfg, tm):
    b, n, d = x.shape
    nm = k_mem.shape[1]
    tok = lambda i, j: (i, j, 0)
    head = lambda i, j: (i, 0, j, 0)
    batch = lambda i, j: (i, 0, 0)
    vec = _const_spec((1, d))
    mat = _const_spec((d, d))
    return pl.pallas_call(
        _mix_cross_kernel,
        grid=(b, n // tm),
        in_specs=[pl.BlockSpec((1, tm, d), tok),
                  pl.BlockSpec((1, ATT_HEADS, ATT_HEAD_DIM, tm), lambda i, j: (i, 0, 0, j)),
                  pl.BlockSpec((1, HG_HEADS, tm, HG_DIM), head), pl.BlockSpec((1, HG_HEADS, tm, HG_DIM), head),
                  pl.BlockSpec((1, tm, HG_WIDTH), tok), _const_spec((1, HG_DIM)),
                  mat, vec, vec, mat,
                  pl.BlockSpec((1, nm, d), batch), pl.BlockSpec((1, nm, d), batch),
                  mat, vec, vec],
        out_specs=[pl.BlockSpec((1, tm, d), tok), pl.BlockSpec((1, tm, d), tok)],
        out_shape=[jax.ShapeDtypeStruct((b, n, d), F32), jax.ShapeDtypeStruct((b, n, d), BF16)],
        compiler_params=_params("parallel", "parallel"),
        name="mix_cross",
    )(x, att, o_f, o_b, sg, og, w_out, pmg, pxg, w_xq, k_mem, v_mem, w_xo, poxg, pfg)


def _conv_ffn_kernel(x_ref, h_ref, hp_ref, hn_ref, wu_ref, cw_ref, cb_ref, wd_ref, g_ref, o_ref, act_scr):
    j = pl.program_id(1)
    tm = h_ref.shape[1]
    d_ff = wd_ref.shape[0]
    ck = FF_CHUNK
    prev = jnp.where(j > 0, hp_ref[0], jnp.zeros_like(hp_ref[0]))
    nxt = jnp.where(j < pl.num_programs(1) - 1, hn_ref[0], jnp.zeros_like(hn_ref[0]))
    hext = jnp.concatenate([prev, h_ref[0], nxt], axis=0)

    def conv(col):
        u = _dot(hext, wu_ref[:, col:col + ck])
        taps = (pltpu.roll(u, 1, 0), u, pltpu.roll(u, u.shape[0] - 1, 0))
        out = cb_ref[:, col:col + ck]
        for t in range(CONV_W):
            out = out + taps[t][HALO:HALO + tm] * cw_ref[t:t + 1, col:col + ck]
        return out

    for c in range(d_ff // ck):
        act_scr[:, c * ck:(c + 1) * ck] = (_silu(conv(c * ck)) * conv(d_ff + c * ck)).astype(BF16)
    o_ref[0] = x_ref[0] + _rms(_dot(act_scr[...], wd_ref[...]), g_ref[...])


def _conv_ffn(x2, h3, w_up, conv_w, conv_b, wd, g, tm):
    b, n, d = x2.shape
    d_ff = wd.shape[0]
    ck = FF_CHUNK
    hb = tm // HALO
    last = n // HALO - 1
    tok = lambda i, j: (i, j, 0)
    full = lambda shape: pl.BlockSpec(shape, lambda *_: (0,) * len(shape), pipeline_mode=pl.Buffered(1))
    return pl.pallas_call(
        _conv_ffn_kernel,
        grid=(b, n // tm),
        in_specs=[pl.BlockSpec((1, tm, d), tok), pl.BlockSpec((1, tm, d), tok),
                  pl.BlockSpec((1, HALO, d), lambda i, j: (i, jnp.maximum(j * hb - 1, 0), 0)),
                  pl.BlockSpec((1, HALO, d), lambda i, j: (i, jnp.minimum((j + 1) * hb, last), 0)),
                  full((d, 2 * d_ff)), full((CONV_W, 2 * d_ff)), full((1, 2 * d_ff)),
                  full((d_ff, d)), _const_spec((1, d))],
        out_specs=pl.BlockSpec((1, tm, d), tok),
        out_shape=jax.ShapeDtypeStruct((b, n, d), F32),
        scratch_shapes=[pltpu.VMEM((tm, d_ff), BF16)],
        compiler_params=_params("parallel", "parallel"),
        name="conv_ffn",
    )(x2, h3, h3, h3, w_up, conv_w, conv_b, wd, g)


def _rope_tables(n):
    pairs = ATT_HEAD_DIM // 4
    pos = jnp.arange(n)
    inv = jnp.power(ROPE_THETA, -jnp.arange(pairs, dtype=F32) / pairs)
    ang = jnp.concatenate([(pos // GRID_W).astype(F32)[:, None] * inv,
                           (pos % GRID_W).astype(F32)[:, None] * inv], axis=-1)
    cos = jnp.repeat(jnp.cos(ang), 2, axis=-1)
    sin = jnp.repeat(jnp.sin(ang), 2, axis=-1) * jnp.tile(jnp.array([-1.0, 1.0], F32), ATT_HEAD_DIM // 2)
    return jnp.tile(cos, (1, 2)), jnp.tile(sin, (1, 2))


def _segment_ones(width):
    i = jnp.arange(width) // ATT_HEAD_DIM
    return (i[:, None] == i[None, :]).astype(BF16)


def _layer(x, mem, lb, pre_mix_g, w_in, q_norm_g, k_norm_g, hg_out_norm_g, w_out, post_mix_g, pre_x_g,
           mem_norm_g, w_xq, w_xkv, w_xo, post_x_g, pre_ffn_g, w_up, conv_w, conv_b, w_down, post_ffn_g):
    b, n, d = x.shape
    d_ff = w_down.shape[0]
    assert n % HG_CHUNK == 0 and n % GRID_W == 0 and d_ff % FF_CHUNK == 0
    tm = min(256, n)
    tf = min(512, n)
    row = lambda g: g.reshape(1, -1).astype(F32)

    cos, sin = _rope_tables(n)
    qg = jnp.tile(q_norm_g, ATT_HEADS).reshape(1, -1) * (ATT_HEAD_DIM ** -0.5 * LOG2E)
    kg = jnp.tile(k_norm_g, ATT_KV_HEADS).reshape(1, -1)

    k_mem, v_mem = _mem_proj(mem, row(mem_norm_g), w_xkv.astype(BF16))
    q, k, v_t, hq, lf_f, lf_b, hi, sg = _in_proj(
        x, row(pre_mix_g), w_in.astype(BF16), qg, kg, cos, sin, lb,
        _segment_ones(ATT_Q_DIM), _segment_ones(ATT_KV_DIM), tf)
    att = _attention(q, k, v_t, tm)
    o_f, o_b = _hgrn(hq, lf_f, lf_b, hi)
    x2, h3 = _mix_cross(x, att, o_f, o_b, sg, row(hg_out_norm_g), w_out.astype(BF16), row(post_mix_g),
                        row(pre_x_g), w_xq.astype(BF16), k_mem, v_mem, w_xo.astype(BF16),
                        row(post_x_g), row(pre_ffn_g), tf)

    return _conv_ffn(x2, h3, w_up.astype(BF16), conv_w, row(conv_b), w_down.astype(BF16), row(post_ffn_g), tf)


def kernel(x, mem, pre_mix_g, w_in, q_norm_g, k_norm_g, hg_lb, hg_out_norm_g, w_out, post_mix_g, pre_x_g,
           mem_norm_g, w_xq, w_xkv, w_xo, post_x_g, pre_ffn_g, w_up, conv_w, conv_b, w_down, post_ffn_g):
    lb_all = jnp.cumsum(jax.nn.softmax(hg_lb.astype(F32), axis=1), axis=1)
    for l in range(w_in.shape[0]):
        x = _layer(x, mem, lb_all[:, l], pre_mix_g[l], w_in[l], q_norm_g[l], k_norm_g[l], hg_out_norm_g[l],
                   w_out[l], post_mix_g[l], pre_x_g[l], mem_norm_g[l], w_xq[l], w_xkv[l], w_xo[l],
                   post_x_g[l], pre_ffn_g[l], w_up[l], conv_w[l], conv_b[l], w_down[l], post_ffn_g[l])
    return x
```

```python
import functools
import itertools
import math

import numpy as np
import jax
import jax.numpy as jnp
from jax import lax
from jax.experimental import pallas as pl
from jax.experimental.pallas import tpu as pltpu

F32 = jnp.float32
BF16 = jnp.bfloat16

EPS = 1e-6
LOG2E = math.log2(math.e)
GRID_W = 64
ROPE_THETA = 10000.0

ATT_HEADS = 8
ATT_KV_HEADS = 2
ATT_GROUP = ATT_HEADS // ATT_KV_HEADS
ATT_HEAD_DIM = 64
ATT_Q_DIM = ATT_HEADS * ATT_HEAD_DIM
ATT_KV_DIM = ATT_KV_HEADS * ATT_HEAD_DIM
ATT_KEY_SPLITS = 2

HG_HEADS = 4
HG_DIM = 128
HG_WIDTH = HG_HEADS * HG_DIM
HG_CHUNK = 128
HG_STEP_CHUNKS = 4
HG_FAST_BLOCK = 32
HG_FAST_MAX_LOG2 = 96.0

X_HEADS = 4
IN_ROW_PARTS = 2
MIX_ROW_PARTS = 2
CONV_W = 3
FF_CHUNK = 256
HALO = 16

VMEM_LIMIT = 56 * 1024 * 1024
SUBLANES = 8
LANES = 128


def _params(*sem):
    return pltpu.CompilerParams(dimension_semantics=sem, vmem_limit_bytes=VMEM_LIMIT)


def _rms(x, g):
    ms = jnp.mean(x * x, axis=-1, keepdims=True)
    return x * lax.rsqrt(ms + EPS) * g


def _silu(x):
    return x / (1.0 + jnp.exp(-x))


def _dot(a, b):
    return jnp.dot(a, b, preferred_element_type=F32)


def _dot_nt(a, b):
    return lax.dot_general(a, b, (((1,), (1,)), ((), ())), preferred_element_type=F32)


def _dot_tn(a, b):
    return lax.dot_general(a, b, (((0,), (0,)), ((), ())), preferred_element_type=F32)


def _const_spec(shape):
    return pl.BlockSpec(shape, lambda *_: (0,) * len(shape))


def _resident_spec(shape):
    return pl.BlockSpec(shape, lambda *_: (0,) * len(shape), pipeline_mode=pl.Buffered(1))


def _mem_proj_kernel(m_ref, g_ref, w_ref, k_ref, v_ref):
    d = m_ref.shape[-1]
    m = _rms(m_ref[0], g_ref[...]).astype(BF16)
    kv = _dot(m, w_ref[...].astype(BF16))
    k_ref[0] = kv[:, :d].astype(BF16)
    v_ref[0] = kv[:, d:].astype(BF16)


def _mem_proj(mem, g, w_xkv):
    b, nm, d = mem.shape
    out = jax.ShapeDtypeStruct((b, nm, d), BF16)
    return pl.pallas_call(
        _mem_proj_kernel,
        grid=(b,),
        in_specs=[pl.BlockSpec((1, nm, d), lambda i: (i, 0, 0)),
                  _const_spec((1, d)), _resident_spec((d, 2 * d))],
        out_specs=[pl.BlockSpec((1, nm, d), lambda i: (i, 0, 0))] * 2,
        out_shape=[out, out],
        compiler_params=_params("parallel"),
        name="mem_proj",
    )(mem, g, w_xkv)


def _head_norm_rope(a, seg, gain, cos, sin_signed):
    w = a.shape[-1]
    sq = a * a
    hi = sq.astype(BF16)
    lo = (sq - hi.astype(F32)).astype(BF16)
    ss = [_dot(jnp.concatenate([hi[:, j:j + LANES], lo[:, j:j + LANES]], axis=-1), seg) for j in range(0, w, LANES)]
    ss = jnp.concatenate(ss, axis=-1) if len(ss) > 1 else ss[0]
    an = a * lax.rsqrt(ss * (1.0 / ATT_HEAD_DIM) + EPS) * gain
    lane = lax.broadcasted_iota(jnp.int32, a.shape, 1)
    partner = jnp.where(lane % 2 == 0, pltpu.roll(an, w - 1, 1), pltpu.roll(an, 1, 1))
    return an * cos + partner * sin_signed


def _in_proj_kernel(x_ref, g_ref, w_ref, qg_ref, kg_ref, cos_ref, sin_ref, lb_ref, seg_ref,
                    q_ref, k_ref, vt_ref, hq_ref, lff_ref, lfb_ref, hi_ref, hg_ref):
    part = x_ref.shape[1] // IN_ROW_PARTS

    def rows_of(r0):
        rows = slice(r0, r0 + part)
        h = _rms(x_ref[0, rows, :], g_ref[...]).astype(BF16)

        def proj(lo, width):
            return _dot(h, w_ref[:, lo:lo + width].astype(BF16))

        def heads_out(ref, val):
            for i in range(HG_HEADS):
                ref[0, i, rows, :] = val[:, i * HG_DIM:(i + 1) * HG_DIM].astype(ref.dtype)

        cos = cos_ref[rows, :]
        sin = sin_ref[rows, :]
        c0 = 0
        q = _head_norm_rope(proj(c0, ATT_Q_DIM), seg_ref[...], qg_ref[...],
                            jnp.concatenate([cos] * (ATT_Q_DIM // 128), axis=-1),
                            jnp.concatenate([sin] * (ATT_Q_DIM // 128), axis=-1))
        for i in range(ATT_HEADS):
            q_ref[0, i, rows, :] = q[:, i * ATT_HEAD_DIM:(i + 1) * ATT_HEAD_DIM].astype(BF16)
        c0 += ATT_Q_DIM
        yield
        k = _head_norm_rope(proj(c0, ATT_KV_DIM), seg_ref[...], kg_ref[...], cos, sin)
        c0 += ATT_KV_DIM
        v_t = jnp.transpose(proj(c0, ATT_KV_DIM))
        c0 += ATT_KV_DIM
        for i in range(ATT_KV_HEADS):
            k_ref[0, i, rows, :] = k[:, i * ATT_HEAD_DIM:(i + 1) * ATT_HEAD_DIM].astype(BF16)
            vt_ref[0, i, :, rows] = v_t[i * ATT_HEAD_DIM:(i + 1) * ATT_HEAD_DIM, :].astype(BF16)
        yield
        heads_out(hq_ref, _silu(proj(c0, HG_WIDTH)))
        c0 += HG_WIDTH
        yield
        for d, ref in enumerate((lff_ref, lfb_ref)):
            lb = lb_ref[d:d + 1, :]
            z = proj(c0, HG_WIDTH)
            heads_out(ref, jnp.log(lb + (1.0 - lb) / (1.0 + jnp.exp(-z))) * LOG2E)
            c0 += HG_WIDTH
            yield
        heads_out(hi_ref, proj(c0, HG_WIDTH))
        c0 += HG_WIDTH
        yield
        hg_ref[0, rows, :] = _silu(proj(c0, HG_WIDTH))

    for _ in itertools.zip_longest(*[rows_of(j * part) for j in range(IN_ROW_PARTS)]):
        pass


def _in_proj(x, g, w_in, qg, kg, cos, sin, lb, seg, tm):
    b, n, d = x.shape
    nt = n // tm
    n_in = w_in.shape[1]
    tok = lambda i, j: (i, j, 0)
    head = lambda i, j: (i, 0, j, 0)
    pos = lambda i, j: (j, 0)
    return pl.pallas_call(
        _in_proj_kernel,
        grid=(b, nt),
        in_specs=[pl.BlockSpec((1, tm, d), tok), _const_spec((1, d)), _resident_spec((d, n_in)),
                  _const_spec((1, ATT_Q_DIM)), _const_spec((1, ATT_KV_DIM)),
                  pl.BlockSpec((tm, 128), pos), pl.BlockSpec((tm, 128), pos),
                  _const_spec((2, HG_WIDTH)),
                  _const_spec((2 * LANES, LANES))],
        out_specs=[pl.BlockSpec((1, ATT_HEADS, tm, ATT_HEAD_DIM), head),
                   pl.BlockSpec((1, ATT_KV_HEADS, tm, ATT_HEAD_DIM), head),
                   pl.BlockSpec((1, ATT_KV_HEADS, ATT_HEAD_DIM, tm), lambda i, j: (i, 0, 0, j)),
                   pl.BlockSpec((1, HG_HEADS, tm, HG_DIM), head),
                   pl.BlockSpec((1, HG_HEADS, tm, HG_DIM), head),
                   pl.BlockSpec((1, HG_HEADS, tm, HG_DIM), head),
                   pl.BlockSpec((1, HG_HEADS, tm, HG_DIM), head),
                   pl.BlockSpec((1, tm, HG_WIDTH), tok)],
        out_shape=[jax.ShapeDtypeStruct((b, ATT_HEADS, n, ATT_HEAD_DIM), BF16),
                   jax.ShapeDtypeStruct((b, ATT_KV_HEADS, n, ATT_HEAD_DIM), BF16),
                   jax.ShapeDtypeStruct((b, ATT_KV_HEADS, ATT_HEAD_DIM, n), BF16),
                   jax.ShapeDtypeStruct((b, HG_HEADS, n, HG_DIM), BF16),
                   jax.ShapeDtypeStruct((b, HG_HEADS, n, HG_DIM), F32),
                   jax.ShapeDtypeStruct((b, HG_HEADS, n, HG_DIM), F32),
                   jax.ShapeDtypeStruct((b, HG_HEADS, n, HG_DIM), BF16),
                   jax.ShapeDtypeStruct((b, n, HG_WIDTH), F32)],
        compiler_params=_params("parallel", "parallel"),
        name="in_proj",
    )(x, g, w_in, qg, kg, cos, sin, lb, seg)


def _attention_kernel(q_ref, qn_ref, k_ref, vt_ref, o_ref, s_even, s_odd):
    n = k_ref.shape[2]
    part = n // ATT_KEY_SPLITS
    parts = [slice(j * part, (j + 1) * part) for j in range(ATT_KEY_SPLITS)]

    def scores(i, s_scr, queries=q_ref):
        for rows in parts:
            s_scr[rows, :] = _dot_nt(k_ref[0, i // ATT_GROUP, rows, :], queries[0, i])

    def finish(i, s_scr):
        s = [s_scr[rows, :] for rows in parts]
        m = functools.reduce(jnp.maximum, [jnp.max(x, axis=0, keepdims=True) for x in s])
        p = [jnp.exp2(x - m) for x in s]
        l = sum(jnp.sum(x, axis=0, keepdims=True) for x in p)
        o_t = sum(_dot(vt_ref[0, i // ATT_GROUP, :, rows], x.astype(BF16)) for rows, x in zip(parts, p))
        o_ref[0, i] = (o_t / l).astype(BF16)

    @pl.when(pl.program_id(1) == 0)
    def _():
        scores(0, s_even)

    def pair(j, carry):
        scores(2 * j + 1, s_odd)
        finish(2 * j, s_even)
        scores(2 * j + 2, s_even)
        finish(2 * j + 1, s_odd)
        return carry

    lax.fori_loop(0, ATT_HEADS // 2 - 1, pair, 0)
    scores(ATT_HEADS - 1, s_odd)
    finish(ATT_HEADS - 2, s_even)
    scores(0, s_even, qn_ref)
    finish(ATT_HEADS - 1, s_odd)


def _attention(q, k, v_t, tq):
    b, _, n, dh = q.shape
    last = n // tq - 1
    return pl.pallas_call(
        _attention_kernel,
        grid=(b, n // tq),
        in_specs=[pl.BlockSpec((1, ATT_HEADS, tq, dh), lambda i, j: (i, 0, j, 0)),
                  pl.BlockSpec((1, ATT_HEADS, tq, dh), lambda i, j: (i, 0, jnp.minimum(j + 1, last), 0)),
                  pl.BlockSpec((1, ATT_KV_HEADS, n, dh), lambda i, j: (i, 0, 0, 0)),
                  pl.BlockSpec((1, ATT_KV_HEADS, dh, n), lambda i, j: (i, 0, 0, 0))],
        out_specs=pl.BlockSpec((1, ATT_HEADS, dh, tq), lambda i, j: (i, 0, 0, j)),
        out_shape=jax.ShapeDtypeStruct((b, ATT_HEADS, dh, n), BF16),
        scratch_shapes=[pltpu.VMEM((n, tq), F32), pltpu.VMEM((n, tq), F32)],
        compiler_params=_params("parallel", "arbitrary"),
        name="attention",
    )(q, q, k, v_t)


def _hgrn_levels():
    m, out = 1, []
    while m < HG_CHUNK:
        out.append(m)
        m *= 2
    return out


def _hgrn_tables():
    c = HG_CHUNK
    t = np.arange(c)[:, None]
    u = np.arange(c)[None, :]
    mats, masks, fast = [], [], []
    for reverse in (False, True):
        blocks, pairs = [], [t == u]
        for m in _hgrn_levels():
            base = (t // (2 * m)) * (2 * m)
            if not reverse:
                r = base + m - 1
                block = np.where(t > r, (u > r) & (u <= t), (u > t) & (u <= r))
                roles = ((t // m) % 2 == 1) & ((u // m) % 2 == 0)
            else:
                r = base + m
                block = np.where(t < r, (u >= t) & (u < r), (u >= r) & (u < t))
                roles = ((t // m) % 2 == 0) & ((u // m) % 2 == 1)
            if m < SUBLANES:
                blocks.append(block)
            pairs.append(roles & (t // (2 * m) == u // (2 * m)))
        blocks.append(u >= t if reverse else u <= t)
        mats.append(np.tile(np.concatenate(blocks, axis=0), (1, 2)))
        masks.append(np.stack(pairs))
        fast.append((t // HG_FAST_BLOCK == u // HG_FAST_BLOCK) & (u >= t if reverse else u <= t))
    return tuple(np.stack(x).astype(np.float32) for x in (mats, masks, fast))


def _hgrn_sums(lf2, sum_mat):
    hi = lf2.astype(BF16)
    mid = (lf2 - hi.astype(F32)).astype(BF16)
    return _dot(sum_mat, jnp.concatenate([hi, mid], axis=0))


def _hgrn_chunk(q_ref, lf_ref, v_ref, o_ref, state_ref, sum_mat, pair_mask, fast_mask, b_scr, reverse, fast):
    c = HG_CHUNK
    levels = list(enumerate(_hgrn_levels()))
    n_small = sum(m < SUBLANES for _, m in levels)
    q, lf2, v, state = q_ref[...], lf_ref[...], v_ref[...], state_ref[...]
    qb = q.astype(BF16)
    kb = (1.0 - jnp.exp2(lf2)).astype(BF16)
    if fast:
        b = _hgrn_sums(lf2, sum_mat[n_small * c:, :])
        yield
        b_scr[...] = b
        parts = []
        for g in range(c // HG_FAST_BLOCK):
            lo, up = g * HG_FAST_BLOCK, (g + 1) * HG_FAST_BLOCK
            before = up if reverse else lo - 1
            parts.append(b[lo:up] - b_scr[before:before + 1, :] if 0 <= before < c else b[lo:up])
        loc = jnp.concatenate(parts, axis=0)
        a = _dot_nt(qb * jnp.exp2(loc).astype(BF16), kb * jnp.exp2(-loc).astype(BF16)) * fast_mask[...]
        yield
        levels = [(j, m) for j, m in levels if m >= HG_FAST_BLOCK]
    else:
        sums = _hgrn_sums(lf2, sum_mat[...])
        yield
        b = sums[n_small * c:]
        b_scr[...] = b
        a = _dot_nt(qb, kb) * pair_mask[0]
        yield
    for j, m in levels:
        if m < SUBLANES:
            e = sums[j * c:(j + 1) * c]
        else:
            parts = []
            for g in range(c // (2 * m)):
                lo, mid_row, up = 2 * m * g, 2 * m * g + m, 2 * m * (g + 1)
                r = mid_row if reverse else mid_row - 1
                b_r = b_scr[r:r + 1, :]
                parts += [b[lo:mid_row] - b_r, b_r - b[mid_row:up]] if reverse else \
                         [b_r - b[lo:mid_row], b[mid_row:up] - b_r]
            e = jnp.concatenate(parts, axis=0)
        fac = jnp.exp2(e).astype(BF16)
        a = a + _dot_nt(qb * fac, kb * fac) * pair_mask[j + 1]
        yield
    vb = v.astype(BF16)
    last = 0 if reverse else c - 1
    total = b_scr[last:last + 1, :]
    o_ref[...] = _dot(a.astype(BF16), vb) + _dot_nt(qb * jnp.exp2(b).astype(BF16), state.astype(BF16))
    yield
    k_dec = kb * jnp.exp2(total - b).astype(BF16)
    state_ref[...] = state * jnp.exp2(total) + _dot_tn(vb, k_dec)


def _hgrn_kernel(qf_ref, lff_ref, vf_ref, qb_ref, lfb_ref, vb_ref, sm_ref, pm_ref, fm_ref, of_ref, ob_ref,
                 s_ref, b_scr):
    @pl.when(pl.program_id(1) == 0)
    def _():
        s_ref[...] = jnp.zeros_like(s_ref)

    c = HG_CHUNK
    rows_per_step = qf_ref.shape[2]
    chains = [(d, h) for h in range(HG_HEADS) for d in (0, 1)]
    q_refs, lf_refs, v_refs, o_refs = (qf_ref, qb_ref), (lff_ref, lfb_ref), (vf_ref, vb_ref), (of_ref, ob_ref)

    worst = jnp.zeros((1, HG_DIM), F32)
    for d, h in chains:
        for g in range(rows_per_step // HG_FAST_BLOCK):
            block = lf_refs[d][0, h, g * HG_FAST_BLOCK:(g + 1) * HG_FAST_BLOCK, :]
            worst = jnp.minimum(worst, jnp.sum(block, axis=0, keepdims=True))
    mild = jnp.min(worst) >= -HG_FAST_MAX_LOG2

    def run(fast):
        n_sub = rows_per_step // c
        for sub in range(n_sub):
            work = []
            for d, h in chains:
                k = n_sub - 1 - sub if d else sub
                rows = slice(k * c, (k + 1) * c)
                work.append(_hgrn_chunk(q_refs[d].at[0, h, rows, :], lf_refs[d].at[0, h, rows, :],
                                        v_refs[d].at[0, h, rows, :], o_refs[d].at[0, h, rows, :],
                                        s_ref.at[d, h], sm_ref.at[d], pm_ref.at[d], fm_ref.at[d],
                                        b_scr.at[d, h], bool(d), fast))
            for _ in itertools.zip_longest(*work):
                pass

    pl.when(mild)(lambda: run(True))
    pl.when(jnp.logical_not(mild))(lambda: run(False))


def _hgrn(hq, lf_f, lf_b, hi):
    b, nh, n, dk = hq.shape
    rows_per_step = HG_CHUNK * HG_STEP_CHUNKS if n % (HG_CHUNK * HG_STEP_CHUNKS) == 0 else HG_CHUNK
    nc = n // rows_per_step
    blk = (1, nh, rows_per_step, dk)
    fwd = pl.BlockSpec(blk, lambda i, j: (i, 0, j, 0))
    bwd = pl.BlockSpec(blk, lambda i, j: (i, 0, nc - 1 - j, 0))
    out = jax.ShapeDtypeStruct((b, nh, n, dk), F32)
    tables = _hgrn_tables()
    return pl.pallas_call(
        _hgrn_kernel,
        grid=(b, nc),
        in_specs=[fwd, fwd, fwd, bwd, bwd, bwd] + [_const_spec(t.shape) for t in tables],
        out_specs=[fwd, bwd],
        out_shape=[out, out],
        scratch_shapes=[pltpu.VMEM((2, nh, dk, dk), F32), pltpu.VMEM((2, nh, HG_CHUNK, dk), F32)],
        compiler_params=_params("parallel", "arbitrary"),
        name="hgrn",
    )(hq, lf_f, hi, hq, lf_b, hi, jnp.asarray(tables[0], BF16), jnp.asarray(tables[1]), jnp.asarray(tables[2]))


def _mix_cross_kernel(x_ref, att_ref, of_ref, ob_ref, sg_ref, og_ref, wo_ref, pmg_ref, pxg_ref,
                      wq_ref, km_ref, vm_ref, wxo_ref, poxg_ref, pfg_ref, x2_ref, h3_ref):
    tm = x_ref.shape[1]
    part = tm // MIX_ROW_PARTS

    def rows_of(lo):
        rows = slice(lo, lo + part)
        rec = []
        for i in range(HG_HEADS):
            o = _rms(of_ref[0, i, rows, :] + ob_ref[0, i, rows, :], og_ref[...])
            rec.append((o * sg_ref[0, rows, i * HG_DIM:(i + 1) * HG_DIM]).astype(BF16))
        att_t = att_ref[0, :, :, rows].reshape(ATT_Q_DIM, part)
        mixed = _dot_tn(att_t, wo_ref[:ATT_Q_DIM, :].astype(BF16)) + \
            _dot(jnp.concatenate(rec, axis=-1), wo_ref[ATT_Q_DIM:, :].astype(BF16))
        yield
        x1 = x_ref[0, rows, :] + _rms(mixed, pmg_ref[...])
        h2 = _rms(x1, pxg_ref[...]).astype(BF16)
        d = h2.shape[-1]
        dh = d // X_HEADS
        q = _dot(h2, wq_ref[...].astype(BF16)) * (dh ** -0.5)
        yield
        heads = []
        for i in range(X_HEADS):
            sl = slice(i * dh, (i + 1) * dh)
            s = _dot_nt(q[:, sl].astype(BF16), km_ref[0, :, sl])
            p = jnp.exp(s - jnp.max(s, axis=-1, keepdims=True))
            l = jnp.sum(p, axis=-1, keepdims=True)
            heads.append((_dot(p.astype(BF16), vm_ref[0, :, sl]) / l).astype(BF16))
        yield
        xo = _dot(jnp.concatenate(heads, axis=-1), wxo_ref[...].astype(BF16))
        yield
        x2 = x1 + _rms(xo, poxg_ref[...])
        x2_ref[0, rows, :] = x2
        h3_ref[0, rows, :] = _rms(x2, pfg_ref[...]).astype(BF16)

    for _ in itertools.zip_longest(*[rows_of(j * part) for j in range(MIX_ROW_PARTS)]):
        pass


def _mix_cross(x, att, o_f, o_b, sg, og, w_out, pmg, pxg, w_xq, k_mem, v_mem, w_xo, poxg, pfg, tm):
    b, n, d = x.shape
    nm = k_mem.shape[1]
    tok = lambda i, j: (i, j, 0)
    head = lambda i, j: (i, 0, j, 0)
    batch = lambda i, j: (i, 0, 0)
    vec = _const_spec((1, d))
    mat = _resident_spec((d, d))
    return pl.pallas_call(
        _mix_cross_kernel,
        grid=(b, n // tm),
        in_specs=[pl.BlockSpec((1, tm, d), tok),
                  pl.BlockSpec((1, ATT_HEADS, ATT_HEAD_DIM, tm), lambda i, j: (i, 0, 0, j)),
                  pl.BlockSpec((1, HG_HEADS, tm, HG_DIM), head), pl.BlockSpec((1, HG_HEADS, tm, HG_DIM), head),
                  pl.BlockSpec((1, tm, HG_WIDTH), tok), _const_spec((1, HG_DIM)),
                  mat, vec, vec, mat,
                  pl.BlockSpec((1, nm, d), batch), pl.BlockSpec((1, nm, d), batch),
                  mat, vec, vec],
        out_specs=[pl.BlockSpec((1, tm, d), tok), pl.BlockSpec((1, tm, d), tok)],
        out_shape=[jax.ShapeDtypeStruct((b, n, d), F32), jax.ShapeDtypeStruct((b, n, d), BF16)],
        compiler_params=_params("parallel", "parallel"),
        name="mix_cross",
    )(x, att, o_f, o_b, sg, og, w_out, pmg, pxg, w_xq, k_mem, v_mem, w_xo, poxg, pfg)


def _conv_ffn_kernel(x_ref, h_ref, hp_ref, hn_ref, wu_ref, cw_ref, cb_ref, wd_ref, g_ref, o_ref, act_scr):
    j = pl.program_id(1)
    tm = h_ref.shape[1]
    d_ff = wd_ref.shape[0]
    ck = FF_CHUNK
    prev = jnp.where(j > 0, hp_ref[0], jnp.zeros_like(hp_ref[0]))
    nxt = jnp.where(j < pl.num_programs(1) - 1, hn_ref[0], jnp.zeros_like(hn_ref[0]))
    hext = jnp.concatenate([prev, h_ref[0], nxt], axis=0)

    def conv(col):
        u = _dot(hext, wu_ref[:, col:col + ck])
        taps = (pltpu.roll(u, 1, 0), u, pltpu.roll(u, u.shape[0] - 1, 0))
        out = cb_ref[:, col:col + ck]
        for t in range(CONV_W):
            out = out + taps[t][HALO:HALO + tm] * cw_ref[t:t + 1, col:col + ck]
        return out

    for c in range(d_ff // ck):
        act_scr[:, c * ck:(c + 1) * ck] = (_silu(conv(c * ck)) * conv(d_ff + c * ck)).astype(BF16)
    o_ref[0] = x_ref[0] + _rms(_dot(act_scr[...], wd_ref[...]), g_ref[...])


def _conv_ffn(x2, h3, w_up, conv_w, conv_b, wd, g, tm):
    b, n, d = x2.shape
    d_ff = wd.shape[0]
    ck = FF_CHUNK
    hb = tm // HALO
    last = n // HALO - 1
    tok = lambda i, j: (i, j, 0)
    full = _resident_spec
    return pl.pallas_call(
        _conv_ffn_kernel,
        grid=(b, n // tm),
        in_specs=[pl.BlockSpec((1, tm, d), tok), pl.BlockSpec((1, tm, d), tok),
                  pl.BlockSpec((1, HALO, d), lambda i, j: (i, jnp.maximum(j * hb - 1, 0), 0)),
                  pl.BlockSpec((1, HALO, d), lambda i, j: (i, jnp.minimum((j + 1) * hb, last), 0)),
                  full((d, 2 * d_ff)), full((CONV_W, 2 * d_ff)), full((1, 2 * d_ff)),
                  full((d_ff, d)), _const_spec((1, d))],
        out_specs=pl.BlockSpec((1, tm, d), tok),
        out_shape=jax.ShapeDtypeStruct((b, n, d), F32),
        scratch_shapes=[pltpu.VMEM((tm, d_ff), BF16)],
        compiler_params=_params("parallel", "parallel"),
        name="conv_ffn",
    )(x2, h3, h3, h3, w_up, conv_w, conv_b, wd, g)


def _rope_tables(n):
    pairs = ATT_HEAD_DIM // 4
    pos = jnp.arange(n)
    inv = jnp.power(ROPE_THETA, -jnp.arange(pairs, dtype=F32) / pairs)
    ang = jnp.concatenate([(pos // GRID_W).astype(F32)[:, None] * inv,
                           (pos % GRID_W).astype(F32)[:, None] * inv], axis=-1)
    cos = jnp.repeat(jnp.cos(ang), 2, axis=-1)
    sin = jnp.repeat(jnp.sin(ang), 2, axis=-1) * jnp.tile(jnp.array([-1.0, 1.0], F32), ATT_HEAD_DIM // 2)
    return jnp.tile(cos, (1, 2)), jnp.tile(sin, (1, 2))


def _segment_ones():
    i = jnp.arange(LANES) // ATT_HEAD_DIM
    return jnp.tile((i[:, None] == i[None, :]).astype(BF16), (2, 1))


def _layer(x, mem, lb, pre_mix_g, w_in, q_norm_g, k_norm_g, hg_out_norm_g, w_out, post_mix_g, pre_x_g,
           mem_norm_g, w_xq, w_xkv, w_xo, post_x_g, pre_ffn_g, w_up, conv_w, conv_b, w_down, post_ffn_g):
    b, n, d = x.shape
    d_ff = w_down.shape[0]
    assert n % HG_CHUNK == 0 and n % GRID_W == 0 and d_ff % FF_CHUNK == 0
    tm = min(256, n)
    tf = min(512, n)
    row = lambda g: g.reshape(1, -1).astype(F32)

    cos, sin = _rope_tables(n)
    qg = jnp.tile(q_norm_g, ATT_HEADS).reshape(1, -1) * (ATT_HEAD_DIM ** -0.5 * LOG2E)
    kg = jnp.tile(k_norm_g, ATT_KV_HEADS).reshape(1, -1)

    k_mem, v_mem = _mem_proj(mem, row(mem_norm_g), w_xkv)
    q, k, v_t, hq, lf_f, lf_b, hi, sg = _in_proj(
        x, row(pre_mix_g), w_in, qg, kg, cos, sin, lb,
        _segment_ones(), tf)
    att = _attention(q, k, v_t, tm)
    o_f, o_b = _hgrn(hq, lf_f, lf_b, hi)
    x2, h3 = _mix_cross(x, att, o_f, o_b, sg, row(hg_out_norm_g), w_out, row(post_mix_g),
                        row(pre_x_g), w_xq, k_mem, v_mem, w_xo,
                        row(post_x_g), row(pre_ffn_g), tf)

    return _conv_ffn(x2, h3, w_up.astype(BF16), conv_w, row(conv_b), w_down.astype(BF16), row(post_ffn_g), tf)


def kernel(x, mem, pre_mix_g, w_in, q_norm_g, k_norm_g, hg_lb, hg_out_norm_g, w_out, post_mix_g, pre_x_g,
           mem_norm_g, w_xq, w_xkv, w_xo, post_x_g, pre_ffn_g, w_up, conv_w, conv_b, w_down, post_ffn_g):
    lb_all = jnp.cumsum(jax.nn.softmax(hg_lb.astype(F32), axis=1), axis=1)
    for l in range(w_in.shape[0]):
        x = _layer(x, mem, lb_all[:, l], pre_mix_g[l], w_in[l], q_norm_g[l], k_norm_g[l], hg_out_norm_g[l],
                   w_out[l], post_mix_g[l], pre_x_g[l], mem_norm_g[l], w_xq[l], w_xkv[l], w_xo[l],
                   post_x_g[l], pre_ffn_g[l], w_up[l], conv_w[l], conv_b[l], w_down[l], post_ffn_g[l])
    return x
```

```python
import functools
import itertools
import math

import numpy as np
import jax
import jax.numpy as jnp
from jax import lax
from jax.experimental import pallas as pl
from jax.experimental.pallas import tpu as pltpu

F32 = jnp.float32
BF16 = jnp.bfloat16

EPS = 1e-6
LOG2E = math.log2(math.e)
GRID_W = 64
ROPE_THETA = 10000.0

ATT_HEADS = 8
ATT_KV_HEADS = 2
ATT_GROUP = ATT_HEADS // ATT_KV_HEADS
ATT_HEAD_DIM = 64
ATT_Q_DIM = ATT_HEADS * ATT_HEAD_DIM
ATT_KV_DIM = ATT_KV_HEADS * ATT_HEAD_DIM
ATT_KEY_SPLITS = 2

HG_HEADS = 4
HG_DIM = 128
HG_WIDTH = HG_HEADS * HG_DIM
HG_CHUNK = 128
HG_STEP_CHUNKS = 4
HG_FAST_BLOCK = 32
HG_FAST_MAX_LOG2 = 96.0

X_HEADS = 4
IN_ROW_PARTS = 2
MIX_ROW_PARTS = 2
CONV_W = 3
FF_CHUNK = 256
HALO = 16

VMEM_LIMIT = 56 * 1024 * 1024
SUBLANES = 8
BF16_SUBLANES = 16
LANES = 128


def _params(*sem):
    return pltpu.CompilerParams(dimension_semantics=sem, vmem_limit_bytes=VMEM_LIMIT)


def _rms(x, g):
    ms = jnp.mean(x * x, axis=-1, keepdims=True)
    return x * lax.rsqrt(ms + EPS) * g


def _silu(x):
    return x / (1.0 + jnp.exp(-x))


def _dot(a, b):
    return jnp.dot(a, b, preferred_element_type=F32)


def _dot_nt(a, b):
    return lax.dot_general(a, b, (((1,), (1,)), ((), ())), preferred_element_type=F32)


def _dot_tn(a, b):
    return lax.dot_general(a, b, (((0,), (0,)), ((), ())), preferred_element_type=F32)


def _const_spec(shape):
    return pl.BlockSpec(shape, lambda *_: (0,) * len(shape))


def _resident_spec(shape):
    return pl.BlockSpec(shape, lambda *_: (0,) * len(shape), pipeline_mode=pl.Buffered(1))


def _mem_proj_kernel(m_ref, g_ref, w_ref, k_ref, v_ref):
    d = m_ref.shape[-1]
    m = _rms(m_ref[0], g_ref[...]).astype(BF16)
    kv = _dot(m, w_ref[...].astype(BF16))
    k_ref[0] = kv[:, :d].astype(BF16)
    v_ref[0] = kv[:, d:].astype(BF16)


def _mem_proj(mem, g, w_xkv):
    b, nm, d = mem.shape
    out = jax.ShapeDtypeStruct((b, nm, d), BF16)
    return pl.pallas_call(
        _mem_proj_kernel,
        grid=(b,),
        in_specs=[pl.BlockSpec((1, nm, d), lambda i: (i, 0, 0)),
                  _const_spec((1, d)), _resident_spec((d, 2 * d))],
        out_specs=[pl.BlockSpec((1, nm, d), lambda i: (i, 0, 0))] * 2,
        out_shape=[out, out],
        compiler_params=_params("parallel"),
        name="mem_proj",
    )(mem, g, w_xkv)


def _head_norm_rope(a, seg, gain, cos, sin_signed):
    w = a.shape[-1]
    sq = a * a
    hi = sq.astype(BF16)
    lo = (sq - hi.astype(F32)).astype(BF16)
    ss = [_dot(jnp.concatenate([hi[:, j:j + LANES], lo[:, j:j + LANES]], axis=-1), seg) for j in range(0, w, LANES)]
    ss = jnp.concatenate(ss, axis=-1) if len(ss) > 1 else ss[0]
    an = a * lax.rsqrt(ss * (1.0 / ATT_HEAD_DIM) + EPS) * gain
    lane = lax.broadcasted_iota(jnp.int32, a.shape, 1)
    partner = jnp.where(lane % 2 == 0, pltpu.roll(an, w - 1, 1), pltpu.roll(an, 1, 1))
    return an * cos + partner * sin_signed


def _in_proj_kernel(x_ref, g_ref, w_ref, qg_ref, kg_ref, cos_ref, sin_ref, lb_ref, seg_ref,
                    q_ref, k_ref, vt_ref, hq_ref, lff_ref, lfb_ref, hi_ref, hg_ref):
    part = x_ref.shape[1] // IN_ROW_PARTS

    def rows_of(r0):
        rows = slice(r0, r0 + part)
        h = _rms(x_ref[0, rows, :], g_ref[...]).astype(BF16)

        def proj(lo, width):
            return _dot(h, w_ref[:, lo:lo + width].astype(BF16))

        def heads_out(ref, val):
            for i in range(HG_HEADS):
                ref[0, i, rows, :] = val[:, i * HG_DIM:(i + 1) * HG_DIM].astype(ref.dtype)

        cos = cos_ref[rows, :]
        sin = sin_ref[rows, :]
        c0 = 0
        q = _head_norm_rope(proj(c0, ATT_Q_DIM), seg_ref[...], qg_ref[...],
                            jnp.concatenate([cos] * (ATT_Q_DIM // 128), axis=-1),
                            jnp.concatenate([sin] * (ATT_Q_DIM // 128), axis=-1))
        for i in range(ATT_HEADS):
            q_ref[0, i, rows, :] = q[:, i * ATT_HEAD_DIM:(i + 1) * ATT_HEAD_DIM].astype(BF16)
        c0 += ATT_Q_DIM
        yield
        k = _head_norm_rope(proj(c0, ATT_KV_DIM), seg_ref[...], kg_ref[...], cos, sin)
        c0 += ATT_KV_DIM
        v_t = jnp.transpose(proj(c0, ATT_KV_DIM))
        c0 += ATT_KV_DIM
        for i in range(ATT_KV_HEADS):
            k_ref[0, i, rows, :] = k[:, i * ATT_HEAD_DIM:(i + 1) * ATT_HEAD_DIM].astype(BF16)
            vt_ref[0, i, :, rows] = v_t[i * ATT_HEAD_DIM:(i + 1) * ATT_HEAD_DIM, :].astype(BF16)
        yield
        heads_out(hq_ref, _silu(proj(c0, HG_WIDTH)))
        c0 += HG_WIDTH
        yield
        for d, ref in enumerate((lff_ref, lfb_ref)):
            lb = lb_ref[d:d + 1, :]
            z = proj(c0, HG_WIDTH)
            heads_out(ref, jnp.log(lb + (1.0 - lb) / (1.0 + jnp.exp(-z))) * LOG2E)
            c0 += HG_WIDTH
            yield
        heads_out(hi_ref, proj(c0, HG_WIDTH))
        c0 += HG_WIDTH
        yield
        hg_ref[0, rows, :] = _silu(proj(c0, HG_WIDTH))

    for _ in itertools.zip_longest(*[rows_of(j * part) for j in range(IN_ROW_PARTS)]):
        pass


def _in_proj(x, g, w_in, qg, kg, cos, sin, lb, seg, tm):
    b, n, d = x.shape
    nt = n // tm
    n_in = w_in.shape[1]
    tok = lambda i, j: (i, j, 0)
    head = lambda i, j: (i, 0, j, 0)
    pos = lambda i, j: (j, 0)
    return pl.pallas_call(
        _in_proj_kernel,
        grid=(b, nt),
        in_specs=[pl.BlockSpec((1, tm, d), tok), _const_spec((1, d)), _resident_spec((d, n_in)),
                  _const_spec((1, ATT_Q_DIM)), _const_spec((1, ATT_KV_DIM)),
                  pl.BlockSpec((tm, 128), pos), pl.BlockSpec((tm, 128), pos),
                  _const_spec((2, HG_WIDTH)),
                  _const_spec((2 * LANES, LANES))],
        out_specs=[pl.BlockSpec((1, ATT_HEADS, tm, ATT_HEAD_DIM), head),
                   pl.BlockSpec((1, ATT_KV_HEADS, tm, ATT_HEAD_DIM), head),
                   pl.BlockSpec((1, ATT_KV_HEADS, ATT_HEAD_DIM, tm), lambda i, j: (i, 0, 0, j)),
                   pl.BlockSpec((1, HG_HEADS, tm, HG_DIM), head),
                   pl.BlockSpec((1, HG_HEADS, tm, HG_DIM), head),
                   pl.BlockSpec((1, HG_HEADS, tm, HG_DIM), head),
                   pl.BlockSpec((1, HG_HEADS, tm, HG_DIM), head),
                   pl.BlockSpec((1, tm, HG_WIDTH), tok)],
        out_shape=[jax.ShapeDtypeStruct((b, ATT_HEADS, n, ATT_HEAD_DIM), BF16),
                   jax.ShapeDtypeStruct((b, ATT_KV_HEADS, n, ATT_HEAD_DIM), BF16),
                   jax.ShapeDtypeStruct((b, ATT_KV_HEADS, ATT_HEAD_DIM, n), BF16),
                   jax.ShapeDtypeStruct((b, HG_HEADS, n, HG_DIM), BF16),
                   jax.ShapeDtypeStruct((b, HG_HEADS, n, HG_DIM), F32),
                   jax.ShapeDtypeStruct((b, HG_HEADS, n, HG_DIM), F32),
                   jax.ShapeDtypeStruct((b, HG_HEADS, n, HG_DIM), BF16),
                   jax.ShapeDtypeStruct((b, n, HG_WIDTH), F32)],
        compiler_params=_params("parallel", "parallel"),
        name="in_proj",
    )(x, g, w_in, qg, kg, cos, sin, lb, seg)


def _attention_kernel(q_ref, qn_ref, k_ref, vt_ref, o_ref, s_even, s_odd):
    n = k_ref.shape[2]
    part = n // ATT_KEY_SPLITS
    parts = [slice(j * part, (j + 1) * part) for j in range(ATT_KEY_SPLITS)]

    def scores(i, s_scr, queries=q_ref):
        for rows in parts:
            s_scr[rows, :] = _dot_nt(k_ref[0, i // ATT_GROUP, rows, :], queries[0, i])

    def finish(i, s_scr):
        s = [s_scr[rows, :] for rows in parts]
        m = functools.reduce(jnp.maximum, [jnp.max(x, axis=0, keepdims=True) for x in s])
        p = [jnp.exp2(x - m) for x in s]
        l = sum(jnp.sum(x, axis=0, keepdims=True) for x in p)
        o_t = sum(_dot(vt_ref[0, i // ATT_GROUP, :, rows], x.astype(BF16)) for rows, x in zip(parts, p))
        o_ref[0, i] = (o_t / l).astype(BF16)

    @pl.when(pl.program_id(1) == 0)
    def _():
        scores(0, s_even)

    def pair(j, carry):
        scores(2 * j + 1, s_odd)
        finish(2 * j, s_even)
        scores(2 * j + 2, s_even)
        finish(2 * j + 1, s_odd)
        return carry

    lax.fori_loop(0, ATT_HEADS // 2 - 1, pair, 0)
    scores(ATT_HEADS - 1, s_odd)
    finish(ATT_HEADS - 2, s_even)
    scores(0, s_even, qn_ref)
    finish(ATT_HEADS - 1, s_odd)


def _attention(q, k, v_t, tq):
    b, _, n, dh = q.shape
    last = n // tq - 1
    return pl.pallas_call(
        _attention_kernel,
        grid=(b, n // tq),
        in_specs=[pl.BlockSpec((1, ATT_HEADS, tq, dh), lambda i, j: (i, 0, j, 0)),
                  pl.BlockSpec((1, ATT_HEADS, tq, dh), lambda i, j: (i, 0, jnp.minimum(j + 1, last), 0)),
                  pl.BlockSpec((1, ATT_KV_HEADS, n, dh), lambda i, j: (i, 0, 0, 0)),
                  pl.BlockSpec((1, ATT_KV_HEADS, dh, n), lambda i, j: (i, 0, 0, 0))],
        out_specs=pl.BlockSpec((1, ATT_HEADS, dh, tq), lambda i, j: (i, 0, 0, j)),
        out_shape=jax.ShapeDtypeStruct((b, ATT_HEADS, dh, n), BF16),
        scratch_shapes=[pltpu.VMEM((n, tq), F32), pltpu.VMEM((n, tq), F32)],
        compiler_params=_params("parallel", "arbitrary"),
        name="attention",
    )(q, q, k, v_t)


def _hgrn_levels():
    m, out = 1, []
    while m < HG_CHUNK:
        out.append(m)
        m *= 2
    return out


def _hgrn_tables():
    c = HG_CHUNK
    t = np.arange(c)[:, None]
    u = np.arange(c)[None, :]
    mats, masks, fast = [], [], []
    for reverse in (False, True):
        blocks, pairs = [], [t == u]
        for m in _hgrn_levels():
            base = (t // (2 * m)) * (2 * m)
            if not reverse:
                r = base + m - 1
                block = np.where(t > r, (u > r) & (u <= t), (u > t) & (u <= r))
                roles = ((t // m) % 2 == 1) & ((u // m) % 2 == 0)
            else:
                r = base + m
                block = np.where(t < r, (u >= t) & (u < r), (u >= r) & (u < t))
                roles = ((t // m) % 2 == 0) & ((u // m) % 2 == 1)
            if m < SUBLANES:
                blocks.append(block)
            pairs.append(roles & (t // (2 * m) == u // (2 * m)))
        blocks.append(u >= t if reverse else u <= t)
        mats.append(np.tile(np.concatenate(blocks, axis=0), (1, 2)))
        masks.append(np.stack(pairs))
        fast.append((t // HG_FAST_BLOCK == u // HG_FAST_BLOCK) & (u >= t if reverse else u <= t))
    return tuple(np.stack(x).astype(np.float32) for x in (mats, masks, fast))


def _hgrn_sums(lf2, sum_mat):
    hi = lf2.astype(BF16)
    mid = (lf2 - hi.astype(F32)).astype(BF16)
    return _dot(sum_mat, jnp.concatenate([hi, mid], axis=0))


def _hgrn_chunk(q_ref, lf_ref, v_ref, o_ref, state_ref, sum_mat, pair_mask, fast_mask, b_scr, reverse, fast):
    c = HG_CHUNK
    levels = list(enumerate(_hgrn_levels()))
    n_small = sum(m < SUBLANES for _, m in levels)
    q, lf2, v, state = q_ref[...], lf_ref[...], v_ref[...], state_ref[...]
    qb = q.astype(BF16)
    kb = (1.0 - jnp.exp2(lf2)).astype(BF16)
    if fast:
        b = _hgrn_sums(lf2, sum_mat[n_small * c:, :])
        yield
        b_scr[...] = b
        parts = []
        for g in range(c // HG_FAST_BLOCK):
            lo, up = g * HG_FAST_BLOCK, (g + 1) * HG_FAST_BLOCK
            before = up if reverse else lo - 1
            parts.append(b[lo:up] - b_scr[before:before + 1, :] if 0 <= before < c else b[lo:up])
        loc = jnp.concatenate(parts, axis=0)
        a = _dot_nt(qb * jnp.exp2(loc).astype(BF16), kb * jnp.exp2(-loc).astype(BF16)) * fast_mask[...]
        yield
        levels = [(j, m) for j, m in levels if m >= HG_FAST_BLOCK]
    else:
        sums = _hgrn_sums(lf2, sum_mat[...])
        yield
        b = sums[n_small * c:]
        b_scr[...] = b
        a = _dot_nt(qb, kb) * pair_mask[0]
        yield
    for j, m in levels:
        if m < SUBLANES:
            e = sums[j * c:(j + 1) * c]
        else:
            parts = []
            for g in range(c // (2 * m)):
                lo, mid_row, up = 2 * m * g, 2 * m * g + m, 2 * m * (g + 1)
                r = mid_row if reverse else mid_row - 1
                b_r = b_scr[r:r + 1, :]
                parts += [b[lo:mid_row] - b_r, b_r - b[mid_row:up]] if reverse else \
                         [b_r - b[lo:mid_row], b[mid_row:up] - b_r]
            e = jnp.concatenate(parts, axis=0)
        fac = jnp.exp2(e).astype(BF16)
        a = a + _dot_nt(qb * fac, kb * fac) * pair_mask[j + 1]
        yield
    vb = v.astype(BF16)
    last = 0 if reverse else c - 1
    total = b_scr[last:last + 1, :]
    o_ref[...] = _dot(a.astype(BF16), vb) + _dot_nt(qb * jnp.exp2(b).astype(BF16), state.astype(BF16))
    yield
    k_dec = kb * jnp.exp2(total - b).astype(BF16)
    state_ref[...] = state * jnp.exp2(total) + _dot_tn(vb, k_dec)


def _hgrn_kernel(qf_ref, lff_ref, vf_ref, qb_ref, lfb_ref, vb_ref, sm_ref, pm_ref, fm_ref, *rest):
    n_cast = (len(rest) - 4) // 2
    cast_in, (of_ref, ob_ref), cast_out = rest[:n_cast], rest[n_cast:n_cast + 2], rest[n_cast + 2:2 * n_cast + 2]
    s_ref, b_scr = rest[2 * n_cast + 2:]

    @pl.when(pl.program_id(1) == 0)
    def _():
        s_ref[...] = jnp.zeros_like(s_ref)

    for src, dst in zip(cast_in, cast_out):
        dst[...] = src[...].astype(dst.dtype)

    c = HG_CHUNK
    rows_per_step = qf_ref.shape[2]
    chains = [(d, h) for h in range(HG_HEADS) for d in (0, 1)]
    q_refs, lf_refs, v_refs, o_refs = (qf_ref, qb_ref), (lff_ref, lfb_ref), (vf_ref, vb_ref), (of_ref, ob_ref)

    worst = jnp.zeros((1, HG_DIM), F32)
    for d, h in chains:
        for g in range(rows_per_step // HG_FAST_BLOCK):
            block = lf_refs[d][0, h, g * HG_FAST_BLOCK:(g + 1) * HG_FAST_BLOCK, :]
            worst = jnp.minimum(worst, jnp.sum(block, axis=0, keepdims=True))
    mild = jnp.min(worst) >= -HG_FAST_MAX_LOG2

    def run(fast):
        n_sub = rows_per_step // c
        for sub in range(n_sub):
            work = []
            for d, h in chains:
                k = n_sub - 1 - sub if d else sub
                rows = slice(k * c, (k + 1) * c)
                work.append(_hgrn_chunk(q_refs[d].at[0, h, rows, :], lf_refs[d].at[0, h, rows, :],
                                        v_refs[d].at[0, h, rows, :], o_refs[d].at[0, h, rows, :],
                                        s_ref.at[d, h], sm_ref.at[d], pm_ref.at[d], fm_ref.at[d],
                                        b_scr.at[d, h], bool(d), fast))
            for _ in itertools.zip_longest(*work):
                pass

    pl.when(mild)(lambda: run(True))
    pl.when(jnp.logical_not(mild))(lambda: run(False))


def _hgrn(hq, lf_f, lf_b, hi, to_cast):
    b, nh, n, dk = hq.shape
    rows_per_step = HG_CHUNK * HG_STEP_CHUNKS if n % (HG_CHUNK * HG_STEP_CHUNKS) == 0 else HG_CHUNK
    nc = n // rows_per_step
    blk = (1, nh, rows_per_step, dk)
    fwd = pl.BlockSpec(blk, lambda i, j: (i, 0, j, 0))
    bwd = pl.BlockSpec(blk, lambda i, j: (i, 0, nc - 1 - j, 0))
    out = jax.ShapeDtypeStruct((b, nh, n, dk), F32)
    tables = _hgrn_tables()
    steps = b * nc

    def slab_spec(w):
        hold = next(h for h in range(1, steps + 1)
                    if steps % h == 0 and w.shape[0] % (steps // h * BF16_SUBLANES) == 0)
        return pl.BlockSpec((w.shape[0] // (steps // hold), w.shape[1]), lambda i, j: ((i * nc + j) // hold, 0))

    slabs = [slab_spec(w) for w in to_cast]
    res = pl.pallas_call(
        _hgrn_kernel,
        grid=(b, nc),
        in_specs=[fwd, fwd, fwd, bwd, bwd, bwd] + [_const_spec(t.shape) for t in tables] + slabs,
        out_specs=[fwd, bwd] + slabs,
        out_shape=[out, out] + [jax.ShapeDtypeStruct(w.shape, BF16) for w in to_cast],
        scratch_shapes=[pltpu.VMEM((2, nh, dk, dk), F32), pltpu.VMEM((2, nh, HG_CHUNK, dk), F32)],
        compiler_params=_params("parallel", "arbitrary"),
        name="hgrn",
    )(hq, lf_f, hi, hq, lf_b, hi, jnp.asarray(tables[0], BF16), jnp.asarray(tables[1]), jnp.asarray(tables[2]),
      *to_cast)
    return res[0], res[1], res[2:]


def _mix_cross_kernel(x_ref, att_ref, of_ref, ob_ref, sg_ref, og_ref, wo_ref, pmg_ref, pxg_ref,
                      wq_ref, km_ref, vm_ref, wxo_ref, poxg_ref, pfg_ref, x2_ref, h3_ref):
    tm = x_ref.shape[1]
    part = tm // MIX_ROW_PARTS

    def rows_of(lo):
        rows = slice(lo, lo + part)
        rec = []
        for i in range(HG_HEADS):
            o = _rms(of_ref[0, i, rows, :] + ob_ref[0, i, rows, :], og_ref[...])
            rec.append((o * sg_ref[0, rows, i * HG_DIM:(i + 1) * HG_DIM]).astype(BF16))
        att_t = att_ref[0, :, :, rows].reshape(ATT_Q_DIM, part)
        mixed = _dot_tn(att_t, wo_ref[:ATT_Q_DIM, :].astype(BF16)) + \
            _dot(jnp.concatenate(rec, axis=-1), wo_ref[ATT_Q_DIM:, :].astype(BF16))
        yield
        x1 = x_ref[0, rows, :] + _rms(mixed, pmg_ref[...])
        h2 = _rms(x1, pxg_ref[...]).astype(BF16)
        d = h2.shape[-1]
        dh = d // X_HEADS
        q = _dot(h2, wq_ref[...].astype(BF16)) * (dh ** -0.5)
        yield
        heads = []
        for i in range(X_HEADS):
            sl = slice(i * dh, (i + 1) * dh)
            s = _dot_nt(q[:, sl].astype(BF16), km_ref[0, :, sl])
            p = jnp.exp(s - jnp.max(s, axis=-1, keepdims=True))
            l = jnp.sum(p, axis=-1, keepdims=True)
            heads.append((_dot(p.astype(BF16), vm_ref[0, :, sl]) / l).astype(BF16))
        yield
        xo = _dot(jnp.concatenate(heads, axis=-1), wxo_ref[...].astype(BF16))
        yield
        x2 = x1 + _rms(xo, poxg_ref[...])
        x2_ref[0, rows, :] = x2
        h3_ref[0, rows, :] = _rms(x2, pfg_ref[...]).astype(BF16)

    for _ in itertools.zip_longest(*[rows_of(j * part) for j in range(MIX_ROW_PARTS)]):
        pass


def _mix_cross(x, att, o_f, o_b, sg, og, w_out, pmg, pxg, w_xq, k_mem, v_mem, w_xo, poxg, pfg, tm):
    b, n, d = x.shape
    nm = k_mem.shape[1]
    tok = lambda i, j: (i, j, 0)
    head = lambda i, j: (i, 0, j, 0)
    batch = lambda i, j: (i, 0, 0)
    vec = _const_spec((1, d))
    mat = _resident_spec((d, d))
    return pl.pallas_call(
        _mix_cross_kernel,
        grid=(b, n // tm),
        in_specs=[pl.BlockSpec((1, tm, d), tok),
                  pl.BlockSpec((1, ATT_HEADS, ATT_HEAD_DIM, tm), lambda i, j: (i, 0, 0, j)),
                  pl.BlockSpec((1, HG_HEADS, tm, HG_DIM), head), pl.BlockSpec((1, HG_HEADS, tm, HG_DIM), head),
                  pl.BlockSpec((1, tm, HG_WIDTH), tok), _const_spec((1, HG_DIM)),
                  mat, vec, vec, mat,
                  pl.BlockSpec((1, nm, d), batch), pl.BlockSpec((1, nm, d), batch),
                  mat, vec, vec],
        out_specs=[pl.BlockSpec((1, tm, d), tok), pl.BlockSpec((1, tm, d), tok)],
        out_shape=[jax.ShapeDtypeStruct((b, n, d), F32), jax.ShapeDtypeStruct((b, n, d), BF16)],
        compiler_params=_params("parallel", "parallel"),
        name="mix_cross",
    )(x, att, o_f, o_b, sg, og, w_out, pmg, pxg, w_xq, k_mem, v_mem, w_xo, poxg, pfg)


def _conv_ffn_kernel(x_ref, h_ref, hp_ref, hn_ref, wu_ref, cw_ref, cb_ref, wd_ref, g_ref, o_ref, act_scr):
    j = pl.program_id(1)
    tm = h_ref.shape[1]
    d_ff = wd_ref.shape[0]
    ck = FF_CHUNK
    prev = jnp.where(j > 0, hp_ref[0], jnp.zeros_like(hp_ref[0]))
    nxt = jnp.where(j < pl.num_programs(1) - 1, hn_ref[0], jnp.zeros_like(hn_ref[0]))
    hext = jnp.concatenate([prev, h_ref[0], nxt], axis=0)

    def conv(col):
        u = _dot(hext, wu_ref[:, col:col + ck])
        taps = (pltpu.roll(u, 1, 0), u, pltpu.roll(u, u.shape[0] - 1, 0))
        out = cb_ref[:, col:col + ck]
        for t in range(CONV_W):
            out = out + taps[t][HALO:HALO + tm] * cw_ref[t:t + 1, col:col + ck]
        return out

    for c in range(d_ff // ck):
        act_scr[:, c * ck:(c + 1) * ck] = (_silu(conv(c * ck)) * conv(d_ff + c * ck)).astype(BF16)
    o_ref[0] = x_ref[0] + _rms(_dot(act_scr[...], wd_ref[...]), g_ref[...])


def _conv_ffn(x2, h3, w_up, conv_w, conv_b, wd, g, tm):
    b, n, d = x2.shape
    d_ff = wd.shape[0]
    ck = FF_CHUNK
    hb = tm // HALO
    last = n // HALO - 1
    tok = lambda i, j: (i, j, 0)
    full = _resident_spec
    return pl.pallas_call(
        _conv_ffn_kernel,
        grid=(b, n // tm),
        in_specs=[pl.BlockSpec((1, tm, d), tok), pl.BlockSpec((1, tm, d), tok),
                  pl.BlockSpec((1, HALO, d), lambda i, j: (i, jnp.maximum(j * hb - 1, 0), 0)),
                  pl.BlockSpec((1, HALO, d), lambda i, j: (i, jnp.minimum((j + 1) * hb, last), 0)),
                  full((d, 2 * d_ff)), full((CONV_W, 2 * d_ff)), full((1, 2 * d_ff)),
                  full((d_ff, d)), _const_spec((1, d))],
        out_specs=pl.BlockSpec((1, tm, d), tok),
        out_shape=jax.ShapeDtypeStruct((b, n, d), F32),
        scratch_shapes=[pltpu.VMEM((tm, d_ff), BF16)],
        compiler_params=_params("parallel", "parallel"),
        name="conv_ffn",
    )(x2, h3, h3, h3, w_up, conv_w, conv_b, wd, g)


def _rope_tables(n):
    pairs = ATT_HEAD_DIM // 4
    pos = np.arange(n)
    inv = np.power(np.float32(ROPE_THETA), -np.arange(pairs, dtype=np.float32) / np.float32(pairs))
    ang = np.concatenate([(pos // GRID_W).astype(np.float32)[:, None] * inv,
                          (pos % GRID_W).astype(np.float32)[:, None] * inv], axis=-1)
    cos = np.repeat(np.cos(ang), 2, axis=-1)
    sin = np.repeat(np.sin(ang), 2, axis=-1) * np.tile(np.array([-1.0, 1.0], np.float32), ATT_HEAD_DIM // 2)
    return jnp.asarray(np.tile(cos, (1, 2)), F32), jnp.asarray(np.tile(sin, (1, 2)), F32)


def _segment_ones():
    i = jnp.arange(LANES) // ATT_HEAD_DIM
    return jnp.tile((i[:, None] == i[None, :]).astype(BF16), (2, 1))


def _layer(x, mem, lb, pre_mix_g, w_in, q_norm_g, k_norm_g, hg_out_norm_g, w_out, post_mix_g, pre_x_g,
           mem_norm_g, w_xq, w_xkv, w_xo, post_x_g, pre_ffn_g, w_up, conv_w, conv_b, w_down, post_ffn_g):
    b, n, d = x.shape
    d_ff = w_down.shape[0]
    assert n % HG_CHUNK == 0 and n % GRID_W == 0 and d_ff % FF_CHUNK == 0
    tm = min(256, n)
    tf = min(512, n)
    row = lambda g: g.reshape(1, -1).astype(F32)

    cos, sin = _rope_tables(n)
    qg = jnp.tile(q_norm_g, ATT_HEADS).reshape(1, -1) * (ATT_HEAD_DIM ** -0.5 * LOG2E)
    kg = jnp.tile(k_norm_g, ATT_KV_HEADS).reshape(1, -1)

    k_mem, v_mem = _mem_proj(mem, row(mem_norm_g), w_xkv)
    q, k, v_t, hq, lf_f, lf_b, hi, sg = _in_proj(
        x, row(pre_mix_g), w_in, qg, kg, cos, sin, lb,
        _segment_ones(), tf)
    att = _attention(q, k, v_t, tm)
    o_f, o_b, (w_up_bf16, w_down_bf16) = _hgrn(hq, lf_f, lf_b, hi, [w_up, w_down])
    x2, h3 = _mix_cross(x, att, o_f, o_b, sg, row(hg_out_norm_g), w_out, row(post_mix_g),
                        row(pre_x_g), w_xq, k_mem, v_mem, w_xo,
                        row(post_x_g), row(pre_ffn_g), tf)

    return _conv_ffn(x2, h3, w_up_bf16, conv_w, row(conv_b), w_down_bf16, row(post_ffn_g), tf)


def kernel(x, mem, pre_mix_g, w_in, q_norm_g, k_norm_g, hg_lb, hg_out_norm_g, w_out, post_mix_g, pre_x_g,
           mem_norm_g, w_xq, w_xkv, w_xo, post_x_g, pre_ffn_g, w_up, conv_w, conv_b, w_down, post_ffn_g):
    lb_all = jnp.cumsum(jax.nn.softmax(hg_lb.astype(F32), axis=1), axis=1)
    for l in range(w_in.shape[0]):
        x = _layer(x, mem, lb_all[:, l], pre_mix_g[l], w_in[l], q_norm_g[l], k_norm_g[l], hg_out_norm_g[l],
                   w_out[l], post_mix_g[l], pre_x_g[l], mem_norm_g[l], w_xq[l], w_xkv[l], w_xo[l],
                   post_x_g[l], pre_ffn_g[l], w_up[l], conv_w[l], conv_b[l], w_down[l], post_ffn_g[l])
    return x
```

```python
import functools
import itertools
import math

import numpy as np
import jax
import jax.numpy as jnp
from jax import lax
from jax.experimental import pallas as pl
from jax.experimental.pallas import tpu as pltpu

F32 = jnp.float32
BF16 = jnp.bfloat16

EPS = 1e-6
LOG2E = math.log2(math.e)
GRID_W = 64
ROPE_THETA = 10000.0

ATT_HEADS = 8
ATT_KV_HEADS = 2
ATT_GROUP = ATT_HEADS // ATT_KV_HEADS
ATT_HEAD_DIM = 64
ATT_Q_DIM = ATT_HEADS * ATT_HEAD_DIM
ATT_KV_DIM = ATT_KV_HEADS * ATT_HEAD_DIM
ATT_KEY_SPLITS = 2

HG_HEADS = 4
HG_DIM = 128
HG_WIDTH = HG_HEADS * HG_DIM
HG_CHUNK = 128
HG_STEP_CHUNKS = 4
HG_FAST_BLOCK = 64
HG_FAST_MAX_LOG2 = 96.0

X_HEADS = 4
IN_ROW_PARTS = 2
MIX_ROW_PARTS = 2
CONV_W = 3
FF_CHUNK = 256
HALO = 16

VMEM_LIMIT = 56 * 1024 * 1024
SUBLANES = 8
BF16_SUBLANES = 16
LANES = 128


def _params(*sem):
    return pltpu.CompilerParams(dimension_semantics=sem, vmem_limit_bytes=VMEM_LIMIT)


def _rms(x, g):
    ms = jnp.mean(x * x, axis=-1, keepdims=True)
    return x * lax.rsqrt(ms + EPS) * g


def _silu(x):
    return x / (1.0 + jnp.exp(-x))


def _dot(a, b):
    return jnp.dot(a, b, preferred_element_type=F32)


def _dot_nt(a, b):
    return lax.dot_general(a, b, (((1,), (1,)), ((), ())), preferred_element_type=F32)


def _dot_tn(a, b):
    return lax.dot_general(a, b, (((0,), (0,)), ((), ())), preferred_element_type=F32)


def _const_spec(shape):
    return pl.BlockSpec(shape, lambda *_: (0,) * len(shape))


def _resident_spec(shape):
    return pl.BlockSpec(shape, lambda *_: (0,) * len(shape), pipeline_mode=pl.Buffered(1))


def _mem_proj_kernel(m_ref, g_ref, w_ref, k_ref, v_ref):
    d = m_ref.shape[-1]
    m = _rms(m_ref[0], g_ref[...]).astype(BF16)
    kv = _dot(m, w_ref[...].astype(BF16))
    k_ref[0] = kv[:, :d].astype(BF16)
    v_ref[0] = kv[:, d:].astype(BF16)


def _mem_proj(mem, g, w_xkv):
    b, nm, d = mem.shape
    out = jax.ShapeDtypeStruct((b, nm, d), BF16)
    return pl.pallas_call(
        _mem_proj_kernel,
        grid=(b,),
        in_specs=[pl.BlockSpec((1, nm, d), lambda i: (i, 0, 0)),
                  _const_spec((1, d)), _resident_spec((d, 2 * d))],
        out_specs=[pl.BlockSpec((1, nm, d), lambda i: (i, 0, 0))] * 2,
        out_shape=[out, out],
        compiler_params=_params("parallel"),
        name="mem_proj",
    )(mem, g, w_xkv)


def _head_norm_rope(a, seg, gain, cos, sin_signed):
    w = a.shape[-1]
    sq = a * a
    hi = sq.astype(BF16)
    lo = (sq - hi.astype(F32)).astype(BF16)
    ss = [_dot(jnp.concatenate([hi[:, j:j + LANES], lo[:, j:j + LANES]], axis=-1), seg) for j in range(0, w, LANES)]
    ss = jnp.concatenate(ss, axis=-1) if len(ss) > 1 else ss[0]
    an = a * lax.rsqrt(ss * (1.0 / ATT_HEAD_DIM) + EPS) * gain
    lane = lax.broadcasted_iota(jnp.int32, a.shape, 1)
    partner = jnp.where(lane % 2 == 0, pltpu.roll(an, w - 1, 1), pltpu.roll(an, 1, 1))
    return an * cos + partner * sin_signed


def _in_proj_kernel(x_ref, g_ref, w_ref, qg_ref, kg_ref, cos_ref, sin_ref, lb_ref, seg_ref,
                    q_ref, k_ref, vt_ref, hq_ref, lff_ref, lfb_ref, hi_ref, hg_ref):
    part = x_ref.shape[1] // IN_ROW_PARTS

    def rows_of(r0):
        rows = slice(r0, r0 + part)
        h = _rms(x_ref[0, rows, :], g_ref[...]).astype(BF16)

        def proj(lo, width):
            return _dot(h, w_ref[:, lo:lo + width].astype(BF16))

        def heads_out(ref, val):
            for i in range(HG_HEADS):
                ref[0, i, rows, :] = val[:, i * HG_DIM:(i + 1) * HG_DIM].astype(ref.dtype)

        cos = cos_ref[rows, :]
        sin = sin_ref[rows, :]
        c0 = 0
        q = _head_norm_rope(proj(c0, ATT_Q_DIM), seg_ref[...], qg_ref[...],
                            jnp.concatenate([cos] * (ATT_Q_DIM // 128), axis=-1),
                            jnp.concatenate([sin] * (ATT_Q_DIM // 128), axis=-1))
        for i in range(ATT_HEADS):
            q_ref[0, i, rows, :] = q[:, i * ATT_HEAD_DIM:(i + 1) * ATT_HEAD_DIM].astype(BF16)
        c0 += ATT_Q_DIM
        yield
        k = _head_norm_rope(proj(c0, ATT_KV_DIM), seg_ref[...], kg_ref[...], cos, sin)
        c0 += ATT_KV_DIM
        v_t = jnp.transpose(proj(c0, ATT_KV_DIM))
        c0 += ATT_KV_DIM
        for i in range(ATT_KV_HEADS):
            k_ref[0, i, rows, :] = k[:, i * ATT_HEAD_DIM:(i + 1) * ATT_HEAD_DIM].astype(BF16)
            vt_ref[0, i, :, rows] = v_t[i * ATT_HEAD_DIM:(i + 1) * ATT_HEAD_DIM, :].astype(BF16)
        yield
        heads_out(hq_ref, _silu(proj(c0, HG_WIDTH)))
        c0 += HG_WIDTH
        yield
        for d, ref in enumerate((lff_ref, lfb_ref)):
            lb = lb_ref[d:d + 1, :]
            z = proj(c0, HG_WIDTH)
            heads_out(ref, jnp.log(lb + (1.0 - lb) / (1.0 + jnp.exp(-z))) * LOG2E)
            c0 += HG_WIDTH
            yield
        heads_out(hi_ref, proj(c0, HG_WIDTH))
        c0 += HG_WIDTH
        yield
        hg_ref[0, rows, :] = _silu(proj(c0, HG_WIDTH))

    for _ in itertools.zip_longest(*[rows_of(j * part) for j in range(IN_ROW_PARTS)]):
        pass


def _in_proj(x, g, w_in, qg, kg, cos, sin, lb, seg, tm):
    b, n, d = x.shape
    nt = n // tm
    n_in = w_in.shape[1]
    tok = lambda i, j: (i, j, 0)
    head = lambda i, j: (i, 0, j, 0)
    pos = lambda i, j: (j, 0)
    return pl.pallas_call(
        _in_proj_kernel,
        grid=(b, nt),
        in_specs=[pl.BlockSpec((1, tm, d), tok), _const_spec((1, d)), _resident_spec((d, n_in)),
                  _const_spec((1, ATT_Q_DIM)), _const_spec((1, ATT_KV_DIM)),
                  pl.BlockSpec((tm, 128), pos), pl.BlockSpec((tm, 128), pos),
                  _const_spec((2, HG_WIDTH)),
                  _const_spec((2 * LANES, LANES))],
        out_specs=[pl.BlockSpec((1, ATT_HEADS, tm, ATT_HEAD_DIM), head),
                   pl.BlockSpec((1, ATT_KV_HEADS, tm, ATT_HEAD_DIM), head),
                   pl.BlockSpec((1, ATT_KV_HEADS, ATT_HEAD_DIM, tm), lambda i, j: (i, 0, 0, j)),
                   pl.BlockSpec((1, HG_HEADS, tm, HG_DIM), head),
                   pl.BlockSpec((1, HG_HEADS, tm, HG_DIM), head),
                   pl.BlockSpec((1, HG_HEADS, tm, HG_DIM), head),
                   pl.BlockSpec((1, HG_HEADS, tm, HG_DIM), head),
                   pl.BlockSpec((1, tm, HG_WIDTH), tok)],
        out_shape=[jax.ShapeDtypeStruct((b, ATT_HEADS, n, ATT_HEAD_DIM), BF16),
                   jax.ShapeDtypeStruct((b, ATT_KV_HEADS, n, ATT_HEAD_DIM), BF16),
                   jax.ShapeDtypeStruct((b, ATT_KV_HEADS, ATT_HEAD_DIM, n), BF16),
                   jax.ShapeDtypeStruct((b, HG_HEADS, n, HG_DIM), BF16),
                   jax.ShapeDtypeStruct((b, HG_HEADS, n, HG_DIM), F32),
                   jax.ShapeDtypeStruct((b, HG_HEADS, n, HG_DIM), F32),
                   jax.ShapeDtypeStruct((b, HG_HEADS, n, HG_DIM), BF16),
                   jax.ShapeDtypeStruct((b, n, HG_WIDTH), F32)],
        compiler_params=_params("parallel", "parallel"),
        name="in_proj",
    )(x, g, w_in, qg, kg, cos, sin, lb, seg)


def _attention_kernel(q_ref, qn_ref, k_ref, vt_ref, o_ref, s_even, s_odd):
    n = k_ref.shape[2]
    part = n // ATT_KEY_SPLITS
    parts = [slice(j * part, (j + 1) * part) for j in range(ATT_KEY_SPLITS)]

    def scores(i, s_scr, queries=q_ref):
        for rows in parts:
            s_scr[rows, :] = _dot_nt(k_ref[0, i // ATT_GROUP, rows, :], queries[0, i])

    def finish(i, s_scr):
        s = [s_scr[rows, :] for rows in parts]
        m = functools.reduce(jnp.maximum, [jnp.max(x, axis=0, keepdims=True) for x in s])
        p = [jnp.exp2(x - m) for x in s]
        l = sum(jnp.sum(x, axis=0, keepdims=True) for x in p)
        o_t = sum(_dot(vt_ref[0, i // ATT_GROUP, :, rows], x.astype(BF16)) for rows, x in zip(parts, p))
        o_ref[0, i] = (o_t / l).astype(BF16)

    @pl.when(pl.program_id(1) == 0)
    def _():
        scores(0, s_even)

    def pair(j, carry):
        scores(2 * j + 1, s_odd)
        finish(2 * j, s_even)
        scores(2 * j + 2, s_even)
        finish(2 * j + 1, s_odd)
        return carry

    lax.fori_loop(0, ATT_HEADS // 2 - 1, pair, 0)
    scores(ATT_HEADS - 1, s_odd)
    finish(ATT_HEADS - 2, s_even)
    scores(0, s_even, qn_ref)
    finish(ATT_HEADS - 1, s_odd)


def _attention(q, k, v_t, tq):
    b, _, n, dh = q.shape
    last = n // tq - 1
    return pl.pallas_call(
        _attention_kernel,
        grid=(b, n // tq),
        in_specs=[pl.BlockSpec((1, ATT_HEADS, tq, dh), lambda i, j: (i, 0, j, 0)),
                  pl.BlockSpec((1, ATT_HEADS, tq, dh), lambda i, j: (i, 0, jnp.minimum(j + 1, last), 0)),
                  pl.BlockSpec((1, ATT_KV_HEADS, n, dh), lambda i, j: (i, 0, 0, 0)),
                  pl.BlockSpec((1, ATT_KV_HEADS, dh, n), lambda i, j: (i, 0, 0, 0))],
        out_specs=pl.BlockSpec((1, ATT_HEADS, dh, tq), lambda i, j: (i, 0, 0, j)),
        out_shape=jax.ShapeDtypeStruct((b, ATT_HEADS, dh, n), BF16),
        scratch_shapes=[pltpu.VMEM((n, tq), F32), pltpu.VMEM((n, tq), F32)],
        compiler_params=_params("parallel", "arbitrary"),
        name="attention",
    )(q, q, k, v_t)


def _hgrn_levels():
    m, out = 1, []
    while m < HG_CHUNK:
        out.append(m)
        m *= 2
    return out


def _hgrn_tables():
    c = HG_CHUNK
    t = np.arange(c)[:, None]
    u = np.arange(c)[None, :]
    mats, masks, fast = [], [], []
    for reverse in (False, True):
        blocks, pairs = [], [t == u]
        for m in _hgrn_levels():
            base = (t // (2 * m)) * (2 * m)
            if not reverse:
                r = base + m - 1
                block = np.where(t > r, (u > r) & (u <= t), (u > t) & (u <= r))
                roles = ((t // m) % 2 == 1) & ((u // m) % 2 == 0)
            else:
                r = base + m
                block = np.where(t < r, (u >= t) & (u < r), (u >= r) & (u < t))
                roles = ((t // m) % 2 == 0) & ((u // m) % 2 == 1)
            if m < SUBLANES:
                blocks.append(block)
            pairs.append(roles & (t // (2 * m) == u // (2 * m)))
        blocks.append(u >= t if reverse else u <= t)
        mats.append(np.tile(np.concatenate(blocks, axis=0), (1, 2)))
        masks.append(np.stack(pairs))
        fast.append((t // HG_FAST_BLOCK == u // HG_FAST_BLOCK) & (u >= t if reverse else u <= t))
    return tuple(np.stack(x).astype(np.float32) for x in (mats, masks, fast))


def _hgrn_sums(lf2, sum_mat):
    hi = lf2.astype(BF16)
    mid = (lf2 - hi.astype(F32)).astype(BF16)
    return _dot(sum_mat, jnp.concatenate([hi, mid], axis=0))


def _hgrn_chunk(q_ref, lf_ref, v_ref, o_ref, state_ref, sum_mat, pair_mask, fast_mask, b_scr, reverse, fast):
    c = HG_CHUNK
    levels = list(enumerate(_hgrn_levels()))
    n_small = sum(m < SUBLANES for _, m in levels)
    q, lf2, v, state = q_ref[...], lf_ref[...], v_ref[...], state_ref[...]
    qb = q.astype(BF16)
    kb = (1.0 - jnp.exp2(lf2)).astype(BF16)
    if fast:
        b = _hgrn_sums(lf2, sum_mat[n_small * c:, :])
        yield
        b_scr[...] = b
        parts = []
        for g in range(c // HG_FAST_BLOCK):
            lo, up = g * HG_FAST_BLOCK, (g + 1) * HG_FAST_BLOCK
            before = up if reverse else lo - 1
            parts.append(b[lo:up] - b_scr[before:before + 1, :] if 0 <= before < c else b[lo:up])
        loc = jnp.concatenate(parts, axis=0)
        a = _dot_nt(qb * jnp.exp2(loc).astype(BF16), kb * jnp.exp2(-loc).astype(BF16)) * fast_mask[...]
        yield
        levels = [(j, m) for j, m in levels if m >= HG_FAST_BLOCK]
    else:
        sums = _hgrn_sums(lf2, sum_mat[...])
        yield
        b = sums[n_small * c:]
        b_scr[...] = b
        a = _dot_nt(qb, kb) * pair_mask[0]
        yield
    for j, m in levels:
        if m < SUBLANES:
            e = sums[j * c:(j + 1) * c]
        else:
            parts = []
            for g in range(c // (2 * m)):
                lo, mid_row, up = 2 * m * g, 2 * m * g + m, 2 * m * (g + 1)
                r = mid_row if reverse else mid_row - 1
                b_r = b_scr[r:r + 1, :]
                parts += [b[lo:mid_row] - b_r, b_r - b[mid_row:up]] if reverse else \
                         [b_r - b[lo:mid_row], b[mid_row:up] - b_r]
            e = jnp.concatenate(parts, axis=0)
        fac = jnp.exp2(e).astype(BF16)
        a = a + _dot_nt(qb * fac, kb * fac) * pair_mask[j + 1]
        yield
    vb = v.astype(BF16)
    last = 0 if reverse else c - 1
    total = b_scr[last:last + 1, :]
    o_ref[...] = _dot(a.astype(BF16), vb) + _dot_nt(qb * jnp.exp2(b).astype(BF16), state.astype(BF16))
    yield
    k_dec = kb * jnp.exp2(total - b).astype(BF16)
    state_ref[...] = state * jnp.exp2(total) + _dot_tn(vb, k_dec)


def _hgrn_kernel(qf_ref, lff_ref, vf_ref, qb_ref, lfb_ref, vb_ref, sm_ref, pm_ref, fm_ref, *rest):
    n_cast = (len(rest) - 4) // 2
    cast_in, (of_ref, ob_ref), cast_out = rest[:n_cast], rest[n_cast:n_cast + 2], rest[n_cast + 2:2 * n_cast + 2]
    s_ref, b_scr = rest[2 * n_cast + 2:]

    @pl.when(pl.program_id(1) == 0)
    def _():
        s_ref[...] = jnp.zeros_like(s_ref)

    for src, dst in zip(cast_in, cast_out):
        dst[...] = src[...].astype(dst.dtype)

    c = HG_CHUNK
    rows_per_step = qf_ref.shape[2]
    chains = [(d, h) for h in range(HG_HEADS) for d in (0, 1)]
    q_refs, lf_refs, v_refs, o_refs = (qf_ref, qb_ref), (lff_ref, lfb_ref), (vf_ref, vb_ref), (of_ref, ob_ref)

    worst = jnp.zeros((1, HG_DIM), F32)
    for d, h in chains:
        for g in range(rows_per_step // HG_FAST_BLOCK):
            block = lf_refs[d][0, h, g * HG_FAST_BLOCK:(g + 1) * HG_FAST_BLOCK, :]
            worst = jnp.minimum(worst, jnp.sum(block, axis=0, keepdims=True))
    mild = jnp.min(worst) >= -HG_FAST_MAX_LOG2

    def run(fast):
        n_sub = rows_per_step // c
        for sub in range(n_sub):
            work = []
            for d, h in chains:
                k = n_sub - 1 - sub if d else sub
                rows = slice(k * c, (k + 1) * c)
                work.append(_hgrn_chunk(q_refs[d].at[0, h, rows, :], lf_refs[d].at[0, h, rows, :],
                                        v_refs[d].at[0, h, rows, :], o_refs[d].at[0, h, rows, :],
                                        s_ref.at[d, h], sm_ref.at[d], pm_ref.at[d], fm_ref.at[d],
                                        b_scr.at[d, h], bool(d), fast))
            for _ in itertools.zip_longest(*work):
                pass

    pl.when(mild)(lambda: run(True))
    pl.when(jnp.logical_not(mild))(lambda: run(False))


def _hgrn(hq, lf_f, lf_b, hi, to_cast):
    b, nh, n, dk = hq.shape
    rows_per_step = HG_CHUNK * HG_STEP_CHUNKS if n % (HG_CHUNK * HG_STEP_CHUNKS) == 0 else HG_CHUNK
    nc = n // rows_per_step
    blk = (1, nh, rows_per_step, dk)
    fwd = pl.BlockSpec(blk, lambda i, j: (i, 0, j, 0))
    bwd = pl.BlockSpec(blk, lambda i, j: (i, 0, nc - 1 - j, 0))
    out = jax.ShapeDtypeStruct((b, nh, n, dk), F32)
    tables = _hgrn_tables()
    steps = b * nc

    def slab_spec(w):
        hold = next(h for h in range(1, steps + 1)
                    if steps % h == 0 and w.shape[0] % (steps // h * BF16_SUBLANES) == 0)
        return pl.BlockSpec((w.shape[0] // (steps // hold), w.shape[1]), lambda i, j: ((i * nc + j) // hold, 0))

    slabs = [slab_spec(w) for w in to_cast]
    res = pl.pallas_call(
        _hgrn_kernel,
        grid=(b, nc),
        in_specs=[fwd, fwd, fwd, bwd, bwd, bwd] + [_const_spec(t.shape) for t in tables] + slabs,
        out_specs=[fwd, bwd] + slabs,
        out_shape=[out, out] + [jax.ShapeDtypeStruct(w.shape, BF16) for w in to_cast],
        scratch_shapes=[pltpu.VMEM((2, nh, dk, dk), F32), pltpu.VMEM((2, nh, HG_CHUNK, dk), F32)],
        compiler_params=_params("parallel", "arbitrary"),
        name="hgrn",
    )(hq, lf_f, hi, hq, lf_b, hi, jnp.asarray(tables[0], BF16), jnp.asarray(tables[1]), jnp.asarray(tables[2]),
      *to_cast)
    return res[0], res[1], res[2:]


def _mix_cross_kernel(x_ref, att_ref, of_ref, ob_ref, sg_ref, og_ref, wo_ref, pmg_ref, pxg_ref,
                      wq_ref, km_ref, vm_ref, wxo_ref, poxg_ref, pfg_ref, x2_ref, h3_ref):
    tm = x_ref.shape[1]
    part = tm // MIX_ROW_PARTS

    def rows_of(lo):
        rows = slice(lo, lo + part)
        rec = []
        for i in range(HG_HEADS):
            o = _rms(of_ref[0, i, rows, :] + ob_ref[0, i, rows, :], og_ref[...])
            rec.append((o * sg_ref[0, rows, i * HG_DIM:(i + 1) * HG_DIM]).astype(BF16))
        att_t = att_ref[0, :, :, rows].reshape(ATT_Q_DIM, part)
        mixed = _dot_tn(att_t, wo_ref[:ATT_Q_DIM, :].astype(BF16)) + \
            _dot(jnp.concatenate(rec, axis=-1), wo_ref[ATT_Q_DIM:, :].astype(BF16))
        yield
        x1 = x_ref[0, rows, :] + _rms(mixed, pmg_ref[...])
        h2 = _rms(x1, pxg_ref[...]).astype(BF16)
        d = h2.shape[-1]
        dh = d // X_HEADS
        q = _dot(h2, wq_ref[...].astype(BF16)) * (dh ** -0.5)
        yield
        heads = []
        for i in range(X_HEADS):
            sl = slice(i * dh, (i + 1) * dh)
            s = _dot_nt(q[:, sl].astype(BF16), km_ref[0, :, sl])
            p = jnp.exp(s - jnp.max(s, axis=-1, keepdims=True))
            l = jnp.sum(p, axis=-1, keepdims=True)
            heads.append((_dot(p.astype(BF16), vm_ref[0, :, sl]) / l).astype(BF16))
        yield
        xo = _dot(jnp.concatenate(heads, axis=-1), wxo_ref[...].astype(BF16))
        yield
        x2 = x1 + _rms(xo, poxg_ref[...])
        x2_ref[0, rows, :] = x2
        h3_ref[0, rows, :] = _rms(x2, pfg_ref[...]).astype(BF16)

    for _ in itertools.zip_longest(*[rows_of(j * part) for j in range(MIX_ROW_PARTS)]):
        pass


def _mix_cross(x, att, o_f, o_b, sg, og, w_out, pmg, pxg, w_xq, k_mem, v_mem, w_xo, poxg, pfg, tm):
    b, n, d = x.shape
    nm = k_mem.shape[1]
    tok = lambda i, j: (i, j, 0)
    head = lambda i, j: (i, 0, j, 0)
    batch = lambda i, j: (i, 0, 0)
    vec = _const_spec((1, d))
    mat = _resident_spec((d, d))
    return pl.pallas_call(
        _mix_cross_kernel,
        grid=(b, n // tm),
        in_specs=[pl.BlockSpec((1, tm, d), tok),
                  pl.BlockSpec((1, ATT_HEADS, ATT_HEAD_DIM, tm), lambda i, j: (i, 0, 0, j)),
                  pl.BlockSpec((1, HG_HEADS, tm, HG_DIM), head), pl.BlockSpec((1, HG_HEADS, tm, HG_DIM), head),
                  pl.BlockSpec((1, tm, HG_WIDTH), tok), _const_spec((1, HG_DIM)),
                  mat, vec, vec, mat,
                  pl.BlockSpec((1, nm, d), batch), pl.BlockSpec((1, nm, d), batch),
                  mat, vec, vec],
        out_specs=[pl.BlockSpec((1, tm, d), tok), pl.BlockSpec((1, tm, d), tok)],
        out_shape=[jax.ShapeDtypeStruct((b, n, d), F32), jax.ShapeDtypeStruct((b, n, d), BF16)],
        compiler_params=_params("parallel", "parallel"),
        name="mix_cross",
    )(x, att, o_f, o_b, sg, og, w_out, pmg, pxg, w_xq, k_mem, v_mem, w_xo, poxg, pfg)


def _conv_ffn_kernel(x_ref, h_ref, hp_ref, hn_ref, wu_ref, cw_ref, cb_ref, wd_ref, g_ref, o_ref, act_scr):
    j = pl.program_id(1)
    tm = h_ref.shape[1]
    d_ff = wd_ref.shape[0]
    ck = FF_CHUNK
    prev = jnp.where(j > 0, hp_ref[0], jnp.zeros_like(hp_ref[0]))
    nxt = jnp.where(j < pl.num_programs(1) - 1, hn_ref[0], jnp.zeros_like(hn_ref[0]))
    hext = jnp.concatenate([prev, h_ref[0], nxt], axis=0)

    def conv(col):
        u = _dot(hext, wu_ref[:, col:col + ck])
        taps = (pltpu.roll(u, 1, 0), u, pltpu.roll(u, u.shape[0] - 1, 0))
        out = cb_ref[:, col:col + ck]
        for t in range(CONV_W):
            out = out + taps[t][HALO:HALO + tm] * cw_ref[t:t + 1, col:col + ck]
        return out

    for c in range(d_ff // ck):
        act_scr[:, c * ck:(c + 1) * ck] = (_silu(conv(c * ck)) * conv(d_ff + c * ck)).astype(BF16)
    o_ref[0] = x_ref[0] + _rms(_dot(act_scr[...], wd_ref[...]), g_ref[...])


def _conv_ffn(x2, h3, w_up, conv_w, conv_b, wd, g, tm):
    b, n, d = x2.shape
    d_ff = wd.shape[0]
    ck = FF_CHUNK
    hb = tm // HALO
    last = n // HALO - 1
    tok = lambda i, j: (i, j, 0)
    full = _resident_spec
    return pl.pallas_call(
        _conv_ffn_kernel,
        grid=(b, n // tm),
        in_specs=[pl.BlockSpec((1, tm, d), tok), pl.BlockSpec((1, tm, d), tok),
                  pl.BlockSpec((1, HALO, d), lambda i, j: (i, jnp.maximum(j * hb - 1, 0), 0)),
                  pl.BlockSpec((1, HALO, d), lambda i, j: (i, jnp.minimum((j + 1) * hb, last), 0)),
                  full((d, 2 * d_ff)), full((CONV_W, 2 * d_ff)), full((1, 2 * d_ff)),
                  full((d_ff, d)), _const_spec((1, d))],
        out_specs=pl.BlockSpec((1, tm, d), tok),
        out_shape=jax.ShapeDtypeStruct((b, n, d), F32),
        scratch_shapes=[pltpu.VMEM((tm, d_ff), BF16)],
        compiler_params=_params("parallel", "parallel"),
        name="conv_ffn",
    )(x2, h3, h3, h3, w_up, conv_w, conv_b, wd, g)


def _rope_tables(n):
    pairs = ATT_HEAD_DIM // 4
    pos = np.arange(n)
    inv = np.power(np.float32(ROPE_THETA), -np.arange(pairs, dtype=np.float32) / np.float32(pairs))
    ang = np.concatenate([(pos // GRID_W).astype(np.float32)[:, None] * inv,
                          (pos % GRID_W).astype(np.float32)[:, None] * inv], axis=-1)
    cos = np.repeat(np.cos(ang), 2, axis=-1)
    sin = np.repeat(np.sin(ang), 2, axis=-1) * np.tile(np.array([-1.0, 1.0], np.float32), ATT_HEAD_DIM // 2)
    return jnp.asarray(np.tile(cos, (1, 2)), F32), jnp.asarray(np.tile(sin, (1, 2)), F32)


def _segment_ones():
    i = jnp.arange(LANES) // ATT_HEAD_DIM
    return jnp.tile((i[:, None] == i[None, :]).astype(BF16), (2, 1))


def _layer(x, mem, lb, pre_mix_g, w_in, q_norm_g, k_norm_g, hg_out_norm_g, w_out, post_mix_g, pre_x_g,
           mem_norm_g, w_xq, w_xkv, w_xo, post_x_g, pre_ffn_g, w_up, conv_w, conv_b, w_down, post_ffn_g):
    b, n, d = x.shape
    d_ff = w_down.shape[0]
    assert n % HG_CHUNK == 0 and n % GRID_W == 0 and d_ff % FF_CHUNK == 0
    tm = min(256, n)
    tf = min(512, n)
    row = lambda g: g.reshape(1, -1).astype(F32)

    cos, sin = _rope_tables(n)
    qg = jnp.tile(q_norm_g, ATT_HEADS).reshape(1, -1) * (ATT_HEAD_DIM ** -0.5 * LOG2E)
    kg = jnp.tile(k_norm_g, ATT_KV_HEADS).reshape(1, -1)

    k_mem, v_mem = _mem_proj(mem, row(mem_norm_g), w_xkv)
    q, k, v_t, hq, lf_f, lf_b, hi, sg = _in_proj(
        x, row(pre_mix_g), w_in, qg, kg, cos, sin, lb,
        _segment_ones(), tf)
    att = _attention(q, k, v_t, tm)
    o_f, o_b, (w_up_bf16, w_down_bf16) = _hgrn(hq, lf_f, lf_b, hi, [w_up, w_down])
    x2, h3 = _mix_cross(x, att, o_f, o_b, sg, row(hg_out_norm_g), w_out, row(post_mix_g),
                        row(pre_x_g), w_xq, k_mem, v_mem, w_xo,
                        row(post_x_g), row(pre_ffn_g), tf)

    return _conv_ffn(x2, h3, w_up_bf16, conv_w, row(conv_b), w_down_bf16, row(post_ffn_g), tf)


def kernel(x, mem, pre_mix_g, w_in, q_norm_g, k_norm_g, hg_lb, hg_out_norm_g, w_out, post_mix_g, pre_x_g,
           mem_norm_g, w_xq, w_xkv, w_xo, post_x_g, pre_ffn_g, w_up, conv_w, conv_b, w_down, post_ffn_g):
    lb_all = jnp.cumsum(jax.nn.softmax(hg_lb.astype(F32), axis=1), axis=1)
    for l in range(w_in.shape[0]):
        x = _layer(x, mem, lb_all[:, l], pre_mix_g[l], w_in[l], q_norm_g[l], k_norm_g[l], hg_out_norm_g[l],
                   w_out[l], post_mix_g[l], pre_x_g[l], mem_norm_g[l], w_xq[l], w_xkv[l], w_xo[l],
                   post_x_g[l], pre_ffn_g[l], w_up[l], conv_w[l], conv_b[l], w_down[l], post_ffn_g[l])
    return x
```

```python
import functools
import itertools
import math

import numpy as np
import jax
import jax.numpy as jnp
from jax import lax
from jax.experimental import pallas as pl
from jax.experimental.pallas import tpu as pltpu

F32 = jnp.float32
BF16 = jnp.bfloat16

EPS = 1e-6
LOG2E = math.log2(math.e)
GRID_W = 64
ROPE_THETA = 10000.0

ATT_HEADS = 8
ATT_KV_HEADS = 2
ATT_GROUP = ATT_HEADS // ATT_KV_HEADS
ATT_HEAD_DIM = 64
ATT_Q_DIM = ATT_HEADS * ATT_HEAD_DIM
ATT_KV_DIM = ATT_KV_HEADS * ATT_HEAD_DIM
ATT_KEY_SPLITS = 2

HG_HEADS = 4
HG_DIM = 128
HG_WIDTH = HG_HEADS * HG_DIM
HG_CHUNK = 128
HG_STEP_CHUNKS = 4
HG_FAST_BLOCK = 64
HG_FAST_MAX_LOG2 = 96.0

X_HEADS = 4
IN_ROW_PARTS = 2
MIX_ROW_PARTS = 2
CONV_W = 3
FF_CHUNK = 256
HALO = 16

VMEM_LIMIT = 56 * 1024 * 1024
SUBLANES = 8
BF16_SUBLANES = 16
LANES = 128


def _params(*sem):
    return pltpu.CompilerParams(dimension_semantics=sem, vmem_limit_bytes=VMEM_LIMIT)


def _rms(x, g):
    ms = jnp.mean(x * x, axis=-1, keepdims=True)
    return x * lax.rsqrt(ms + EPS) * g


def _silu(x):
    return x / (1.0 + jnp.exp(-x))


def _dot(a, b):
    return jnp.dot(a, b, preferred_element_type=F32)


def _dot_nt(a, b):
    return lax.dot_general(a, b, (((1,), (1,)), ((), ())), preferred_element_type=F32)


def _dot_tn(a, b):
    return lax.dot_general(a, b, (((0,), (0,)), ((), ())), preferred_element_type=F32)


def _const_spec(shape):
    return pl.BlockSpec(shape, lambda *_: (0,) * len(shape))


def _resident_spec(shape):
    return pl.BlockSpec(shape, lambda *_: (0,) * len(shape), pipeline_mode=pl.Buffered(1))


def _mem_proj_kernel(m_ref, g_ref, w_ref, k_ref, v_ref):
    d = m_ref.shape[-1]
    m = _rms(m_ref[0], g_ref[...]).astype(BF16)
    kv = _dot(m, w_ref[...].astype(BF16))
    k_ref[0] = kv[:, :d].astype(BF16)
    v_ref[0] = kv[:, d:].astype(BF16)


def _mem_proj(mem, g, w_xkv):
    b, nm, d = mem.shape
    out = jax.ShapeDtypeStruct((b, nm, d), BF16)
    return pl.pallas_call(
        _mem_proj_kernel,
        grid=(b,),
        in_specs=[pl.BlockSpec((1, nm, d), lambda i: (i, 0, 0)),
                  _const_spec((1, d)), _resident_spec((d, 2 * d))],
        out_specs=[pl.BlockSpec((1, nm, d), lambda i: (i, 0, 0))] * 2,
        out_shape=[out, out],
        compiler_params=_params("parallel"),
        name="mem_proj",
    )(mem, g, w_xkv)


def _head_norm_rope(a, seg, gain, cos, sin_signed):
    w = a.shape[-1]
    sq = a * a
    hi = sq.astype(BF16)
    lo = (sq - hi.astype(F32)).astype(BF16)
    ss = [_dot(jnp.concatenate([hi[:, j:j + LANES], lo[:, j:j + LANES]], axis=-1), seg) for j in range(0, w, LANES)]
    ss = jnp.concatenate(ss, axis=-1) if len(ss) > 1 else ss[0]
    an = a * lax.rsqrt(ss * (1.0 / ATT_HEAD_DIM) + EPS) * gain
    lane = lax.broadcasted_iota(jnp.int32, a.shape, 1)
    partner = jnp.where(lane % 2 == 0, pltpu.roll(an, w - 1, 1), pltpu.roll(an, 1, 1))
    return an * cos + partner * sin_signed


def _in_proj_kernel(x_ref, g_ref, w_ref, qg_ref, kg_ref, cos_ref, sin_ref, lb_ref, seg_ref,
                    q_ref, k_ref, vt_ref, hq_ref, lff_ref, lfb_ref, hi_ref, hg_ref):
    part = x_ref.shape[1] // IN_ROW_PARTS

    def rows_of(r0):
        rows = slice(r0, r0 + part)
        h = _rms(x_ref[0, rows, :], g_ref[...]).astype(BF16)

        def proj(lo, width):
            return _dot(h, w_ref[:, lo:lo + width].astype(BF16))

        def heads_out(ref, val):
            for i in range(HG_HEADS):
                ref[0, i, rows, :] = val[:, i * HG_DIM:(i + 1) * HG_DIM].astype(ref.dtype)

        cos = cos_ref[rows, :]
        sin = sin_ref[rows, :]
        c0 = 0
        q = _head_norm_rope(proj(c0, ATT_Q_DIM), seg_ref[...], qg_ref[...],
                            jnp.concatenate([cos] * (ATT_Q_DIM // 128), axis=-1),
                            jnp.concatenate([sin] * (ATT_Q_DIM // 128), axis=-1))
        for i in range(ATT_HEADS):
            q_ref[0, i, rows, :] = q[:, i * ATT_HEAD_DIM:(i + 1) * ATT_HEAD_DIM].astype(BF16)
        c0 += ATT_Q_DIM
        yield
        k = _head_norm_rope(proj(c0, ATT_KV_DIM), seg_ref[...], kg_ref[...], cos, sin)
        c0 += ATT_KV_DIM
        v_t = jnp.transpose(proj(c0, ATT_KV_DIM))
        c0 += ATT_KV_DIM
        for i in range(ATT_KV_HEADS):
            k_ref[0, i, rows, :] = k[:, i * ATT_HEAD_DIM:(i + 1) * ATT_HEAD_DIM].astype(BF16)
            vt_ref[0, i, :, rows] = v_t[i * ATT_HEAD_DIM:(i + 1) * ATT_HEAD_DIM, :].astype(BF16)
        yield
        heads_out(hq_ref, _silu(proj(c0, HG_WIDTH)))
        c0 += HG_WIDTH
        yield
        for d, ref in enumerate((lff_ref, lfb_ref)):
            lb = lb_ref[d:d + 1, :]
            z = proj(c0, HG_WIDTH)
            heads_out(ref, jnp.log(lb + (1.0 - lb) / (1.0 + jnp.exp(-z))) * LOG2E)
            c0 += HG_WIDTH
            yield
        heads_out(hi_ref, proj(c0, HG_WIDTH))
        c0 += HG_WIDTH
        yield
        hg_ref[0, rows, :] = _silu(proj(c0, HG_WIDTH))

    for _ in itertools.zip_longest(*[rows_of(j * part) for j in range(IN_ROW_PARTS)]):
        pass


def _in_proj(x, g, w_in, qg, kg, cos, sin, lb, seg, tm):
    b, n, d = x.shape
    nt = n // tm
    n_in = w_in.shape[1]
    tok = lambda i, j: (i, j, 0)
    head = lambda i, j: (i, 0, j, 0)
    pos = lambda i, j: (j, 0)
    return pl.pallas_call(
        _in_proj_kernel,
        grid=(b, nt),
        in_specs=[pl.BlockSpec((1, tm, d), tok), _const_spec((1, d)), _resident_spec((d, n_in)),
                  _const_spec((1, ATT_Q_DIM)), _const_spec((1, ATT_KV_DIM)),
                  pl.BlockSpec((tm, 128), pos), pl.BlockSpec((tm, 128), pos),
                  _const_spec((2, HG_WIDTH)),
                  _const_spec((2 * LANES, LANES))],
        out_specs=[pl.BlockSpec((1, ATT_HEADS, tm, ATT_HEAD_DIM), head),
                   pl.BlockSpec((1, ATT_KV_HEADS, tm, ATT_HEAD_DIM), head),
                   pl.BlockSpec((1, ATT_KV_HEADS, ATT_HEAD_DIM, tm), lambda i, j: (i, 0, 0, j)),
                   pl.BlockSpec((1, HG_HEADS, tm, HG_DIM), head),
                   pl.BlockSpec((1, HG_HEADS, tm, HG_DIM), head),
                   pl.BlockSpec((1, HG_HEADS, tm, HG_DIM), head),
                   pl.BlockSpec((1, HG_HEADS, tm, HG_DIM), head),
                   pl.BlockSpec((1, tm, HG_WIDTH), tok)],
        out_shape=[jax.ShapeDtypeStruct((b, ATT_HEADS, n, ATT_HEAD_DIM), BF16),
                   jax.ShapeDtypeStruct((b, ATT_KV_HEADS, n, ATT_HEAD_DIM), BF16),
                   jax.ShapeDtypeStruct((b, ATT_KV_HEADS, ATT_HEAD_DIM, n), BF16),
                   jax.ShapeDtypeStruct((b, HG_HEADS, n, HG_DIM), BF16),
                   jax.ShapeDtypeStruct((b, HG_HEADS, n, HG_DIM), F32),
                   jax.ShapeDtypeStruct((b, HG_HEADS, n, HG_DIM), F32),
                   jax.ShapeDtypeStruct((b, HG_HEADS, n, HG_DIM), BF16),
                   jax.ShapeDtypeStruct((b, n, HG_WIDTH), F32)],
        compiler_params=_params("parallel", "parallel"),
        name="in_proj",
    )(x, g, w_in, qg, kg, cos, sin, lb, seg)


def _attention_kernel(q_ref, qn_ref, k_ref, vt_ref, o_ref, s_even, s_odd):
    n = k_ref.shape[2]
    part = n // ATT_KEY_SPLITS
    parts = [slice(j * part, (j + 1) * part) for j in range(ATT_KEY_SPLITS)]

    def scores(i, s_scr, queries=q_ref):
        for rows in parts:
            s_scr[rows, :] = _dot_nt(k_ref[0, i // ATT_GROUP, rows, :], queries[0, i])

    def finish(i, s_scr):
        s = [s_scr[rows, :] for rows in parts]
        m = functools.reduce(jnp.maximum, [jnp.max(x, axis=0, keepdims=True) for x in s])
        p = [jnp.exp2(x - m) for x in s]
        l = sum(jnp.sum(x, axis=0, keepdims=True) for x in p)
        o_t = sum(_dot(vt_ref[0, i // ATT_GROUP, :, rows], x.astype(BF16)) for rows, x in zip(parts, p))
        o_ref[0, i] = (o_t / l).astype(BF16)

    @pl.when(pl.program_id(1) == 0)
    def _():
        scores(0, s_even)

    def pair(j, carry):
        scores(2 * j + 1, s_odd)
        finish(2 * j, s_even)
        scores(2 * j + 2, s_even)
        finish(2 * j + 1, s_odd)
        return carry

    lax.fori_loop(0, ATT_HEADS // 2 - 1, pair, 0)
    scores(ATT_HEADS - 1, s_odd)
    finish(ATT_HEADS - 2, s_even)
    scores(0, s_even, qn_ref)
    finish(ATT_HEADS - 1, s_odd)


def _attention(q, k, v_t, tq):
    b, _, n, dh = q.shape
    last = n // tq - 1
    return pl.pallas_call(
        _attention_kernel,
        grid=(b, n // tq),
        in_specs=[pl.BlockSpec((1, ATT_HEADS, tq, dh), lambda i, j: (i, 0, j, 0)),
                  pl.BlockSpec((1, ATT_HEADS, tq, dh), lambda i, j: (i, 0, jnp.minimum(j + 1, last), 0)),
                  pl.BlockSpec((1, ATT_KV_HEADS, n, dh), lambda i, j: (i, 0, 0, 0)),
                  pl.BlockSpec((1, ATT_KV_HEADS, dh, n), lambda i, j: (i, 0, 0, 0))],
        out_specs=pl.BlockSpec((1, ATT_HEADS, dh, tq), lambda i, j: (i, 0, 0, j)),
        out_shape=jax.ShapeDtypeStruct((b, ATT_HEADS, dh, n), BF16),
        scratch_shapes=[pltpu.VMEM((n, tq), F32), pltpu.VMEM((n, tq), F32)],
        compiler_params=_params("parallel", "arbitrary"),
        name="attention",
    )(q, q, k, v_t)


def _hgrn_levels():
    m, out = 1, []
    while m < HG_CHUNK:
        out.append(m)
        m *= 2
    return out


def _hgrn_tables():
    c = HG_CHUNK
    t = np.arange(c)[:, None]
    u = np.arange(c)[None, :]
    mats, masks, fast = [], [], []
    for reverse in (False, True):
        blocks, pairs = [], [t == u]
        for m in _hgrn_levels():
            base = (t // (2 * m)) * (2 * m)
            if not reverse:
                r = base + m - 1
                block = np.where(t > r, (u > r) & (u <= t), (u > t) & (u <= r))
                roles = ((t // m) % 2 == 1) & ((u // m) % 2 == 0)
            else:
                r = base + m
                block = np.where(t < r, (u >= t) & (u < r), (u >= r) & (u < t))
                roles = ((t // m) % 2 == 0) & ((u // m) % 2 == 1)
            if m < SUBLANES:
                blocks.append(block)
            pairs.append(roles & (t // (2 * m) == u // (2 * m)))
        blocks.append(u >= t if reverse else u <= t)
        mats.append(np.tile(np.concatenate(blocks, axis=0), (1, 2)))
        masks.append(np.stack(pairs))
        fast.append((t // HG_FAST_BLOCK == u // HG_FAST_BLOCK) & (u >= t if reverse else u <= t))
    return tuple(np.stack(x).astype(np.float32) for x in (mats, masks, fast))


def _hgrn_sums(lf2, sum_mat):
    hi = lf2.astype(BF16)
    mid = (lf2 - hi.astype(F32)).astype(BF16)
    return _dot(sum_mat, jnp.concatenate([hi, mid], axis=0))


def _hgrn_chunk(q_ref, lf_ref, v_ref, o_ref, state_ref, sum_mat, pair_mask, fast_mask, b_scr, reverse, fast):
    c = HG_CHUNK
    levels = list(enumerate(_hgrn_levels()))
    n_small = sum(m < SUBLANES for _, m in levels)
    q, lf2, v, state = q_ref[...], lf_ref[...], v_ref[...], state_ref[...]
    qb = q.astype(BF16)
    kb = (1.0 - jnp.exp2(lf2)).astype(BF16)
    if fast:
        b = _hgrn_sums(lf2, sum_mat[n_small * c:, :])
        yield
        b_scr[...] = b
        fb = HG_FAST_BLOCK
        blocks = [slice(g * fb, (g + 1) * fb) for g in range(c // fb)]
        before, whole = [], []
        for rows in blocks:
            r0 = rows.stop if reverse else rows.start - 1
            r1 = rows.start if reverse else rows.stop - 1
            before.append(b_scr[r0:r0 + 1, :] if 0 <= r0 < c else jnp.zeros((1, HG_DIM), F32))
            whole.append(b_scr[r1:r1 + 1, :] - before[-1])
        loc = jnp.concatenate([b[rows] - before[g] for g, rows in enumerate(blocks)], axis=0)
        qs = qb * jnp.exp2(loc).astype(BF16)
        ks = kb * jnp.exp2(-loc).astype(BF16)

        def per_block(x, log2_rows):
            return jnp.concatenate(
                [x[rows] if e is None else x[rows] * jnp.broadcast_to(jnp.exp2(e), (fb, HG_DIM)).astype(BF16)
                 for rows, e in zip(blocks, log2_rows)], axis=0)

        a = _dot_nt(qs, ks) * fast_mask[...]
        yield
        for j, m in levels:
            if m == fb:
                later_is_odd = not reverse
                k_side = per_block(ks, [None if (g % 2 == 1) == later_is_odd else whole[g]
                                        for g in range(len(blocks))])
                a = a + _dot_nt(qs, k_side) * pair_mask[j + 1]
                yield
        levels = [(j, m) for j, m in levels if m > fb]
    else:
        sums = _hgrn_sums(lf2, sum_mat[...])
        yield
        b = sums[n_small * c:]
        b_scr[...] = b
        a = _dot_nt(qb, kb) * pair_mask[0]
        yield
    for j, m in levels:
        if m < SUBLANES:
            e = sums[j * c:(j + 1) * c]
        else:
            parts = []
            for g in range(c // (2 * m)):
                lo, mid_row, up = 2 * m * g, 2 * m * g + m, 2 * m * (g + 1)
                r = mid_row if reverse else mid_row - 1
                b_r = b_scr[r:r + 1, :]
                parts += [b[lo:mid_row] - b_r, b_r - b[mid_row:up]] if reverse else \
                         [b_r - b[lo:mid_row], b[mid_row:up] - b_r]
            e = jnp.concatenate(parts, axis=0)
        fac = jnp.exp2(e).astype(BF16)
        a = a + _dot_nt(qb * fac, kb * fac) * pair_mask[j + 1]
        yield
    vb = v.astype(BF16)
    last = 0 if reverse else c - 1
    total = b_scr[last:last + 1, :]
    if fast:
        q_dec = per_block(qs, before)
        k_dec = per_block(ks, [total - e for e in before])
    else:
        q_dec = qb * jnp.exp2(b).astype(BF16)
        k_dec = kb * jnp.exp2(total - b).astype(BF16)
    o_ref[...] = _dot(a.astype(BF16), vb) + _dot_nt(q_dec, state.astype(BF16))
    yield
    state_ref[...] = state * jnp.exp2(total) + _dot_tn(vb, k_dec)


def _hgrn_kernel(qf_ref, lff_ref, vf_ref, qb_ref, lfb_ref, vb_ref, sm_ref, pm_ref, fm_ref, *rest):
    n_cast = (len(rest) - 4) // 2
    cast_in, (of_ref, ob_ref), cast_out = rest[:n_cast], rest[n_cast:n_cast + 2], rest[n_cast + 2:2 * n_cast + 2]
    s_ref, b_scr = rest[2 * n_cast + 2:]

    @pl.when(pl.program_id(1) == 0)
    def _():
        s_ref[...] = jnp.zeros_like(s_ref)

    for src, dst in zip(cast_in, cast_out):
        dst[...] = src[...].astype(dst.dtype)

    c = HG_CHUNK
    rows_per_step = qf_ref.shape[2]
    chains = [(d, h) for h in range(HG_HEADS) for d in (0, 1)]
    q_refs, lf_refs, v_refs, o_refs = (qf_ref, qb_ref), (lff_ref, lfb_ref), (vf_ref, vb_ref), (of_ref, ob_ref)

    worst = jnp.zeros((1, HG_DIM), F32)
    for d, h in chains:
        for g in range(rows_per_step // HG_FAST_BLOCK):
            block = lf_refs[d][0, h, g * HG_FAST_BLOCK:(g + 1) * HG_FAST_BLOCK, :]
            worst = jnp.minimum(worst, jnp.sum(block, axis=0, keepdims=True))
    mild = jnp.min(worst) >= -HG_FAST_MAX_LOG2

    def run(fast):
        n_sub = rows_per_step // c
        for sub in range(n_sub):
            work = []
            for d, h in chains:
                k = n_sub - 1 - sub if d else sub
                rows = slice(k * c, (k + 1) * c)
                work.append(_hgrn_chunk(q_refs[d].at[0, h, rows, :], lf_refs[d].at[0, h, rows, :],
                                        v_refs[d].at[0, h, rows, :], o_refs[d].at[0, h, rows, :],
                                        s_ref.at[d, h], sm_ref.at[d], pm_ref.at[d], fm_ref.at[d],
                                        b_scr.at[d, h], bool(d), fast))
            for _ in itertools.zip_longest(*work):
                pass

    pl.when(mild)(lambda: run(True))
    pl.when(jnp.logical_not(mild))(lambda: run(False))


def _hgrn(hq, lf_f, lf_b, hi, to_cast):
    b, nh, n, dk = hq.shape
    rows_per_step = HG_CHUNK * HG_STEP_CHUNKS if n % (HG_CHUNK * HG_STEP_CHUNKS) == 0 else HG_CHUNK
    nc = n // rows_per_step
    blk = (1, nh, rows_per_step, dk)
    fwd = pl.BlockSpec(blk, lambda i, j: (i, 0, j, 0))
    bwd = pl.BlockSpec(blk, lambda i, j: (i, 0, nc - 1 - j, 0))
    out = jax.ShapeDtypeStruct((b, nh, n, dk), F32)
    tables = _hgrn_tables()
    steps = b * nc

    def slab_spec(w):
        hold = next(h for h in range(1, steps + 1)
                    if steps % h == 0 and w.shape[0] % (steps // h * BF16_SUBLANES) == 0)
        return pl.BlockSpec((w.shape[0] // (steps // hold), w.shape[1]), lambda i, j: ((i * nc + j) // hold, 0))

    slabs = [slab_spec(w) for w in to_cast]
    res = pl.pallas_call(
        _hgrn_kernel,
        grid=(b, nc),
        in_specs=[fwd, fwd, fwd, bwd, bwd, bwd] + [_const_spec(t.shape) for t in tables] + slabs,
        out_specs=[fwd, bwd] + slabs,
        out_shape=[out, out] + [jax.ShapeDtypeStruct(w.shape, BF16) for w in to_cast],
        scratch_shapes=[pltpu.VMEM((2, nh, dk, dk), F32), pltpu.VMEM((2, nh, HG_CHUNK, dk), F32)],
        compiler_params=_params("parallel", "arbitrary"),
        name="hgrn",
    )(hq, lf_f, hi, hq, lf_b, hi, jnp.asarray(tables[0], BF16), jnp.asarray(tables[1]), jnp.asarray(tables[2]),
      *to_cast)
    return res[0], res[1], res[2:]


def _mix_cross_kernel(x_ref, att_ref, of_ref, ob_ref, sg_ref, og_ref, wo_ref, pmg_ref, pxg_ref,
                      wq_ref, km_ref, vm_ref, wxo_ref, poxg_ref, pfg_ref, x2_ref, h3_ref):
    tm = x_ref.shape[1]
    part = tm // MIX_ROW_PARTS

    def rows_of(lo):
        rows = slice(lo, lo + part)
        rec = []
        for i in range(HG_HEADS):
            o = _rms(of_ref[0, i, rows, :] + ob_ref[0, i, rows, :], og_ref[...])
            rec.append((o * sg_ref[0, rows, i * HG_DIM:(i + 1) * HG_DIM]).astype(BF16))
        att_t = att_ref[0, :, :, rows].reshape(ATT_Q_DIM, part)
        mixed = _dot_tn(att_t, wo_ref[:ATT_Q_DIM, :].astype(BF16)) + \
            _dot(jnp.concatenate(rec, axis=-1), wo_ref[ATT_Q_DIM:, :].astype(BF16))
        yield
        x1 = x_ref[0, rows, :] + _rms(mixed, pmg_ref[...])
        h2 = _rms(x1, pxg_ref[...]).astype(BF16)
        d = h2.shape[-1]
        dh = d // X_HEADS
        q = _dot(h2, wq_ref[...].astype(BF16)) * (dh ** -0.5)
        yield
        heads = []
        for i in range(X_HEADS):
            sl = slice(i * dh, (i + 1) * dh)
            s = _dot_nt(q[:, sl].astype(BF16), km_ref[0, :, sl])
            p = jnp.exp(s - jnp.max(s, axis=-1, keepdims=True))
            l = jnp.sum(p, axis=-1, keepdims=True)
            heads.append((_dot(p.astype(BF16), vm_ref[0, :, sl]) / l).astype(BF16))
        yield
        xo = _dot(jnp.concatenate(heads, axis=-1), wxo_ref[...].astype(BF16))
        yield
        x2 = x1 + _rms(xo, poxg_ref[...])
        x2_ref[0, rows, :] = x2
        h3_ref[0, rows, :] = _rms(x2, pfg_ref[...]).astype(BF16)

    for _ in itertools.zip_longest(*[rows_of(j * part) for j in range(MIX_ROW_PARTS)]):
        pass


def _mix_cross(x, att, o_f, o_b, sg, og, w_out, pmg, pxg, w_xq, k_mem, v_mem, w_xo, poxg, pfg, tm):
    b, n, d = x.shape
    nm = k_mem.shape[1]
    tok = lambda i, j: (i, j, 0)
    head = lambda i, j: (i, 0, j, 0)
    batch = lambda i, j: (i, 0, 0)
    vec = _const_spec((1, d))
    mat = _resident_spec((d, d))
    return pl.pallas_call(
        _mix_cross_kernel,
        grid=(b, n // tm),
        in_specs=[pl.BlockSpec((1, tm, d), tok),
                  pl.BlockSpec((1, ATT_HEADS, ATT_HEAD_DIM, tm), lambda i, j: (i, 0, 0, j)),
                  pl.BlockSpec((1, HG_HEADS, tm, HG_DIM), head), pl.BlockSpec((1, HG_HEADS, tm, HG_DIM), head),
                  pl.BlockSpec((1, tm, HG_WIDTH), tok), _const_spec((1, HG_DIM)),
                  mat, vec, vec, mat,
                  pl.BlockSpec((1, nm, d), batch), pl.BlockSpec((1, nm, d), batch),
                  mat, vec, vec],
        out_specs=[pl.BlockSpec((1, tm, d), tok), pl.BlockSpec((1, tm, d), tok)],
        out_shape=[jax.ShapeDtypeStruct((b, n, d), F32), jax.ShapeDtypeStruct((b, n, d), BF16)],
        compiler_params=_params("parallel", "parallel"),
        name="mix_cross",
    )(x, att, o_f, o_b, sg, og, w_out, pmg, pxg, w_xq, k_mem, v_mem, w_xo, poxg, pfg)


def _conv_ffn_kernel(x_ref, h_ref, hp_ref, hn_ref, wu_ref, cw_ref, cb_ref, wd_ref, g_ref, o_ref, act_scr):
    j = pl.program_id(1)
    tm = h_ref.shape[1]
    d_ff = wd_ref.shape[0]
    ck = FF_CHUNK
    prev = jnp.where(j > 0, hp_ref[0], jnp.zeros_like(hp_ref[0]))
    nxt = jnp.where(j < pl.num_programs(1) - 1, hn_ref[0], jnp.zeros_like(hn_ref[0]))
    hext = jnp.concatenate([prev, h_ref[0], nxt], axis=0)

    def conv(col):
        u = _dot(hext, wu_ref[:, col:col + ck])
        taps = (pltpu.roll(u, 1, 0), u, pltpu.roll(u, u.shape[0] - 1, 0))
        out = cb_ref[:, col:col + ck]
        for t in range(CONV_W):
            out = out + taps[t][HALO:HALO + tm] * cw_ref[t:t + 1, col:col + ck]
        return out

    for c in range(d_ff // ck):
        act_scr[:, c * ck:(c + 1) * ck] = (_silu(conv(c * ck)) * conv(d_ff + c * ck)).astype(BF16)
    o_ref[0] = x_ref[0] + _rms(_dot(act_scr[...], wd_ref[...]), g_ref[...])


def _conv_ffn(x2, h3, w_up, conv_w, conv_b, wd, g, tm):
    b, n, d = x2.shape
    d_ff = wd.shape[0]
    ck = FF_CHUNK
    hb = tm // HALO
    last = n // HALO - 1
    tok = lambda i, j: (i, j, 0)
    full = _resident_spec
    return pl.pallas_call(
        _conv_ffn_kernel,
        grid=(b, n // tm),
        in_specs=[pl.BlockSpec((1, tm, d), tok), pl.BlockSpec((1, tm, d), tok),
                  pl.BlockSpec((1, HALO, d), lambda i, j: (i, jnp.maximum(j * hb - 1, 0), 0)),
                  pl.BlockSpec((1, HALO, d), lambda i, j: (i, jnp.minimum((j + 1) * hb, last), 0)),
                  full((d, 2 * d_ff)), full((CONV_W, 2 * d_ff)), full((1, 2 * d_ff)),
                  full((d_ff, d)), _const_spec((1, d))],
        out_specs=pl.BlockSpec((1, tm, d), tok),
        out_shape=jax.ShapeDtypeStruct((b, n, d), F32),
        scratch_shapes=[pltpu.VMEM((tm, d_ff), BF16)],
        compiler_params=_params("parallel", "parallel"),
        name="conv_ffn",
    )(x2, h3, h3, h3, w_up, conv_w, conv_b, wd, g)


def _rope_tables(n):
    pairs = ATT_HEAD_DIM // 4
    pos = np.arange(n)
    inv = np.power(np.float32(ROPE_THETA), -np.arange(pairs, dtype=np.float32) / np.float32(pairs))
    ang = np.concatenate([(pos // GRID_W).astype(np.float32)[:, None] * inv,
                          (pos % GRID_W).astype(np.float32)[:, None] * inv], axis=-1)
    cos = np.repeat(np.cos(ang), 2, axis=-1)
    sin = np.repeat(np.sin(ang), 2, axis=-1) * np.tile(np.array([-1.0, 1.0], np.float32), ATT_HEAD_DIM // 2)
    return jnp.asarray(np.tile(cos, (1, 2)), F32), jnp.asarray(np.tile(sin, (1, 2)), F32)


def _segment_ones():
    i = jnp.arange(LANES) // ATT_HEAD_DIM
    return jnp.tile((i[:, None] == i[None, :]).astype(BF16), (2, 1))


def _layer(x, mem, lb, pre_mix_g, w_in, q_norm_g, k_norm_g, hg_out_norm_g, w_out, post_mix_g, pre_x_g,
           mem_norm_g, w_xq, w_xkv, w_xo, post_x_g, pre_ffn_g, w_up, conv_w, conv_b, w_down, post_ffn_g):
    b, n, d = x.shape
    d_ff = w_down.shape[0]
    assert n % HG_CHUNK == 0 and n % GRID_W == 0 and d_ff % FF_CHUNK == 0
    tm = min(256, n)
    tf = min(512, n)
    row = lambda g: g.reshape(1, -1).astype(F32)

    cos, sin = _rope_tables(n)
    qg = jnp.tile(q_norm_g, ATT_HEADS).reshape(1, -1) * (ATT_HEAD_DIM ** -0.5 * LOG2E)
    kg = jnp.tile(k_norm_g, ATT_KV_HEADS).reshape(1, -1)

    k_mem, v_mem = _mem_proj(mem, row(mem_norm_g), w_xkv)
    q, k, v_t, hq, lf_f, lf_b, hi, sg = _in_proj(
        x, row(pre_mix_g), w_in, qg, kg, cos, sin, lb,
        _segment_ones(), tf)
    att = _attention(q, k, v_t, tf)
    o_f, o_b, (w_up_bf16, w_down_bf16) = _hgrn(hq, lf_f, lf_b, hi, [w_up, w_down])
    x2, h3 = _mix_cross(x, att, o_f, o_b, sg, row(hg_out_norm_g), w_out, row(post_mix_g),
                        row(pre_x_g), w_xq, k_mem, v_mem, w_xo,
                        row(post_x_g), row(pre_ffn_g), tf)

    return _conv_ffn(x2, h3, w_up_bf16, conv_w, row(conv_b), w_down_bf16, row(post_ffn_g), tf)


def kernel(x, mem, pre_mix_g, w_in, q_norm_g, k_norm_g, hg_lb, hg_out_norm_g, w_out, post_mix_g, pre_x_g,
           mem_norm_g, w_xq, w_xkv, w_xo, post_x_g, pre_ffn_g, w_up, conv_w, conv_b, w_down, post_ffn_g):
    lb_all = jnp.cumsum(jax.nn.softmax(hg_lb.astype(F32), axis=1), axis=1)
    for l in range(w_in.shape[0]):
        x = _layer(x, mem, lb_all[:, l], pre_mix_g[l], w_in[l], q_norm_g[l], k_norm_g[l], hg_out_norm_g[l],
                   w_out[l], post_mix_g[l], pre_x_g[l], mem_norm_g[l], w_xq[l], w_xkv[l], w_xo[l],
                   post_x_g[l], pre_ffn_g[l], w_up[l], conv_w[l], conv_b[l], w_down[l], post_ffn_g[l])
    return x
```

```python
import functools
import itertools
import math

import numpy as np
import jax
import jax.numpy as jnp
from jax import lax
from jax.experimental import pallas as pl
from jax.experimental.pallas import tpu as pltpu

F32 = jnp.float32
BF16 = jnp.bfloat16

EPS = 1e-6
LOG2E = math.log2(math.e)
GRID_W = 64
ROPE_THETA = 10000.0

ATT_HEADS = 8
ATT_KV_HEADS = 2
ATT_GROUP = ATT_HEADS // ATT_KV_HEADS
ATT_HEAD_DIM = 64
ATT_Q_DIM = ATT_HEADS * ATT_HEAD_DIM
ATT_KV_DIM = ATT_KV_HEADS * ATT_HEAD_DIM
ATT_KEY_SPLITS = 2

HG_HEADS = 4
HG_DIM = 128
HG_WIDTH = HG_HEADS * HG_DIM
HG_CHUNK = 128
HG_STEP_CHUNKS = 4
HG_FAST_BLOCK = 64
HG_FAST_MAX_LOG2 = 96.0

X_HEADS = 4
IN_ROW_PARTS = 2
MIX_ROW_PARTS = 2
CONV_W = 3
FF_CHUNK = 256
HALO = 16

VMEM_LIMIT = 56 * 1024 * 1024
SUBLANES = 8
BF16_SUBLANES = 16
LANES = 128


def _params(*sem):
    return pltpu.CompilerParams(dimension_semantics=sem, vmem_limit_bytes=VMEM_LIMIT)


def _rms(x, g):
    ms = jnp.mean(x * x, axis=-1, keepdims=True)
    return x * lax.rsqrt(ms + EPS) * g


def _silu(x):
    return x / (1.0 + jnp.exp(-x))


def _dot(a, b):
    return jnp.dot(a, b, preferred_element_type=F32)


def _dot_nt(a, b):
    return lax.dot_general(a, b, (((1,), (1,)), ((), ())), preferred_element_type=F32)


def _dot_tn(a, b):
    return lax.dot_general(a, b, (((0,), (0,)), ((), ())), preferred_element_type=F32)


def _const_spec(shape):
    return pl.BlockSpec(shape, lambda *_: (0,) * len(shape))


def _resident_spec(shape):
    return pl.BlockSpec(shape, lambda *_: (0,) * len(shape), pipeline_mode=pl.Buffered(1))


def _mem_proj_kernel(m_ref, g_ref, w_ref, k_ref, v_ref):
    d = m_ref.shape[-1]
    m = _rms(m_ref[0], g_ref[...]).astype(BF16)
    kv = _dot(m, w_ref[...].astype(BF16))
    k_ref[0] = kv[:, :d].astype(BF16)
    v_ref[0] = kv[:, d:].astype(BF16)


def _mem_proj(mem, g, w_xkv):
    b, nm, d = mem.shape
    out = jax.ShapeDtypeStruct((b, nm, d), BF16)
    return pl.pallas_call(
        _mem_proj_kernel,
        grid=(b,),
        in_specs=[pl.BlockSpec((1, nm, d), lambda i: (i, 0, 0)),
                  _const_spec((1, d)), _resident_spec((d, 2 * d))],
        out_specs=[pl.BlockSpec((1, nm, d), lambda i: (i, 0, 0))] * 2,
        out_shape=[out, out],
        compiler_params=_params("parallel"),
        name="mem_proj",
    )(mem, g, w_xkv)


def _head_norm_rope(a, seg, gain, cos, sin_signed):
    w = a.shape[-1]
    sq = a * a
    hi = sq.astype(BF16)
    lo = (sq - hi.astype(F32)).astype(BF16)
    ss = [_dot(jnp.concatenate([hi[:, j:j + LANES], lo[:, j:j + LANES]], axis=-1), seg) for j in range(0, w, LANES)]
    ss = jnp.concatenate(ss, axis=-1) if len(ss) > 1 else ss[0]
    an = a * lax.rsqrt(ss * (1.0 / ATT_HEAD_DIM) + EPS) * gain
    lane = lax.broadcasted_iota(jnp.int32, a.shape, 1)
    partner = jnp.where(lane % 2 == 0, pltpu.roll(an, w - 1, 1), pltpu.roll(an, 1, 1))
    return an * cos + partner * sin_signed


def _in_proj_kernel(x_ref, g_ref, w_ref, qg_ref, kg_ref, cos_ref, sin_ref, lb_ref, seg_ref,
                    q_ref, k_ref, vt_ref, hq_ref, lff_ref, lfb_ref, hi_ref, hg_ref):
    part = x_ref.shape[1] // IN_ROW_PARTS

    def rows_of(r0):
        rows = slice(r0, r0 + part)
        h = _rms(x_ref[0, rows, :], g_ref[...]).astype(BF16)

        def proj(lo, width):
            return _dot(h, w_ref[:, lo:lo + width].astype(BF16))

        def heads_out(ref, val):
            for i in range(HG_HEADS):
                ref[0, i, rows, :] = val[:, i * HG_DIM:(i + 1) * HG_DIM].astype(ref.dtype)

        cos = cos_ref[rows, :]
        sin = sin_ref[rows, :]
        c0 = 0
        q = _head_norm_rope(proj(c0, ATT_Q_DIM), seg_ref[...], qg_ref[...],
                            jnp.concatenate([cos] * (ATT_Q_DIM // 128), axis=-1),
                            jnp.concatenate([sin] * (ATT_Q_DIM // 128), axis=-1))
        for i in range(ATT_HEADS):
            q_ref[0, i, rows, :] = q[:, i * ATT_HEAD_DIM:(i + 1) * ATT_HEAD_DIM].astype(BF16)
        c0 += ATT_Q_DIM
        yield
        k = _head_norm_rope(proj(c0, ATT_KV_DIM), seg_ref[...], kg_ref[...], cos, sin)
        c0 += ATT_KV_DIM
        v_t = jnp.transpose(proj(c0, ATT_KV_DIM))
        c0 += ATT_KV_DIM
        for i in range(ATT_KV_HEADS):
            k_ref[0, i, rows, :] = k[:, i * ATT_HEAD_DIM:(i + 1) * ATT_HEAD_DIM].astype(BF16)
            vt_ref[0, i, :, rows] = v_t[i * ATT_HEAD_DIM:(i + 1) * ATT_HEAD_DIM, :].astype(BF16)
        yield
        heads_out(hq_ref, _silu(proj(c0, HG_WIDTH)))
        c0 += HG_WIDTH
        yield
        for d, ref in enumerate((lff_ref, lfb_ref)):
            lb = lb_ref[d:d + 1, :]
            z = proj(c0, HG_WIDTH)
            heads_out(ref, jnp.log(lb + (1.0 - lb) / (1.0 + jnp.exp(-z))) * LOG2E)
            c0 += HG_WIDTH
            yield
        heads_out(hi_ref, proj(c0, HG_WIDTH))
        c0 += HG_WIDTH
        yield
        hg_ref[0, rows, :] = _silu(proj(c0, HG_WIDTH))

    for _ in itertools.zip_longest(*[rows_of(j * part) for j in range(IN_ROW_PARTS)]):
        pass


def _in_proj(x, g, w_in, qg, kg, cos, sin, lb, seg, tm):
    b, n, d = x.shape
    nt = n // tm
    n_in = w_in.shape[1]
    tok = lambda i, j: (i, j, 0)
    head = lambda i, j: (i, 0, j, 0)
    pos = lambda i, j: (j, 0)
    return pl.pallas_call(
        _in_proj_kernel,
        grid=(b, nt),
        in_specs=[pl.BlockSpec((1, tm, d), tok), _const_spec((1, d)), _resident_spec((d, n_in)),
                  _const_spec((1, ATT_Q_DIM)), _const_spec((1, ATT_KV_DIM)),
                  pl.BlockSpec((tm, 128), pos), pl.BlockSpec((tm, 128), pos),
                  _const_spec((2, HG_WIDTH)),
                  _const_spec((2 * LANES, LANES))],
        out_specs=[pl.BlockSpec((1, ATT_HEADS, tm, ATT_HEAD_DIM), head),
                   pl.BlockSpec((1, ATT_KV_HEADS, tm, ATT_HEAD_DIM), head),
                   pl.BlockSpec((1, ATT_KV_HEADS, ATT_HEAD_DIM, tm), lambda i, j: (i, 0, 0, j)),
                   pl.BlockSpec((1, HG_HEADS, tm, HG_DIM), head),
                   pl.BlockSpec((1, HG_HEADS, tm, HG_DIM), head),
                   pl.BlockSpec((1, HG_HEADS, tm, HG_DIM), head),
                   pl.BlockSpec((1, HG_HEADS, tm, HG_DIM), head),
                   pl.BlockSpec((1, tm, HG_WIDTH), tok)],
        out_shape=[jax.ShapeDtypeStruct((b, ATT_HEADS, n, ATT_HEAD_DIM), BF16),
                   jax.ShapeDtypeStruct((b, ATT_KV_HEADS, n, ATT_HEAD_DIM), BF16),
                   jax.ShapeDtypeStruct((b, ATT_KV_HEADS, ATT_HEAD_DIM, n), BF16),
                   jax.ShapeDtypeStruct((b, HG_HEADS, n, HG_DIM), BF16),
                   jax.ShapeDtypeStruct((b, HG_HEADS, n, HG_DIM), F32),
                   jax.ShapeDtypeStruct((b, HG_HEADS, n, HG_DIM), F32),
                   jax.ShapeDtypeStruct((b, HG_HEADS, n, HG_DIM), BF16),
                   jax.ShapeDtypeStruct((b, n, HG_WIDTH), F32)],
        compiler_params=_params("parallel", "parallel"),
        name="in_proj",
    )(x, g, w_in, qg, kg, cos, sin, lb, seg)


def _attention_kernel(q_ref, qn_ref, k_ref, vt_ref, o_ref, s_even, s_odd):
    n = k_ref.shape[2]
    part = n // ATT_KEY_SPLITS
    parts = [slice(j * part, (j + 1) * part) for j in range(ATT_KEY_SPLITS)]

    def scores(i, s_scr, queries=q_ref):
        for rows in parts:
            s_scr[rows, :] = _dot_nt(k_ref[0, i // ATT_GROUP, rows, :], queries[0, i])

    def finish(i, s_scr):
        s = [s_scr[rows, :] for rows in parts]
        m = functools.reduce(jnp.maximum, [jnp.max(x, axis=0, keepdims=True) for x in s])
        p = [jnp.exp2(x - m) for x in s]
        l = sum(jnp.sum(x, axis=0, keepdims=True) for x in p)
        o_t = sum(_dot(vt_ref[0, i // ATT_GROUP, :, rows], x.astype(BF16)) for rows, x in zip(parts, p))
        o_ref[0, i] = (o_t / l).astype(BF16)

    @pl.when(pl.program_id(1) == 0)
    def _():
        scores(0, s_even)

    def pair(j, carry):
        scores(2 * j + 1, s_odd)
        finish(2 * j, s_even)
        scores(2 * j + 2, s_even)
        finish(2 * j + 1, s_odd)
        return carry

    lax.fori_loop(0, ATT_HEADS // 2 - 1, pair, 0)
    scores(ATT_HEADS - 1, s_odd)
    finish(ATT_HEADS - 2, s_even)
    scores(0, s_even, qn_ref)
    finish(ATT_HEADS - 1, s_odd)


def _attention(q, k, v_t, tq):
    b, _, n, dh = q.shape
    last = n // tq - 1
    return pl.pallas_call(
        _attention_kernel,
        grid=(b, n // tq),
        in_specs=[pl.BlockSpec((1, ATT_HEADS, tq, dh), lambda i, j: (i, 0, j, 0)),
                  pl.BlockSpec((1, ATT_HEADS, tq, dh), lambda i, j: (i, 0, jnp.minimum(j + 1, last), 0)),
                  pl.BlockSpec((1, ATT_KV_HEADS, n, dh), lambda i, j: (i, 0, 0, 0)),
                  pl.BlockSpec((1, ATT_KV_HEADS, dh, n), lambda i, j: (i, 0, 0, 0))],
        out_specs=pl.BlockSpec((1, ATT_HEADS, dh, tq), lambda i, j: (i, 0, 0, j)),
        out_shape=jax.ShapeDtypeStruct((b, ATT_HEADS, dh, n), BF16),
        scratch_shapes=[pltpu.VMEM((n, tq), F32), pltpu.VMEM((n, tq), F32)],
        compiler_params=_params("parallel", "arbitrary"),
        name="attention",
    )(q, q, k, v_t)


def _hgrn_levels():
    m, out = 1, []
    while m < HG_CHUNK:
        out.append(m)
        m *= 2
    return out


def _hgrn_tables():
    c = HG_CHUNK
    t = np.arange(c)[:, None]
    u = np.arange(c)[None, :]
    mats, masks, fast = [], [], []
    for reverse in (False, True):
        blocks, pairs = [], [t == u]
        for m in _hgrn_levels():
            base = (t // (2 * m)) * (2 * m)
            if not reverse:
                r = base + m - 1
                block = np.where(t > r, (u > r) & (u <= t), (u > t) & (u <= r))
                roles = ((t // m) % 2 == 1) & ((u // m) % 2 == 0)
            else:
                r = base + m
                block = np.where(t < r, (u >= t) & (u < r), (u >= r) & (u < t))
                roles = ((t // m) % 2 == 0) & ((u // m) % 2 == 1)
            if m < SUBLANES:
                blocks.append(block)
            pairs.append(roles & (t // (2 * m) == u // (2 * m)))
        blocks.append(u >= t if reverse else u <= t)
        mats.append(np.tile(np.concatenate(blocks, axis=0), (1, 2)))
        masks.append(np.stack(pairs))
        fast.append((t // HG_FAST_BLOCK == u // HG_FAST_BLOCK) & (u >= t if reverse else u <= t))
    return tuple(np.stack(x).astype(np.float32) for x in (mats, masks, fast))


def _hgrn_sums(lf2, sum_mat):
    hi = lf2.astype(BF16)
    mid = (lf2 - hi.astype(F32)).astype(BF16)
    return _dot(sum_mat, jnp.concatenate([hi, mid], axis=0))


def _hgrn_chunk(q_ref, lf_ref, v_ref, o_ref, state_ref, sum_mat, pair_mask, fast_mask, b_scr, reverse, fast):
    c = HG_CHUNK
    levels = list(enumerate(_hgrn_levels()))
    n_small = sum(m < SUBLANES for _, m in levels)
    q, lf2, v, state = q_ref[...], lf_ref[...], v_ref[...], state_ref[...]
    qb = q.astype(BF16)
    kb = (1.0 - jnp.exp2(lf2)).astype(BF16)
    if fast:
        b = _hgrn_sums(lf2, sum_mat[n_small * c:, :])
        yield
        b_scr[...] = b
        parts = []
        for g in range(c // HG_FAST_BLOCK):
            lo, up = g * HG_FAST_BLOCK, (g + 1) * HG_FAST_BLOCK
            before = up if reverse else lo - 1
            parts.append(b[lo:up] - b_scr[before:before + 1, :] if 0 <= before < c else b[lo:up])
        loc = jnp.concatenate(parts, axis=0)
        a = _dot_nt(qb * jnp.exp2(loc).astype(BF16), kb * jnp.exp2(-loc).astype(BF16)) * fast_mask[...]
        yield
        levels = [(j, m) for j, m in levels if m >= HG_FAST_BLOCK]
    else:
        sums = _hgrn_sums(lf2, sum_mat[...])
        yield
        b = sums[n_small * c:]
        b_scr[...] = b
        a = _dot_nt(qb, kb) * pair_mask[0]
        yield
    for j, m in levels:
        if m < SUBLANES:
            e = sums[j * c:(j + 1) * c]
        else:
            parts = []
            for g in range(c // (2 * m)):
                lo, mid_row, up = 2 * m * g, 2 * m * g + m, 2 * m * (g + 1)
                r = mid_row if reverse else mid_row - 1
                b_r = b_scr[r:r + 1, :]
                parts += [b[lo:mid_row] - b_r, b_r - b[mid_row:up]] if reverse else \
                         [b_r - b[lo:mid_row], b[mid_row:up] - b_r]
            e = jnp.concatenate(parts, axis=0)
        fac = jnp.exp2(e).astype(BF16)
        a = a + _dot_nt(qb * fac, kb * fac) * pair_mask[j + 1]
        yield
    vb = v.astype(BF16)
    last = 0 if reverse else c - 1
    total = b_scr[last:last + 1, :]
    o_ref[...] = _dot(a.astype(BF16), vb) + _dot_nt(qb * jnp.exp2(b).astype(BF16), state.astype(BF16))
    yield
    k_dec = kb * jnp.exp2(total - b).astype(BF16)
    state_ref[...] = state * jnp.exp2(total) + _dot_tn(vb, k_dec)


def _hgrn_kernel(qf_ref, lff_ref, vf_ref, qb_ref, lfb_ref, vb_ref, sm_ref, pm_ref, fm_ref, *rest):
    n_cast = (len(rest) - 4) // 2
    cast_in, (of_ref, ob_ref), cast_out = rest[:n_cast], rest[n_cast:n_cast + 2], rest[n_cast + 2:2 * n_cast + 2]
    s_ref, b_scr = rest[2 * n_cast + 2:]

    @pl.when(pl.program_id(1) == 0)
    def _():
        s_ref[...] = jnp.zeros_like(s_ref)

    for src, dst in zip(cast_in, cast_out):
        dst[...] = src[...].astype(dst.dtype)

    c = HG_CHUNK
    rows_per_step = qf_ref.shape[2]
    chains = [(d, h) for h in range(HG_HEADS) for d in (0, 1)]
    q_refs, lf_refs, v_refs, o_refs = (qf_ref, qb_ref), (lff_ref, lfb_ref), (vf_ref, vb_ref), (of_ref, ob_ref)

    worst = jnp.zeros((1, HG_DIM), F32)
    for d, h in chains:
        for g in range(rows_per_step // HG_FAST_BLOCK):
            block = lf_refs[d][0, h, g * HG_FAST_BLOCK:(g + 1) * HG_FAST_BLOCK, :]
            worst = jnp.minimum(worst, jnp.sum(block, axis=0, keepdims=True))
    mild = jnp.min(worst) >= -HG_FAST_MAX_LOG2

    def run(fast):
        n_sub = rows_per_step // c
        for sub in range(n_sub):
            work = []
            for d, h in chains:
                k = n_sub - 1 - sub if d else sub
                rows = slice(k * c, (k + 1) * c)
                work.append(_hgrn_chunk(q_refs[d].at[0, h, rows, :], lf_refs[d].at[0, h, rows, :],
                                        v_refs[d].at[0, h, rows, :], o_refs[d].at[0, h, rows, :],
                                        s_ref.at[d, h], sm_ref.at[d], pm_ref.at[d], fm_ref.at[d],
                                        b_scr.at[d, h], bool(d), fast))
            for _ in itertools.zip_longest(*work):
                pass

    pl.when(mild)(lambda: run(True))
    pl.when(jnp.logical_not(mild))(lambda: run(False))


def _hgrn(hq, lf_f, lf_b, hi, to_cast):
    b, nh, n, dk = hq.shape
    rows_per_step = HG_CHUNK * HG_STEP_CHUNKS if n % (HG_CHUNK * HG_STEP_CHUNKS) == 0 else HG_CHUNK
    nc = n // rows_per_step
    blk = (1, nh, rows_per_step, dk)
    fwd = pl.BlockSpec(blk, lambda i, j: (i, 0, j, 0))
    bwd = pl.BlockSpec(blk, lambda i, j: (i, 0, nc - 1 - j, 0))
    out = jax.ShapeDtypeStruct((b, nh, n, dk), F32)
    tables = _hgrn_tables()
    steps = b * nc

    def slab_spec(w):
        hold = next(h for h in range(1, steps + 1)
                    if steps % h == 0 and w.shape[0] % (steps // h * BF16_SUBLANES) == 0)
        return pl.BlockSpec((w.shape[0] // (steps // hold), w.shape[1]), lambda i, j: ((i * nc + j) // hold, 0))

    slabs = [slab_spec(w) for w in to_cast]
    res = pl.pallas_call(
        _hgrn_kernel,
        grid=(b, nc),
        in_specs=[fwd, fwd, fwd, bwd, bwd, bwd] + [_const_spec(t.shape) for t in tables] + slabs,
        out_specs=[fwd, bwd] + slabs,
        out_shape=[out, out] + [jax.ShapeDtypeStruct(w.shape, BF16) for w in to_cast],
        scratch_shapes=[pltpu.VMEM((2, nh, dk, dk), F32), pltpu.VMEM((2, nh, HG_CHUNK, dk), F32)],
        compiler_params=_params("parallel", "arbitrary"),
        name="hgrn",
    )(hq, lf_f, hi, hq, lf_b, hi, jnp.asarray(tables[0], BF16), jnp.asarray(tables[1]), jnp.asarray(tables[2]),
      *to_cast)
    return res[0], res[1], res[2:]


def _mix_cross_kernel(x_ref, att_ref, of_ref, ob_ref, sg_ref, og_ref, wo_ref, pmg_ref, pxg_ref,
                      wq_ref, km_ref, vm_ref, wxo_ref, poxg_ref, pfg_ref, x2_ref, h3_ref):
    tm = x_ref.shape[1]
    part = tm // MIX_ROW_PARTS

    def rows_of(lo):
        rows = slice(lo, lo + part)
        rec = []
        for i in range(HG_HEADS):
            o = _rms(of_ref[0, i, rows, :] + ob_ref[0, i, rows, :], og_ref[...])
            rec.append((o * sg_ref[0, rows, i * HG_DIM:(i + 1) * HG_DIM]).astype(BF16))
        att_t = att_ref[0, :, :, rows].reshape(ATT_Q_DIM, part)
        mixed = _dot_tn(att_t, wo_ref[:ATT_Q_DIM, :].astype(BF16)) + \
            _dot(jnp.concatenate(rec, axis=-1), wo_ref[ATT_Q_DIM:, :].astype(BF16))
        yield
        x1 = x_ref[0, rows, :] + _rms(mixed, pmg_ref[...])
        h2 = _rms(x1, pxg_ref[...]).astype(BF16)
        d = h2.shape[-1]
        dh = d // X_HEADS
        q = _dot(h2, wq_ref[...].astype(BF16)) * (dh ** -0.5)
        yield
        heads = []
        for i in range(X_HEADS):
            sl = slice(i * dh, (i + 1) * dh)
            s = _dot_nt(q[:, sl].astype(BF16), km_ref[0, :, sl])
            p = jnp.exp(s - jnp.max(s, axis=-1, keepdims=True))
            l = jnp.sum(p, axis=-1, keepdims=True)
            heads.append((_dot(p.astype(BF16), vm_ref[0, :, sl]) / l).astype(BF16))
        yield
        xo = _dot(jnp.concatenate(heads, axis=-1), wxo_ref[...].astype(BF16))
        yield
        x2 = x1 + _rms(xo, poxg_ref[...])
        x2_ref[0, rows, :] = x2
        h3_ref[0, rows, :] = _rms(x2, pfg_ref[...]).astype(BF16)

    for _ in itertools.zip_longest(*[rows_of(j * part) for j in range(MIX_ROW_PARTS)]):
        pass


def _mix_cross(x, att, o_f, o_b, sg, og, w_out, pmg, pxg, w_xq, k_mem, v_mem, w_xo, poxg, pfg, tm):
    b, n, d = x.shape
    nm = k_mem.shape[1]
    tok = lambda i, j: (i, j, 0)
    head = lambda i, j: (i, 0, j, 0)
    batch = lambda i, j: (i, 0, 0)
    vec = _const_spec((1, d))
    mat = _resident_spec((d, d))
    return pl.pallas_call(
        _mix_cross_kernel,
        grid=(b, n // tm),
        in_specs=[pl.BlockSpec((1, tm, d), tok),
                  pl.BlockSpec((1, ATT_HEADS, ATT_HEAD_DIM, tm), lambda i, j: (i, 0, 0, j)),
                  pl.BlockSpec((1, HG_HEADS, tm, HG_DIM), head), pl.BlockSpec((1, HG_HEADS, tm, HG_DIM), head),
                  pl.BlockSpec((1, tm, HG_WIDTH), tok), _const_spec((1, HG_DIM)),
                  mat, vec, vec, mat,
                  pl.BlockSpec((1, nm, d), batch), pl.BlockSpec((1, nm, d), batch),
                  mat, vec, vec],
        out_specs=[pl.BlockSpec((1, tm, d), tok), pl.BlockSpec((1, tm, d), tok)],
        out_shape=[jax.ShapeDtypeStruct((b, n, d), F32), jax.ShapeDtypeStruct((b, n, d), BF16)],
        compiler_params=_params("parallel", "parallel"),
        name="mix_cross",
    )(x, att, o_f, o_b, sg, og, w_out, pmg, pxg, w_xq, k_mem, v_mem, w_xo, poxg, pfg)


def _conv_ffn_kernel(x_ref, h_ref, hp_ref, hn_ref, wu_ref, cw_ref, cb_ref, wd_ref, g_ref, o_ref, act_scr):
    j = pl.program_id(1)
    tm = h_ref.shape[1]
    d_ff = wd_ref.shape[0]
    ck = FF_CHUNK
    prev = jnp.where(j > 0, hp_ref[0], jnp.zeros_like(hp_ref[0]))
    nxt = jnp.where(j < pl.num_programs(1) - 1, hn_ref[0], jnp.zeros_like(hn_ref[0]))
    hext = jnp.concatenate([prev, h_ref[0], nxt], axis=0)

    def conv(col):
        u = _dot(hext, wu_ref[:, col:col + ck])
        taps = (pltpu.roll(u, 1, 0), u, pltpu.roll(u, u.shape[0] - 1, 0))
        out = cb_ref[:, col:col + ck]
        for t in range(CONV_W):
            out = out + taps[t][HALO:HALO + tm] * cw_ref[t:t + 1, col:col + ck]
        return out

    for c in range(d_ff // ck):
        act_scr[:, c * ck:(c + 1) * ck] = (_silu(conv(c * ck)) * conv(d_ff + c * ck)).astype(BF16)
    o_ref[0] = x_ref[0] + _rms(_dot(act_scr[...], wd_ref[...]), g_ref[...])


def _conv_ffn(x2, h3, w_up, conv_w, conv_b, wd, g, tm):
    b, n, d = x2.shape
    d_ff = wd.shape[0]
    ck = FF_CHUNK
    hb = tm // HALO
    last = n // HALO - 1
    tok = lambda i, j: (i, j, 0)
    full = _resident_spec
    return pl.pallas_call(
        _conv_ffn_kernel,
        grid=(b, n // tm),
        in_specs=[pl.BlockSpec((1, tm, d), tok), pl.BlockSpec((1, tm, d), tok),
                  pl.BlockSpec((1, HALO, d), lambda i, j: (i, jnp.maximum(j * hb - 1, 0), 0)),
                  pl.BlockSpec((1, HALO, d), lambda i, j: (i, jnp.minimum((j + 1) * hb, last), 0)),
                  full((d, 2 * d_ff)), full((CONV_W, 2 * d_ff)), full((1, 2 * d_ff)),
                  full((d_ff, d)), _const_spec((1, d))],
        out_specs=pl.BlockSpec((1, tm, d), tok),
        out_shape=jax.ShapeDtypeStruct((b, n, d), F32),
        scratch_shapes=[pltpu.VMEM((tm, d_ff), BF16)],
        compiler_params=_params("parallel", "parallel"),
        name="conv_ffn",
    )(x2, h3, h3, h3, w_up, conv_w, conv_b, wd, g)


def _rope_tables(n):
    pairs = ATT_HEAD_DIM // 4
    pos = np.arange(n)
    inv = np.power(np.float32(ROPE_THETA), -np.arange(pairs, dtype=np.float32) / np.float32(pairs))
    ang = np.concatenate([(pos // GRID_W).astype(np.float32)[:, None] * inv,
                          (pos % GRID_W).astype(np.float32)[:, None] * inv], axis=-1)
    cos = np.repeat(np.cos(ang), 2, axis=-1)
    sin = np.repeat(np.sin(ang), 2, axis=-1) * np.tile(np.array([-1.0, 1.0], np.float32), ATT_HEAD_DIM // 2)
    return jnp.asarray(np.tile(cos, (1, 2)), F32), jnp.asarray(np.tile(sin, (1, 2)), F32)


def _segment_ones():
    i = jnp.arange(LANES) // ATT_HEAD_DIM
    return jnp.tile((i[:, None] == i[None, :]).astype(BF16), (2, 1))


def _layer(x, mem, lb, pre_mix_g, w_in, q_norm_g, k_norm_g, hg_out_norm_g, w_out, post_mix_g, pre_x_g,
           mem_norm_g, w_xq, w_xkv, w_xo, post_x_g, pre_ffn_g, w_up, conv_w, conv_b, w_down, post_ffn_g):
    b, n, d = x.shape
    d_ff = w_down.shape[0]
    assert n % HG_CHUNK == 0 and n % GRID_W == 0 and d_ff % FF_CHUNK == 0
    tm = min(256, n)
    tf = min(512, n)
    row = lambda g: g.reshape(1, -1).astype(F32)

    cos, sin = _rope_tables(n)
    qg = jnp.tile(q_norm_g, ATT_HEADS).reshape(1, -1) * (ATT_HEAD_DIM ** -0.5 * LOG2E)
    kg = jnp.tile(k_norm_g, ATT_KV_HEADS).reshape(1, -1)

    k_mem, v_mem = _mem_proj(mem, row(mem_norm_g), w_xkv)
    q, k, v_t, hq, lf_f, lf_b, hi, sg = _in_proj(
        x, row(pre_mix_g), w_in, qg, kg, cos, sin, lb,
        _segment_ones(), tf)
    att = _attention(q, k, v_t, tm)
    o_f, o_b, (w_up_bf16, w_down_bf16) = _hgrn(hq, lf_f, lf_b, hi, [w_up, w_down])
    x2, h3 = _mix_cross(x, att, o_f, o_b, sg, row(hg_out_norm_g), w_out, row(post_mix_g),
                        row(pre_x_g), w_xq, k_mem, v_mem, w_xo,
                        row(post_x_g), row(pre_ffn_g), tf)

    return _conv_ffn(x2, h3, w_up_bf16, conv_w, row(conv_b), w_down_bf16, row(post_ffn_g), min(1024, n))


def kernel(x, mem, pre_mix_g, w_in, q_norm_g, k_norm_g, hg_lb, hg_out_norm_g, w_out, post_mix_g, pre_x_g,
           mem_norm_g, w_xq, w_xkv, w_xo, post_x_g, pre_ffn_g, w_up, conv_w, conv_b, w_down, post_ffn_g):
    lb_all = jnp.cumsum(jax.nn.softmax(hg_lb.astype(F32), axis=1), axis=1)
    for l in range(w_in.shape[0]):
        x = _layer(x, mem, lb_all[:, l], pre_mix_g[l], w_in[l], q_norm_g[l], k_norm_g[l], hg_out_norm_g[l],
                   w_out[l], post_mix_g[l], pre_x_g[l], mem_norm_g[l], w_xq[l], w_xkv[l], w_xo[l],
                   post_x_g[l], pre_ffn_g[l], w_up[l], conv_w[l], conv_b[l], w_down[l], post_ffn_g[l])
    return x
```

```python
import functools
import itertools
import math

import numpy as np
import jax
import jax.numpy as jnp
from jax import lax
from jax.experimental import pallas as pl
from jax.experimental.pallas import tpu as pltpu

F32 = jnp.float32
BF16 = jnp.bfloat16

EPS = 1e-6
LOG2E = math.log2(math.e)
GRID_W = 64
ROPE_THETA = 10000.0

ATT_HEADS = 8
ATT_KV_HEADS = 2
ATT_GROUP = ATT_HEADS // ATT_KV_HEADS
ATT_HEAD_DIM = 64
ATT_Q_DIM = ATT_HEADS * ATT_HEAD_DIM
ATT_KV_DIM = ATT_KV_HEADS * ATT_HEAD_DIM
ATT_KEY_SPLITS = 2

HG_HEADS = 4
HG_DIM = 128
HG_WIDTH = HG_HEADS * HG_DIM
HG_CHUNK = 128
HG_STEP_CHUNKS = 4
HG_FAST_BLOCK = 64
HG_FAST_MAX_LOG2 = 96.0

X_HEADS = 4
IN_ROW_PARTS = 2
MIX_ROW_PARTS = 2
CONV_W = 3
FF_CHUNK = 256

VMEM_LIMIT = 56 * 1024 * 1024
SUBLANES = 8
BF16_SUBLANES = 16
LANES = 128

HALO = BF16_SUBLANES


def _params(*sem):
    return pltpu.CompilerParams(dimension_semantics=sem, vmem_limit_bytes=VMEM_LIMIT)


def _rms(x, g):
    ms = jnp.mean(x * x, axis=-1, keepdims=True)
    return x * lax.rsqrt(ms + EPS) * g


def _silu(x):
    return x / (1.0 + jnp.exp(-x))


def _dot(a, b):
    return jnp.dot(a, b, preferred_element_type=F32)


def _dot_nt(a, b):
    return lax.dot_general(a, b, (((1,), (1,)), ((), ())), preferred_element_type=F32)


def _dot_tn(a, b):
    return lax.dot_general(a, b, (((0,), (0,)), ((), ())), preferred_element_type=F32)


def _const_spec(shape):
    return pl.BlockSpec(shape, lambda *_: (0,) * len(shape))


def _resident_spec(shape):
    return pl.BlockSpec(shape, lambda *_: (0,) * len(shape), pipeline_mode=pl.Buffered(1))


def _mem_proj_kernel(m_ref, g_ref, w_ref, k_ref, v_ref):
    d = m_ref.shape[-1]
    m = _rms(m_ref[0], g_ref[...]).astype(BF16)
    kv = _dot(m, w_ref[...].astype(BF16))
    k_ref[0] = kv[:, :d].astype(BF16)
    v_ref[0] = kv[:, d:].astype(BF16)


def _mem_proj(mem, g, w_xkv):
    b, nm, d = mem.shape
    out = jax.ShapeDtypeStruct((b, nm, d), BF16)
    return pl.pallas_call(
        _mem_proj_kernel,
        grid=(b,),
        in_specs=[pl.BlockSpec((1, nm, d), lambda i: (i, 0, 0)),
                  _const_spec((1, d)), _resident_spec((d, 2 * d))],
        out_specs=[pl.BlockSpec((1, nm, d), lambda i: (i, 0, 0))] * 2,
        out_shape=[out, out],
        compiler_params=_params("parallel"),
        name="mem_proj",
    )(mem, g, w_xkv)


def _head_norm_rope(a, seg, gain, cos, sin_signed):
    w = a.shape[-1]
    sq = a * a
    hi = sq.astype(BF16)
    lo = (sq - hi.astype(F32)).astype(BF16)
    ss = [_dot(jnp.concatenate([hi[:, j:j + LANES], lo[:, j:j + LANES]], axis=-1), seg) for j in range(0, w, LANES)]
    ss = jnp.concatenate(ss, axis=-1) if len(ss) > 1 else ss[0]
    an = a * lax.rsqrt(ss * (1.0 / ATT_HEAD_DIM) + EPS) * gain
    lane = lax.broadcasted_iota(jnp.int32, a.shape, 1)
    partner = jnp.where(lane % 2 == 0, pltpu.roll(an, w - 1, 1), pltpu.roll(an, 1, 1))
    return an * cos + partner * sin_signed


def _in_proj_kernel(x_ref, g_ref, w_ref, qg_ref, kg_ref, cos_ref, sin_ref, lb_ref, seg_ref,
                    q_ref, k_ref, vt_ref, hq_ref, lff_ref, lfb_ref, hi_ref, hg_ref):
    part = x_ref.shape[1] // IN_ROW_PARTS

    def rows_of(r0):
        rows = slice(r0, r0 + part)
        h = _rms(x_ref[0, rows, :], g_ref[...]).astype(BF16)

        def proj(lo, width):
            return _dot(h, w_ref[:, lo:lo + width].astype(BF16))

        def heads_out(ref, val):
            for i in range(HG_HEADS):
                ref[0, i, rows, :] = val[:, i * HG_DIM:(i + 1) * HG_DIM].astype(ref.dtype)

        cos = cos_ref[rows, :]
        sin = sin_ref[rows, :]
        c0 = 0
        q = _head_norm_rope(proj(c0, ATT_Q_DIM), seg_ref[...], qg_ref[...],
                            jnp.concatenate([cos] * (ATT_Q_DIM // LANES), axis=-1),
                            jnp.concatenate([sin] * (ATT_Q_DIM // LANES), axis=-1))
        for i in range(ATT_HEADS):
            q_ref[0, i, rows, :] = q[:, i * ATT_HEAD_DIM:(i + 1) * ATT_HEAD_DIM].astype(BF16)
        c0 += ATT_Q_DIM
        yield
        k = _head_norm_rope(proj(c0, ATT_KV_DIM), seg_ref[...], kg_ref[...], cos, sin)
        c0 += ATT_KV_DIM
        v_t = jnp.transpose(proj(c0, ATT_KV_DIM))
        c0 += ATT_KV_DIM
        for i in range(ATT_KV_HEADS):
            k_ref[0, i, rows, :] = k[:, i * ATT_HEAD_DIM:(i + 1) * ATT_HEAD_DIM].astype(BF16)
            vt_ref[0, i, :, rows] = v_t[i * ATT_HEAD_DIM:(i + 1) * ATT_HEAD_DIM, :].astype(BF16)
        yield
        heads_out(hq_ref, _silu(proj(c0, HG_WIDTH)))
        c0 += HG_WIDTH
        yield
        for d, ref in enumerate((lff_ref, lfb_ref)):
            lb = lb_ref[d:d + 1, :]
            z = proj(c0, HG_WIDTH)
            heads_out(ref, jnp.log(lb + (1.0 - lb) / (1.0 + jnp.exp(-z))) * LOG2E)
            c0 += HG_WIDTH
            yield
        heads_out(hi_ref, proj(c0, HG_WIDTH))
        c0 += HG_WIDTH
        yield
        hg_ref[0, rows, :] = _silu(proj(c0, HG_WIDTH))

    for _ in itertools.zip_longest(*[rows_of(j * part) for j in range(IN_ROW_PARTS)]):
        pass


def _in_proj(x, g, w_in, qg, kg, cos, sin, lb, seg, tm):
    b, n, d = x.shape
    nt = n // tm
    n_in = w_in.shape[1]
    tok = lambda i, j: (i, j, 0)
    head = lambda i, j: (i, 0, j, 0)
    pos = lambda i, j: (j, 0)
    return pl.pallas_call(
        _in_proj_kernel,
        grid=(b, nt),
        in_specs=[pl.BlockSpec((1, tm, d), tok), _const_spec((1, d)), _resident_spec((d, n_in)),
                  _const_spec((1, ATT_Q_DIM)), _const_spec((1, ATT_KV_DIM)),
                  pl.BlockSpec((tm, LANES), pos), pl.BlockSpec((tm, LANES), pos),
                  _const_spec((2, HG_WIDTH)),
                  _const_spec((2 * LANES, LANES))],
        out_specs=[pl.BlockSpec((1, ATT_HEADS, tm, ATT_HEAD_DIM), head),
                   pl.BlockSpec((1, ATT_KV_HEADS, tm, ATT_HEAD_DIM), head),
                   pl.BlockSpec((1, ATT_KV_HEADS, ATT_HEAD_DIM, tm), lambda i, j: (i, 0, 0, j)),
                   pl.BlockSpec((1, HG_HEADS, tm, HG_DIM), head),
                   pl.BlockSpec((1, HG_HEADS, tm, HG_DIM), head),
                   pl.BlockSpec((1, HG_HEADS, tm, HG_DIM), head),
                   pl.BlockSpec((1, HG_HEADS, tm, HG_DIM), head),
                   pl.BlockSpec((1, tm, HG_WIDTH), tok)],
        out_shape=[jax.ShapeDtypeStruct((b, ATT_HEADS, n, ATT_HEAD_DIM), BF16),
                   jax.ShapeDtypeStruct((b, ATT_KV_HEADS, n, ATT_HEAD_DIM), BF16),
                   jax.ShapeDtypeStruct((b, ATT_KV_HEADS, ATT_HEAD_DIM, n), BF16),
                   jax.ShapeDtypeStruct((b, HG_HEADS, n, HG_DIM), BF16),
                   jax.ShapeDtypeStruct((b, HG_HEADS, n, HG_DIM), F32),
                   jax.ShapeDtypeStruct((b, HG_HEADS, n, HG_DIM), F32),
                   jax.ShapeDtypeStruct((b, HG_HEADS, n, HG_DIM), BF16),
                   jax.ShapeDtypeStruct((b, n, HG_WIDTH), F32)],
        compiler_params=_params("parallel", "parallel"),
        name="in_proj",
    )(x, g, w_in, qg, kg, cos, sin, lb, seg)


def _attention_kernel(q_ref, qn_ref, k_ref, vt_ref, o_ref, s_even, s_odd):
    n = k_ref.shape[2]
    part = n // ATT_KEY_SPLITS
    parts = [slice(j * part, (j + 1) * part) for j in range(ATT_KEY_SPLITS)]

    def scores(i, s_scr, queries=q_ref):
        for rows in parts:
            s_scr[rows, :] = _dot_nt(k_ref[0, i // ATT_GROUP, rows, :], queries[0, i])

    def finish(i, s_scr):
        s = [s_scr[rows, :] for rows in parts]
        m = functools.reduce(jnp.maximum, [jnp.max(x, axis=0, keepdims=True) for x in s])
        p = [jnp.exp2(x - m) for x in s]
        l = sum(jnp.sum(x, axis=0, keepdims=True) for x in p)
        o_t = sum(_dot(vt_ref[0, i // ATT_GROUP, :, rows], x.astype(BF16)) for rows, x in zip(parts, p))
        o_ref[0, i] = (o_t / l).astype(BF16)

    @pl.when(pl.program_id(1) == 0)
    def _():
        scores(0, s_even)

    def pair(j, carry):
        scores(2 * j + 1, s_odd)
        finish(2 * j, s_even)
        scores(2 * j + 2, s_even)
        finish(2 * j + 1, s_odd)
        return carry

    lax.fori_loop(0, ATT_HEADS // 2 - 1, pair, 0)
    scores(ATT_HEADS - 1, s_odd)
    finish(ATT_HEADS - 2, s_even)
    scores(0, s_even, qn_ref)
    finish(ATT_HEADS - 1, s_odd)


def _attention(q, k, v_t, tq):
    b, _, n, dh = q.shape
    last = n // tq - 1
    return pl.pallas_call(
        _attention_kernel,
        grid=(b, n // tq),
        in_specs=[pl.BlockSpec((1, ATT_HEADS, tq, dh), lambda i, j: (i, 0, j, 0)),
                  pl.BlockSpec((1, ATT_HEADS, tq, dh), lambda i, j: (i, 0, jnp.minimum(j + 1, last), 0)),
                  pl.BlockSpec((1, ATT_KV_HEADS, n, dh), lambda i, j: (i, 0, 0, 0)),
                  pl.BlockSpec((1, ATT_KV_HEADS, dh, n), lambda i, j: (i, 0, 0, 0))],
        out_specs=pl.BlockSpec((1, ATT_HEADS, dh, tq), lambda i, j: (i, 0, 0, j)),
        out_shape=jax.ShapeDtypeStruct((b, ATT_HEADS, dh, n), BF16),
        scratch_shapes=[pltpu.VMEM((n, tq), F32), pltpu.VMEM((n, tq), F32)],
        compiler_params=_params("parallel", "arbitrary"),
        name="attention",
    )(q, q, k, v_t)


def _hgrn_levels():
    m, out = 1, []
    while m < HG_CHUNK:
        out.append(m)
        m *= 2
    return out


def _hgrn_tables():
    c = HG_CHUNK
    t = np.arange(c)[:, None]
    u = np.arange(c)[None, :]
    mats, masks, fast = [], [], []
    for reverse in (False, True):
        blocks, pairs = [], [t == u]
        for m in _hgrn_levels():
            base = (t // (2 * m)) * (2 * m)
            if not reverse:
                r = base + m - 1
                block = np.where(t > r, (u > r) & (u <= t), (u > t) & (u <= r))
                roles = ((t // m) % 2 == 1) & ((u // m) % 2 == 0)
            else:
                r = base + m
                block = np.where(t < r, (u >= t) & (u < r), (u >= r) & (u < t))
                roles = ((t // m) % 2 == 0) & ((u // m) % 2 == 1)
            if m < SUBLANES:
                blocks.append(block)
            pairs.append(roles & (t // (2 * m) == u // (2 * m)))
        blocks.append(u >= t if reverse else u <= t)
        mats.append(np.tile(np.concatenate(blocks, axis=0), (1, 2)))
        masks.append(np.stack(pairs))
        fast.append((t // HG_FAST_BLOCK == u // HG_FAST_BLOCK) & (u >= t if reverse else u <= t))
    return tuple(np.stack(x).astype(np.float32) for x in (mats, masks, fast))


def _hgrn_sums(lf2, sum_mat):
    hi = lf2.astype(BF16)
    mid = (lf2 - hi.astype(F32)).astype(BF16)
    return _dot(sum_mat, jnp.concatenate([hi, mid], axis=0))


def _hgrn_chunk(q_ref, lf_ref, v_ref, o_ref, state_ref, sum_mat, pair_mask, fast_mask, b_scr, reverse, fast):
    c = HG_CHUNK
    levels = list(enumerate(_hgrn_levels()))
    n_small = sum(m < SUBLANES for _, m in levels)
    q, lf2, v, state = q_ref[...], lf_ref[...], v_ref[...], state_ref[...]
    qb = q.astype(BF16)
    kb = (1.0 - jnp.exp2(lf2)).astype(BF16)
    if fast:
        b = _hgrn_sums(lf2, sum_mat[n_small * c:, :])
        yield
        b_scr[...] = b
        parts = []
        for g in range(c // HG_FAST_BLOCK):
            lo, up = g * HG_FAST_BLOCK, (g + 1) * HG_FAST_BLOCK
            before = up if reverse else lo - 1
            parts.append(b[lo:up] - b_scr[before:before + 1, :] if 0 <= before < c else b[lo:up])
        loc = jnp.concatenate(parts, axis=0)
        a = _dot_nt(qb * jnp.exp2(loc).astype(BF16), kb * jnp.exp2(-loc).astype(BF16)) * fast_mask[...]
        yield
        levels = [(j, m) for j, m in levels if m >= HG_FAST_BLOCK]
    else:
        sums = _hgrn_sums(lf2, sum_mat[...])
        yield
        b = sums[n_small * c:]
        b_scr[...] = b
        a = _dot_nt(qb, kb) * pair_mask[0]
        yield
    for j, m in levels:
        if m < SUBLANES:
            e = sums[j * c:(j + 1) * c]
        else:
            parts = []
            for g in range(c // (2 * m)):
                lo, mid_row, up = 2 * m * g, 2 * m * g + m, 2 * m * (g + 1)
                r = mid_row if reverse else mid_row - 1
                b_r = b_scr[r:r + 1, :]
                parts += [b[lo:mid_row] - b_r, b_r - b[mid_row:up]] if reverse else \
                         [b_r - b[lo:mid_row], b[mid_row:up] - b_r]
            e = jnp.concatenate(parts, axis=0)
        fac = jnp.exp2(e).astype(BF16)
        a = a + _dot_nt(qb * fac, kb * fac) * pair_mask[j + 1]
        yield
    vb = v.astype(BF16)
    last = 0 if reverse else c - 1
    total = b_scr[last:last + 1, :]
    o_ref[...] = _dot(a.astype(BF16), vb) + _dot_nt(qb * jnp.exp2(b).astype(BF16), state.astype(BF16))
    yield
    k_dec = kb * jnp.exp2(total - b).astype(BF16)
    state_ref[...] = state * jnp.exp2(total) + _dot_tn(vb, k_dec)


def _hgrn_kernel(qf_ref, lff_ref, vf_ref, qb_ref, lfb_ref, vb_ref, sm_ref, pm_ref, fm_ref, *rest):
    n_cast = (len(rest) - 4) // 2
    cast_in, (of_ref, ob_ref), cast_out = rest[:n_cast], rest[n_cast:n_cast + 2], rest[n_cast + 2:2 * n_cast + 2]
    s_ref, b_scr = rest[2 * n_cast + 2:]

    @pl.when(pl.program_id(1) == 0)
    def _():
        s_ref[...] = jnp.zeros_like(s_ref)

    for src, dst in zip(cast_in, cast_out):
        dst[...] = src[...].astype(dst.dtype)

    c = HG_CHUNK
    rows_per_step = qf_ref.shape[2]
    chains = [(d, h) for h in range(HG_HEADS) for d in (0, 1)]
    q_refs, lf_refs, v_refs, o_refs = (qf_ref, qb_ref), (lff_ref, lfb_ref), (vf_ref, vb_ref), (of_ref, ob_ref)

    worst = jnp.zeros((1, HG_DIM), F32)
    for d, h in chains:
        for g in range(rows_per_step // HG_FAST_BLOCK):
            block = lf_refs[d][0, h, g * HG_FAST_BLOCK:(g + 1) * HG_FAST_BLOCK, :]
            worst = jnp.minimum(worst, jnp.sum(block, axis=0, keepdims=True))
    mild = jnp.min(worst) >= -HG_FAST_MAX_LOG2

    def run(fast):
        n_sub = rows_per_step // c
        for sub in range(n_sub):
            work = []
            for d, h in chains:
                k = n_sub - 1 - sub if d else sub
                rows = slice(k * c, (k + 1) * c)
                work.append(_hgrn_chunk(q_refs[d].at[0, h, rows, :], lf_refs[d].at[0, h, rows, :],
                                        v_refs[d].at[0, h, rows, :], o_refs[d].at[0, h, rows, :],
                                        s_ref.at[d, h], sm_ref.at[d], pm_ref.at[d], fm_ref.at[d],
                                        b_scr.at[d, h], bool(d), fast))
            for _ in itertools.zip_longest(*work):
                pass

    pl.when(mild)(lambda: run(True))
    pl.when(jnp.logical_not(mild))(lambda: run(False))


def _hgrn(hq, lf_f, lf_b, hi, to_cast):
    b, nh, n, dk = hq.shape
    rows_per_step = HG_CHUNK * HG_STEP_CHUNKS if n % (HG_CHUNK * HG_STEP_CHUNKS) == 0 else HG_CHUNK
    nc = n // rows_per_step
    blk = (1, nh, rows_per_step, dk)
    fwd = pl.BlockSpec(blk, lambda i, j: (i, 0, j, 0))
    bwd = pl.BlockSpec(blk, lambda i, j: (i, 0, nc - 1 - j, 0))
    out = jax.ShapeDtypeStruct((b, nh, n, dk), F32)
    tables = _hgrn_tables()
    steps = b * nc

    def slab_spec(w):
        hold = next(h for h in range(1, steps + 1)
                    if steps % h == 0 and w.shape[0] % (steps // h * BF16_SUBLANES) == 0)
        return pl.BlockSpec((w.shape[0] // (steps // hold), w.shape[1]), lambda i, j: ((i * nc + j) // hold, 0))

    slabs = [slab_spec(w) for w in to_cast]
    res = pl.pallas_call(
        _hgrn_kernel,
        grid=(b, nc),
        in_specs=[fwd, fwd, fwd, bwd, bwd, bwd] + [_const_spec(t.shape) for t in tables] + slabs,
        out_specs=[fwd, bwd] + slabs,
        out_shape=[out, out] + [jax.ShapeDtypeStruct(w.shape, BF16) for w in to_cast],
        scratch_shapes=[pltpu.VMEM((2, nh, dk, dk), F32), pltpu.VMEM((2, nh, HG_CHUNK, dk), F32)],
        compiler_params=_params("parallel", "arbitrary"),
        name="hgrn",
    )(hq, lf_f, hi, hq, lf_b, hi, jnp.asarray(tables[0], BF16), jnp.asarray(tables[1]), jnp.asarray(tables[2]),
      *to_cast)
    return res[0], res[1], res[2:]


def _mix_cross_kernel(x_ref, att_ref, of_ref, ob_ref, sg_ref, og_ref, wo_ref, pmg_ref, pxg_ref,
                      wq_ref, km_ref, vm_ref, wxo_ref, poxg_ref, pfg_ref, x2_ref, h3_ref):
    tm = x_ref.shape[1]
    part = tm // MIX_ROW_PARTS

    def rows_of(lo):
        rows = slice(lo, lo + part)
        rec = []
        for i in range(HG_HEADS):
            o = _rms(of_ref[0, i, rows, :] + ob_ref[0, i, rows, :], og_ref[...])
            rec.append((o * sg_ref[0, rows, i * HG_DIM:(i + 1) * HG_DIM]).astype(BF16))
        att_t = att_ref[0, :, :, rows].reshape(ATT_Q_DIM, part)
        mixed = _dot_tn(att_t, wo_ref[:ATT_Q_DIM, :].astype(BF16)) + \
            _dot(jnp.concatenate(rec, axis=-1), wo_ref[ATT_Q_DIM:, :].astype(BF16))
        yield
        x1 = x_ref[0, rows, :] + _rms(mixed, pmg_ref[...])
        h2 = _rms(x1, pxg_ref[...]).astype(BF16)
        d = h2.shape[-1]
        dh = d // X_HEADS
        q = _dot(h2, wq_ref[...].astype(BF16)) * (dh ** -0.5)
        yield
        heads = []
        for i in range(X_HEADS):
            sl = slice(i * dh, (i + 1) * dh)
            s = _dot_nt(q[:, sl].astype(BF16), km_ref[0, :, sl])
            p = jnp.exp(s - jnp.max(s, axis=-1, keepdims=True))
            l = jnp.sum(p, axis=-1, keepdims=True)
            heads.append((_dot(p.astype(BF16), vm_ref[0, :, sl]) / l).astype(BF16))
        yield
        xo = _dot(jnp.concatenate(heads, axis=-1), wxo_ref[...].astype(BF16))
        yield
        x2 = x1 + _rms(xo, poxg_ref[...])
        x2_ref[0, rows, :] = x2
        h3_ref[0, rows, :] = _rms(x2, pfg_ref[...]).astype(BF16)

    for _ in itertools.zip_longest(*[rows_of(j * part) for j in range(MIX_ROW_PARTS)]):
        pass


def _mix_cross(x, att, o_f, o_b, sg, og, w_out, pmg, pxg, w_xq, k_mem, v_mem, w_xo, poxg, pfg, tm):
    b, n, d = x.shape
    nm = k_mem.shape[1]
    tok = lambda i, j: (i, j, 0)
    head = lambda i, j: (i, 0, j, 0)
    batch = lambda i, j: (i, 0, 0)
    vec = _const_spec((1, d))
    mat = _resident_spec((d, d))
    return pl.pallas_call(
        _mix_cross_kernel,
        grid=(b, n // tm),
        in_specs=[pl.BlockSpec((1, tm, d), tok),
                  pl.BlockSpec((1, ATT_HEADS, ATT_HEAD_DIM, tm), lambda i, j: (i, 0, 0, j)),
                  pl.BlockSpec((1, HG_HEADS, tm, HG_DIM), head), pl.BlockSpec((1, HG_HEADS, tm, HG_DIM), head),
                  pl.BlockSpec((1, tm, HG_WIDTH), tok), _const_spec((1, HG_DIM)),
                  mat, vec, vec, mat,
                  pl.BlockSpec((1, nm, d), batch), pl.BlockSpec((1, nm, d), batch),
                  mat, vec, vec],
        out_specs=[pl.BlockSpec((1, tm, d), tok), pl.BlockSpec((1, tm, d), tok)],
        out_shape=[jax.ShapeDtypeStruct((b, n, d), F32), jax.ShapeDtypeStruct((b, n, d), BF16)],
        compiler_params=_params("parallel", "parallel"),
        name="mix_cross",
    )(x, att, o_f, o_b, sg, og, w_out, pmg, pxg, w_xq, k_mem, v_mem, w_xo, poxg, pfg)


def _conv_ffn_kernel(x_ref, h_ref, hp_ref, hn_ref, wu_ref, cw_ref, cb_ref, wd_ref, g_ref, o_ref, act_scr):
    j = pl.program_id(1)
    tm = h_ref.shape[1]
    d_ff = wd_ref.shape[0]
    ck = FF_CHUNK
    prev = jnp.where(j > 0, hp_ref[0], jnp.zeros_like(hp_ref[0]))
    nxt = jnp.where(j < pl.num_programs(1) - 1, hn_ref[0], jnp.zeros_like(hn_ref[0]))
    hext = jnp.concatenate([prev, h_ref[0], nxt], axis=0)

    def conv(col):
        u = _dot(hext, wu_ref[:, col:col + ck])
        taps = (pltpu.roll(u, 1, 0), u, pltpu.roll(u, u.shape[0] - 1, 0))
        out = cb_ref[:, col:col + ck]
        for t in range(CONV_W):
            out = out + taps[t][HALO:HALO + tm] * cw_ref[t:t + 1, col:col + ck]
        return out

    for c in range(d_ff // ck):
        act_scr[:, c * ck:(c + 1) * ck] = (_silu(conv(c * ck)) * conv(d_ff + c * ck)).astype(BF16)
    o_ref[0] = x_ref[0] + _rms(_dot(act_scr[...], wd_ref[...]), g_ref[...])


def _conv_ffn(x2, h3, w_up, conv_w, conv_b, wd, g, tm):
    b, n, d = x2.shape
    d_ff = wd.shape[0]
    ck = FF_CHUNK
    hb = tm // HALO
    last = n // HALO - 1
    tok = lambda i, j: (i, j, 0)
    full = _resident_spec
    return pl.pallas_call(
        _conv_ffn_kernel,
        grid=(b, n // tm),
        in_specs=[pl.BlockSpec((1, tm, d), tok), pl.BlockSpec((1, tm, d), tok),
                  pl.BlockSpec((1, HALO, d), lambda i, j: (i, jnp.maximum(j * hb - 1, 0), 0)),
                  pl.BlockSpec((1, HALO, d), lambda i, j: (i, jnp.minimum((j + 1) * hb, last), 0)),
                  full((d, 2 * d_ff)), full((CONV_W, 2 * d_ff)), full((1, 2 * d_ff)),
                  full((d_ff, d)), _const_spec((1, d))],
        out_specs=pl.BlockSpec((1, tm, d), tok),
        out_shape=jax.ShapeDtypeStruct((b, n, d), F32),
        scratch_shapes=[pltpu.VMEM((tm, d_ff), BF16)],
        compiler_params=_params("parallel", "parallel"),
        name="conv_ffn",
    )(x2, h3, h3, h3, w_up, conv_w, conv_b, wd, g)


def _rope_tables(n):
    pairs = ATT_HEAD_DIM // 4
    pos = np.arange(n)
    inv = np.power(np.float32(ROPE_THETA), -np.arange(pairs, dtype=np.float32) / np.float32(pairs))
    ang = np.concatenate([(pos // GRID_W).astype(np.float32)[:, None] * inv,
                          (pos % GRID_W).astype(np.float32)[:, None] * inv], axis=-1)
    cos = np.repeat(np.cos(ang), 2, axis=-1)
    sin = np.repeat(np.sin(ang), 2, axis=-1) * np.tile(np.array([-1.0, 1.0], np.float32), ATT_HEAD_DIM // 2)
    return jnp.asarray(np.tile(cos, (1, 2)), F32), jnp.asarray(np.tile(sin, (1, 2)), F32)


def _segment_ones():
    i = jnp.arange(LANES) // ATT_HEAD_DIM
    return jnp.tile((i[:, None] == i[None, :]).astype(BF16), (2, 1))


def _layer(x, mem, lb, pre_mix_g, w_in, q_norm_g, k_norm_g, hg_out_norm_g, w_out, post_mix_g, pre_x_g,
           mem_norm_g, w_xq, w_xkv, w_xo, post_x_g, pre_ffn_g, w_up, conv_w, conv_b, w_down, post_ffn_g):
    b, n, d = x.shape
    d_ff = w_down.shape[0]
    assert n % HG_CHUNK == 0 and n % GRID_W == 0 and d_ff % FF_CHUNK == 0
    tm = min(256, n)
    tf = min(512, n)
    row = lambda g: g.reshape(1, -1).astype(F32)

    cos, sin = _rope_tables(n)
    qg = jnp.tile(q_norm_g, ATT_HEADS).reshape(1, -1) * (ATT_HEAD_DIM ** -0.5 * LOG2E)
    kg = jnp.tile(k_norm_g, ATT_KV_HEADS).reshape(1, -1)

    k_mem, v_mem = _mem_proj(mem, row(mem_norm_g), w_xkv)
    q, k, v_t, hq, lf_f, lf_b, hi, sg = _in_proj(
        x, row(pre_mix_g), w_in, qg, kg, cos, sin, lb,
        _segment_ones(), tf)
    att = _attention(q, k, v_t, tm)
    o_f, o_b, (w_up_bf16, w_down_bf16) = _hgrn(hq, lf_f, lf_b, hi, [w_up, w_down])
    x2, h3 = _mix_cross(x, att, o_f, o_b, sg, row(hg_out_norm_g), w_out, row(post_mix_g),
                        row(pre_x_g), w_xq, k_mem, v_mem, w_xo,
                        row(post_x_g), row(pre_ffn_g), tf)

    return _conv_ffn(x2, h3, w_up_bf16, conv_w, row(conv_b), w_down_bf16, row(post_ffn_g), tf)


def kernel(x, mem, pre_mix_g, w_in, q_norm_g, k_norm_g, hg_lb, hg_out_norm_g, w_out, post_mix_g, pre_x_g,
           mem_norm_g, w_xq, w_xkv, w_xo, post_x_g, pre_ffn_g, w_up, conv_w, conv_b, w_down, post_ffn_g):
    lb_all = jnp.cumsum(jax.nn.softmax(hg_lb.astype(F32), axis=1), axis=1)
    for l in range(w_in.shape[0]):
        x = _layer(x, mem, lb_all[:, l], pre_mix_g[l], w_in[l], q_norm_g[l], k_norm_g[l], hg_out_norm_g[l],
                   w_out[l], post_mix_g[l], pre_x_g[l], mem_norm_g[l], w_xq[l], w_xkv[l], w_xo[l],
                   post_x_g[l], pre_ffn_g[l], w_up[l], conv_w[l], conv_b[l], w_down[l], post_ffn_g[l])
    return x
```

```python
import functools
import itertools
import math

import numpy as np
import jax
import jax.numpy as jnp
from jax import lax
from jax.experimental import pallas as pl
from jax.experimental.pallas import tpu as pltpu

F32 = jnp.float32
BF16 = jnp.bfloat16

EPS = 1e-6
LOG2E = math.log2(math.e)
GRID_W = 64
ROPE_THETA = 10000.0

ATT_HEADS = 8
ATT_KV_HEADS = 2
ATT_GROUP = ATT_HEADS // ATT_KV_HEADS
ATT_HEAD_DIM = 64
ATT_Q_DIM = ATT_HEADS * ATT_HEAD_DIM
ATT_KV_DIM = ATT_KV_HEADS * ATT_HEAD_DIM
ATT_KEY_SPLITS = 2

HG_HEADS = 4
HG_DIM = 128
HG_WIDTH = HG_HEADS * HG_DIM
HG_CHUNK = 128
HG_STEP_CHUNKS = 8
HG_FAST_BLOCK = 64
HG_FAST_MAX_LOG2 = 96.0

X_HEADS = 4
IN_ROW_PARTS = 2
MIX_ROW_PARTS = 2
CONV_W = 3
FF_CHUNK = 256

VMEM_LIMIT = 56 * 1024 * 1024
SUBLANES = 8
BF16_SUBLANES = 16
LANES = 128

HALO = BF16_SUBLANES


def _params(*sem):
    return pltpu.CompilerParams(dimension_semantics=sem, vmem_limit_bytes=VMEM_LIMIT)


def _rms(x, g):
    ms = jnp.mean(x * x, axis=-1, keepdims=True)
    return x * lax.rsqrt(ms + EPS) * g


def _silu(x):
    return x / (1.0 + jnp.exp(-x))


def _dot(a, b):
    return jnp.dot(a, b, preferred_element_type=F32)


def _dot_nt(a, b):
    return lax.dot_general(a, b, (((1,), (1,)), ((), ())), preferred_element_type=F32)


def _dot_tn(a, b):
    return lax.dot_general(a, b, (((0,), (0,)), ((), ())), preferred_element_type=F32)


def _const_spec(shape):
    return pl.BlockSpec(shape, lambda *_: (0,) * len(shape))


def _resident_spec(shape):
    return pl.BlockSpec(shape, lambda *_: (0,) * len(shape), pipeline_mode=pl.Buffered(1))


def _mem_proj_kernel(m_ref, g_ref, w_ref, k_ref, v_ref):
    d = m_ref.shape[-1]
    m = _rms(m_ref[0], g_ref[...]).astype(BF16)
    kv = _dot(m, w_ref[...].astype(BF16))
    k_ref[0] = kv[:, :d].astype(BF16)
    v_ref[0] = kv[:, d:].astype(BF16)


def _mem_proj(mem, g, w_xkv):
    b, nm, d = mem.shape
    out = jax.ShapeDtypeStruct((b, nm, d), BF16)
    return pl.pallas_call(
        _mem_proj_kernel,
        grid=(b,),
        in_specs=[pl.BlockSpec((1, nm, d), lambda i: (i, 0, 0)),
                  _const_spec((1, d)), _resident_spec((d, 2 * d))],
        out_specs=[pl.BlockSpec((1, nm, d), lambda i: (i, 0, 0))] * 2,
        out_shape=[out, out],
        compiler_params=_params("parallel"),
        name="mem_proj",
    )(mem, g, w_xkv)


def _head_norm_rope(a, seg, gain, cos, sin_signed):
    w = a.shape[-1]
    sq = a * a
    hi = sq.astype(BF16)
    lo = (sq - hi.astype(F32)).astype(BF16)
    ss = [_dot(jnp.concatenate([hi[:, j:j + LANES], lo[:, j:j + LANES]], axis=-1), seg) for j in range(0, w, LANES)]
    ss = jnp.concatenate(ss, axis=-1) if len(ss) > 1 else ss[0]
    an = a * lax.rsqrt(ss * (1.0 / ATT_HEAD_DIM) + EPS) * gain
    lane = lax.broadcasted_iota(jnp.int32, a.shape, 1)
    partner = jnp.where(lane % 2 == 0, pltpu.roll(an, w - 1, 1), pltpu.roll(an, 1, 1))
    return an * cos + partner * sin_signed


def _in_proj_kernel(x_ref, g_ref, w_ref, qg_ref, kg_ref, cos_ref, sin_ref, lb_ref, seg_ref,
                    qt_ref, k_ref, vt_ref, hq_ref, lff_ref, lfb_ref, hi_ref, hg_ref):
    part = x_ref.shape[1] // IN_ROW_PARTS

    def rows_of(r0):
        rows = slice(r0, r0 + part)
        h = _rms(x_ref[0, rows, :], g_ref[...]).astype(BF16)

        def proj(lo, width):
            return _dot(h, w_ref[:, lo:lo + width].astype(BF16))

        def heads_out(ref, val):
            for i in range(HG_HEADS):
                ref[0, i, rows, :] = val[:, i * HG_DIM:(i + 1) * HG_DIM].astype(ref.dtype)

        cos = cos_ref[rows, :]
        sin = sin_ref[rows, :]
        c0 = 0
        q = _head_norm_rope(proj(c0, ATT_Q_DIM), seg_ref[...], qg_ref[...],
                            jnp.concatenate([cos] * (ATT_Q_DIM // LANES), axis=-1),
                            jnp.concatenate([sin] * (ATT_Q_DIM // LANES), axis=-1))
        q_t = jnp.transpose(q)
        for i in range(ATT_HEADS):
            qt_ref[0, i, :, rows] = q_t[i * ATT_HEAD_DIM:(i + 1) * ATT_HEAD_DIM, :].astype(BF16)
        c0 += ATT_Q_DIM
        yield
        k = _head_norm_rope(proj(c0, ATT_KV_DIM), seg_ref[...], kg_ref[...], cos, sin)
        c0 += ATT_KV_DIM
        v_t = jnp.transpose(proj(c0, ATT_KV_DIM))
        c0 += ATT_KV_DIM
        for i in range(ATT_KV_HEADS):
            k_ref[0, i, rows, :] = k[:, i * ATT_HEAD_DIM:(i + 1) * ATT_HEAD_DIM].astype(BF16)
            vt_ref[0, i, :, rows] = v_t[i * ATT_HEAD_DIM:(i + 1) * ATT_HEAD_DIM, :].astype(BF16)
        yield
        heads_out(hq_ref, _silu(proj(c0, HG_WIDTH)))
        c0 += HG_WIDTH
        yield
        for d, ref in enumerate((lff_ref, lfb_ref)):
            lb = lb_ref[d:d + 1, :]
            z = proj(c0, HG_WIDTH)
            heads_out(ref, jnp.log(lb + (1.0 - lb) / (1.0 + jnp.exp(-z))) * LOG2E)
            c0 += HG_WIDTH
            yield
        heads_out(hi_ref, proj(c0, HG_WIDTH))
        c0 += HG_WIDTH
        yield
        hg_ref[0, rows, :] = _silu(proj(c0, HG_WIDTH))

    for _ in itertools.zip_longest(*[rows_of(j * part) for j in range(IN_ROW_PARTS)]):
        pass


def _in_proj(x, g, w_in, qg, kg, cos, sin, lb, seg, tm):
    b, n, d = x.shape
    nt = n // tm
    n_in = w_in.shape[1]
    tok = lambda i, j: (i, j, 0)
    head = lambda i, j: (i, 0, j, 0)
    pos = lambda i, j: (j, 0)
    return pl.pallas_call(
        _in_proj_kernel,
        grid=(b, nt),
        in_specs=[pl.BlockSpec((1, tm, d), tok), _const_spec((1, d)), _resident_spec((d, n_in)),
                  _const_spec((1, ATT_Q_DIM)), _const_spec((1, ATT_KV_DIM)),
                  pl.BlockSpec((tm, LANES), pos), pl.BlockSpec((tm, LANES), pos),
                  _const_spec((2, HG_WIDTH)),
                  _const_spec((2 * LANES, LANES))],
        out_specs=[pl.BlockSpec((1, ATT_HEADS, ATT_HEAD_DIM, tm), lambda i, j: (i, 0, 0, j)),
                   pl.BlockSpec((1, ATT_KV_HEADS, tm, ATT_HEAD_DIM), head),
                   pl.BlockSpec((1, ATT_KV_HEADS, ATT_HEAD_DIM, tm), lambda i, j: (i, 0, 0, j)),
                   pl.BlockSpec((1, HG_HEADS, tm, HG_DIM), head),
                   pl.BlockSpec((1, HG_HEADS, tm, HG_DIM), head),
                   pl.BlockSpec((1, HG_HEADS, tm, HG_DIM), head),
                   pl.BlockSpec((1, HG_HEADS, tm, HG_DIM), head),
                   pl.BlockSpec((1, tm, HG_WIDTH), tok)],
        out_shape=[jax.ShapeDtypeStruct((b, ATT_HEADS, ATT_HEAD_DIM, n), BF16),
                   jax.ShapeDtypeStruct((b, ATT_KV_HEADS, n, ATT_HEAD_DIM), BF16),
                   jax.ShapeDtypeStruct((b, ATT_KV_HEADS, ATT_HEAD_DIM, n), BF16),
                   jax.ShapeDtypeStruct((b, HG_HEADS, n, HG_DIM), BF16),
                   jax.ShapeDtypeStruct((b, HG_HEADS, n, HG_DIM), F32),
                   jax.ShapeDtypeStruct((b, HG_HEADS, n, HG_DIM), F32),
                   jax.ShapeDtypeStruct((b, HG_HEADS, n, HG_DIM), BF16),
                   jax.ShapeDtypeStruct((b, n, HG_WIDTH), F32)],
        compiler_params=_params("parallel", "parallel"),
        name="in_proj",
    )(x, g, w_in, qg, kg, cos, sin, lb, seg)


def _attention_kernel(qt_ref, qtn_ref, k_ref, vt_ref, o_ref, s_even, s_odd):
    n = k_ref.shape[2]
    part = n // ATT_KEY_SPLITS
    parts = [slice(j * part, (j + 1) * part) for j in range(ATT_KEY_SPLITS)]

    def scores(i, s_scr, queries_t=qt_ref):
        for rows in parts:
            s_scr[rows, :] = _dot(k_ref[0, i // ATT_GROUP, rows, :], queries_t[0, i])

    def finish(i, s_scr):
        s = [s_scr[rows, :] for rows in parts]
        m = functools.reduce(jnp.maximum, [jnp.max(x, axis=0, keepdims=True) for x in s])
        p = [jnp.exp2(x - m) for x in s]
        l = sum(jnp.sum(x, axis=0, keepdims=True) for x in p)
        o_t = sum(_dot(vt_ref[0, i // ATT_GROUP, :, rows], x.astype(BF16)) for rows, x in zip(parts, p))
        o_ref[0, i] = (o_t / l).astype(BF16)

    @pl.when(pl.program_id(1) == 0)
    def _():
        scores(0, s_even)

    def pair(j, carry):
        scores(2 * j + 1, s_odd)
        finish(2 * j, s_even)
        scores(2 * j + 2, s_even)
        finish(2 * j + 1, s_odd)
        return carry

    lax.fori_loop(0, ATT_HEADS // 2 - 1, pair, 0)
    scores(ATT_HEADS - 1, s_odd)
    finish(ATT_HEADS - 2, s_even)
    scores(0, s_even, qtn_ref)
    finish(ATT_HEADS - 1, s_odd)


def _attention(q_t, k, v_t, tq):
    b, _, dh, n = q_t.shape
    last = n // tq - 1
    return pl.pallas_call(
        _attention_kernel,
        grid=(b, n // tq),
        in_specs=[pl.BlockSpec((1, ATT_HEADS, dh, tq), lambda i, j: (i, 0, 0, j)),
                  pl.BlockSpec((1, ATT_HEADS, dh, tq), lambda i, j: (i, 0, 0, jnp.minimum(j + 1, last))),
                  pl.BlockSpec((1, ATT_KV_HEADS, n, dh), lambda i, j: (i, 0, 0, 0)),
                  pl.BlockSpec((1, ATT_KV_HEADS, dh, n), lambda i, j: (i, 0, 0, 0))],
        out_specs=pl.BlockSpec((1, ATT_HEADS, dh, tq), lambda i, j: (i, 0, 0, j)),
        out_shape=jax.ShapeDtypeStruct((b, ATT_HEADS, dh, n), BF16),
        scratch_shapes=[pltpu.VMEM((n, tq), F32), pltpu.VMEM((n, tq), F32)],
        compiler_params=_params("parallel", "arbitrary"),
        name="attention",
    )(q_t, q_t, k, v_t)


def _hgrn_levels():
    m, out = 1, []
    while m < HG_CHUNK:
        out.append(m)
        m *= 2
    return out


def _hgrn_tables():
    c = HG_CHUNK
    t = np.arange(c)[:, None]
    u = np.arange(c)[None, :]
    mats, masks, fast = [], [], []
    for reverse in (False, True):
        blocks, pairs = [], [t == u]
        for m in _hgrn_levels():
            base = (t // (2 * m)) * (2 * m)
            if not reverse:
                r = base + m - 1
                block = np.where(t > r, (u > r) & (u <= t), (u > t) & (u <= r))
                roles = ((t // m) % 2 == 1) & ((u // m) % 2 == 0)
            else:
                r = base + m
                block = np.where(t < r, (u >= t) & (u < r), (u >= r) & (u < t))
                roles = ((t // m) % 2 == 0) & ((u // m) % 2 == 1)
            if m < SUBLANES:
                blocks.append(block)
            pairs.append(roles & (t // (2 * m) == u // (2 * m)))
        blocks.append(u >= t if reverse else u <= t)
        mats.append(np.tile(np.concatenate(blocks, axis=0), (1, 2)))
        masks.append(np.stack(pairs))
        fast.append((t // HG_FAST_BLOCK == u // HG_FAST_BLOCK) & (u >= t if reverse else u <= t))
    return tuple(np.stack(x).astype(np.float32) for x in (mats, masks, fast))


def _hgrn_sums(lf2, sum_mat):
    hi = lf2.astype(BF16)
    mid = (lf2 - hi.astype(F32)).astype(BF16)
    return _dot(sum_mat, jnp.concatenate([hi, mid], axis=0))


def _hgrn_chunk(q_ref, lf_ref, v_ref, o_ref, state_ref, sum_mat, pair_mask, fast_mask, b_scr, reverse, fast):
    c = HG_CHUNK
    levels = list(enumerate(_hgrn_levels()))
    n_small = sum(m < SUBLANES for _, m in levels)
    q, lf2, v, state = q_ref[...], lf_ref[...], v_ref[...], state_ref[...]
    qb = q.astype(BF16)
    kb = (1.0 - jnp.exp2(lf2)).astype(BF16)
    if fast:
        b = _hgrn_sums(lf2, sum_mat[n_small * c:, :])
        yield
        b_scr[...] = b
        parts = []
        for g in range(c // HG_FAST_BLOCK):
            lo, up = g * HG_FAST_BLOCK, (g + 1) * HG_FAST_BLOCK
            before = up if reverse else lo - 1
            parts.append(b[lo:up] - b_scr[before:before + 1, :] if 0 <= before < c else b[lo:up])
        loc = jnp.concatenate(parts, axis=0)
        a = _dot_nt(qb * jnp.exp2(loc).astype(BF16), kb * jnp.exp2(-loc).astype(BF16)) * fast_mask[...]
        yield
        levels = [(j, m) for j, m in levels if m >= HG_FAST_BLOCK]
    else:
        sums = _hgrn_sums(lf2, sum_mat[...])
        yield
        b = sums[n_small * c:]
        b_scr[...] = b
        a = _dot_nt(qb, kb) * pair_mask[0]
        yield
    for j, m in levels:
        if m < SUBLANES:
            e = sums[j * c:(j + 1) * c]
        else:
            parts = []
            for g in range(c // (2 * m)):
                lo, mid_row, up = 2 * m * g, 2 * m * g + m, 2 * m * (g + 1)
                r = mid_row if reverse else mid_row - 1
                b_r = b_scr[r:r + 1, :]
                parts += [b[lo:mid_row] - b_r, b_r - b[mid_row:up]] if reverse else \
                         [b_r - b[lo:mid_row], b[mid_row:up] - b_r]
            e = jnp.concatenate(parts, axis=0)
        fac = jnp.exp2(e).astype(BF16)
        a = a + _dot_nt(qb * fac, kb * fac) * pair_mask[j + 1]
        yield
    vb = v.astype(BF16)
    last = 0 if reverse else c - 1
    total = b_scr[last:last + 1, :]
    o_ref[...] = _dot(a.astype(BF16), vb) + _dot_nt(qb * jnp.exp2(b).astype(BF16), state.astype(BF16))
    yield
    k_dec = kb * jnp.exp2(total - b).astype(BF16)
    state_ref[...] = state * jnp.exp2(total) + _dot_tn(vb, k_dec)


def _hgrn_kernel(qf_ref, lff_ref, vf_ref, qb_ref, lfb_ref, vb_ref, sm_ref, pm_ref, fm_ref, *rest):
    n_cast = (len(rest) - 4) // 2
    cast_in, (of_ref, ob_ref), cast_out = rest[:n_cast], rest[n_cast:n_cast + 2], rest[n_cast + 2:2 * n_cast + 2]
    s_ref, b_scr = rest[2 * n_cast + 2:]

    @pl.when(pl.program_id(1) == 0)
    def _():
        s_ref[...] = jnp.zeros_like(s_ref)

    for src, dst in zip(cast_in, cast_out):
        dst[...] = src[...].astype(dst.dtype)

    c = HG_CHUNK
    rows_per_step = qf_ref.shape[2]
    chains = [(d, h) for h in range(HG_HEADS) for d in (0, 1)]
    q_refs, lf_refs, v_refs, o_refs = (qf_ref, qb_ref), (lff_ref, lfb_ref), (vf_ref, vb_ref), (of_ref, ob_ref)

    worst = jnp.zeros((1, HG_DIM), F32)
    for d, h in chains:
        for g in range(rows_per_step // HG_FAST_BLOCK):
            block = lf_refs[d][0, h, g * HG_FAST_BLOCK:(g + 1) * HG_FAST_BLOCK, :]
            worst = jnp.minimum(worst, jnp.sum(block, axis=0, keepdims=True))
    mild = jnp.min(worst) >= -HG_FAST_MAX_LOG2

    def run(fast):
        n_sub = rows_per_step // c
        for sub in range(n_sub):
            work = []
            for d, h in chains:
                k = n_sub - 1 - sub if d else sub
                rows = slice(k * c, (k + 1) * c)
                work.append(_hgrn_chunk(q_refs[d].at[0, h, rows, :], lf_refs[d].at[0, h, rows, :],
                                        v_refs[d].at[0, h, rows, :], o_refs[d].at[0, h, rows, :],
                                        s_ref.at[d, h], sm_ref.at[d], pm_ref.at[d], fm_ref.at[d],
                                        b_scr.at[d, h], bool(d), fast))
            for _ in itertools.zip_longest(*work):
                pass

    pl.when(mild)(lambda: run(True))
    pl.when(jnp.logical_not(mild))(lambda: run(False))


def _hgrn(hq, lf_f, lf_b, hi, to_cast):
    b, nh, n, dk = hq.shape
    rows_per_step = HG_CHUNK * HG_STEP_CHUNKS if n % (HG_CHUNK * HG_STEP_CHUNKS) == 0 else HG_CHUNK
    nc = n // rows_per_step
    blk = (1, nh, rows_per_step, dk)
    fwd = pl.BlockSpec(blk, lambda i, j: (i, 0, j, 0))
    bwd = pl.BlockSpec(blk, lambda i, j: (i, 0, nc - 1 - j, 0))
    out = jax.ShapeDtypeStruct((b, nh, n, dk), F32)
    tables = _hgrn_tables()
    steps = b * nc

    def slab_spec(w):
        hold = next(h for h in range(1, steps + 1)
                    if steps % h == 0 and w.shape[0] % (steps // h * BF16_SUBLANES) == 0)
        return pl.BlockSpec((w.shape[0] // (steps // hold), w.shape[1]), lambda i, j: ((i * nc + j) // hold, 0))

    slabs = [slab_spec(w) for w in to_cast]
    res = pl.pallas_call(
        _hgrn_kernel,
        grid=(b, nc),
        in_specs=[fwd, fwd, fwd, bwd, bwd, bwd] + [_const_spec(t.shape) for t in tables] + slabs,
        out_specs=[fwd, bwd] + slabs,
        out_shape=[out, out] + [jax.ShapeDtypeStruct(w.shape, BF16) for w in to_cast],
        scratch_shapes=[pltpu.VMEM((2, nh, dk, dk), F32), pltpu.VMEM((2, nh, HG_CHUNK, dk), F32)],
        compiler_params=_params("parallel", "arbitrary"),
        name="hgrn",
    )(hq, lf_f, hi, hq, lf_b, hi, jnp.asarray(tables[0], BF16), jnp.asarray(tables[1]), jnp.asarray(tables[2]),
      *to_cast)
    return res[0], res[1], res[2:]


def _mix_cross_kernel(x_ref, att_ref, of_ref, ob_ref, sg_ref, og_ref, wo_ref, pmg_ref, pxg_ref,
                      wq_ref, km_ref, vm_ref, wxo_ref, poxg_ref, pfg_ref, x2_ref, h3_ref):
    tm = x_ref.shape[1]
    part = tm // MIX_ROW_PARTS

    def rows_of(lo):
        rows = slice(lo, lo + part)
        rec = []
        for i in range(HG_HEADS):
            o = _rms(of_ref[0, i, rows, :] + ob_ref[0, i, rows, :], og_ref[...])
            rec.append((o * sg_ref[0, rows, i * HG_DIM:(i + 1) * HG_DIM]).astype(BF16))
        att_t = att_ref[0, :, :, rows].reshape(ATT_Q_DIM, part)
        mixed = _dot_tn(att_t, wo_ref[:ATT_Q_DIM, :].astype(BF16)) + \
            _dot(jnp.concatenate(rec, axis=-1), wo_ref[ATT_Q_DIM:, :].astype(BF16))
        yield
        x1 = x_ref[0, rows, :] + _rms(mixed, pmg_ref[...])
        h2 = _rms(x1, pxg_ref[...]).astype(BF16)
        d = h2.shape[-1]
        dh = d // X_HEADS
        q = _dot(h2, wq_ref[...].astype(BF16)) * (dh ** -0.5)
        yield
        heads = []
        for i in range(X_HEADS):
            sl = slice(i * dh, (i + 1) * dh)
            s = _dot_nt(q[:, sl].astype(BF16), km_ref[0, :, sl])
            p = jnp.exp(s - jnp.max(s, axis=-1, keepdims=True))
            l = jnp.sum(p, axis=-1, keepdims=True)
            heads.append((_dot(p.astype(BF16), vm_ref[0, :, sl]) / l).astype(BF16))
        yield
        xo = _dot(jnp.concatenate(heads, axis=-1), wxo_ref[...].astype(BF16))
        yield
        x2 = x1 + _rms(xo, poxg_ref[...])
        x2_ref[0, rows, :] = x2
        h3_ref[0, rows, :] = _rms(x2, pfg_ref[...]).astype(BF16)

    for _ in itertools.zip_longest(*[rows_of(j * part) for j in range(MIX_ROW_PARTS)]):
        pass


def _mix_cross(x, att, o_f, o_b, sg, og, w_out, pmg, pxg, w_xq, k_mem, v_mem, w_xo, poxg, pfg, tm):
    b, n, d = x.shape
    nm = k_mem.shape[1]
    tok = lambda i, j: (i, j, 0)
    head = lambda i, j: (i, 0, j, 0)
    batch = lambda i, j: (i, 0, 0)
    vec = _const_spec((1, d))
    mat = _resident_spec((d, d))
    return pl.pallas_call(
        _mix_cross_kernel,
        grid=(b, n // tm),
        in_specs=[pl.BlockSpec((1, tm, d), tok),
                  pl.BlockSpec((1, ATT_HEADS, ATT_HEAD_DIM, tm), lambda i, j: (i, 0, 0, j)),
                  pl.BlockSpec((1, HG_HEADS, tm, HG_DIM), head), pl.BlockSpec((1, HG_HEADS, tm, HG_DIM), head),
                  pl.BlockSpec((1, tm, HG_WIDTH), tok), _const_spec((1, HG_DIM)),
                  mat, vec, vec, mat,
                  pl.BlockSpec((1, nm, d), batch), pl.BlockSpec((1, nm, d), batch),
                  mat, vec, vec],
        out_specs=[pl.BlockSpec((1, tm, d), tok), pl.BlockSpec((1, tm, d), tok)],
        out_shape=[jax.ShapeDtypeStruct((b, n, d), F32), jax.ShapeDtypeStruct((b, n, d), BF16)],
        compiler_params=_params("parallel", "parallel"),
        name="mix_cross",
    )(x, att, o_f, o_b, sg, og, w_out, pmg, pxg, w_xq, k_mem, v_mem, w_xo, poxg, pfg)


def _conv_ffn_kernel(x_ref, h_ref, hp_ref, hn_ref, wu_ref, cw_ref, cb_ref, wd_ref, g_ref, o_ref, act_scr):
    j = pl.program_id(1)
    tm = h_ref.shape[1]
    d_ff = wd_ref.shape[0]
    ck = FF_CHUNK
    prev = jnp.where(j > 0, hp_ref[0], jnp.zeros_like(hp_ref[0]))
    nxt = jnp.where(j < pl.num_programs(1) - 1, hn_ref[0], jnp.zeros_like(hn_ref[0]))
    hext = jnp.concatenate([prev, h_ref[0], nxt], axis=0)

    def conv(col):
        u = _dot(hext, wu_ref[:, col:col + ck])
        taps = (pltpu.roll(u, 1, 0), u, pltpu.roll(u, u.shape[0] - 1, 0))
        out = cb_ref[:, col:col + ck]
        for t in range(CONV_W):
            out = out + taps[t][HALO:HALO + tm] * cw_ref[t:t + 1, col:col + ck]
        return out

    for c in range(d_ff // ck):
        act_scr[:, c * ck:(c + 1) * ck] = (_silu(conv(c * ck)) * conv(d_ff + c * ck)).astype(BF16)
    o_ref[0] = x_ref[0] + _rms(_dot(act_scr[...], wd_ref[...]), g_ref[...])


def _conv_ffn(x2, h3, w_up, conv_w, conv_b, wd, g, tm):
    b, n, d = x2.shape
    d_ff = wd.shape[0]
    ck = FF_CHUNK
    hb = tm // HALO
    last = n // HALO - 1
    tok = lambda i, j: (i, j, 0)
    full = _resident_spec
    return pl.pallas_call(
        _conv_ffn_kernel,
        grid=(b, n // tm),
        in_specs=[pl.BlockSpec((1, tm, d), tok), pl.BlockSpec((1, tm, d), tok),
                  pl.BlockSpec((1, HALO, d), lambda i, j: (i, jnp.maximum(j * hb - 1, 0), 0)),
                  pl.BlockSpec((1, HALO, d), lambda i, j: (i, jnp.minimum((j + 1) * hb, last), 0)),
                  full((d, 2 * d_ff)), full((CONV_W, 2 * d_ff)), full((1, 2 * d_ff)),
                  full((d_ff, d)), _const_spec((1, d))],
        out_specs=pl.BlockSpec((1, tm, d), tok),
        out_shape=jax.ShapeDtypeStruct((b, n, d), F32),
        scratch_shapes=[pltpu.VMEM((tm, d_ff), BF16)],
        compiler_params=_params("parallel", "parallel"),
        name="conv_ffn",
    )(x2, h3, h3, h3, w_up, conv_w, conv_b, wd, g)


def _rope_tables(n):
    pairs = ATT_HEAD_DIM // 4
    pos = np.arange(n)
    inv = np.power(np.float32(ROPE_THETA), -np.arange(pairs, dtype=np.float32) / np.float32(pairs))
    ang = np.concatenate([(pos // GRID_W).astype(np.float32)[:, None] * inv,
                          (pos % GRID_W).astype(np.float32)[:, None] * inv], axis=-1)
    cos = np.repeat(np.cos(ang), 2, axis=-1)
    sin = np.repeat(np.sin(ang), 2, axis=-1) * np.tile(np.array([-1.0, 1.0], np.float32), ATT_HEAD_DIM // 2)
    return jnp.asarray(np.tile(cos, (1, 2)), F32), jnp.asarray(np.tile(sin, (1, 2)), F32)


def _segment_ones():
    i = jnp.arange(LANES) // ATT_HEAD_DIM
    return jnp.tile((i[:, None] == i[None, :]).astype(BF16), (2, 1))


def _layer(x, mem, lb, pre_mix_g, w_in, q_norm_g, k_norm_g, hg_out_norm_g, w_out, post_mix_g, pre_x_g,
           mem_norm_g, w_xq, w_xkv, w_xo, post_x_g, pre_ffn_g, w_up, conv_w, conv_b, w_down, post_ffn_g):
    b, n, d = x.shape
    d_ff = w_down.shape[0]
    assert n % HG_CHUNK == 0 and n % GRID_W == 0 and d_ff % FF_CHUNK == 0
    tm = min(256, n)
    tf = min(512, n)
    row = lambda g: g.reshape(1, -1).astype(F32)

    cos, sin = _rope_tables(n)
    qg = jnp.tile(q_norm_g, ATT_HEADS).reshape(1, -1) * (ATT_HEAD_DIM ** -0.5 * LOG2E)
    kg = jnp.tile(k_norm_g, ATT_KV_HEADS).reshape(1, -1)

    k_mem, v_mem = _mem_proj(mem, row(mem_norm_g), w_xkv)
    q_t, k, v_t, hq, lf_f, lf_b, hi, sg = _in_proj(
        x, row(pre_mix_g), w_in, qg, kg, cos, sin, lb,
        _segment_ones(), tf)
    att = _attention(q_t, k, v_t, tm)
    o_f, o_b, (w_up_bf16, w_down_bf16) = _hgrn(hq, lf_f, lf_b, hi, [w_up, w_down])
    x2, h3 = _mix_cross(x, att, o_f, o_b, sg, row(hg_out_norm_g), w_out, row(post_mix_g),
                        row(pre_x_g), w_xq, k_mem, v_mem, w_xo,
                        row(post_x_g), row(pre_ffn_g), tf)

    return _conv_ffn(x2, h3, w_up_bf16, conv_w, row(conv_b), w_down_bf16, row(post_ffn_g), tf)


def kernel(x, mem, pre_mix_g, w_in, q_norm_g, k_norm_g, hg_lb, hg_out_norm_g, w_out, post_mix_g, pre_x_g,
           mem_norm_g, w_xq, w_xkv, w_xo, post_x_g, pre_ffn_g, w_up, conv_w, conv_b, w_down, post_ffn_g):
    lb_all = jnp.cumsum(jax.nn.softmax(hg_lb.astype(F32), axis=1), axis=1)
    for l in range(w_in.shape[0]):
        x = _layer(x, mem, lb_all[:, l], pre_mix_g[l], w_in[l], q_norm_g[l], k_norm_g[l], hg_out_norm_g[l],
                   w_out[l], post_mix_g[l], pre_x_g[l], mem_norm_g[l], w_xq[l], w_xkv[l], w_xo[l],
                   post_x_g[l], pre_ffn_g[l], w_up[l], conv_w[l], conv_b[l], w_down[l], post_ffn_g[l])
    return x
```

```python
import functools
import itertools
import math

import numpy as np
import jax
import jax.numpy as jnp
from jax import lax
from jax.experimental import pallas as pl
from jax.experimental.pallas import tpu as pltpu

F32 = jnp.float32
BF16 = jnp.bfloat16

EPS = 1e-6
LOG2E = math.log2(math.e)
GRID_W = 64
ROPE_THETA = 10000.0

ATT_HEADS = 8
ATT_KV_HEADS = 2
ATT_GROUP = ATT_HEADS // ATT_KV_HEADS
ATT_HEAD_DIM = 64
ATT_Q_DIM = ATT_HEADS * ATT_HEAD_DIM
ATT_KV_DIM = ATT_KV_HEADS * ATT_HEAD_DIM
ATT_KEY_SPLITS = 2

HG_HEADS = 4
HG_DIM = 128
HG_WIDTH = HG_HEADS * HG_DIM
HG_CHUNK = 128
HG_STEP_CHUNKS = 8
HG_UNROLL = 4
HG_FAST_BLOCK = 64
HG_FAST_MAX_LOG2 = 96.0

X_HEADS = 4
IN_ROW_PARTS = 2
MIX_ROW_PARTS = 2
CONV_W = 3
FF_CHUNK = 256

VMEM_LIMIT = 56 * 1024 * 1024
SUBLANES = 8
BF16_SUBLANES = 16
LANES = 128

HALO = BF16_SUBLANES


def _params(*sem):
    return pltpu.CompilerParams(dimension_semantics=sem, vmem_limit_bytes=VMEM_LIMIT)


def _rms(x, g):
    ms = jnp.mean(x * x, axis=-1, keepdims=True)
    return x * lax.rsqrt(ms + EPS) * g


def _silu(x):
    return x / (1.0 + jnp.exp(-x))


def _dot(a, b):
    return jnp.dot(a, b, preferred_element_type=F32)


def _dot_nt(a, b):
    return lax.dot_general(a, b, (((1,), (1,)), ((), ())), preferred_element_type=F32)


def _dot_tn(a, b):
    return lax.dot_general(a, b, (((0,), (0,)), ((), ())), preferred_element_type=F32)


def _const_spec(shape):
    return pl.BlockSpec(shape, lambda *_: (0,) * len(shape))


def _resident_spec(shape):
    return pl.BlockSpec(shape, lambda *_: (0,) * len(shape), pipeline_mode=pl.Buffered(1))


def _mem_proj_kernel(m_ref, g_ref, w_ref, k_ref, v_ref):
    d = m_ref.shape[-1]
    m = _rms(m_ref[0], g_ref[...]).astype(BF16)
    kv = _dot(m, w_ref[...].astype(BF16))
    k_ref[0] = kv[:, :d].astype(BF16)
    v_ref[0] = kv[:, d:].astype(BF16)


def _mem_proj(mem, g, w_xkv):
    b, nm, d = mem.shape
    out = jax.ShapeDtypeStruct((b, nm, d), BF16)
    return pl.pallas_call(
        _mem_proj_kernel,
        grid=(b,),
        in_specs=[pl.BlockSpec((1, nm, d), lambda i: (i, 0, 0)),
                  _const_spec((1, d)), _resident_spec((d, 2 * d))],
        out_specs=[pl.BlockSpec((1, nm, d), lambda i: (i, 0, 0))] * 2,
        out_shape=[out, out],
        compiler_params=_params("parallel"),
        name="mem_proj",
    )(mem, g, w_xkv)


def _head_norm_rope(a, seg, gain, cos, sin_signed):
    w = a.shape[-1]
    sq = a * a
    hi = sq.astype(BF16)
    lo = (sq - hi.astype(F32)).astype(BF16)
    ss = [_dot(jnp.concatenate([hi[:, j:j + LANES], lo[:, j:j + LANES]], axis=-1), seg) for j in range(0, w, LANES)]
    ss = jnp.concatenate(ss, axis=-1) if len(ss) > 1 else ss[0]
    an = a * lax.rsqrt(ss * (1.0 / ATT_HEAD_DIM) + EPS) * gain
    lane = lax.broadcasted_iota(jnp.int32, a.shape, 1)
    partner = jnp.where(lane % 2 == 0, pltpu.roll(an, w - 1, 1), pltpu.roll(an, 1, 1))
    return an * cos + partner * sin_signed


def _in_proj_kernel(x_ref, g_ref, w_ref, qg_ref, kg_ref, cos_ref, sin_ref, lb_ref, seg_ref,
                    qt_ref, k_ref, vt_ref, hq_ref, lff_ref, lfb_ref, hi_ref, hg_ref):
    part = x_ref.shape[1] // IN_ROW_PARTS

    def rows_of(r0):
        rows = slice(r0, r0 + part)
        h = _rms(x_ref[0, rows, :], g_ref[...]).astype(BF16)

        def proj(lo, width):
            return _dot(h, w_ref[:, lo:lo + width].astype(BF16))

        def heads_out(ref, val):
            for i in range(HG_HEADS):
                ref[0, i, rows, :] = val[:, i * HG_DIM:(i + 1) * HG_DIM].astype(ref.dtype)

        cos = cos_ref[rows, :]
        sin = sin_ref[rows, :]
        c0 = 0
        q = _head_norm_rope(proj(c0, ATT_Q_DIM), seg_ref[...], qg_ref[...],
                            jnp.concatenate([cos] * (ATT_Q_DIM // LANES), axis=-1),
                            jnp.concatenate([sin] * (ATT_Q_DIM // LANES), axis=-1))
        q_t = jnp.transpose(q)
        for i in range(ATT_HEADS):
            qt_ref[0, i, :, rows] = q_t[i * ATT_HEAD_DIM:(i + 1) * ATT_HEAD_DIM, :].astype(BF16)
        c0 += ATT_Q_DIM
        yield
        k = _head_norm_rope(proj(c0, ATT_KV_DIM), seg_ref[...], kg_ref[...], cos, sin)
        c0 += ATT_KV_DIM
        v_t = jnp.transpose(proj(c0, ATT_KV_DIM))
        c0 += ATT_KV_DIM
        for i in range(ATT_KV_HEADS):
            k_ref[0, i, rows, :] = k[:, i * ATT_HEAD_DIM:(i + 1) * ATT_HEAD_DIM].astype(BF16)
            vt_ref[0, i, :, rows] = v_t[i * ATT_HEAD_DIM:(i + 1) * ATT_HEAD_DIM, :].astype(BF16)
        yield
        heads_out(hq_ref, _silu(proj(c0, HG_WIDTH)))
        c0 += HG_WIDTH
        yield
        for d, ref in enumerate((lff_ref, lfb_ref)):
            lb = lb_ref[d:d + 1, :]
            z = proj(c0, HG_WIDTH)
            heads_out(ref, jnp.log(lb + (1.0 - lb) / (1.0 + jnp.exp(-z))) * LOG2E)
            c0 += HG_WIDTH
            yield
        heads_out(hi_ref, proj(c0, HG_WIDTH))
        c0 += HG_WIDTH
        yield
        hg_ref[0, rows, :] = _silu(proj(c0, HG_WIDTH))

    for _ in itertools.zip_longest(*[rows_of(j * part) for j in range(IN_ROW_PARTS)]):
        pass


def _in_proj(x, g, w_in, qg, kg, cos, sin, lb, seg, tm):
    b, n, d = x.shape
    nt = n // tm
    n_in = w_in.shape[1]
    tok = lambda i, j: (i, j, 0)
    head = lambda i, j: (i, 0, j, 0)
    pos = lambda i, j: (j, 0)
    return pl.pallas_call(
        _in_proj_kernel,
        grid=(b, nt),
        in_specs=[pl.BlockSpec((1, tm, d), tok), _const_spec((1, d)), _resident_spec((d, n_in)),
                  _const_spec((1, ATT_Q_DIM)), _const_spec((1, ATT_KV_DIM)),
                  pl.BlockSpec((tm, LANES), pos), pl.BlockSpec((tm, LANES), pos),
                  _const_spec((2, HG_WIDTH)),
                  _const_spec((2 * LANES, LANES))],
        out_specs=[pl.BlockSpec((1, ATT_HEADS, ATT_HEAD_DIM, tm), lambda i, j: (i, 0, 0, j)),
                   pl.BlockSpec((1, ATT_KV_HEADS, tm, ATT_HEAD_DIM), head),
                   pl.BlockSpec((1, ATT_KV_HEADS, ATT_HEAD_DIM, tm), lambda i, j: (i, 0, 0, j)),
                   pl.BlockSpec((1, HG_HEADS, tm, HG_DIM), head),
                   pl.BlockSpec((1, HG_HEADS, tm, HG_DIM), head),
                   pl.BlockSpec((1, HG_HEADS, tm, HG_DIM), head),
                   pl.BlockSpec((1, HG_HEADS, tm, HG_DIM), head),
                   pl.BlockSpec((1, tm, HG_WIDTH), tok)],
        out_shape=[jax.ShapeDtypeStruct((b, ATT_HEADS, ATT_HEAD_DIM, n), BF16),
                   jax.ShapeDtypeStruct((b, ATT_KV_HEADS, n, ATT_HEAD_DIM), BF16),
                   jax.ShapeDtypeStruct((b, ATT_KV_HEADS, ATT_HEAD_DIM, n), BF16),
                   jax.ShapeDtypeStruct((b, HG_HEADS, n, HG_DIM), BF16),
                   jax.ShapeDtypeStruct((b, HG_HEADS, n, HG_DIM), F32),
                   jax.ShapeDtypeStruct((b, HG_HEADS, n, HG_DIM), F32),
                   jax.ShapeDtypeStruct((b, HG_HEADS, n, HG_DIM), BF16),
                   jax.ShapeDtypeStruct((b, n, HG_WIDTH), F32)],
        compiler_params=_params("parallel", "parallel"),
        name="in_proj",
    )(x, g, w_in, qg, kg, cos, sin, lb, seg)


def _attention_kernel(qt_ref, qtn_ref, k_ref, vt_ref, o_ref, s_even, s_odd):
    n = k_ref.shape[2]
    part = n // ATT_KEY_SPLITS
    parts = [slice(j * part, (j + 1) * part) for j in range(ATT_KEY_SPLITS)]

    def scores(i, s_scr, queries_t=qt_ref):
        for rows in parts:
            s_scr[rows, :] = _dot(k_ref[0, i // ATT_GROUP, rows, :], queries_t[0, i])

    def finish(i, s_scr):
        s = [s_scr[rows, :] for rows in parts]
        m = functools.reduce(jnp.maximum, [jnp.max(x, axis=0, keepdims=True) for x in s])
        p = [jnp.exp2(x - m) for x in s]
        l = sum(jnp.sum(x, axis=0, keepdims=True) for x in p)
        o_t = sum(_dot(vt_ref[0, i // ATT_GROUP, :, rows], x.astype(BF16)) for rows, x in zip(parts, p))
        o_ref[0, i] = (o_t / l).astype(BF16)

    @pl.when(pl.program_id(1) == 0)
    def _():
        scores(0, s_even)

    def pair(j, carry):
        scores(2 * j + 1, s_odd)
        finish(2 * j, s_even)
        scores(2 * j + 2, s_even)
        finish(2 * j + 1, s_odd)
        return carry

    lax.fori_loop(0, ATT_HEADS // 2 - 1, pair, 0)
    scores(ATT_HEADS - 1, s_odd)
    finish(ATT_HEADS - 2, s_even)
    scores(0, s_even, qtn_ref)
    finish(ATT_HEADS - 1, s_odd)


def _attention(q_t, k, v_t, tq):
    b, _, dh, n = q_t.shape
    last = n // tq - 1
    return pl.pallas_call(
        _attention_kernel,
        grid=(b, n // tq),
        in_specs=[pl.BlockSpec((1, ATT_HEADS, dh, tq), lambda i, j: (i, 0, 0, j)),
                  pl.BlockSpec((1, ATT_HEADS, dh, tq), lambda i, j: (i, 0, 0, jnp.minimum(j + 1, last))),
                  pl.BlockSpec((1, ATT_KV_HEADS, n, dh), lambda i, j: (i, 0, 0, 0)),
                  pl.BlockSpec((1, ATT_KV_HEADS, dh, n), lambda i, j: (i, 0, 0, 0))],
        out_specs=pl.BlockSpec((1, ATT_HEADS, dh, tq), lambda i, j: (i, 0, 0, j)),
        out_shape=jax.ShapeDtypeStruct((b, ATT_HEADS, dh, n), BF16),
        scratch_shapes=[pltpu.VMEM((n, tq), F32), pltpu.VMEM((n, tq), F32)],
        compiler_params=_params("parallel", "arbitrary"),
        name="attention",
    )(q_t, q_t, k, v_t)


def _hgrn_levels():
    m, out = 1, []
    while m < HG_CHUNK:
        out.append(m)
        m *= 2
    return out


def _hgrn_tables():
    c = HG_CHUNK
    t = np.arange(c)[:, None]
    u = np.arange(c)[None, :]
    mats, masks, fast = [], [], []
    for reverse in (False, True):
        blocks, pairs = [], [t == u]
        for m in _hgrn_levels():
            base = (t // (2 * m)) * (2 * m)
            if not reverse:
                r = base + m - 1
                block = np.where(t > r, (u > r) & (u <= t), (u > t) & (u <= r))
                roles = ((t // m) % 2 == 1) & ((u // m) % 2 == 0)
            else:
                r = base + m
                block = np.where(t < r, (u >= t) & (u < r), (u >= r) & (u < t))
                roles = ((t // m) % 2 == 0) & ((u // m) % 2 == 1)
            if m < SUBLANES:
                blocks.append(block)
            pairs.append(roles & (t // (2 * m) == u // (2 * m)))
        blocks.append(u >= t if reverse else u <= t)
        mats.append(np.tile(np.concatenate(blocks, axis=0), (1, 2)))
        masks.append(np.stack(pairs))
        fast.append((t // HG_FAST_BLOCK == u // HG_FAST_BLOCK) & (u >= t if reverse else u <= t))
    return tuple(np.stack(x).astype(np.float32) for x in (mats, masks, fast))


def _hgrn_sums(lf2, sum_mat):
    hi = lf2.astype(BF16)
    mid = (lf2 - hi.astype(F32)).astype(BF16)
    return _dot(sum_mat, jnp.concatenate([hi, mid], axis=0))


def _hgrn_chunk(q_ref, lf_ref, v_ref, o_ref, state_ref, sum_mat, pair_mask, fast_mask, b_scr, reverse, fast):
    c = HG_CHUNK
    levels = list(enumerate(_hgrn_levels()))
    n_small = sum(m < SUBLANES for _, m in levels)
    q, lf2, v, state = q_ref[...], lf_ref[...], v_ref[...], state_ref[...]
    qb = q.astype(BF16)
    kb = (1.0 - jnp.exp2(lf2)).astype(BF16)
    if fast:
        b = _hgrn_sums(lf2, sum_mat[n_small * c:, :])
        yield
        b_scr[...] = b
        parts = []
        for g in range(c // HG_FAST_BLOCK):
            lo, up = g * HG_FAST_BLOCK, (g + 1) * HG_FAST_BLOCK
            before = up if reverse else lo - 1
            parts.append(b[lo:up] - b_scr[before:before + 1, :] if 0 <= before < c else b[lo:up])
        loc = jnp.concatenate(parts, axis=0)
        a = _dot_nt(qb * jnp.exp2(loc).astype(BF16), kb * jnp.exp2(-loc).astype(BF16)) * fast_mask[...]
        yield
        levels = [(j, m) for j, m in levels if m >= HG_FAST_BLOCK]
    else:
        sums = _hgrn_sums(lf2, sum_mat[...])
        yield
        b = sums[n_small * c:]
        b_scr[...] = b
        a = _dot_nt(qb, kb) * pair_mask[0]
        yield
    for j, m in levels:
        if m < SUBLANES:
            e = sums[j * c:(j + 1) * c]
        else:
            parts = []
            for g in range(c // (2 * m)):
                lo, mid_row, up = 2 * m * g, 2 * m * g + m, 2 * m * (g + 1)
                r = mid_row if reverse else mid_row - 1
                b_r = b_scr[r:r + 1, :]
                parts += [b[lo:mid_row] - b_r, b_r - b[mid_row:up]] if reverse else \
                         [b_r - b[lo:mid_row], b[mid_row:up] - b_r]
            e = jnp.concatenate(parts, axis=0)
        fac = jnp.exp2(e).astype(BF16)
        a = a + _dot_nt(qb * fac, kb * fac) * pair_mask[j + 1]
        yield
    vb = v.astype(BF16)
    last = 0 if reverse else c - 1
    total = b_scr[last:last + 1, :]
    o_ref[...] = _dot(a.astype(BF16), vb) + _dot_nt(qb * jnp.exp2(b).astype(BF16), state.astype(BF16))
    yield
    k_dec = kb * jnp.exp2(total - b).astype(BF16)
    state_ref[...] = state * jnp.exp2(total) + _dot_tn(vb, k_dec)


def _hgrn_kernel(qf_ref, lff_ref, vf_ref, qb_ref, lfb_ref, vb_ref, sm_ref, pm_ref, fm_ref, *rest):
    n_cast = (len(rest) - 4) // 2
    cast_in, (of_ref, ob_ref), cast_out = rest[:n_cast], rest[n_cast:n_cast + 2], rest[n_cast + 2:2 * n_cast + 2]
    s_ref, b_scr = rest[2 * n_cast + 2:]

    @pl.when(pl.program_id(1) == 0)
    def _():
        s_ref[...] = jnp.zeros_like(s_ref)

    for src, dst in zip(cast_in, cast_out):
        dst[...] = src[...].astype(dst.dtype)

    c = HG_CHUNK
    rows_per_step = qf_ref.shape[2]
    chains = [(d, h) for h in range(HG_HEADS) for d in (0, 1)]
    q_refs, lf_refs, v_refs, o_refs = (qf_ref, qb_ref), (lff_ref, lfb_ref), (vf_ref, vb_ref), (of_ref, ob_ref)

    worst = jnp.zeros((1, HG_DIM), F32)
    for d, h in chains:
        for g in range(rows_per_step // HG_FAST_BLOCK):
            block = lf_refs[d][0, h, g * HG_FAST_BLOCK:(g + 1) * HG_FAST_BLOCK, :]
            worst = jnp.minimum(worst, jnp.sum(block, axis=0, keepdims=True))
    mild = jnp.min(worst) >= -HG_FAST_MAX_LOG2

    def run(fast):
        n_sub = rows_per_step // c
        unroll = min(HG_UNROLL, n_sub)

        def trip(t, carry):
            for u in range(unroll):
                sub = t * unroll + u
                work = []
                for d, h in chains:
                    k = n_sub - 1 - sub if d else sub
                    rows = pl.ds(pl.multiple_of(k * c, c), c)
                    work.append(_hgrn_chunk(q_refs[d].at[0, h, rows, :], lf_refs[d].at[0, h, rows, :],
                                            v_refs[d].at[0, h, rows, :], o_refs[d].at[0, h, rows, :],
                                            s_ref.at[d, h], sm_ref.at[d], pm_ref.at[d], fm_ref.at[d],
                                            b_scr.at[d, h], bool(d), fast))
                for _ in itertools.zip_longest(*work):
                    pass
            return carry

        lax.fori_loop(0, n_sub // unroll, trip, 0)

    pl.when(mild)(lambda: run(True))
    pl.when(jnp.logical_not(mild))(lambda: run(False))


def _hgrn(hq, lf_f, lf_b, hi, to_cast):
    b, nh, n, dk = hq.shape
    rows_per_step = HG_CHUNK * HG_STEP_CHUNKS if n % (HG_CHUNK * HG_STEP_CHUNKS) == 0 else HG_CHUNK
    nc = n // rows_per_step
    blk = (1, nh, rows_per_step, dk)
    fwd = pl.BlockSpec(blk, lambda i, j: (i, 0, j, 0))
    bwd = pl.BlockSpec(blk, lambda i, j: (i, 0, nc - 1 - j, 0))
    out = jax.ShapeDtypeStruct((b, nh, n, dk), F32)
    tables = _hgrn_tables()
    steps = b * nc

    def slab_spec(w):
        hold = next(h for h in range(1, steps + 1)
                    if steps % h == 0 and w.shape[0] % (steps // h * BF16_SUBLANES) == 0)
        return pl.BlockSpec((w.shape[0] // (steps // hold), w.shape[1]), lambda i, j: ((i * nc + j) // hold, 0))

    slabs = [slab_spec(w) for w in to_cast]
    res = pl.pallas_call(
        _hgrn_kernel,
        grid=(b, nc),
        in_specs=[fwd, fwd, fwd, bwd, bwd, bwd] + [_const_spec(t.shape) for t in tables] + slabs,
        out_specs=[fwd, bwd] + slabs,
        out_shape=[out, out] + [jax.ShapeDtypeStruct(w.shape, BF16) for w in to_cast],
        scratch_shapes=[pltpu.VMEM((2, nh, dk, dk), F32), pltpu.VMEM((2, nh, HG_CHUNK, dk), F32)],
        compiler_params=_params("parallel", "arbitrary"),
        name="hgrn",
    )(hq, lf_f, hi, hq, lf_b, hi, jnp.asarray(tables[0], BF16), jnp.asarray(tables[1]), jnp.asarray(tables[2]),
      *to_cast)
    return res[0], res[1], res[2:]


def _mix_cross_kernel(x_ref, att_ref, of_ref, ob_ref, sg_ref, og_ref, wo_ref, pmg_ref, pxg_ref,
                      wq_ref, km_ref, vm_ref, wxo_ref, poxg_ref, pfg_ref, x2_ref, h3_ref):
    tm = x_ref.shape[1]
    part = tm // MIX_ROW_PARTS

    def rows_of(lo):
        rows = slice(lo, lo + part)
        rec = []
        for i in range(HG_HEADS):
            o = _rms(of_ref[0, i, rows, :] + ob_ref[0, i, rows, :], og_ref[...])
            rec.append((o * sg_ref[0, rows, i * HG_DIM:(i + 1) * HG_DIM]).astype(BF16))
        att_t = att_ref[0, :, :, rows].reshape(ATT_Q_DIM, part)
        mixed = _dot_tn(att_t, wo_ref[:ATT_Q_DIM, :].astype(BF16)) + \
            _dot(jnp.concatenate(rec, axis=-1), wo_ref[ATT_Q_DIM:, :].astype(BF16))
        yield
        x1 = x_ref[0, rows, :] + _rms(mixed, pmg_ref[...])
        h2 = _rms(x1, pxg_ref[...]).astype(BF16)
        d = h2.shape[-1]
        dh = d // X_HEADS
        q = _dot(h2, wq_ref[...].astype(BF16)) * (dh ** -0.5)
        yield
        heads = []
        for i in range(X_HEADS):
            sl = slice(i * dh, (i + 1) * dh)
            s = _dot_nt(q[:, sl].astype(BF16), km_ref[0, :, sl])
            p = jnp.exp(s - jnp.max(s, axis=-1, keepdims=True))
            l = jnp.sum(p, axis=-1, keepdims=True)
            heads.append((_dot(p.astype(BF16), vm_ref[0, :, sl]) / l).astype(BF16))
        yield
        xo = _dot(jnp.concatenate(heads, axis=-1), wxo_ref[...].astype(BF16))
        yield
        x2 = x1 + _rms(xo, poxg_ref[...])
        x2_ref[0, rows, :] = x2
        h3_ref[0, rows, :] = _rms(x2, pfg_ref[...]).astype(BF16)

    for _ in itertools.zip_longest(*[rows_of(j * part) for j in range(MIX_ROW_PARTS)]):
        pass


def _mix_cross(x, att, o_f, o_b, sg, og, w_out, pmg, pxg, w_xq, k_mem, v_mem, w_xo, poxg, pfg, tm):
    b, n, d = x.shape
    nm = k_mem.shape[1]
    tok = lambda i, j: (i, j, 0)
    head = lambda i, j: (i, 0, j, 0)
    batch = lambda i, j: (i, 0, 0)
    vec = _const_spec((1, d))
    mat = _resident_spec((d, d))
    return pl.pallas_call(
        _mix_cross_kernel,
        grid=(b, n // tm),
        in_specs=[pl.BlockSpec((1, tm, d), tok),
                  pl.BlockSpec((1, ATT_HEADS, ATT_HEAD_DIM, tm), lambda i, j: (i, 0, 0, j)),
                  pl.BlockSpec((1, HG_HEADS, tm, HG_DIM), head), pl.BlockSpec((1, HG_HEADS, tm, HG_DIM), head),
                  pl.BlockSpec((1, tm, HG_WIDTH), tok), _const_spec((1, HG_DIM)),
                  mat, vec, vec, mat,
                  pl.BlockSpec((1, nm, d), batch), pl.BlockSpec((1, nm, d), batch),
                  mat, vec, vec],
        out_specs=[pl.BlockSpec((1, tm, d), tok), pl.BlockSpec((1, tm, d), tok)],
        out_shape=[jax.ShapeDtypeStruct((b, n, d), F32), jax.ShapeDtypeStruct((b, n, d), BF16)],
        compiler_params=_params("parallel", "parallel"),
        name="mix_cross",
    )(x, att, o_f, o_b, sg, og, w_out, pmg, pxg, w_xq, k_mem, v_mem, w_xo, poxg, pfg)


def _conv_ffn_kernel(x_ref, h_ref, hp_ref, hn_ref, wu_ref, cw_ref, cb_ref, wd_ref, g_ref, o_ref, act_scr):
    j = pl.program_id(1)
    tm = h_ref.shape[1]
    d_ff = wd_ref.shape[0]
    ck = FF_CHUNK
    prev = jnp.where(j > 0, hp_ref[0], jnp.zeros_like(hp_ref[0]))
    nxt = jnp.where(j < pl.num_programs(1) - 1, hn_ref[0], jnp.zeros_like(hn_ref[0]))
    hext = jnp.concatenate([prev, h_ref[0], nxt], axis=0)

    def conv(col):
        u = _dot(hext, wu_ref[:, col:col + ck])
        taps = (pltpu.roll(u, 1, 0), u, pltpu.roll(u, u.shape[0] - 1, 0))
        out = cb_ref[:, col:col + ck]
        for t in range(CONV_W):
            out = out + taps[t][HALO:HALO + tm] * cw_ref[t:t + 1, col:col + ck]
        return out

    for c in range(d_ff // ck):
        act_scr[:, c * ck:(c + 1) * ck] = (_silu(conv(c * ck)) * conv(d_ff + c * ck)).astype(BF16)
    o_ref[0] = x_ref[0] + _rms(_dot(act_scr[...], wd_ref[...]), g_ref[...])


def _conv_ffn(x2, h3, w_up, conv_w, conv_b, wd, g, tm):
    b, n, d = x2.shape
    d_ff = wd.shape[0]
    ck = FF_CHUNK
    hb = tm // HALO
    last = n // HALO - 1
    tok = lambda i, j: (i, j, 0)
    full = _resident_spec
    return pl.pallas_call(
        _conv_ffn_kernel,
        grid=(b, n // tm),
        in_specs=[pl.BlockSpec((1, tm, d), tok), pl.BlockSpec((1, tm, d), tok),
                  pl.BlockSpec((1, HALO, d), lambda i, j: (i, jnp.maximum(j * hb - 1, 0), 0)),
                  pl.BlockSpec((1, HALO, d), lambda i, j: (i, jnp.minimum((j + 1) * hb, last), 0)),
                  full((d, 2 * d_ff)), full((CONV_W, 2 * d_ff)), full((1, 2 * d_ff)),
                  full((d_ff, d)), _const_spec((1, d))],
        out_specs=pl.BlockSpec((1, tm, d), tok),
        out_shape=jax.ShapeDtypeStruct((b, n, d), F32),
        scratch_shapes=[pltpu.VMEM((tm, d_ff), BF16)],
        compiler_params=_params("parallel", "parallel"),
        name="conv_ffn",
    )(x2, h3, h3, h3, w_up, conv_w, conv_b, wd, g)


def _rope_tables(n):
    pairs = ATT_HEAD_DIM // 4
    pos = np.arange(n)
    inv = np.power(np.float32(ROPE_THETA), -np.arange(pairs, dtype=np.float32) / np.float32(pairs))
    ang = np.concatenate([(pos // GRID_W).astype(np.float32)[:, None] * inv,
                          (pos % GRID_W).astype(np.float32)[:, None] * inv], axis=-1)
    cos = np.repeat(np.cos(ang), 2, axis=-1)
    sin = np.repeat(np.sin(ang), 2, axis=-1) * np.tile(np.array([-1.0, 1.0], np.float32), ATT_HEAD_DIM // 2)
    return jnp.asarray(np.tile(cos, (1, 2)), F32), jnp.asarray(np.tile(sin, (1, 2)), F32)


def _segment_ones():
    i = jnp.arange(LANES) // ATT_HEAD_DIM
    return jnp.tile((i[:, None] == i[None, :]).astype(BF16), (2, 1))


def _layer(x, mem, lb, pre_mix_g, w_in, q_norm_g, k_norm_g, hg_out_norm_g, w_out, post_mix_g, pre_x_g,
           mem_norm_g, w_xq, w_xkv, w_xo, post_x_g, pre_ffn_g, w_up, conv_w, conv_b, w_down, post_ffn_g):
    b, n, d = x.shape
    d_ff = w_down.shape[0]
    assert n % HG_CHUNK == 0 and n % GRID_W == 0 and d_ff % FF_CHUNK == 0
    tm = min(256, n)
    tf = min(512, n)
    row = lambda g: g.reshape(1, -1).astype(F32)

    cos, sin = _rope_tables(n)
    qg = jnp.tile(q_norm_g, ATT_HEADS).reshape(1, -1) * (ATT_HEAD_DIM ** -0.5 * LOG2E)
    kg = jnp.tile(k_norm_g, ATT_KV_HEADS).reshape(1, -1)

    k_mem, v_mem = _mem_proj(mem, row(mem_norm_g), w_xkv)
    q_t, k, v_t, hq, lf_f, lf_b, hi, sg = _in_proj(
        x, row(pre_mix_g), w_in, qg, kg, cos, sin, lb,
        _segment_ones(), tf)
    att = _attention(q_t, k, v_t, tm)
    o_f, o_b, (w_up_bf16, w_down_bf16) = _hgrn(hq, lf_f, lf_b, hi, [w_up, w_down])
    x2, h3 = _mix_cross(x, att, o_f, o_b, sg, row(hg_out_norm_g), w_out, row(post_mix_g),
                        row(pre_x_g), w_xq, k_mem, v_mem, w_xo,
                        row(post_x_g), row(pre_ffn_g), tf)

    return _conv_ffn(x2, h3, w_up_bf16, conv_w, row(conv_b), w_down_bf16, row(post_ffn_g), tf)


def kernel(x, mem, pre_mix_g, w_in, q_norm_g, k_norm_g, hg_lb, hg_out_norm_g, w_out, post_mix_g, pre_x_g,
           mem_norm_g, w_xq, w_xkv, w_xo, post_x_g, pre_ffn_g, w_up, conv_w, conv_b, w_down, post_ffn_g):
    lb_all = jnp.cumsum(jax.nn.softmax(hg_lb.astype(F32), axis=1), axis=1)
    for l in range(w_in.shape[0]):
        x = _layer(x, mem, lb_all[:, l], pre_mix_g[l], w_in[l], q_norm_g[l], k_norm_g[l], hg_out_norm_g[l],
                   w_out[l], post_mix_g[l], pre_x_g[l], mem_norm_g[l], w_xq[l], w_xkv[l], w_xo[l],
                   post_x_g[l], pre_ffn_g[l], w_up[l], conv_w[l], conv_b[l], w_down[l], post_ffn_g[l])
    return x
```

```python
import functools
import itertools
import math

import numpy as np
import jax
import jax.numpy as jnp
from jax import lax
from jax.experimental import pallas as pl
from jax.experimental.pallas import tpu as pltpu

F32 = jnp.float32
BF16 = jnp.bfloat16

EPS = 1e-6
LOG2E = math.log2(math.e)
GRID_W = 64
ROPE_THETA = 10000.0

ATT_HEADS = 8
ATT_KV_HEADS = 2
ATT_GROUP = ATT_HEADS // ATT_KV_HEADS
ATT_HEAD_DIM = 64
ATT_Q_DIM = ATT_HEADS * ATT_HEAD_DIM
ATT_KV_DIM = ATT_KV_HEADS * ATT_HEAD_DIM
ATT_KEY_SPLITS = 2

HG_HEADS = 4
HG_DIM = 128
HG_WIDTH = HG_HEADS * HG_DIM
HG_CHUNK = 128
HG_STEP_CHUNKS = 8
HG_UNROLL = 4
HG_FAST_BLOCK = 64
HG_FAST_MAX_LOG2 = 96.0

X_HEADS = 4
IN_ROW_PARTS = 2
MIX_ROW_PARTS = 2
CONV_W = 3
FF_CHUNK = 256

VMEM_LIMIT = 56 * 1024 * 1024
SUBLANES = 8
BF16_SUBLANES = 16
LANES = 128

HALO = BF16_SUBLANES


def _params(*sem):
    return pltpu.CompilerParams(dimension_semantics=sem, vmem_limit_bytes=VMEM_LIMIT)


def _rms(x, g):
    ms = jnp.mean(x * x, axis=-1, keepdims=True)
    return x * lax.rsqrt(ms + EPS) * g


def _silu(x):
    return x / (1.0 + jnp.exp(-x))


def _dot(a, b):
    return jnp.dot(a, b, preferred_element_type=F32)


def _dot_nt(a, b):
    return lax.dot_general(a, b, (((1,), (1,)), ((), ())), preferred_element_type=F32)


def _dot_tn(a, b):
    return lax.dot_general(a, b, (((0,), (0,)), ((), ())), preferred_element_type=F32)


def _const_spec(shape):
    return pl.BlockSpec(shape, lambda *_: (0,) * len(shape))


def _resident_spec(shape):
    return pl.BlockSpec(shape, lambda *_: (0,) * len(shape), pipeline_mode=pl.Buffered(1))


def _mem_proj_kernel(m_ref, g_ref, w_ref, k_ref, v_ref):
    d = m_ref.shape[-1]
    m = _rms(m_ref[0], g_ref[...]).astype(BF16)
    kv = _dot(m, w_ref[...].astype(BF16))
    k_ref[0] = kv[:, :d].astype(BF16)
    v_ref[0] = kv[:, d:].astype(BF16)


def _mem_proj(mem, g, w_xkv):
    b, nm, d = mem.shape
    out = jax.ShapeDtypeStruct((b, nm, d), BF16)
    return pl.pallas_call(
        _mem_proj_kernel,
        grid=(b,),
        in_specs=[pl.BlockSpec((1, nm, d), lambda i: (i, 0, 0)),
                  _const_spec((1, d)), _resident_spec((d, 2 * d))],
        out_specs=[pl.BlockSpec((1, nm, d), lambda i: (i, 0, 0))] * 2,
        out_shape=[out, out],
        compiler_params=_params("parallel"),
        name="mem_proj",
    )(mem, g, w_xkv)


def _head_norm_rope(a, seg, gain, cos, sin_signed):
    w = a.shape[-1]
    sq = a * a
    hi = sq.astype(BF16)
    lo = (sq - hi.astype(F32)).astype(BF16)
    ss = [_dot(jnp.concatenate([hi[:, j:j + LANES], lo[:, j:j + LANES]], axis=-1), seg) for j in range(0, w, LANES)]
    ss = jnp.concatenate(ss, axis=-1) if len(ss) > 1 else ss[0]
    an = a * lax.rsqrt(ss * (1.0 / ATT_HEAD_DIM) + EPS) * gain
    lane = lax.broadcasted_iota(jnp.int32, a.shape, 1)
    partner = jnp.where(lane % 2 == 0, pltpu.roll(an, w - 1, 1), pltpu.roll(an, 1, 1))
    return an * cos + partner * sin_signed


def _in_proj_kernel(x_ref, g_ref, w_ref, qg_ref, kg_ref, cos_ref, sin_ref, lb_ref, seg_ref,
                    qt_ref, k_ref, vt_ref, hq_ref, lff_ref, lfb_ref, hi_ref, hg_ref):
    part = x_ref.shape[1] // IN_ROW_PARTS

    def rows_of(r0):
        rows = slice(r0, r0 + part)
        h = _rms(x_ref[0, rows, :], g_ref[...]).astype(BF16)

        def proj(lo, width):
            return _dot(h, w_ref[:, lo:lo + width].astype(BF16))

        def heads_out(ref, val):
            for i in range(HG_HEADS):
                ref[0, i, rows, :] = val[:, i * HG_DIM:(i + 1) * HG_DIM].astype(ref.dtype)

        cos = cos_ref[rows, :]
        sin = sin_ref[rows, :]
        c0 = 0
        q = _head_norm_rope(proj(c0, ATT_Q_DIM), seg_ref[...], qg_ref[...],
                            jnp.concatenate([cos] * (ATT_Q_DIM // LANES), axis=-1),
                            jnp.concatenate([sin] * (ATT_Q_DIM // LANES), axis=-1))
        q_t = jnp.transpose(q)
        for i in range(ATT_HEADS):
            qt_ref[0, i, :, rows] = q_t[i * ATT_HEAD_DIM:(i + 1) * ATT_HEAD_DIM, :].astype(BF16)
        c0 += ATT_Q_DIM
        yield
        k = _head_norm_rope(proj(c0, ATT_KV_DIM), seg_ref[...], kg_ref[...], cos, sin)
        c0 += ATT_KV_DIM
        v_t = jnp.transpose(proj(c0, ATT_KV_DIM))
        c0 += ATT_KV_DIM
        for i in range(ATT_KV_HEADS):
            k_ref[0, i, rows, :] = k[:, i * ATT_HEAD_DIM:(i + 1) * ATT_HEAD_DIM].astype(BF16)
            vt_ref[0, i, :, rows] = v_t[i * ATT_HEAD_DIM:(i + 1) * ATT_HEAD_DIM, :].astype(BF16)
        yield
        heads_out(hq_ref, _silu(proj(c0, HG_WIDTH)))
        c0 += HG_WIDTH
        yield
        for d, ref in enumerate((lff_ref, lfb_ref)):
            lb = lb_ref[d:d + 1, :]
            z = proj(c0, HG_WIDTH)
            heads_out(ref, jnp.log(lb + (1.0 - lb) / (1.0 + jnp.exp(-z))) * LOG2E)
            c0 += HG_WIDTH
            yield
        heads_out(hi_ref, proj(c0, HG_WIDTH))
        c0 += HG_WIDTH
        yield
        hg_ref[0, rows, :] = _silu(proj(c0, HG_WIDTH))

    for _ in itertools.zip_longest(*[rows_of(j * part) for j in range(IN_ROW_PARTS)]):
        pass


def _in_proj(x, g, w_in, qg, kg, cos, sin, lb, seg, tm):
    b, n, d = x.shape
    nt = n // tm
    n_in = w_in.shape[1]
    tok = lambda i, j: (i, j, 0)
    head = lambda i, j: (i, 0, j, 0)
    pos = lambda i, j: (j, 0)
    return pl.pallas_call(
        _in_proj_kernel,
        grid=(b, nt),
        in_specs=[pl.BlockSpec((1, tm, d), tok), _const_spec((1, d)), _resident_spec((d, n_in)),
                  _const_spec((1, ATT_Q_DIM)), _const_spec((1, ATT_KV_DIM)),
                  pl.BlockSpec((tm, LANES), pos), pl.BlockSpec((tm, LANES), pos),
                  _const_spec((2, HG_WIDTH)),
                  _const_spec((2 * LANES, LANES))],
        out_specs=[pl.BlockSpec((1, ATT_HEADS, ATT_HEAD_DIM, tm), lambda i, j: (i, 0, 0, j)),
                   pl.BlockSpec((1, ATT_KV_HEADS, tm, ATT_HEAD_DIM), head),
                   pl.BlockSpec((1, ATT_KV_HEADS, ATT_HEAD_DIM, tm), lambda i, j: (i, 0, 0, j)),
                   pl.BlockSpec((1, HG_HEADS, tm, HG_DIM), head),
                   pl.BlockSpec((1, HG_HEADS, tm, HG_DIM), head),
                   pl.BlockSpec((1, HG_HEADS, tm, HG_DIM), head),
                   pl.BlockSpec((1, HG_HEADS, tm, HG_DIM), head),
                   pl.BlockSpec((1, tm, HG_WIDTH), tok)],
        out_shape=[jax.ShapeDtypeStruct((b, ATT_HEADS, ATT_HEAD_DIM, n), BF16),
                   jax.ShapeDtypeStruct((b, ATT_KV_HEADS, n, ATT_HEAD_DIM), BF16),
                   jax.ShapeDtypeStruct((b, ATT_KV_HEADS, ATT_HEAD_DIM, n), BF16),
                   jax.ShapeDtypeStruct((b, HG_HEADS, n, HG_DIM), BF16),
                   jax.ShapeDtypeStruct((b, HG_HEADS, n, HG_DIM), F32),
                   jax.ShapeDtypeStruct((b, HG_HEADS, n, HG_DIM), F32),
                   jax.ShapeDtypeStruct((b, HG_HEADS, n, HG_DIM), BF16),
                   jax.ShapeDtypeStruct((b, n, HG_WIDTH), F32)],
        compiler_params=_params("parallel", "parallel"),
        name="in_proj",
    )(x, g, w_in, qg, kg, cos, sin, lb, seg)


def _attention_kernel(qt_ref, qtn_ref, k_ref, vt_ref, o_ref, s_even, s_odd):
    n = k_ref.shape[2]
    part = n // ATT_KEY_SPLITS
    parts = [slice(j * part, (j + 1) * part) for j in range(ATT_KEY_SPLITS)]

    def scores(i, s_scr, queries_t=qt_ref):
        for rows in parts:
            s_scr[rows, :] = _dot(k_ref[0, i // ATT_GROUP, rows, :], queries_t[0, i])

    def finish(i, s_scr):
        s = [s_scr[rows, :] for rows in parts]
        m = functools.reduce(jnp.maximum, [jnp.max(x, axis=0, keepdims=True) for x in s])
        p = [jnp.exp2(x - m) for x in s]
        l = sum(jnp.sum(x, axis=0, keepdims=True) for x in p)
        o_t = sum(_dot(vt_ref[0, i // ATT_GROUP, :, rows], x.astype(BF16)) for rows, x in zip(parts, p))
        o_ref[0, i] = (o_t / l).astype(BF16)

    @pl.when(pl.program_id(1) == 0)
    def _():
        scores(0, s_even)

    def pair(j, carry):
        scores(2 * j + 1, s_odd)
        finish(2 * j, s_even)
        scores(2 * j + 2, s_even)
        finish(2 * j + 1, s_odd)
        return carry

    lax.fori_loop(0, ATT_HEADS // 2 - 1, pair, 0)
    scores(ATT_HEADS - 1, s_odd)
    finish(ATT_HEADS - 2, s_even)
    scores(0, s_even, qtn_ref)
    finish(ATT_HEADS - 1, s_odd)


def _attention(q_t, k, v_t, tq):
    b, _, dh, n = q_t.shape
    last = n // tq - 1
    return pl.pallas_call(
        _attention_kernel,
        grid=(b, n // tq),
        in_specs=[pl.BlockSpec((1, ATT_HEADS, dh, tq), lambda i, j: (i, 0, 0, j)),
                  pl.BlockSpec((1, ATT_HEADS, dh, tq), lambda i, j: (i, 0, 0, jnp.minimum(j + 1, last))),
                  pl.BlockSpec((1, ATT_KV_HEADS, n, dh), lambda i, j: (i, 0, 0, 0)),
                  pl.BlockSpec((1, ATT_KV_HEADS, dh, n), lambda i, j: (i, 0, 0, 0))],
        out_specs=pl.BlockSpec((1, ATT_HEADS, dh, tq), lambda i, j: (i, 0, 0, j)),
        out_shape=jax.ShapeDtypeStruct((b, ATT_HEADS, dh, n), BF16),
        scratch_shapes=[pltpu.VMEM((n, tq), F32), pltpu.VMEM((n, tq), F32)],
        compiler_params=_params("parallel", "arbitrary"),
        name="attention",
    )(q_t, q_t, k, v_t)


def _hgrn_levels():
    m, out = 1, []
    while m < HG_CHUNK:
        out.append(m)
        m *= 2
    return out


def _hgrn_tables():
    c = HG_CHUNK
    t = np.arange(c)[:, None]
    u = np.arange(c)[None, :]
    mats, masks, fast = [], [], []
    for reverse in (False, True):
        blocks, pairs = [], [t == u]
        for m in _hgrn_levels():
            base = (t // (2 * m)) * (2 * m)
            if not reverse:
                r = base + m - 1
                block = np.where(t > r, (u > r) & (u <= t), (u > t) & (u <= r))
                roles = ((t // m) % 2 == 1) & ((u // m) % 2 == 0)
            else:
                r = base + m
                block = np.where(t < r, (u >= t) & (u < r), (u >= r) & (u < t))
                roles = ((t // m) % 2 == 0) & ((u // m) % 2 == 1)
            if m < SUBLANES:
                blocks.append(block)
            pairs.append(roles & (t // (2 * m) == u // (2 * m)))
        blocks.append(u >= t if reverse else u <= t)
        mats.append(np.tile(np.concatenate(blocks, axis=0), (1, 2)))
        masks.append(np.stack(pairs))
        fast.append((t // HG_FAST_BLOCK == u // HG_FAST_BLOCK) & (u >= t if reverse else u <= t))
    return tuple(np.stack(x).astype(np.float32) for x in (mats, masks, fast))


def _hgrn_sums(lf2, sum_mat):
    hi = lf2.astype(BF16)
    mid = (lf2 - hi.astype(F32)).astype(BF16)
    return _dot(sum_mat, jnp.concatenate([hi, mid], axis=0))


def _hgrn_chunk(q_ref, lf_ref, v_ref, o_ref, state_ref, sum_mat, pair_mask, fast_mask, b_scr, reverse, fast):
    c = HG_CHUNK
    levels = list(enumerate(_hgrn_levels()))
    n_small = sum(m < SUBLANES for _, m in levels)
    q, lf2, v, state = q_ref[...], lf_ref[...], v_ref[...], state_ref[...]
    qb = q.astype(BF16)
    kb = (1.0 - jnp.exp2(lf2)).astype(BF16)
    if fast:
        b = _hgrn_sums(lf2, sum_mat[n_small * c:, :])
        yield
        b_scr[...] = b
        parts = []
        for g in range(c // HG_FAST_BLOCK):
            lo, up = g * HG_FAST_BLOCK, (g + 1) * HG_FAST_BLOCK
            before = up if reverse else lo - 1
            parts.append(b[lo:up] - b_scr[before:before + 1, :] if 0 <= before < c else b[lo:up])
        loc = jnp.concatenate(parts, axis=0)
        a = _dot_nt(qb * jnp.exp2(loc).astype(BF16), kb * jnp.exp2(-loc).astype(BF16)) * fast_mask[...]
        yield
        levels = [(j, m) for j, m in levels if m >= HG_FAST_BLOCK]
    else:
        sums = _hgrn_sums(lf2, sum_mat[...])
        yield
        b = sums[n_small * c:]
        b_scr[...] = b
        a = _dot_nt(qb, kb) * pair_mask[0]
        yield
    for j, m in levels:
        if m < SUBLANES:
            e = sums[j * c:(j + 1) * c]
        else:
            parts = []
            for g in range(c // (2 * m)):
                lo, mid_row, up = 2 * m * g, 2 * m * g + m, 2 * m * (g + 1)
                r = mid_row if reverse else mid_row - 1
                b_r = b_scr[r:r + 1, :]
                parts += [b[lo:mid_row] - b_r, b_r - b[mid_row:up]] if reverse else \
                         [b_r - b[lo:mid_row], b[mid_row:up] - b_r]
            e = jnp.concatenate(parts, axis=0)
        fac = jnp.exp2(e).astype(BF16)
        a = a + _dot_nt(qb * fac, kb * fac) * pair_mask[j + 1]
        yield
    vb = v.astype(BF16)
    last = 0 if reverse else c - 1
    total = b_scr[last:last + 1, :]
    o_ref[...] = _dot(a.astype(BF16), vb) + _dot_nt(qb * jnp.exp2(b).astype(BF16), state.astype(BF16))
    yield
    k_dec = kb * jnp.exp2(total - b).astype(BF16)
    state_ref[...] = state * jnp.exp2(total) + _dot_tn(vb, k_dec)


def _hgrn_kernel(qf_ref, lff_ref, vf_ref, qb_ref, lfb_ref, vb_ref, sm_ref, pm_ref, fm_ref, *rest):
    n_cast = (len(rest) - 4) // 2
    cast_in, (of_ref, ob_ref), cast_out = rest[:n_cast], rest[n_cast:n_cast + 2], rest[n_cast + 2:2 * n_cast + 2]
    s_ref, b_scr = rest[2 * n_cast + 2:]

    @pl.when(pl.program_id(1) == 0)
    def _():
        s_ref[...] = jnp.zeros_like(s_ref)

    for src, dst in zip(cast_in, cast_out):
        dst[...] = src[...].astype(dst.dtype)

    c = HG_CHUNK
    rows_per_step = qf_ref.shape[2]
    chains = [(d, h) for h in range(HG_HEADS) for d in (0, 1)]
    q_refs, lf_refs, v_refs, o_refs = (qf_ref, qb_ref), (lff_ref, lfb_ref), (vf_ref, vb_ref), (of_ref, ob_ref)

    worst = jnp.zeros((1, HG_DIM), F32)
    for d, h in chains:
        for g in range(rows_per_step // HG_FAST_BLOCK):
            block = lf_refs[d][0, h, g * HG_FAST_BLOCK:(g + 1) * HG_FAST_BLOCK, :]
            worst = jnp.minimum(worst, jnp.sum(block, axis=0, keepdims=True))
    mild = jnp.min(worst) >= -HG_FAST_MAX_LOG2

    def run(fast):
        n_sub = rows_per_step // c
        unroll = min(HG_UNROLL, n_sub) if fast else 1

        def trip(t, carry):
            for u in range(unroll):
                sub = t * unroll + u
                work = []
                for d, h in chains:
                    k = n_sub - 1 - sub if d else sub
                    rows = pl.ds(pl.multiple_of(k * c, c), c)
                    work.append(_hgrn_chunk(q_refs[d].at[0, h, rows, :], lf_refs[d].at[0, h, rows, :],
                                            v_refs[d].at[0, h, rows, :], o_refs[d].at[0, h, rows, :],
                                            s_ref.at[d, h], sm_ref.at[d], pm_ref.at[d], fm_ref.at[d],
                                            b_scr.at[d, h], bool(d), fast))
                for _ in itertools.zip_longest(*work):
                    pass
            return carry

        lax.fori_loop(0, n_sub // unroll, trip, 0)

    pl.when(mild)(lambda: run(True))
    pl.when(jnp.logical_not(mild))(lambda: run(False))


def _hgrn(hq, lf_f, lf_b, hi, to_cast):
    b, nh, n, dk = hq.shape
    rows_per_step = HG_CHUNK * HG_STEP_CHUNKS if n % (HG_CHUNK * HG_STEP_CHUNKS) == 0 else HG_CHUNK
    nc = n // rows_per_step
    blk = (1, nh, rows_per_step, dk)
    fwd = pl.BlockSpec(blk, lambda i, j: (i, 0, j, 0))
    bwd = pl.BlockSpec(blk, lambda i, j: (i, 0, nc - 1 - j, 0))
    out = jax.ShapeDtypeStruct((b, nh, n, dk), F32)
    tables = _hgrn_tables()
    steps = b * nc

    def slab_spec(w):
        hold = next(h for h in range(1, steps + 1)
                    if steps % h == 0 and w.shape[0] % (steps // h * BF16_SUBLANES) == 0)
        return pl.BlockSpec((w.shape[0] // (steps // hold), w.shape[1]), lambda i, j: ((i * nc + j) // hold, 0))

    slabs = [slab_spec(w) for w in to_cast]
    res = pl.pallas_call(
        _hgrn_kernel,
        grid=(b, nc),
        in_specs=[fwd, fwd, fwd, bwd, bwd, bwd] + [_const_spec(t.shape) for t in tables] + slabs,
        out_specs=[fwd, bwd] + slabs,
        out_shape=[out, out] + [jax.ShapeDtypeStruct(w.shape, BF16) for w in to_cast],
        scratch_shapes=[pltpu.VMEM((2, nh, dk, dk), F32), pltpu.VMEM((2, nh, HG_CHUNK, dk), F32)],
        compiler_params=_params("parallel", "arbitrary"),
        name="hgrn",
    )(hq, lf_f, hi, hq, lf_b, hi, jnp.asarray(tables[0], BF16), jnp.asarray(tables[1]), jnp.asarray(tables[2]),
      *to_cast)
    return res[0], res[1], res[2:]


def _mix_cross_kernel(x_ref, att_ref, of_ref, ob_ref, sg_ref, og_ref, wo_ref, pmg_ref, pxg_ref,
                      wq_ref, km_ref, vm_ref, wxo_ref, poxg_ref, pfg_ref, x2_ref, h3_ref):
    tm = x_ref.shape[1]
    part = tm // MIX_ROW_PARTS

    def rows_of(lo):
        rows = slice(lo, lo + part)
        rec = []
        for i in range(HG_HEADS):
            o = _rms(of_ref[0, i, rows, :] + ob_ref[0, i, rows, :], og_ref[...])
            rec.append((o * sg_ref[0, rows, i * HG_DIM:(i + 1) * HG_DIM]).astype(BF16))
        att_t = att_ref[0, :, :, rows].reshape(ATT_Q_DIM, part)
        mixed = _dot_tn(att_t, wo_ref[:ATT_Q_DIM, :].astype(BF16)) + \
            _dot(jnp.concatenate(rec, axis=-1), wo_ref[ATT_Q_DIM:, :].astype(BF16))
        yield
        x1 = x_ref[0, rows, :] + _rms(mixed, pmg_ref[...])
        h2 = _rms(x1, pxg_ref[...]).astype(BF16)
        d = h2.shape[-1]
        dh = d // X_HEADS
        q = _dot(h2, wq_ref[...].astype(BF16)) * (dh ** -0.5)
        yield
        heads = []
        for i in range(X_HEADS):
            sl = slice(i * dh, (i + 1) * dh)
            s = _dot_nt(q[:, sl].astype(BF16), km_ref[0, :, sl])
            p = jnp.exp(s - jnp.max(s, axis=-1, keepdims=True))
            l = jnp.sum(p, axis=-1, keepdims=True)
            heads.append((_dot(p.astype(BF16), vm_ref[0, :, sl]) / l).astype(BF16))
        yield
        xo = _dot(jnp.concatenate(heads, axis=-1), wxo_ref[...].astype(BF16))
        yield
        x2 = x1 + _rms(xo, poxg_ref[...])
        x2_ref[0, rows, :] = x2
        h3_ref[0, rows, :] = _rms(x2, pfg_ref[...]).astype(BF16)

    for _ in itertools.zip_longest(*[rows_of(j * part) for j in range(MIX_ROW_PARTS)]):
        pass


def _mix_cross(x, att, o_f, o_b, sg, og, w_out, pmg, pxg, w_xq, k_mem, v_mem, w_xo, poxg, pfg, tm):
    b, n, d = x.shape
    nm = k_mem.shape[1]
    tok = lambda i, j: (i, j, 0)
    head = lambda i, j: (i, 0, j, 0)
    batch = lambda i, j: (i, 0, 0)
    vec = _const_spec((1, d))
    mat = _resident_spec((d, d))
    return pl.pallas_call(
        _mix_cross_kernel,
        grid=(b, n // tm),
        in_specs=[pl.BlockSpec((1, tm, d), tok),
                  pl.BlockSpec((1, ATT_HEADS, ATT_HEAD_DIM, tm), lambda i, j: (i, 0, 0, j)),
                  pl.BlockSpec((1, HG_HEADS, tm, HG_DIM), head), pl.BlockSpec((1, HG_HEADS, tm, HG_DIM), head),
                  pl.BlockSpec((1, tm, HG_WIDTH), tok), _const_spec((1, HG_DIM)),
                  mat, vec, vec, mat,
                  pl.BlockSpec((1, nm, d), batch), pl.BlockSpec((1, nm, d), batch),
                  mat, vec, vec],
        out_specs=[pl.BlockSpec((1, tm, d), tok), pl.BlockSpec((1, tm, d), tok)],
        out_shape=[jax.ShapeDtypeStruct((b, n, d), F32), jax.ShapeDtypeStruct((b, n, d), BF16)],
        compiler_params=_params("parallel", "parallel"),
        name="mix_cross",
    )(x, att, o_f, o_b, sg, og, w_out, pmg, pxg, w_xq, k_mem, v_mem, w_xo, poxg, pfg)


def _conv_ffn_kernel(x_ref, h_ref, hp_ref, hn_ref, wu_ref, cw_ref, cb_ref, wd_ref, g_ref, o_ref, act_scr):
    j = pl.program_id(1)
    tm = h_ref.shape[1]
    d_ff = wd_ref.shape[0]
    ck = FF_CHUNK
    prev = jnp.where(j > 0, hp_ref[0], jnp.zeros_like(hp_ref[0]))
    nxt = jnp.where(j < pl.num_programs(1) - 1, hn_ref[0], jnp.zeros_like(hn_ref[0]))
    hext = jnp.concatenate([prev, h_ref[0], nxt], axis=0)

    def conv(col):
        u = _dot(hext, wu_ref[:, col:col + ck])
        taps = (pltpu.roll(u, 1, 0), u, pltpu.roll(u, u.shape[0] - 1, 0))
        out = cb_ref[:, col:col + ck]
        for t in range(CONV_W):
            out = out + taps[t][HALO:HALO + tm] * cw_ref[t:t + 1, col:col + ck]
        return out

    for c in range(d_ff // ck):
        act_scr[:, c * ck:(c + 1) * ck] = (_silu(conv(c * ck)) * conv(d_ff + c * ck)).astype(BF16)
    o_ref[0] = x_ref[0] + _rms(_dot(act_scr[...], wd_ref[...]), g_ref[...])


def _conv_ffn(x2, h3, w_up, conv_w, conv_b, wd, g, tm):
    b, n, d = x2.shape
    d_ff = wd.shape[0]
    ck = FF_CHUNK
    hb = tm // HALO
    last = n // HALO - 1
    tok = lambda i, j: (i, j, 0)
    full = _resident_spec
    return pl.pallas_call(
        _conv_ffn_kernel,
        grid=(b, n // tm),
        in_specs=[pl.BlockSpec((1, tm, d), tok), pl.BlockSpec((1, tm, d), tok),
                  pl.BlockSpec((1, HALO, d), lambda i, j: (i, jnp.maximum(j * hb - 1, 0), 0)),
                  pl.BlockSpec((1, HALO, d), lambda i, j: (i, jnp.minimum((j + 1) * hb, last), 0)),
                  full((d, 2 * d_ff)), full((CONV_W, 2 * d_ff)), full((1, 2 * d_ff)),
                  full((d_ff, d)), _const_spec((1, d))],
        out_specs=pl.BlockSpec((1, tm, d), tok),
        out_shape=jax.ShapeDtypeStruct((b, n, d), F32),
        scratch_shapes=[pltpu.VMEM((tm, d_ff), BF16)],
        compiler_params=_params("parallel", "parallel"),
        name="conv_ffn",
    )(x2, h3, h3, h3, w_up, conv_w, conv_b, wd, g)


def _rope_tables(n):
    pairs = ATT_HEAD_DIM // 4
    pos = np.arange(n)
    inv = np.power(np.float32(ROPE_THETA), -np.arange(pairs, dtype=np.float32) / np.float32(pairs))
    ang = np.concatenate([(pos // GRID_W).astype(np.float32)[:, None] * inv,
                          (pos % GRID_W).astype(np.float32)[:, None] * inv], axis=-1)
    cos = np.repeat(np.cos(ang), 2, axis=-1)
    sin = np.repeat(np.sin(ang), 2, axis=-1) * np.tile(np.array([-1.0, 1.0], np.float32), ATT_HEAD_DIM // 2)
    return jnp.asarray(np.tile(cos, (1, 2)), F32), jnp.asarray(np.tile(sin, (1, 2)), F32)


def _segment_ones():
    i = jnp.arange(LANES) // ATT_HEAD_DIM
    return jnp.tile((i[:, None] == i[None, :]).astype(BF16), (2, 1))


def _layer(x, mem, lb, pre_mix_g, w_in, q_norm_g, k_norm_g, hg_out_norm_g, w_out, post_mix_g, pre_x_g,
           mem_norm_g, w_xq, w_xkv, w_xo, post_x_g, pre_ffn_g, w_up, conv_w, conv_b, w_down, post_ffn_g):
    b, n, d = x.shape
    d_ff = w_down.shape[0]
    assert n % HG_CHUNK == 0 and n % GRID_W == 0 and d_ff % FF_CHUNK == 0
    tm = min(256, n)
    tf = min(512, n)
    row = lambda g: g.reshape(1, -1).astype(F32)

    cos, sin = _rope_tables(n)
    qg = jnp.tile(q_norm_g, ATT_HEADS).reshape(1, -1) * (ATT_HEAD_DIM ** -0.5 * LOG2E)
    kg = jnp.tile(k_norm_g, ATT_KV_HEADS).reshape(1, -1)

    k_mem, v_mem = _mem_proj(mem, row(mem_norm_g), w_xkv)
    q_t, k, v_t, hq, lf_f, lf_b, hi, sg = _in_proj(
        x, row(pre_mix_g), w_in, qg, kg, cos, sin, lb,
        _segment_ones(), tf)
    att = _attention(q_t, k, v_t, tm)
    o_f, o_b, (w_up_bf16, w_down_bf16) = _hgrn(hq, lf_f, lf_b, hi, [w_up, w_down])
    x2, h3 = _mix_cross(x, att, o_f, o_b, sg, row(hg_out_norm_g), w_out, row(post_mix_g),
                        row(pre_x_g), w_xq, k_mem, v_mem, w_xo,
                        row(post_x_g), row(pre_ffn_g), tf)

    return _conv_ffn(x2, h3, w_up_bf16, conv_w, row(conv_b), w_down_bf16, row(post_ffn_g), tf)


def kernel(x, mem, pre_mix_g, w_in, q_norm_g, k_norm_g, hg_lb, hg_out_norm_g, w_out, post_mix_g, pre_x_g,
           mem_norm_g, w_xq, w_xkv, w_xo, post_x_g, pre_ffn_g, w_up, conv_w, conv_b, w_down, post_ffn_g):
    lb_all = jnp.cumsum(jax.nn.softmax(hg_lb.astype(F32), axis=1), axis=1)
    for l in range(w_in.shape[0]):
        x = _layer(x, mem, lb_all[:, l], pre_mix_g[l], w_in[l], q_norm_g[l], k_norm_g[l], hg_out_norm_g[l],
                   w_out[l], post_mix_g[l], pre_x_g[l], mem_norm_g[l], w_xq[l], w_xkv[l], w_xo[l],
                   post_x_g[l], pre_ffn_g[l], w_up[l], conv_w[l], conv_b[l], w_down[l], post_ffn_g[l])
    return x
```

```python
import functools
import itertools
import math

import numpy as np
import jax
import jax.numpy as jnp
from jax import lax
from jax.experimental import pallas as pl
from jax.experimental.pallas import tpu as pltpu

F32 = jnp.float32
BF16 = jnp.bfloat16

EPS = 1e-6
LOG2E = math.log2(math.e)
GRID_W = 64
ROPE_THETA = 10000.0

ATT_HEADS = 8
ATT_KV_HEADS = 2
ATT_GROUP = ATT_HEADS // ATT_KV_HEADS
ATT_HEAD_DIM = 64
ATT_Q_DIM = ATT_HEADS * ATT_HEAD_DIM
ATT_KV_DIM = ATT_KV_HEADS * ATT_HEAD_DIM
ATT_KEY_SPLITS = 2

HG_HEADS = 4
HG_DIM = 128
HG_WIDTH = HG_HEADS * HG_DIM
HG_CHUNK = 128
HG_STEP_CHUNKS = 8
HG_UNROLL = 4
HG_FAST_BLOCK = 64
HG_FAST_MAX_LOG2 = 96.0

X_HEADS = 4
IN_ROW_PARTS = 2
MIX_ROW_PARTS = 2
CONV_W = 3
FF_CHUNK = 256

VMEM_LIMIT = 56 * 1024 * 1024
SUBLANES = 8
BF16_SUBLANES = 16
LANES = 128

HALO = BF16_SUBLANES


def _params(*sem):
    return pltpu.CompilerParams(dimension_semantics=sem, vmem_limit_bytes=VMEM_LIMIT)


def _rms(x, g):
    ms = jnp.mean(x * x, axis=-1, keepdims=True)
    return x * lax.rsqrt(ms + EPS) * g


def _silu(x):
    return x / (1.0 + jnp.exp(-x))


def _dot(a, b):
    return jnp.dot(a, b, preferred_element_type=F32)


def _dot_nt(a, b):
    return lax.dot_general(a, b, (((1,), (1,)), ((), ())), preferred_element_type=F32)


def _dot_tn(a, b):
    return lax.dot_general(a, b, (((0,), (0,)), ((), ())), preferred_element_type=F32)


def _const_spec(shape):
    return pl.BlockSpec(shape, lambda *_: (0,) * len(shape))


def _resident_spec(shape):
    return pl.BlockSpec(shape, lambda *_: (0,) * len(shape), pipeline_mode=pl.Buffered(1))


def _head_norm_rope(a, seg, gain, cos, sin_signed):
    w = a.shape[-1]
    sq = a * a
    hi = sq.astype(BF16)
    lo = (sq - hi.astype(F32)).astype(BF16)
    ss = [_dot(jnp.concatenate([hi[:, j:j + LANES], lo[:, j:j + LANES]], axis=-1), seg) for j in range(0, w, LANES)]
    ss = jnp.concatenate(ss, axis=-1) if len(ss) > 1 else ss[0]
    an = a * lax.rsqrt(ss * (1.0 / ATT_HEAD_DIM) + EPS) * gain
    lane = lax.broadcasted_iota(jnp.int32, a.shape, 1)
    partner = jnp.where(lane % 2 == 0, pltpu.roll(an, w - 1, 1), pltpu.roll(an, 1, 1))
    return an * cos + partner * sin_signed


def _in_proj_kernel(x_ref, g_ref, w_ref, qg_ref, kg_ref, cos_ref, sin_ref, lb_ref, seg_ref,
                    mem_ref, mg_ref, wkv_ref,
                    qt_ref, k_ref, vt_ref, hq_ref, lff_ref, lfb_ref, hi_ref, hg_ref, kv_ref):
    part = x_ref.shape[1] // IN_ROW_PARTS

    kv_ref[0] = _dot(_rms(mem_ref[0], mg_ref[...]).astype(BF16), wkv_ref[...].astype(BF16)).astype(BF16)

    def rows_of(r0):
        rows = slice(r0, r0 + part)
        h = _rms(x_ref[0, rows, :], g_ref[...]).astype(BF16)

        def proj(lo, width):
            return _dot(h, w_ref[:, lo:lo + width].astype(BF16))

        def heads_out(ref, val):
            for i in range(HG_HEADS):
                ref[0, i, rows, :] = val[:, i * HG_DIM:(i + 1) * HG_DIM].astype(ref.dtype)

        cos = cos_ref[rows, :]
        sin = sin_ref[rows, :]
        c0 = 0
        q = _head_norm_rope(proj(c0, ATT_Q_DIM), seg_ref[...], qg_ref[...],
                            jnp.concatenate([cos] * (ATT_Q_DIM // LANES), axis=-1),
                            jnp.concatenate([sin] * (ATT_Q_DIM // LANES), axis=-1))
        q_t = jnp.transpose(q)
        for i in range(ATT_HEADS):
            qt_ref[0, i, :, rows] = q_t[i * ATT_HEAD_DIM:(i + 1) * ATT_HEAD_DIM, :].astype(BF16)
        c0 += ATT_Q_DIM
        yield
        k = _head_norm_rope(proj(c0, ATT_KV_DIM), seg_ref[...], kg_ref[...], cos, sin)
        c0 += ATT_KV_DIM
        v_t = jnp.transpose(proj(c0, ATT_KV_DIM))
        c0 += ATT_KV_DIM
        for i in range(ATT_KV_HEADS):
            k_ref[0, i, rows, :] = k[:, i * ATT_HEAD_DIM:(i + 1) * ATT_HEAD_DIM].astype(BF16)
            vt_ref[0, i, :, rows] = v_t[i * ATT_HEAD_DIM:(i + 1) * ATT_HEAD_DIM, :].astype(BF16)
        yield
        heads_out(hq_ref, _silu(proj(c0, HG_WIDTH)))
        c0 += HG_WIDTH
        yield
        for d, ref in enumerate((lff_ref, lfb_ref)):
            lb = lb_ref[d:d + 1, :]
            z = proj(c0, HG_WIDTH)
            heads_out(ref, jnp.log(lb + (1.0 - lb) / (1.0 + jnp.exp(-z))) * LOG2E)
            c0 += HG_WIDTH
            yield
        heads_out(hi_ref, proj(c0, HG_WIDTH))
        c0 += HG_WIDTH
        yield
        hg_ref[0, rows, :] = _silu(proj(c0, HG_WIDTH))

    for _ in itertools.zip_longest(*[rows_of(j * part) for j in range(IN_ROW_PARTS)]):
        pass


def _in_proj(x, g, w_in, qg, kg, cos, sin, lb, seg, mem, mem_g, w_xkv, tm):
    b, n, d = x.shape
    nt = n // tm
    n_in = w_in.shape[1]
    nm = mem.shape[1]
    kv_cols = w_xkv.shape[1] // nt
    assert w_xkv.shape[1] % nt == 0 and kv_cols % LANES == 0
    tok = lambda i, j: (i, j, 0)
    head = lambda i, j: (i, 0, j, 0)
    pos = lambda i, j: (j, 0)
    return pl.pallas_call(
        _in_proj_kernel,
        grid=(b, nt),
        in_specs=[pl.BlockSpec((1, tm, d), tok), _const_spec((1, d)), _resident_spec((d, n_in)),
                  _const_spec((1, ATT_Q_DIM)), _const_spec((1, ATT_KV_DIM)),
                  pl.BlockSpec((tm, LANES), pos), pl.BlockSpec((tm, LANES), pos),
                  _const_spec((2, HG_WIDTH)),
                  _const_spec((2 * LANES, LANES)),
                  pl.BlockSpec((1, nm, d), lambda i, j: (i, 0, 0)), _const_spec((1, d)),
                  pl.BlockSpec((d, kv_cols), lambda i, j: (0, j))],
        out_specs=[pl.BlockSpec((1, ATT_HEADS, ATT_HEAD_DIM, tm), lambda i, j: (i, 0, 0, j)),
                   pl.BlockSpec((1, ATT_KV_HEADS, tm, ATT_HEAD_DIM), head),
                   pl.BlockSpec((1, ATT_KV_HEADS, ATT_HEAD_DIM, tm), lambda i, j: (i, 0, 0, j)),
                   pl.BlockSpec((1, HG_HEADS, tm, HG_DIM), head),
                   pl.BlockSpec((1, HG_HEADS, tm, HG_DIM), head),
                   pl.BlockSpec((1, HG_HEADS, tm, HG_DIM), head),
                   pl.BlockSpec((1, HG_HEADS, tm, HG_DIM), head),
                   pl.BlockSpec((1, tm, HG_WIDTH), tok),
                   pl.BlockSpec((1, nm, kv_cols), lambda i, j: (i, 0, j))],
        out_shape=[jax.ShapeDtypeStruct((b, ATT_HEADS, ATT_HEAD_DIM, n), BF16),
                   jax.ShapeDtypeStruct((b, ATT_KV_HEADS, n, ATT_HEAD_DIM), BF16),
                   jax.ShapeDtypeStruct((b, ATT_KV_HEADS, ATT_HEAD_DIM, n), BF16),
                   jax.ShapeDtypeStruct((b, HG_HEADS, n, HG_DIM), BF16),
                   jax.ShapeDtypeStruct((b, HG_HEADS, n, HG_DIM), F32),
                   jax.ShapeDtypeStruct((b, HG_HEADS, n, HG_DIM), F32),
                   jax.ShapeDtypeStruct((b, HG_HEADS, n, HG_DIM), BF16),
                   jax.ShapeDtypeStruct((b, n, HG_WIDTH), F32),
                   jax.ShapeDtypeStruct((b, nm, w_xkv.shape[1]), BF16)],
        compiler_params=_params("parallel", "parallel"),
        name="in_proj",
    )(x, g, w_in, qg, kg, cos, sin, lb, seg, mem, mem_g, w_xkv)


def _attention_kernel(qt_ref, qtn_ref, k_ref, vt_ref, o_ref, s_even, s_odd):
    n = k_ref.shape[2]
    part = n // ATT_KEY_SPLITS
    parts = [slice(j * part, (j + 1) * part) for j in range(ATT_KEY_SPLITS)]

    def scores(i, s_scr, queries_t=qt_ref):
        for rows in parts:
            s_scr[rows, :] = _dot(k_ref[0, i // ATT_GROUP, rows, :], queries_t[0, i])

    def finish(i, s_scr):
        s = [s_scr[rows, :] for rows in parts]
        m = functools.reduce(jnp.maximum, [jnp.max(x, axis=0, keepdims=True) for x in s])
        p = [jnp.exp2(x - m) for x in s]
        l = sum(jnp.sum(x, axis=0, keepdims=True) for x in p)
        o_t = sum(_dot(vt_ref[0, i // ATT_GROUP, :, rows], x.astype(BF16)) for rows, x in zip(parts, p))
        o_ref[0, i] = (o_t / l).astype(BF16)

    @pl.when(pl.program_id(1) == 0)
    def _():
        scores(0, s_even)

    def pair(j, carry):
        scores(2 * j + 1, s_odd)
        finish(2 * j, s_even)
        scores(2 * j + 2, s_even)
        finish(2 * j + 1, s_odd)
        return carry

    lax.fori_loop(0, ATT_HEADS // 2 - 1, pair, 0)
    scores(ATT_HEADS - 1, s_odd)
    finish(ATT_HEADS - 2, s_even)
    scores(0, s_even, qtn_ref)
    finish(ATT_HEADS - 1, s_odd)


def _attention(q_t, k, v_t, tq):
    b, _, dh, n = q_t.shape
    last = n // tq - 1
    return pl.pallas_call(
        _attention_kernel,
        grid=(b, n // tq),
        in_specs=[pl.BlockSpec((1, ATT_HEADS, dh, tq), lambda i, j: (i, 0, 0, j)),
                  pl.BlockSpec((1, ATT_HEADS, dh, tq), lambda i, j: (i, 0, 0, jnp.minimum(j + 1, last))),
                  pl.BlockSpec((1, ATT_KV_HEADS, n, dh), lambda i, j: (i, 0, 0, 0)),
                  pl.BlockSpec((1, ATT_KV_HEADS, dh, n), lambda i, j: (i, 0, 0, 0))],
        out_specs=pl.BlockSpec((1, ATT_HEADS, dh, tq), lambda i, j: (i, 0, 0, j)),
        out_shape=jax.ShapeDtypeStruct((b, ATT_HEADS, dh, n), BF16),
        scratch_shapes=[pltpu.VMEM((n, tq), F32), pltpu.VMEM((n, tq), F32)],
        compiler_params=_params("parallel", "arbitrary"),
        name="attention",
    )(q_t, q_t, k, v_t)


def _hgrn_levels():
    m, out = 1, []
    while m < HG_CHUNK:
        out.append(m)
        m *= 2
    return out


def _hgrn_tables():
    c = HG_CHUNK
    t = np.arange(c)[:, None]
    u = np.arange(c)[None, :]
    mats, masks, fast = [], [], []
    for reverse in (False, True):
        blocks, pairs = [], [t == u]
        for m in _hgrn_levels():
            base = (t // (2 * m)) * (2 * m)
            if not reverse:
                r = base + m - 1
                block = np.where(t > r, (u > r) & (u <= t), (u > t) & (u <= r))
                roles = ((t // m) % 2 == 1) & ((u // m) % 2 == 0)
            else:
                r = base + m
                block = np.where(t < r, (u >= t) & (u < r), (u >= r) & (u < t))
                roles = ((t // m) % 2 == 0) & ((u // m) % 2 == 1)
            if m < SUBLANES:
                blocks.append(block)
            pairs.append(roles & (t // (2 * m) == u // (2 * m)))
        blocks.append(u >= t if reverse else u <= t)
        mats.append(np.tile(np.concatenate(blocks, axis=0), (1, 2)))
        masks.append(np.stack(pairs))
        fast.append((t // HG_FAST_BLOCK == u // HG_FAST_BLOCK) & (u >= t if reverse else u <= t))
    return tuple(np.stack(x).astype(np.float32) for x in (mats, masks, fast))


def _hgrn_sums(lf2, sum_mat):
    hi = lf2.astype(BF16)
    mid = (lf2 - hi.astype(F32)).astype(BF16)
    return _dot(sum_mat, jnp.concatenate([hi, mid], axis=0))


def _hgrn_chunk(q_ref, lf_ref, v_ref, o_ref, state_ref, sum_mat, pair_mask, fast_mask, b_scr, reverse, fast):
    c = HG_CHUNK
    levels = list(enumerate(_hgrn_levels()))
    n_small = sum(m < SUBLANES for _, m in levels)
    q, lf2, v, state = q_ref[...], lf_ref[...], v_ref[...], state_ref[...]
    qb = q.astype(BF16)
    kb = (1.0 - jnp.exp2(lf2)).astype(BF16)
    if fast:
        b = _hgrn_sums(lf2, sum_mat[n_small * c:, :])
        yield
        b_scr[...] = b
        parts = []
        for g in range(c // HG_FAST_BLOCK):
            lo, up = g * HG_FAST_BLOCK, (g + 1) * HG_FAST_BLOCK
            before = up if reverse else lo - 1
            parts.append(b[lo:up] - b_scr[before:before + 1, :] if 0 <= before < c else b[lo:up])
        loc = jnp.concatenate(parts, axis=0)
        a = _dot_nt(qb * jnp.exp2(loc).astype(BF16), kb * jnp.exp2(-loc).astype(BF16)) * fast_mask[...]
        yield
        levels = [(j, m) for j, m in levels if m >= HG_FAST_BLOCK]
    else:
        sums = _hgrn_sums(lf2, sum_mat[...])
        yield
        b = sums[n_small * c:]
        b_scr[...] = b
        a = _dot_nt(qb, kb) * pair_mask[0]
        yield
    for j, m in levels:
        if m < SUBLANES:
            e = sums[j * c:(j + 1) * c]
        else:
            parts = []
            for g in range(c // (2 * m)):
                lo, mid_row, up = 2 * m * g, 2 * m * g + m, 2 * m * (g + 1)
                r = mid_row if reverse else mid_row - 1
                b_r = b_scr[r:r + 1, :]
                parts += [b[lo:mid_row] - b_r, b_r - b[mid_row:up]] if reverse else \
                         [b_r - b[lo:mid_row], b[mid_row:up] - b_r]
            e = jnp.concatenate(parts, axis=0)
        fac = jnp.exp2(e).astype(BF16)
        a = a + _dot_nt(qb * fac, kb * fac) * pair_mask[j + 1]
        yield
    vb = v.astype(BF16)
    last = 0 if reverse else c - 1
    total = b_scr[last:last + 1, :]
    o_ref[...] = _dot(a.astype(BF16), vb) + _dot_nt(qb * jnp.exp2(b).astype(BF16), state.astype(BF16))
    yield
    k_dec = kb * jnp.exp2(total - b).astype(BF16)
    state_ref[...] = state * jnp.exp2(total) + _dot_tn(vb, k_dec)


def _hgrn_kernel(qf_ref, lff_ref, vf_ref, qb_ref, lfb_ref, vb_ref, sm_ref, pm_ref, fm_ref, *rest):
    n_cast = (len(rest) - 4) // 2
    cast_in, (of_ref, ob_ref), cast_out = rest[:n_cast], rest[n_cast:n_cast + 2], rest[n_cast + 2:2 * n_cast + 2]
    s_ref, b_scr = rest[2 * n_cast + 2:]

    @pl.when(pl.program_id(1) == 0)
    def _():
        s_ref[...] = jnp.zeros_like(s_ref)

    for src, dst in zip(cast_in, cast_out):
        dst[...] = src[...].astype(dst.dtype)

    c = HG_CHUNK
    rows_per_step = qf_ref.shape[2]
    chains = [(d, h) for h in range(HG_HEADS) for d in (0, 1)]
    q_refs, lf_refs, v_refs, o_refs = (qf_ref, qb_ref), (lff_ref, lfb_ref), (vf_ref, vb_ref), (of_ref, ob_ref)

    worst = jnp.zeros((1, HG_DIM), F32)
    for d, h in chains:
        for g in range(rows_per_step // HG_FAST_BLOCK):
            block = lf_refs[d][0, h, g * HG_FAST_BLOCK:(g + 1) * HG_FAST_BLOCK, :]
            worst = jnp.minimum(worst, jnp.sum(block, axis=0, keepdims=True))
    mild = jnp.min(worst) >= -HG_FAST_MAX_LOG2

    def run(fast):
        n_sub = rows_per_step // c
        unroll = min(HG_UNROLL, n_sub) if fast else 1

        def trip(t, carry):
            for u in range(unroll):
                sub = t * unroll + u
                work = []
                for d, h in chains:
                    k = n_sub - 1 - sub if d else sub
                    rows = pl.ds(pl.multiple_of(k * c, c), c)
                    work.append(_hgrn_chunk(q_refs[d].at[0, h, rows, :], lf_refs[d].at[0, h, rows, :],
                                            v_refs[d].at[0, h, rows, :], o_refs[d].at[0, h, rows, :],
                                            s_ref.at[d, h], sm_ref.at[d], pm_ref.at[d], fm_ref.at[d],
                                            b_scr.at[d, h], bool(d), fast))
                for _ in itertools.zip_longest(*work):
                    pass
            return carry

        lax.fori_loop(0, n_sub // unroll, trip, 0)

    pl.when(mild)(lambda: run(True))
    pl.when(jnp.logical_not(mild))(lambda: run(False))


def _hgrn(hq, lf_f, lf_b, hi, to_cast):
    b, nh, n, dk = hq.shape
    rows_per_step = HG_CHUNK * HG_STEP_CHUNKS if n % (HG_CHUNK * HG_STEP_CHUNKS) == 0 else HG_CHUNK
    nc = n // rows_per_step
    blk = (1, nh, rows_per_step, dk)
    fwd = pl.BlockSpec(blk, lambda i, j: (i, 0, j, 0))
    bwd = pl.BlockSpec(blk, lambda i, j: (i, 0, nc - 1 - j, 0))
    out = jax.ShapeDtypeStruct((b, nh, n, dk), F32)
    tables = _hgrn_tables()
    steps = b * nc

    def slab_spec(w):
        hold = next(h for h in range(1, steps + 1)
                    if steps % h == 0 and w.shape[0] % (steps // h * BF16_SUBLANES) == 0)
        return pl.BlockSpec((w.shape[0] // (steps // hold), w.shape[1]), lambda i, j: ((i * nc + j) // hold, 0))

    slabs = [slab_spec(w) for w in to_cast]
    res = pl.pallas_call(
        _hgrn_kernel,
        grid=(b, nc),
        in_specs=[fwd, fwd, fwd, bwd, bwd, bwd] + [_const_spec(t.shape) for t in tables] + slabs,
        out_specs=[fwd, bwd] + slabs,
        out_shape=[out, out] + [jax.ShapeDtypeStruct(w.shape, BF16) for w in to_cast],
        scratch_shapes=[pltpu.VMEM((2, nh, dk, dk), F32), pltpu.VMEM((2, nh, HG_CHUNK, dk), F32)],
        compiler_params=_params("parallel", "arbitrary"),
        name="hgrn",
    )(hq, lf_f, hi, hq, lf_b, hi, jnp.asarray(tables[0], BF16), jnp.asarray(tables[1]), jnp.asarray(tables[2]),
      *to_cast)
    return res[0], res[1], res[2:]


def _mix_cross_kernel(x_ref, att_ref, of_ref, ob_ref, sg_ref, og_ref, wo_ref, pmg_ref, pxg_ref,
                      wq_ref, km_ref, vm_ref, wxo_ref, poxg_ref, pfg_ref, x2_ref, h3_ref):
    tm = x_ref.shape[1]
    part = tm // MIX_ROW_PARTS

    def rows_of(lo):
        rows = slice(lo, lo + part)
        rec = []
        for i in range(HG_HEADS):
            o = _rms(of_ref[0, i, rows, :] + ob_ref[0, i, rows, :], og_ref[...])
            rec.append((o * sg_ref[0, rows, i * HG_DIM:(i + 1) * HG_DIM]).astype(BF16))
        att_t = att_ref[0, :, :, rows].reshape(ATT_Q_DIM, part)
        mixed = _dot_tn(att_t, wo_ref[:ATT_Q_DIM, :].astype(BF16)) + \
            _dot(jnp.concatenate(rec, axis=-1), wo_ref[ATT_Q_DIM:, :].astype(BF16))
        yield
        x1 = x_ref[0, rows, :] + _rms(mixed, pmg_ref[...])
        h2 = _rms(x1, pxg_ref[...]).astype(BF16)
        d = h2.shape[-1]
        dh = d // X_HEADS
        q = _dot(h2, wq_ref[...].astype(BF16)) * (dh ** -0.5)
        yield
        heads = []
        for i in range(X_HEADS):
            sl = slice(i * dh, (i + 1) * dh)
            s = _dot_nt(q[:, sl].astype(BF16), km_ref[0, :, sl])
            p = jnp.exp(s - jnp.max(s, axis=-1, keepdims=True))
            l = jnp.sum(p, axis=-1, keepdims=True)
            heads.append((_dot(p.astype(BF16), vm_ref[0, :, sl]) / l).astype(BF16))
        yield
        xo = _dot(jnp.concatenate(heads, axis=-1), wxo_ref[...].astype(BF16))
        yield
        x2 = x1 + _rms(xo, poxg_ref[...])
        x2_ref[0, rows, :] = x2
        h3_ref[0, rows, :] = _rms(x2, pfg_ref[...]).astype(BF16)

    for _ in itertools.zip_longest(*[rows_of(j * part) for j in range(MIX_ROW_PARTS)]):
        pass


def _mix_cross(x, att, o_f, o_b, sg, og, w_out, pmg, pxg, w_xq, kv_mem, w_xo, poxg, pfg, tm):
    b, n, d = x.shape
    nm = kv_mem.shape[1]
    tok = lambda i, j: (i, j, 0)
    head = lambda i, j: (i, 0, j, 0)
    batch = lambda i, j: (i, 0, 0)
    vec = _const_spec((1, d))
    mat = _resident_spec((d, d))
    return pl.pallas_call(
        _mix_cross_kernel,
        grid=(b, n // tm),
        in_specs=[pl.BlockSpec((1, tm, d), tok),
                  pl.BlockSpec((1, ATT_HEADS, ATT_HEAD_DIM, tm), lambda i, j: (i, 0, 0, j)),
                  pl.BlockSpec((1, HG_HEADS, tm, HG_DIM), head), pl.BlockSpec((1, HG_HEADS, tm, HG_DIM), head),
                  pl.BlockSpec((1, tm, HG_WIDTH), tok), _const_spec((1, HG_DIM)),
                  mat, vec, vec, mat,
                  pl.BlockSpec((1, nm, d), batch), pl.BlockSpec((1, nm, d), lambda i, j: (i, 0, 1)),
                  mat, vec, vec],
        out_specs=[pl.BlockSpec((1, tm, d), tok), pl.BlockSpec((1, tm, d), tok)],
        out_shape=[jax.ShapeDtypeStruct((b, n, d), F32), jax.ShapeDtypeStruct((b, n, d), BF16)],
        compiler_params=_params("parallel", "parallel"),
        name="mix_cross",
    )(x, att, o_f, o_b, sg, og, w_out, pmg, pxg, w_xq, kv_mem, kv_mem, w_xo, poxg, pfg)


def _conv_ffn_kernel(x_ref, h_ref, hp_ref, hn_ref, wu_ref, cw_ref, cb_ref, wd_ref, g_ref, o_ref, act_scr):
    j = pl.program_id(1)
    tm = h_ref.shape[1]
    d_ff = wd_ref.shape[0]
    ck = FF_CHUNK
    prev = jnp.where(j > 0, hp_ref[0], jnp.zeros_like(hp_ref[0]))
    nxt = jnp.where(j < pl.num_programs(1) - 1, hn_ref[0], jnp.zeros_like(hn_ref[0]))
    hext = jnp.concatenate([prev, h_ref[0], nxt], axis=0)

    def conv(col):
        u = _dot(hext, wu_ref[:, col:col + ck])
        taps = (pltpu.roll(u, 1, 0), u, pltpu.roll(u, u.shape[0] - 1, 0))
        out = cb_ref[:, col:col + ck]
        for t in range(CONV_W):
            out = out + taps[t][HALO:HALO + tm] * cw_ref[t:t + 1, col:col + ck]
        return out

    for c in range(d_ff // ck):
        act_scr[:, c * ck:(c + 1) * ck] = (_silu(conv(c * ck)) * conv(d_ff + c * ck)).astype(BF16)
    o_ref[0] = x_ref[0] + _rms(_dot(act_scr[...], wd_ref[...]), g_ref[...])


def _conv_ffn(x2, h3, w_up, conv_w, conv_b, wd, g, tm):
    b, n, d = x2.shape
    d_ff = wd.shape[0]
    ck = FF_CHUNK
    hb = tm // HALO
    last = n // HALO - 1
    tok = lambda i, j: (i, j, 0)
    full = _resident_spec
    return pl.pallas_call(
        _conv_ffn_kernel,
        grid=(b, n // tm),
        in_specs=[pl.BlockSpec((1, tm, d), tok), pl.BlockSpec((1, tm, d), tok),
                  pl.BlockSpec((1, HALO, d), lambda i, j: (i, jnp.maximum(j * hb - 1, 0), 0)),
                  pl.BlockSpec((1, HALO, d), lambda i, j: (i, jnp.minimum((j + 1) * hb, last), 0)),
                  full((d, 2 * d_ff)), full((CONV_W, 2 * d_ff)), full((1, 2 * d_ff)),
                  full((d_ff, d)), _const_spec((1, d))],
        out_specs=pl.BlockSpec((1, tm, d), tok),
        out_shape=jax.ShapeDtypeStruct((b, n, d), F32),
        scratch_shapes=[pltpu.VMEM((tm, d_ff), BF16)],
        compiler_params=_params("parallel", "parallel"),
        name="conv_ffn",
    )(x2, h3, h3, h3, w_up, conv_w, conv_b, wd, g)


def _rope_tables(n):
    pairs = ATT_HEAD_DIM // 4
    pos = np.arange(n)
    inv = np.power(np.float32(ROPE_THETA), -np.arange(pairs, dtype=np.float32) / np.float32(pairs))
    ang = np.concatenate([(pos // GRID_W).astype(np.float32)[:, None] * inv,
                          (pos % GRID_W).astype(np.float32)[:, None] * inv], axis=-1)
    cos = np.repeat(np.cos(ang), 2, axis=-1)
    sin = np.repeat(np.sin(ang), 2, axis=-1) * np.tile(np.array([-1.0, 1.0], np.float32), ATT_HEAD_DIM // 2)
    return jnp.asarray(np.tile(cos, (1, 2)), F32), jnp.asarray(np.tile(sin, (1, 2)), F32)


def _segment_ones():
    i = jnp.arange(LANES) // ATT_HEAD_DIM
    return jnp.tile((i[:, None] == i[None, :]).astype(BF16), (2, 1))


def _layer(x, mem, lb, pre_mix_g, w_in, q_norm_g, k_norm_g, hg_out_norm_g, w_out, post_mix_g, pre_x_g,
           mem_norm_g, w_xq, w_xkv, w_xo, post_x_g, pre_ffn_g, w_up, conv_w, conv_b, w_down, post_ffn_g):
    b, n, d = x.shape
    d_ff = w_down.shape[0]
    assert n % HG_CHUNK == 0 and n % GRID_W == 0 and d_ff % FF_CHUNK == 0
    tm = min(256, n)
    tf = min(512, n)
    row = lambda g: g.reshape(1, -1).astype(F32)

    cos, sin = _rope_tables(n)
    qg = jnp.tile(q_norm_g, ATT_HEADS).reshape(1, -1) * (ATT_HEAD_DIM ** -0.5 * LOG2E)
    kg = jnp.tile(k_norm_g, ATT_KV_HEADS).reshape(1, -1)

    q_t, k, v_t, hq, lf_f, lf_b, hi, sg, kv_mem = _in_proj(
        x, row(pre_mix_g), w_in, qg, kg, cos, sin, lb,
        _segment_ones(), mem, row(mem_norm_g), w_xkv, tf)
    att = _attention(q_t, k, v_t, tm)
    o_f, o_b, (w_up_bf16, w_down_bf16) = _hgrn(hq, lf_f, lf_b, hi, [w_up, w_down])
    x2, h3 = _mix_cross(x, att, o_f, o_b, sg, row(hg_out_norm_g), w_out, row(post_mix_g),
                        row(pre_x_g), w_xq, kv_mem, w_xo,
                        row(post_x_g), row(pre_ffn_g), tf)

    return _conv_ffn(x2, h3, w_up_bf16, conv_w, row(conv_b), w_down_bf16, row(post_ffn_g), tf)


def kernel(x, mem, pre_mix_g, w_in, q_norm_g, k_norm_g, hg_lb, hg_out_norm_g, w_out, post_mix_g, pre_x_g,
           mem_norm_g, w_xq, w_xkv, w_xo, post_x_g, pre_ffn_g, w_up, conv_w, conv_b, w_down, post_ffn_g):
    lb_all = jnp.cumsum(jax.nn.softmax(hg_lb.astype(F32), axis=1), axis=1)
    for l in range(w_in.shape[0]):
        x = _layer(x, mem, lb_all[:, l], pre_mix_g[l], w_in[l], q_norm_g[l], k_norm_g[l], hg_out_norm_g[l],
                   w_out[l], post_mix_g[l], pre_x_g[l], mem_norm_g[l], w_xq[l], w_xkv[l], w_xo[l],
                   post_x_g[l], pre_ffn_g[l], w_up[l], conv_w[l], conv_b[l], w_down[l], post_ffn_g[l])
    return x
```

```python
import functools
import itertools
import math

import numpy as np
import jax
import jax.numpy as jnp
from jax import lax
from jax.experimental import pallas as pl
from jax.experimental.pallas import tpu as pltpu

F32 = jnp.float32
BF16 = jnp.bfloat16

EPS = 1e-6
LOG2E = math.log2(math.e)
GRID_W = 64
ROPE_THETA = 10000.0

ATT_HEADS = 8
ATT_KV_HEADS = 2
ATT_GROUP = ATT_HEADS // ATT_KV_HEADS
ATT_HEAD_DIM = 64
ATT_Q_DIM = ATT_HEADS * ATT_HEAD_DIM
ATT_KV_DIM = ATT_KV_HEADS * ATT_HEAD_DIM
ATT_KEY_SPLITS = 2

HG_HEADS = 4
HG_DIM = 128
HG_WIDTH = HG_HEADS * HG_DIM
HG_CHUNK = 128
HG_STEP_CHUNKS = 8
HG_UNROLL = 4
HG_FAST_BLOCK = 64
HG_FAST_MAX_LOG2 = 96.0

X_HEADS = 4
IN_ROW_PARTS = 2
MIX_ROW_PARTS = 4
CONV_W = 3
FF_CHUNK = 256

VMEM_LIMIT = 56 * 1024 * 1024
SUBLANES = 8
BF16_SUBLANES = 16
LANES = 128

HALO = BF16_SUBLANES


def _params(*sem):
    return pltpu.CompilerParams(dimension_semantics=sem, vmem_limit_bytes=VMEM_LIMIT)


def _rms(x, g):
    ms = jnp.mean(x * x, axis=-1, keepdims=True)
    return x * lax.rsqrt(ms + EPS) * g


def _silu(x):
    return x / (1.0 + jnp.exp(-x))


def _dot(a, b):
    return jnp.dot(a, b, preferred_element_type=F32)


def _dot_nt(a, b):
    return lax.dot_general(a, b, (((1,), (1,)), ((), ())), preferred_element_type=F32)


def _dot_tn(a, b):
    return lax.dot_general(a, b, (((0,), (0,)), ((), ())), preferred_element_type=F32)


def _const_spec(shape):
    return pl.BlockSpec(shape, lambda *_: (0,) * len(shape))


def _resident_spec(shape):
    return pl.BlockSpec(shape, lambda *_: (0,) * len(shape), pipeline_mode=pl.Buffered(1))


def _head_norm_rope(a, seg, gain, cos, sin_signed):
    w = a.shape[-1]
    sq = a * a
    hi = sq.astype(BF16)
    lo = (sq - hi.astype(F32)).astype(BF16)
    ss = [_dot(jnp.concatenate([hi[:, j:j + LANES], lo[:, j:j + LANES]], axis=-1), seg) for j in range(0, w, LANES)]
    ss = jnp.concatenate(ss, axis=-1) if len(ss) > 1 else ss[0]
    an = a * lax.rsqrt(ss * (1.0 / ATT_HEAD_DIM) + EPS) * gain
    lane = lax.broadcasted_iota(jnp.int32, a.shape, 1)
    partner = jnp.where(lane % 2 == 0, pltpu.roll(an, w - 1, 1), pltpu.roll(an, 1, 1))
    return an * cos + partner * sin_signed


def _in_proj_kernel(x_ref, g_ref, w_ref, qg_ref, kg_ref, cos_ref, sin_ref, lb_ref, seg_ref,
                    mem_ref, mg_ref, wkv_ref,
                    qt_ref, k_ref, vt_ref, hq_ref, lff_ref, lfb_ref, hi_ref, hg_ref, kv_ref):
    part = x_ref.shape[1] // IN_ROW_PARTS

    kv_ref[0] = _dot(_rms(mem_ref[0], mg_ref[...]).astype(BF16), wkv_ref[...].astype(BF16)).astype(BF16)

    def rows_of(r0):
        rows = slice(r0, r0 + part)
        h = _rms(x_ref[0, rows, :], g_ref[...]).astype(BF16)

        def proj(lo, width):
            return _dot(h, w_ref[:, lo:lo + width].astype(BF16))

        def heads_out(ref, val):
            for i in range(HG_HEADS):
                ref[0, i, rows, :] = val[:, i * HG_DIM:(i + 1) * HG_DIM].astype(ref.dtype)

        cos = cos_ref[rows, :]
        sin = sin_ref[rows, :]
        c0 = 0
        q = _head_norm_rope(proj(c0, ATT_Q_DIM), seg_ref[...], qg_ref[...],
                            jnp.concatenate([cos] * (ATT_Q_DIM // LANES), axis=-1),
                            jnp.concatenate([sin] * (ATT_Q_DIM // LANES), axis=-1))
        q_t = jnp.transpose(q)
        for i in range(ATT_HEADS):
            qt_ref[0, i, :, rows] = q_t[i * ATT_HEAD_DIM:(i + 1) * ATT_HEAD_DIM, :].astype(BF16)
        c0 += ATT_Q_DIM
        yield
        k = _head_norm_rope(proj(c0, ATT_KV_DIM), seg_ref[...], kg_ref[...], cos, sin)
        c0 += ATT_KV_DIM
        v_t = jnp.transpose(proj(c0, ATT_KV_DIM))
        c0 += ATT_KV_DIM
        for i in range(ATT_KV_HEADS):
            k_ref[0, i, rows, :] = k[:, i * ATT_HEAD_DIM:(i + 1) * ATT_HEAD_DIM].astype(BF16)
            vt_ref[0, i, :, rows] = v_t[i * ATT_HEAD_DIM:(i + 1) * ATT_HEAD_DIM, :].astype(BF16)
        yield
        heads_out(hq_ref, _silu(proj(c0, HG_WIDTH)))
        c0 += HG_WIDTH
        yield
        for d, ref in enumerate((lff_ref, lfb_ref)):
            lb = lb_ref[d:d + 1, :]
            z = proj(c0, HG_WIDTH)
            heads_out(ref, jnp.log(lb + (1.0 - lb) / (1.0 + jnp.exp(-z))) * LOG2E)
            c0 += HG_WIDTH
            yield
        heads_out(hi_ref, proj(c0, HG_WIDTH))
        c0 += HG_WIDTH
        yield
        hg_ref[0, rows, :] = _silu(proj(c0, HG_WIDTH))

    for _ in itertools.zip_longest(*[rows_of(j * part) for j in range(IN_ROW_PARTS)]):
        pass


def _in_proj(x, g, w_in, qg, kg, cos, sin, lb, seg, mem, mem_g, w_xkv, tm):
    b, n, d = x.shape
    nt = n // tm
    n_in = w_in.shape[1]
    nm = mem.shape[1]
    kv_cols = w_xkv.shape[1] // nt
    assert w_xkv.shape[1] % nt == 0 and kv_cols % LANES == 0
    tok = lambda i, j: (i, j, 0)
    head = lambda i, j: (i, 0, j, 0)
    pos = lambda i, j: (j, 0)
    return pl.pallas_call(
        _in_proj_kernel,
        grid=(b, nt),
        in_specs=[pl.BlockSpec((1, tm, d), tok), _const_spec((1, d)), _resident_spec((d, n_in)),
                  _const_spec((1, ATT_Q_DIM)), _const_spec((1, ATT_KV_DIM)),
                  pl.BlockSpec((tm, LANES), pos), pl.BlockSpec((tm, LANES), pos),
                  _const_spec((2, HG_WIDTH)),
                  _const_spec((2 * LANES, LANES)),
                  pl.BlockSpec((1, nm, d), lambda i, j: (i, 0, 0)), _const_spec((1, d)),
                  pl.BlockSpec((d, kv_cols), lambda i, j: (0, j))],
        out_specs=[pl.BlockSpec((1, ATT_HEADS, ATT_HEAD_DIM, tm), lambda i, j: (i, 0, 0, j)),
                   pl.BlockSpec((1, ATT_KV_HEADS, tm, ATT_HEAD_DIM), head),
                   pl.BlockSpec((1, ATT_KV_HEADS, ATT_HEAD_DIM, tm), lambda i, j: (i, 0, 0, j)),
                   pl.BlockSpec((1, HG_HEADS, tm, HG_DIM), head),
                   pl.BlockSpec((1, HG_HEADS, tm, HG_DIM), head),
                   pl.BlockSpec((1, HG_HEADS, tm, HG_DIM), head),
                   pl.BlockSpec((1, HG_HEADS, tm, HG_DIM), head),
                   pl.BlockSpec((1, tm, HG_WIDTH), tok),
                   pl.BlockSpec((1, nm, kv_cols), lambda i, j: (i, 0, j))],
        out_shape=[jax.ShapeDtypeStruct((b, ATT_HEADS, ATT_HEAD_DIM, n), BF16),
                   jax.ShapeDtypeStruct((b, ATT_KV_HEADS, n, ATT_HEAD_DIM), BF16),
                   jax.ShapeDtypeStruct((b, ATT_KV_HEADS, ATT_HEAD_DIM, n), BF16),
                   jax.ShapeDtypeStruct((b, HG_HEADS, n, HG_DIM), BF16),
                   jax.ShapeDtypeStruct((b, HG_HEADS, n, HG_DIM), F32),
                   jax.ShapeDtypeStruct((b, HG_HEADS, n, HG_DIM), F32),
                   jax.ShapeDtypeStruct((b, HG_HEADS, n, HG_DIM), BF16),
                   jax.ShapeDtypeStruct((b, n, HG_WIDTH), F32),
                   jax.ShapeDtypeStruct((b, nm, w_xkv.shape[1]), BF16)],
        compiler_params=_params("parallel", "parallel"),
        name="in_proj",
    )(x, g, w_in, qg, kg, cos, sin, lb, seg, mem, mem_g, w_xkv)


def _attention_kernel(qt_ref, qtn_ref, k_ref, vt_ref, o_ref, s_even, s_odd):
    n = k_ref.shape[2]
    part = n // ATT_KEY_SPLITS
    parts = [slice(j * part, (j + 1) * part) for j in range(ATT_KEY_SPLITS)]

    def scores(i, s_scr, queries_t=qt_ref):
        for rows in parts:
            s_scr[rows, :] = _dot(k_ref[0, i // ATT_GROUP, rows, :], queries_t[0, i])

    def finish(i, s_scr):
        s = [s_scr[rows, :] for rows in parts]
        m = functools.reduce(jnp.maximum, [jnp.max(x, axis=0, keepdims=True) for x in s])
        p = [jnp.exp2(x - m) for x in s]
        l = sum(jnp.sum(x, axis=0, keepdims=True) for x in p)
        o_t = sum(_dot(vt_ref[0, i // ATT_GROUP, :, rows], x.astype(BF16)) for rows, x in zip(parts, p))
        o_ref[0, i] = (o_t / l).astype(BF16)

    @pl.when(pl.program_id(1) == 0)
    def _():
        scores(0, s_even)

    def pair(j, carry):
        scores(2 * j + 1, s_odd)
        finish(2 * j, s_even)
        scores(2 * j + 2, s_even)
        finish(2 * j + 1, s_odd)
        return carry

    lax.fori_loop(0, ATT_HEADS // 2 - 1, pair, 0)
    scores(ATT_HEADS - 1, s_odd)
    finish(ATT_HEADS - 2, s_even)
    scores(0, s_even, qtn_ref)
    finish(ATT_HEADS - 1, s_odd)


def _attention(q_t, k, v_t, tq):
    b, _, dh, n = q_t.shape
    last = n // tq - 1
    return pl.pallas_call(
        _attention_kernel,
        grid=(b, n // tq),
        in_specs=[pl.BlockSpec((1, ATT_HEADS, dh, tq), lambda i, j: (i, 0, 0, j)),
                  pl.BlockSpec((1, ATT_HEADS, dh, tq), lambda i, j: (i, 0, 0, jnp.minimum(j + 1, last))),
                  pl.BlockSpec((1, ATT_KV_HEADS, n, dh), lambda i, j: (i, 0, 0, 0)),
                  pl.BlockSpec((1, ATT_KV_HEADS, dh, n), lambda i, j: (i, 0, 0, 0))],
        out_specs=pl.BlockSpec((1, ATT_HEADS, dh, tq), lambda i, j: (i, 0, 0, j)),
        out_shape=jax.ShapeDtypeStruct((b, ATT_HEADS, dh, n), BF16),
        scratch_shapes=[pltpu.VMEM((n, tq), F32), pltpu.VMEM((n, tq), F32)],
        compiler_params=_params("parallel", "arbitrary"),
        name="attention",
    )(q_t, q_t, k, v_t)


def _hgrn_levels():
    m, out = 1, []
    while m < HG_CHUNK:
        out.append(m)
        m *= 2
    return out


def _hgrn_tables():
    c = HG_CHUNK
    t = np.arange(c)[:, None]
    u = np.arange(c)[None, :]
    mats, masks, fast = [], [], []
    for reverse in (False, True):
        blocks, pairs = [], [t == u]
        for m in _hgrn_levels():
            base = (t // (2 * m)) * (2 * m)
            if not reverse:
                r = base + m - 1
                block = np.where(t > r, (u > r) & (u <= t), (u > t) & (u <= r))
                roles = ((t // m) % 2 == 1) & ((u // m) % 2 == 0)
            else:
                r = base + m
                block = np.where(t < r, (u >= t) & (u < r), (u >= r) & (u < t))
                roles = ((t // m) % 2 == 0) & ((u // m) % 2 == 1)
            if m < SUBLANES:
                blocks.append(block)
            pairs.append(roles & (t // (2 * m) == u // (2 * m)))
        blocks.append(u >= t if reverse else u <= t)
        mats.append(np.tile(np.concatenate(blocks, axis=0), (1, 2)))
        masks.append(np.stack(pairs))
        fast.append((t // HG_FAST_BLOCK == u // HG_FAST_BLOCK) & (u >= t if reverse else u <= t))
    return tuple(np.stack(x).astype(np.float32) for x in (mats, masks, fast))


def _hgrn_sums(lf2, sum_mat):
    hi = lf2.astype(BF16)
    mid = (lf2 - hi.astype(F32)).astype(BF16)
    return _dot(sum_mat, jnp.concatenate([hi, mid], axis=0))


def _hgrn_chunk(q_ref, lf_ref, v_ref, o_ref, state_ref, sum_mat, pair_mask, fast_mask, b_scr, reverse, fast):
    c = HG_CHUNK
    levels = list(enumerate(_hgrn_levels()))
    n_small = sum(m < SUBLANES for _, m in levels)
    q, lf2, v, state = q_ref[...], lf_ref[...], v_ref[...], state_ref[...]
    qb = q.astype(BF16)
    kb = (1.0 - jnp.exp2(lf2)).astype(BF16)
    if fast:
        b = _hgrn_sums(lf2, sum_mat[n_small * c:, :])
        yield
        b_scr[...] = b
        parts = []
        for g in range(c // HG_FAST_BLOCK):
            lo, up = g * HG_FAST_BLOCK, (g + 1) * HG_FAST_BLOCK
            before = up if reverse else lo - 1
            parts.append(b[lo:up] - b_scr[before:before + 1, :] if 0 <= before < c else b[lo:up])
        loc = jnp.concatenate(parts, axis=0)
        a = _dot_nt(qb * jnp.exp2(loc).astype(BF16), kb * jnp.exp2(-loc).astype(BF16)) * fast_mask[...]
        yield
        levels = [(j, m) for j, m in levels if m >= HG_FAST_BLOCK]
    else:
        sums = _hgrn_sums(lf2, sum_mat[...])
        yield
        b = sums[n_small * c:]
        b_scr[...] = b
        a = _dot_nt(qb, kb) * pair_mask[0]
        yield
    for j, m in levels:
        if m < SUBLANES:
            e = sums[j * c:(j + 1) * c]
        else:
            parts = []
            for g in range(c // (2 * m)):
                lo, mid_row, up = 2 * m * g, 2 * m * g + m, 2 * m * (g + 1)
                r = mid_row if reverse else mid_row - 1
                b_r = b_scr[r:r + 1, :]
                parts += [b[lo:mid_row] - b_r, b_r - b[mid_row:up]] if reverse else \
                         [b_r - b[lo:mid_row], b[mid_row:up] - b_r]
            e = jnp.concatenate(parts, axis=0)
        fac = jnp.exp2(e).astype(BF16)
        a = a + _dot_nt(qb * fac, kb * fac) * pair_mask[j + 1]
        yield
    vb = v.astype(BF16)
    last = 0 if reverse else c - 1
    total = b_scr[last:last + 1, :]
    o_ref[...] = _dot(a.astype(BF16), vb) + _dot_nt(qb * jnp.exp2(b).astype(BF16), state.astype(BF16))
    yield
    k_dec = kb * jnp.exp2(total - b).astype(BF16)
    state_ref[...] = state * jnp.exp2(total) + _dot_tn(vb, k_dec)


def _hgrn_kernel(qf_ref, lff_ref, vf_ref, qb_ref, lfb_ref, vb_ref, sm_ref, pm_ref, fm_ref, *rest):
    n_cast = (len(rest) - 4) // 2
    cast_in, (of_ref, ob_ref), cast_out = rest[:n_cast], rest[n_cast:n_cast + 2], rest[n_cast + 2:2 * n_cast + 2]
    s_ref, b_scr = rest[2 * n_cast + 2:]

    @pl.when(pl.program_id(1) == 0)
    def _():
        s_ref[...] = jnp.zeros_like(s_ref)

    for src, dst in zip(cast_in, cast_out):
        dst[...] = src[...].astype(dst.dtype)

    c = HG_CHUNK
    rows_per_step = qf_ref.shape[2]
    chains = [(d, h) for h in range(HG_HEADS) for d in (0, 1)]
    q_refs, lf_refs, v_refs, o_refs = (qf_ref, qb_ref), (lff_ref, lfb_ref), (vf_ref, vb_ref), (of_ref, ob_ref)

    worst = jnp.zeros((1, HG_DIM), F32)
    for d, h in chains:
        for g in range(rows_per_step // HG_FAST_BLOCK):
            block = lf_refs[d][0, h, g * HG_FAST_BLOCK:(g + 1) * HG_FAST_BLOCK, :]
            worst = jnp.minimum(worst, jnp.sum(block, axis=0, keepdims=True))
    mild = jnp.min(worst) >= -HG_FAST_MAX_LOG2

    def run(fast):
        n_sub = rows_per_step // c
        unroll = min(HG_UNROLL, n_sub) if fast else 1

        def trip(t, carry):
            for u in range(unroll):
                sub = t * unroll + u
                work = []
                for d, h in chains:
                    k = n_sub - 1 - sub if d else sub
                    rows = pl.ds(pl.multiple_of(k * c, c), c)
                    work.append(_hgrn_chunk(q_refs[d].at[0, h, rows, :], lf_refs[d].at[0, h, rows, :],
                                            v_refs[d].at[0, h, rows, :], o_refs[d].at[0, h, rows, :],
                                            s_ref.at[d, h], sm_ref.at[d], pm_ref.at[d], fm_ref.at[d],
                                            b_scr.at[d, h], bool(d), fast))
                for _ in itertools.zip_longest(*work):
                    pass
            return carry

        lax.fori_loop(0, n_sub // unroll, trip, 0)

    pl.when(mild)(lambda: run(True))
    pl.when(jnp.logical_not(mild))(lambda: run(False))


def _hgrn(hq, lf_f, lf_b, hi, to_cast):
    b, nh, n, dk = hq.shape
    rows_per_step = HG_CHUNK * HG_STEP_CHUNKS if n % (HG_CHUNK * HG_STEP_CHUNKS) == 0 else HG_CHUNK
    nc = n // rows_per_step
    blk = (1, nh, rows_per_step, dk)
    fwd = pl.BlockSpec(blk, lambda i, j: (i, 0, j, 0))
    bwd = pl.BlockSpec(blk, lambda i, j: (i, 0, nc - 1 - j, 0))
    out = jax.ShapeDtypeStruct((b, nh, n, dk), F32)
    tables = _hgrn_tables()
    steps = b * nc

    def slab_spec(w):
        hold = next(h for h in range(1, steps + 1)
                    if steps % h == 0 and w.shape[0] % (steps // h * BF16_SUBLANES) == 0)
        return pl.BlockSpec((w.shape[0] // (steps // hold), w.shape[1]), lambda i, j: ((i * nc + j) // hold, 0))

    slabs = [slab_spec(w) for w in to_cast]
    res = pl.pallas_call(
        _hgrn_kernel,
        grid=(b, nc),
        in_specs=[fwd, fwd, fwd, bwd, bwd, bwd] + [_const_spec(t.shape) for t in tables] + slabs,
        out_specs=[fwd, bwd] + slabs,
        out_shape=[out, out] + [jax.ShapeDtypeStruct(w.shape, BF16) for w in to_cast],
        scratch_shapes=[pltpu.VMEM((2, nh, dk, dk), F32), pltpu.VMEM((2, nh, HG_CHUNK, dk), F32)],
        compiler_params=_params("parallel", "arbitrary"),
        name="hgrn",
    )(hq, lf_f, hi, hq, lf_b, hi, jnp.asarray(tables[0], BF16), jnp.asarray(tables[1]), jnp.asarray(tables[2]),
      *to_cast)
    return res[0], res[1], res[2:]


def _mix_cross_kernel(x_ref, att_ref, of_ref, ob_ref, sg_ref, og_ref, wo_ref, pmg_ref, pxg_ref,
                      wq_ref, km_ref, vm_ref, wxo_ref, poxg_ref, pfg_ref, x2_ref, h3_ref):
    tm = x_ref.shape[1]
    part = tm // MIX_ROW_PARTS

    def rows_of(lo):
        rows = slice(lo, lo + part)
        rec = []
        for i in range(HG_HEADS):
            o = _rms(of_ref[0, i, rows, :] + ob_ref[0, i, rows, :], og_ref[...])
            rec.append((o * sg_ref[0, rows, i * HG_DIM:(i + 1) * HG_DIM]).astype(BF16))
        att_t = att_ref[0, :, :, rows].reshape(ATT_Q_DIM, part)
        mixed = _dot_tn(att_t, wo_ref[:ATT_Q_DIM, :].astype(BF16)) + \
            _dot(jnp.concatenate(rec, axis=-1), wo_ref[ATT_Q_DIM:, :].astype(BF16))
        yield
        x1 = x_ref[0, rows, :] + _rms(mixed, pmg_ref[...])
        h2 = _rms(x1, pxg_ref[...]).astype(BF16)
        d = h2.shape[-1]
        dh = d // X_HEADS
        q = _dot(h2, wq_ref[...].astype(BF16)) * (dh ** -0.5)
        yield
        heads = []
        for i in range(X_HEADS):
            sl = slice(i * dh, (i + 1) * dh)
            s = _dot_nt(q[:, sl].astype(BF16), km_ref[0, :, sl])
            p = jnp.exp(s - jnp.max(s, axis=-1, keepdims=True))
            l = jnp.sum(p, axis=-1, keepdims=True)
            heads.append((_dot(p.astype(BF16), vm_ref[0, :, sl]) / l).astype(BF16))
        yield
        xo = _dot(jnp.concatenate(heads, axis=-1), wxo_ref[...].astype(BF16))
        yield
        x2 = x1 + _rms(xo, poxg_ref[...])
        x2_ref[0, rows, :] = x2
        h3_ref[0, rows, :] = _rms(x2, pfg_ref[...]).astype(BF16)

    for _ in itertools.zip_longest(*[rows_of(j * part) for j in range(MIX_ROW_PARTS)]):
        pass


def _mix_cross(x, att, o_f, o_b, sg, og, w_out, pmg, pxg, w_xq, kv_mem, w_xo, poxg, pfg, tm):
    b, n, d = x.shape
    nm = kv_mem.shape[1]
    tok = lambda i, j: (i, j, 0)
    head = lambda i, j: (i, 0, j, 0)
    batch = lambda i, j: (i, 0, 0)
    vec = _const_spec((1, d))
    mat = _resident_spec((d, d))
    return pl.pallas_call(
        _mix_cross_kernel,
        grid=(b, n // tm),
        in_specs=[pl.BlockSpec((1, tm, d), tok),
                  pl.BlockSpec((1, ATT_HEADS, ATT_HEAD_DIM, tm), lambda i, j: (i, 0, 0, j)),
                  pl.BlockSpec((1, HG_HEADS, tm, HG_DIM), head), pl.BlockSpec((1, HG_HEADS, tm, HG_DIM), head),
                  pl.BlockSpec((1, tm, HG_WIDTH), tok), _const_spec((1, HG_DIM)),
                  mat, vec, vec, mat,
                  pl.BlockSpec((1, nm, d), batch), pl.BlockSpec((1, nm, d), lambda i, j: (i, 0, 1)),
                  mat, vec, vec],
        out_specs=[pl.BlockSpec((1, tm, d), tok), pl.BlockSpec((1, tm, d), tok)],
        out_shape=[jax.ShapeDtypeStruct((b, n, d), F32), jax.ShapeDtypeStruct((b, n, d), BF16)],
        compiler_params=pltpu.CompilerParams(dimension_semantics=("parallel", "parallel"),
                                             vmem_limit_bytes=61 * 1024 * 1024),
        name="mix_cross",
    )(x, att, o_f, o_b, sg, og, w_out, pmg, pxg, w_xq, kv_mem, kv_mem, w_xo, poxg, pfg)


def _conv_ffn_kernel(x_ref, h_ref, hp_ref, hn_ref, wu_ref, cw_ref, cb_ref, wd_ref, g_ref, o_ref, act_scr):
    j = pl.program_id(1)
    tm = h_ref.shape[1]
    d_ff = wd_ref.shape[0]
    ck = FF_CHUNK
    prev = jnp.where(j > 0, hp_ref[0], jnp.zeros_like(hp_ref[0]))
    nxt = jnp.where(j < pl.num_programs(1) - 1, hn_ref[0], jnp.zeros_like(hn_ref[0]))
    hext = jnp.concatenate([prev, h_ref[0], nxt], axis=0)

    def conv(col):
        u = _dot(hext, wu_ref[:, col:col + ck])
        taps = (pltpu.roll(u, 1, 0), u, pltpu.roll(u, u.shape[0] - 1, 0))
        out = cb_ref[:, col:col + ck]
        for t in range(CONV_W):
            out = out + taps[t][HALO:HALO + tm] * cw_ref[t:t + 1, col:col + ck]
        return out

    for c in range(d_ff // ck):
        act_scr[:, c * ck:(c + 1) * ck] = (_silu(conv(c * ck)) * conv(d_ff + c * ck)).astype(BF16)
    o_ref[0] = x_ref[0] + _rms(_dot(act_scr[...], wd_ref[...]), g_ref[...])


def _conv_ffn(x2, h3, w_up, conv_w, conv_b, wd, g, tm):
    b, n, d = x2.shape
    d_ff = wd.shape[0]
    ck = FF_CHUNK
    hb = tm // HALO
    last = n // HALO - 1
    tok = lambda i, j: (i, j, 0)
    full = _resident_spec
    return pl.pallas_call(
        _conv_ffn_kernel,
        grid=(b, n // tm),
        in_specs=[pl.BlockSpec((1, tm, d), tok), pl.BlockSpec((1, tm, d), tok),
                  pl.BlockSpec((1, HALO, d), lambda i, j: (i, jnp.maximum(j * hb - 1, 0), 0)),
                  pl.BlockSpec((1, HALO, d), lambda i, j: (i, jnp.minimum((j + 1) * hb, last), 0)),
                  full((d, 2 * d_ff)), full((CONV_W, 2 * d_ff)), full((1, 2 * d_ff)),
                  full((d_ff, d)), _const_spec((1, d))],
        out_specs=pl.BlockSpec((1, tm, d), tok),
        out_shape=jax.ShapeDtypeStruct((b, n, d), F32),
        scratch_shapes=[pltpu.VMEM((tm, d_ff), BF16)],
        compiler_params=_params("parallel", "parallel"),
        name="conv_ffn",
    )(x2, h3, h3, h3, w_up, conv_w, conv_b, wd, g)


def _rope_tables(n):
    pairs = ATT_HEAD_DIM // 4
    pos = np.arange(n)
    inv = np.power(np.float32(ROPE_THETA), -np.arange(pairs, dtype=np.float32) / np.float32(pairs))
    ang = np.concatenate([(pos // GRID_W).astype(np.float32)[:, None] * inv,
                          (pos % GRID_W).astype(np.float32)[:, None] * inv], axis=-1)
    cos = np.repeat(np.cos(ang), 2, axis=-1)
    sin = np.repeat(np.sin(ang), 2, axis=-1) * np.tile(np.array([-1.0, 1.0], np.float32), ATT_HEAD_DIM // 2)
    return jnp.asarray(np.tile(cos, (1, 2)), F32), jnp.asarray(np.tile(sin, (1, 2)), F32)


def _segment_ones():
    i = jnp.arange(LANES) // ATT_HEAD_DIM
    return jnp.tile((i[:, None] == i[None, :]).astype(BF16), (2, 1))


def _layer(x, mem, lb, pre_mix_g, w_in, q_norm_g, k_norm_g, hg_out_norm_g, w_out, post_mix_g, pre_x_g,
           mem_norm_g, w_xq, w_xkv, w_xo, post_x_g, pre_ffn_g, w_up, conv_w, conv_b, w_down, post_ffn_g):
    b, n, d = x.shape
    d_ff = w_down.shape[0]
    assert n % HG_CHUNK == 0 and n % GRID_W == 0 and d_ff % FF_CHUNK == 0
    tm = min(256, n)
    tf = min(512, n)
    row = lambda g: g.reshape(1, -1).astype(F32)

    cos, sin = _rope_tables(n)
    qg = jnp.tile(q_norm_g, ATT_HEADS).reshape(1, -1) * (ATT_HEAD_DIM ** -0.5 * LOG2E)
    kg = jnp.tile(k_norm_g, ATT_KV_HEADS).reshape(1, -1)

    q_t, k, v_t, hq, lf_f, lf_b, hi, sg, kv_mem = _in_proj(
        x, row(pre_mix_g), w_in, qg, kg, cos, sin, lb,
        _segment_ones(), mem, row(mem_norm_g), w_xkv, tf)
    att = _attention(q_t, k, v_t, tm)
    o_f, o_b, (w_up_bf16, w_down_bf16) = _hgrn(hq, lf_f, lf_b, hi, [w_up, w_down])
    x2, h3 = _mix_cross(x, att, o_f, o_b, sg, row(hg_out_norm_g), w_out, row(post_mix_g),
                        row(pre_x_g), w_xq, kv_mem, w_xo,
                        row(post_x_g), row(pre_ffn_g), min(1024, n))

    return _conv_ffn(x2, h3, w_up_bf16, conv_w, row(conv_b), w_down_bf16, row(post_ffn_g), tf)


def kernel(x, mem, pre_mix_g, w_in, q_norm_g, k_norm_g, hg_lb, hg_out_norm_g, w_out, post_mix_g, pre_x_g,
           mem_norm_g, w_xq, w_xkv, w_xo, post_x_g, pre_ffn_g, w_up, conv_w, conv_b, w_down, post_ffn_g):
    lb_all = jnp.cumsum(jax.nn.softmax(hg_lb.astype(F32), axis=1), axis=1)
    for l in range(w_in.shape[0]):
        x = _layer(x, mem, lb_all[:, l], pre_mix_g[l], w_in[l], q_norm_g[l], k_norm_g[l], hg_out_norm_g[l],
                   w_out[l], post_mix_g[l], pre_x_g[l], mem_norm_g[l], w_xq[l], w_xkv[l], w_xo[l],
                   post_x_g[l], pre_ffn_g[l], w_up[l], conv_w[l], conv_b[l], w_down[l], post_ffn_g[l])
    return x
```

```python
import functools
import itertools
import math

import numpy as np
import jax
import jax.numpy as jnp
from jax import lax
from jax.experimental import pallas as pl
from jax.experimental.pallas import tpu as pltpu

F32 = jnp.float32
BF16 = jnp.bfloat16

EPS = 1e-6
LOG2E = math.log2(math.e)
GRID_W = 64
ROPE_THETA = 10000.0

ATT_HEADS = 8
ATT_KV_HEADS = 2
ATT_GROUP = ATT_HEADS // ATT_KV_HEADS
ATT_HEAD_DIM = 64
ATT_Q_DIM = ATT_HEADS * ATT_HEAD_DIM
ATT_KV_DIM = ATT_KV_HEADS * ATT_HEAD_DIM
ATT_KEY_SPLITS = 2

HG_HEADS = 4
HG_DIM = 128
HG_WIDTH = HG_HEADS * HG_DIM
HG_CHUNK = 128
HG_STEP_CHUNKS = 8
HG_UNROLL = 4
HG_FAST_BLOCK = 64
HG_FAST_MAX_LOG2 = 96.0

X_HEADS = 4
IN_ROW_PARTS = 4
MIX_ROW_PARTS = 4
CONV_W = 3
FF_CHUNK = 256

VMEM_LIMIT = 56 * 1024 * 1024
VMEM_LIMIT_WIDE = 61 * 1024 * 1024
SUBLANES = 8
BF16_SUBLANES = 16
LANES = 128

HALO = BF16_SUBLANES


def _params(*sem, vmem_limit=VMEM_LIMIT):
    return pltpu.CompilerParams(dimension_semantics=sem, vmem_limit_bytes=vmem_limit)


def _rms(x, g):
    ms = jnp.mean(x * x, axis=-1, keepdims=True)
    return x * lax.rsqrt(ms + EPS) * g


def _silu(x):
    return x / (1.0 + jnp.exp(-x))


def _dot(a, b):
    return jnp.dot(a, b, preferred_element_type=F32)


def _dot_nt(a, b):
    return lax.dot_general(a, b, (((1,), (1,)), ((), ())), preferred_element_type=F32)


def _dot_tn(a, b):
    return lax.dot_general(a, b, (((0,), (0,)), ((), ())), preferred_element_type=F32)


def _const_spec(shape):
    return pl.BlockSpec(shape, lambda *_: (0,) * len(shape))


def _resident_spec(shape):
    return pl.BlockSpec(shape, lambda *_: (0,) * len(shape), pipeline_mode=pl.Buffered(1))


def _head_norm_rope(a, seg, gain, cos, sin_signed):
    w = a.shape[-1]
    sq = a * a
    hi = sq.astype(BF16)
    lo = (sq - hi.astype(F32)).astype(BF16)
    ss = [_dot(jnp.concatenate([hi[:, j:j + LANES], lo[:, j:j + LANES]], axis=-1), seg) for j in range(0, w, LANES)]
    ss = jnp.concatenate(ss, axis=-1) if len(ss) > 1 else ss[0]
    an = a * lax.rsqrt(ss * (1.0 / ATT_HEAD_DIM) + EPS) * gain
    lane = lax.broadcasted_iota(jnp.int32, a.shape, 1)
    partner = jnp.where(lane % 2 == 0, pltpu.roll(an, w - 1, 1), pltpu.roll(an, 1, 1))
    return an * cos + partner * sin_signed


def _in_proj_kernel(x_ref, g_ref, w_ref, qg_ref, kg_ref, cos_ref, sin_ref, lb_ref, seg_ref,
                    mem_ref, mg_ref, wkv_ref,
                    qt_ref, k_ref, vt_ref, hq_ref, lff_ref, lfb_ref, hi_ref, hg_ref, kv_ref):
    part = x_ref.shape[1] // IN_ROW_PARTS

    kv_ref[0] = _dot(_rms(mem_ref[0], mg_ref[...]).astype(BF16), wkv_ref[...].astype(BF16)).astype(BF16)

    def rows_of(r0):
        rows = slice(r0, r0 + part)
        h = _rms(x_ref[0, rows, :], g_ref[...]).astype(BF16)

        def proj(lo, width):
            return _dot(h, w_ref[:, lo:lo + width].astype(BF16))

        def heads_out(ref, val):
            for i in range(HG_HEADS):
                ref[0, i, rows, :] = val[:, i * HG_DIM:(i + 1) * HG_DIM].astype(ref.dtype)

        cos = cos_ref[rows, :]
        sin = sin_ref[rows, :]
        c0 = 0
        q = _head_norm_rope(proj(c0, ATT_Q_DIM), seg_ref[...], qg_ref[...],
                            jnp.concatenate([cos] * (ATT_Q_DIM // LANES), axis=-1),
                            jnp.concatenate([sin] * (ATT_Q_DIM // LANES), axis=-1))
        q_t = jnp.transpose(q)
        for i in range(ATT_HEADS):
            qt_ref[0, i, :, rows] = q_t[i * ATT_HEAD_DIM:(i + 1) * ATT_HEAD_DIM, :].astype(BF16)
        c0 += ATT_Q_DIM
        yield
        k = _head_norm_rope(proj(c0, ATT_KV_DIM), seg_ref[...], kg_ref[...], cos, sin)
        c0 += ATT_KV_DIM
        v_t = jnp.transpose(proj(c0, ATT_KV_DIM))
        c0 += ATT_KV_DIM
        for i in range(ATT_KV_HEADS):
            k_ref[0, i, rows, :] = k[:, i * ATT_HEAD_DIM:(i + 1) * ATT_HEAD_DIM].astype(BF16)
            vt_ref[0, i, :, rows] = v_t[i * ATT_HEAD_DIM:(i + 1) * ATT_HEAD_DIM, :].astype(BF16)
        yield
        heads_out(hq_ref, _silu(proj(c0, HG_WIDTH)))
        c0 += HG_WIDTH
        yield
        for d, ref in enumerate((lff_ref, lfb_ref)):
            lb = lb_ref[d:d + 1, :]
            z = proj(c0, HG_WIDTH)
            heads_out(ref, jnp.log(lb + (1.0 - lb) / (1.0 + jnp.exp(-z))) * LOG2E)
            c0 += HG_WIDTH
            yield
        heads_out(hi_ref, proj(c0, HG_WIDTH))
        c0 += HG_WIDTH
        yield
        hg_ref[0, rows, :] = _silu(proj(c0, HG_WIDTH))

    for _ in itertools.zip_longest(*[rows_of(j * part) for j in range(IN_ROW_PARTS)]):
        pass


def _in_proj(x, g, w_in, qg, kg, cos, sin, lb, seg, mem, mem_g, w_xkv, tm):
    b, n, d = x.shape
    nt = n // tm
    n_in = w_in.shape[1]
    nm = mem.shape[1]
    kv_cols = w_xkv.shape[1] // nt
    assert w_xkv.shape[1] % nt == 0 and kv_cols % LANES == 0
    tok = lambda i, j: (i, j, 0)
    head = lambda i, j: (i, 0, j, 0)
    pos = lambda i, j: (j, 0)
    return pl.pallas_call(
        _in_proj_kernel,
        grid=(b, nt),
        in_specs=[pl.BlockSpec((1, tm, d), tok), _const_spec((1, d)), _resident_spec((d, n_in)),
                  _const_spec((1, ATT_Q_DIM)), _const_spec((1, ATT_KV_DIM)),
                  pl.BlockSpec((tm, LANES), pos), pl.BlockSpec((tm, LANES), pos),
                  _const_spec((2, HG_WIDTH)),
                  _const_spec((2 * LANES, LANES)),
                  pl.BlockSpec((1, nm, d), lambda i, j: (i, 0, 0)), _const_spec((1, d)),
                  pl.BlockSpec((d, kv_cols), lambda i, j: (0, j))],
        out_specs=[pl.BlockSpec((1, ATT_HEADS, ATT_HEAD_DIM, tm), lambda i, j: (i, 0, 0, j)),
                   pl.BlockSpec((1, ATT_KV_HEADS, tm, ATT_HEAD_DIM), head),
                   pl.BlockSpec((1, ATT_KV_HEADS, ATT_HEAD_DIM, tm), lambda i, j: (i, 0, 0, j)),
                   pl.BlockSpec((1, HG_HEADS, tm, HG_DIM), head),
                   pl.BlockSpec((1, HG_HEADS, tm, HG_DIM), head),
                   pl.BlockSpec((1, HG_HEADS, tm, HG_DIM), head),
                   pl.BlockSpec((1, HG_HEADS, tm, HG_DIM), head),
                   pl.BlockSpec((1, tm, HG_WIDTH), tok),
                   pl.BlockSpec((1, nm, kv_cols), lambda i, j: (i, 0, j))],
        out_shape=[jax.ShapeDtypeStruct((b, ATT_HEADS, ATT_HEAD_DIM, n), BF16),
                   jax.ShapeDtypeStruct((b, ATT_KV_HEADS, n, ATT_HEAD_DIM), BF16),
                   jax.ShapeDtypeStruct((b, ATT_KV_HEADS, ATT_HEAD_DIM, n), BF16),
                   jax.ShapeDtypeStruct((b, HG_HEADS, n, HG_DIM), BF16),
                   jax.ShapeDtypeStruct((b, HG_HEADS, n, HG_DIM), F32),
                   jax.ShapeDtypeStruct((b, HG_HEADS, n, HG_DIM), F32),
                   jax.ShapeDtypeStruct((b, HG_HEADS, n, HG_DIM), BF16),
                   jax.ShapeDtypeStruct((b, n, HG_WIDTH), F32),
                   jax.ShapeDtypeStruct((b, nm, w_xkv.shape[1]), BF16)],
        compiler_params=_params("parallel", "parallel", vmem_limit=VMEM_LIMIT_WIDE),
        name="in_proj",
    )(x, g, w_in, qg, kg, cos, sin, lb, seg, mem, mem_g, w_xkv)


def _attention_kernel(qt_ref, qtn_ref, k_ref, vt_ref, o_ref, s_even, s_odd):
    n = k_ref.shape[2]
    part = n // ATT_KEY_SPLITS
    parts = [slice(j * part, (j + 1) * part) for j in range(ATT_KEY_SPLITS)]

    def scores(i, s_scr, queries_t=qt_ref):
        for rows in parts:
            s_scr[rows, :] = _dot(k_ref[0, i // ATT_GROUP, rows, :], queries_t[0, i])

    def finish(i, s_scr):
        s = [s_scr[rows, :] for rows in parts]
        m = functools.reduce(jnp.maximum, [jnp.max(x, axis=0, keepdims=True) for x in s])
        p = [jnp.exp2(x - m) for x in s]
        l = sum(jnp.sum(x, axis=0, keepdims=True) for x in p)
        o_t = sum(_dot(vt_ref[0, i // ATT_GROUP, :, rows], x.astype(BF16)) for rows, x in zip(parts, p))
        o_ref[0, i] = (o_t / l).astype(BF16)

    @pl.when(pl.program_id(1) == 0)
    def _():
        scores(0, s_even)

    def pair(j, carry):
        scores(2 * j + 1, s_odd)
        finish(2 * j, s_even)
        scores(2 * j + 2, s_even)
        finish(2 * j + 1, s_odd)
        return carry

    lax.fori_loop(0, ATT_HEADS // 2 - 1, pair, 0)
    scores(ATT_HEADS - 1, s_odd)
    finish(ATT_HEADS - 2, s_even)
    scores(0, s_even, qtn_ref)
    finish(ATT_HEADS - 1, s_odd)


def _attention(q_t, k, v_t, tq):
    b, _, dh, n = q_t.shape
    last = n // tq - 1
    return pl.pallas_call(
        _attention_kernel,
        grid=(b, n // tq),
        in_specs=[pl.BlockSpec((1, ATT_HEADS, dh, tq), lambda i, j: (i, 0, 0, j)),
                  pl.BlockSpec((1, ATT_HEADS, dh, tq), lambda i, j: (i, 0, 0, jnp.minimum(j + 1, last))),
                  pl.BlockSpec((1, ATT_KV_HEADS, n, dh), lambda i, j: (i, 0, 0, 0)),
                  pl.BlockSpec((1, ATT_KV_HEADS, dh, n), lambda i, j: (i, 0, 0, 0))],
        out_specs=pl.BlockSpec((1, ATT_HEADS, dh, tq), lambda i, j: (i, 0, 0, j)),
        out_shape=jax.ShapeDtypeStruct((b, ATT_HEADS, dh, n), BF16),
        scratch_shapes=[pltpu.VMEM((n, tq), F32), pltpu.VMEM((n, tq), F32)],
        compiler_params=_params("parallel", "arbitrary"),
        name="attention",
    )(q_t, q_t, k, v_t)


def _hgrn_levels():
    m, out = 1, []
    while m < HG_CHUNK:
        out.append(m)
        m *= 2
    return out


def _hgrn_tables():
    c = HG_CHUNK
    t = np.arange(c)[:, None]
    u = np.arange(c)[None, :]
    mats, masks, fast = [], [], []
    for reverse in (False, True):
        blocks, pairs = [], [t == u]
        for m in _hgrn_levels():
            base = (t // (2 * m)) * (2 * m)
            if not reverse:
                r = base + m - 1
                block = np.where(t > r, (u > r) & (u <= t), (u > t) & (u <= r))
                roles = ((t // m) % 2 == 1) & ((u // m) % 2 == 0)
            else:
                r = base + m
                block = np.where(t < r, (u >= t) & (u < r), (u >= r) & (u < t))
                roles = ((t // m) % 2 == 0) & ((u // m) % 2 == 1)
            if m < SUBLANES:
                blocks.append(block)
            pairs.append(roles & (t // (2 * m) == u // (2 * m)))
        blocks.append(u >= t if reverse else u <= t)
        mats.append(np.tile(np.concatenate(blocks, axis=0), (1, 2)))
        masks.append(np.stack(pairs))
        fast.append((t // HG_FAST_BLOCK == u // HG_FAST_BLOCK) & (u >= t if reverse else u <= t))
    return tuple(np.stack(x).astype(np.float32) for x in (mats, masks, fast))


def _hgrn_sums(lf2, sum_mat):
    hi = lf2.astype(BF16)
    mid = (lf2 - hi.astype(F32)).astype(BF16)
    return _dot(sum_mat, jnp.concatenate([hi, mid], axis=0))


def _hgrn_chunk(q_ref, lf_ref, v_ref, o_ref, state_ref, sum_mat, pair_mask, fast_mask, b_scr, reverse, fast):
    c = HG_CHUNK
    levels = list(enumerate(_hgrn_levels()))
    n_small = sum(m < SUBLANES for _, m in levels)
    q, lf2, v, state = q_ref[...], lf_ref[...], v_ref[...], state_ref[...]
    qb = q.astype(BF16)
    kb = (1.0 - jnp.exp2(lf2)).astype(BF16)
    if fast:
        b = _hgrn_sums(lf2, sum_mat[n_small * c:, :])
        yield
        b_scr[...] = b
        parts = []
        for g in range(c // HG_FAST_BLOCK):
            lo, up = g * HG_FAST_BLOCK, (g + 1) * HG_FAST_BLOCK
            before = up if reverse else lo - 1
            parts.append(b[lo:up] - b_scr[before:before + 1, :] if 0 <= before < c else b[lo:up])
        loc = jnp.concatenate(parts, axis=0)
        a = _dot_nt(qb * jnp.exp2(loc).astype(BF16), kb * jnp.exp2(-loc).astype(BF16)) * fast_mask[...]
        yield
        levels = [(j, m) for j, m in levels if m >= HG_FAST_BLOCK]
    else:
        sums = _hgrn_sums(lf2, sum_mat[...])
        yield
        b = sums[n_small * c:]
        b_scr[...] = b
        a = _dot_nt(qb, kb) * pair_mask[0]
        yield
    for j, m in levels:
        if m < SUBLANES:
            e = sums[j * c:(j + 1) * c]
        else:
            parts = []
            for g in range(c // (2 * m)):
                lo, mid_row, up = 2 * m * g, 2 * m * g + m, 2 * m * (g + 1)
                r = mid_row if reverse else mid_row - 1
                b_r = b_scr[r:r + 1, :]
                parts += [b[lo:mid_row] - b_r, b_r - b[mid_row:up]] if reverse else \
                         [b_r - b[lo:mid_row], b[mid_row:up] - b_r]
            e = jnp.concatenate(parts, axis=0)
        fac = jnp.exp2(e).astype(BF16)
        a = a + _dot_nt(qb * fac, kb * fac) * pair_mask[j + 1]
        yield
    vb = v.astype(BF16)
    last = 0 if reverse else c - 1
    total = b_scr[last:last + 1, :]
    o_ref[...] = _dot(a.astype(BF16), vb) + _dot_nt(qb * jnp.exp2(b).astype(BF16), state.astype(BF16))
    yield
    k_dec = kb * jnp.exp2(total - b).astype(BF16)
    state_ref[...] = state * jnp.exp2(total) + _dot_tn(vb, k_dec)


def _hgrn_kernel(qf_ref, lff_ref, vf_ref, qb_ref, lfb_ref, vb_ref, sm_ref, pm_ref, fm_ref, *rest):
    n_cast = (len(rest) - 4) // 2
    cast_in, (of_ref, ob_ref), cast_out = rest[:n_cast], rest[n_cast:n_cast + 2], rest[n_cast + 2:2 * n_cast + 2]
    s_ref, b_scr = rest[2 * n_cast + 2:]

    @pl.when(pl.program_id(1) == 0)
    def _():
        s_ref[...] = jnp.zeros_like(s_ref)

    for src, dst in zip(cast_in, cast_out):
        dst[...] = src[...].astype(dst.dtype)

    c = HG_CHUNK
    rows_per_step = qf_ref.shape[2]
    chains = [(d, h) for h in range(HG_HEADS) for d in (0, 1)]
    q_refs, lf_refs, v_refs, o_refs = (qf_ref, qb_ref), (lff_ref, lfb_ref), (vf_ref, vb_ref), (of_ref, ob_ref)

    worst = jnp.zeros((1, HG_DIM), F32)
    for d, h in chains:
        for g in range(rows_per_step // HG_FAST_BLOCK):
            block = lf_refs[d][0, h, g * HG_FAST_BLOCK:(g + 1) * HG_FAST_BLOCK, :]
            worst = jnp.minimum(worst, jnp.sum(block, axis=0, keepdims=True))
    mild = jnp.min(worst) >= -HG_FAST_MAX_LOG2

    def run(fast):
        n_sub = rows_per_step // c
        unroll = min(HG_UNROLL, n_sub) if fast else 1

        def trip(t, carry):
            for u in range(unroll):
                sub = t * unroll + u
                work = []
                for d, h in chains:
                    k = n_sub - 1 - sub if d else sub
                    rows = pl.ds(pl.multiple_of(k * c, c), c)
                    work.append(_hgrn_chunk(q_refs[d].at[0, h, rows, :], lf_refs[d].at[0, h, rows, :],
                                            v_refs[d].at[0, h, rows, :], o_refs[d].at[0, h, rows, :],
                                            s_ref.at[d, h], sm_ref.at[d], pm_ref.at[d], fm_ref.at[d],
                                            b_scr.at[d, h], bool(d), fast))
                for _ in itertools.zip_longest(*work):
                    pass
            return carry

        lax.fori_loop(0, n_sub // unroll, trip, 0)

    pl.when(mild)(lambda: run(True))
    pl.when(jnp.logical_not(mild))(lambda: run(False))


def _hgrn(hq, lf_f, lf_b, hi, to_cast):
    b, nh, n, dk = hq.shape
    rows_per_step = HG_CHUNK * HG_STEP_CHUNKS if n % (HG_CHUNK * HG_STEP_CHUNKS) == 0 else HG_CHUNK
    nc = n // rows_per_step
    blk = (1, nh, rows_per_step, dk)
    fwd = pl.BlockSpec(blk, lambda i, j: (i, 0, j, 0))
    bwd = pl.BlockSpec(blk, lambda i, j: (i, 0, nc - 1 - j, 0))
    out = jax.ShapeDtypeStruct((b, nh, n, dk), F32)
    tables = _hgrn_tables()
    steps = b * nc

    def slab_spec(w):
        hold = next(h for h in range(1, steps + 1)
                    if steps % h == 0 and w.shape[0] % (steps // h * BF16_SUBLANES) == 0)
        return pl.BlockSpec((w.shape[0] // (steps // hold), w.shape[1]), lambda i, j: ((i * nc + j) // hold, 0))

    slabs = [slab_spec(w) for w in to_cast]
    res = pl.pallas_call(
        _hgrn_kernel,
        grid=(b, nc),
        in_specs=[fwd, fwd, fwd, bwd, bwd, bwd] + [_const_spec(t.shape) for t in tables] + slabs,
        out_specs=[fwd, bwd] + slabs,
        out_shape=[out, out] + [jax.ShapeDtypeStruct(w.shape, BF16) for w in to_cast],
        scratch_shapes=[pltpu.VMEM((2, nh, dk, dk), F32), pltpu.VMEM((2, nh, HG_CHUNK, dk), F32)],
        compiler_params=_params("parallel", "arbitrary"),
        name="hgrn",
    )(hq, lf_f, hi, hq, lf_b, hi, jnp.asarray(tables[0], BF16), jnp.asarray(tables[1]), jnp.asarray(tables[2]),
      *to_cast)
    return res[0], res[1], res[2:]


def _mix_cross_kernel(x_ref, att_ref, of_ref, ob_ref, sg_ref, og_ref, wo_ref, pmg_ref, pxg_ref,
                      wq_ref, km_ref, vm_ref, wxo_ref, poxg_ref, pfg_ref, x2_ref, h3_ref):
    tm = x_ref.shape[1]
    part = tm // MIX_ROW_PARTS

    def rows_of(lo):
        rows = slice(lo, lo + part)
        rec = []
        for i in range(HG_HEADS):
            o = _rms(of_ref[0, i, rows, :] + ob_ref[0, i, rows, :], og_ref[...])
            rec.append((o * sg_ref[0, rows, i * HG_DIM:(i + 1) * HG_DIM]).astype(BF16))
        att_t = att_ref[0, :, :, rows].reshape(ATT_Q_DIM, part)
        mixed = _dot_tn(att_t, wo_ref[:ATT_Q_DIM, :].astype(BF16)) + \
            _dot(jnp.concatenate(rec, axis=-1), wo_ref[ATT_Q_DIM:, :].astype(BF16))
        yield
        x1 = x_ref[0, rows, :] + _rms(mixed, pmg_ref[...])
        h2 = _rms(x1, pxg_ref[...]).astype(BF16)
        d = h2.shape[-1]
        dh = d // X_HEADS
        q = _dot(h2, wq_ref[...].astype(BF16)) * (dh ** -0.5)
        yield
        heads = []
        for i in range(X_HEADS):
            sl = slice(i * dh, (i + 1) * dh)
            s = _dot_nt(q[:, sl].astype(BF16), km_ref[0, :, sl])
            p = jnp.exp(s - jnp.max(s, axis=-1, keepdims=True))
            l = jnp.sum(p, axis=-1, keepdims=True)
            heads.append((_dot(p.astype(BF16), vm_ref[0, :, sl]) / l).astype(BF16))
        yield
        xo = _dot(jnp.concatenate(heads, axis=-1), wxo_ref[...].astype(BF16))
        yield
        x2 = x1 + _rms(xo, poxg_ref[...])
        x2_ref[0, rows, :] = x2
        h3_ref[0, rows, :] = _rms(x2, pfg_ref[...]).astype(BF16)

    for _ in itertools.zip_longest(*[rows_of(j * part) for j in range(MIX_ROW_PARTS)]):
        pass


def _mix_cross(x, att, o_f, o_b, sg, og, w_out, pmg, pxg, w_xq, kv_mem, w_xo, poxg, pfg, tm):
    b, n, d = x.shape
    nm = kv_mem.shape[1]
    tok = lambda i, j: (i, j, 0)
    head = lambda i, j: (i, 0, j, 0)
    batch = lambda i, j: (i, 0, 0)
    vec = _const_spec((1, d))
    mat = _resident_spec((d, d))
    return pl.pallas_call(
        _mix_cross_kernel,
        grid=(b, n // tm),
        in_specs=[pl.BlockSpec((1, tm, d), tok),
                  pl.BlockSpec((1, ATT_HEADS, ATT_HEAD_DIM, tm), lambda i, j: (i, 0, 0, j)),
                  pl.BlockSpec((1, HG_HEADS, tm, HG_DIM), head), pl.BlockSpec((1, HG_HEADS, tm, HG_DIM), head),
                  pl.BlockSpec((1, tm, HG_WIDTH), tok), _const_spec((1, HG_DIM)),
                  mat, vec, vec, mat,
                  pl.BlockSpec((1, nm, d), batch), pl.BlockSpec((1, nm, d), lambda i, j: (i, 0, 1)),
                  mat, vec, vec],
        out_specs=[pl.BlockSpec((1, tm, d), tok), pl.BlockSpec((1, tm, d), tok)],
        out_shape=[jax.ShapeDtypeStruct((b, n, d), F32), jax.ShapeDtypeStruct((b, n, d), BF16)],
        compiler_params=_params("parallel", "parallel", vmem_limit=VMEM_LIMIT_WIDE),
        name="mix_cross",
    )(x, att, o_f, o_b, sg, og, w_out, pmg, pxg, w_xq, kv_mem, kv_mem, w_xo, poxg, pfg)


def _conv_ffn_kernel(x_ref, h_ref, hp_ref, hn_ref, wu_ref, cw_ref, cb_ref, wd_ref, g_ref, o_ref, act_scr):
    j = pl.program_id(1)
    tm = h_ref.shape[1]
    d_ff = wd_ref.shape[0]
    ck = FF_CHUNK
    prev = jnp.where(j > 0, hp_ref[0], jnp.zeros_like(hp_ref[0]))
    nxt = jnp.where(j < pl.num_programs(1) - 1, hn_ref[0], jnp.zeros_like(hn_ref[0]))
    hext = jnp.concatenate([prev, h_ref[0], nxt], axis=0)

    def conv(col):
        u = _dot(hext, wu_ref[:, col:col + ck])
        taps = (pltpu.roll(u, 1, 0), u, pltpu.roll(u, u.shape[0] - 1, 0))
        out = cb_ref[:, col:col + ck]
        for t in range(CONV_W):
            out = out + taps[t][HALO:HALO + tm] * cw_ref[t:t + 1, col:col + ck]
        return out

    for c in range(d_ff // ck):
        act_scr[:, c * ck:(c + 1) * ck] = (_silu(conv(c * ck)) * conv(d_ff + c * ck)).astype(BF16)
    o_ref[0] = x_ref[0] + _rms(_dot(act_scr[...], wd_ref[...]), g_ref[...])


def _conv_ffn(x2, h3, w_up, conv_w, conv_b, wd, g, tm):
    b, n, d = x2.shape
    d_ff = wd.shape[0]
    ck = FF_CHUNK
    hb = tm // HALO
    last = n // HALO - 1
    tok = lambda i, j: (i, j, 0)
    full = _resident_spec
    return pl.pallas_call(
        _conv_ffn_kernel,
        grid=(b, n // tm),
        in_specs=[pl.BlockSpec((1, tm, d), tok), pl.BlockSpec((1, tm, d), tok),
                  pl.BlockSpec((1, HALO, d), lambda i, j: (i, jnp.maximum(j * hb - 1, 0), 0)),
                  pl.BlockSpec((1, HALO, d), lambda i, j: (i, jnp.minimum((j + 1) * hb, last), 0)),
                  full((d, 2 * d_ff)), full((CONV_W, 2 * d_ff)), full((1, 2 * d_ff)),
                  full((d_ff, d)), _const_spec((1, d))],
        out_specs=pl.BlockSpec((1, tm, d), tok),
        out_shape=jax.ShapeDtypeStruct((b, n, d), F32),
        scratch_shapes=[pltpu.VMEM((tm, d_ff), BF16)],
        compiler_params=_params("parallel", "parallel"),
        name="conv_ffn",
    )(x2, h3, h3, h3, w_up, conv_w, conv_b, wd, g)


def _rope_tables(n):
    pairs = ATT_HEAD_DIM // 4
    pos = np.arange(n)
    inv = np.power(np.float32(ROPE_THETA), -np.arange(pairs, dtype=np.float32) / np.float32(pairs))
    ang = np.concatenate([(pos // GRID_W).astype(np.float32)[:, None] * inv,
                          (pos % GRID_W).astype(np.float32)[:, None] * inv], axis=-1)
    cos = np.repeat(np.cos(ang), 2, axis=-1)
    sin = np.repeat(np.sin(ang), 2, axis=-1) * np.tile(np.array([-1.0, 1.0], np.float32), ATT_HEAD_DIM // 2)
    return jnp.asarray(np.tile(cos, (1, 2)), F32), jnp.asarray(np.tile(sin, (1, 2)), F32)


def _segment_ones():
    i = jnp.arange(LANES) // ATT_HEAD_DIM
    return jnp.tile((i[:, None] == i[None, :]).astype(BF16), (2, 1))


def _layer(x, mem, lb, pre_mix_g, w_in, q_norm_g, k_norm_g, hg_out_norm_g, w_out, post_mix_g, pre_x_g,
           mem_norm_g, w_xq, w_xkv, w_xo, post_x_g, pre_ffn_g, w_up, conv_w, conv_b, w_down, post_ffn_g):
    b, n, d = x.shape
    d_ff = w_down.shape[0]
    assert n % HG_CHUNK == 0 and n % GRID_W == 0 and d_ff % FF_CHUNK == 0
    tm = min(256, n)
    tf = min(512, n)
    row = lambda g: g.reshape(1, -1).astype(F32)

    cos, sin = _rope_tables(n)
    qg = jnp.tile(q_norm_g, ATT_HEADS).reshape(1, -1) * (ATT_HEAD_DIM ** -0.5 * LOG2E)
    kg = jnp.tile(k_norm_g, ATT_KV_HEADS).reshape(1, -1)

    q_t, k, v_t, hq, lf_f, lf_b, hi, sg, kv_mem = _in_proj(
        x, row(pre_mix_g), w_in, qg, kg, cos, sin, lb,
        _segment_ones(), mem, row(mem_norm_g), w_xkv, min(1024, n))
    att = _attention(q_t, k, v_t, tm)
    o_f, o_b, (w_up_bf16, w_down_bf16) = _hgrn(hq, lf_f, lf_b, hi, [w_up, w_down])
    x2, h3 = _mix_cross(x, att, o_f, o_b, sg, row(hg_out_norm_g), w_out, row(post_mix_g),
                        row(pre_x_g), w_xq, kv_mem, w_xo,
                        row(post_x_g), row(pre_ffn_g), min(1024, n))

    return _conv_ffn(x2, h3, w_up_bf16, conv_w, row(conv_b), w_down_bf16, row(post_ffn_g), tf)


def kernel(x, mem, pre_mix_g, w_in, q_norm_g, k_norm_g, hg_lb, hg_out_norm_g, w_out, post_mix_g, pre_x_g,
           mem_norm_g, w_xq, w_xkv, w_xo, post_x_g, pre_ffn_g, w_up, conv_w, conv_b, w_down, post_ffn_g):
    lb_all = jnp.cumsum(jax.nn.softmax(hg_lb.astype(F32), axis=1), axis=1)
    for l in range(w_in.shape[0]):
        x = _layer(x, mem, lb_all[:, l], pre_mix_g[l], w_in[l], q_norm_g[l], k_norm_g[l], hg_out_norm_g[l],
                   w_out[l], post_mix_g[l], pre_x_g[l], mem_norm_g[l], w_xq[l], w_xkv[l], w_xo[l],
                   post_x_g[l], pre_ffn_g[l], w_up[l], conv_w[l], conv_b[l], w_down[l], post_ffn_g[l])
    return x
```

```python
import functools
import itertools
import math

import numpy as np
import jax
import jax.numpy as jnp
from jax import lax
from jax.experimental import pallas as pl
from jax.experimental.pallas import tpu as pltpu

F32 = jnp.float32
BF16 = jnp.bfloat16

EPS = 1e-6
LOG2E = math.log2(math.e)
GRID_W = 64
ROPE_THETA = 10000.0

ATT_HEADS = 8
ATT_KV_HEADS = 2
ATT_GROUP = ATT_HEADS // ATT_KV_HEADS
ATT_HEAD_DIM = 64
ATT_Q_DIM = ATT_HEADS * ATT_HEAD_DIM
ATT_KV_DIM = ATT_KV_HEADS * ATT_HEAD_DIM
ATT_KEY_SPLITS = 2

HG_HEADS = 4
HG_DIM = 128
HG_WIDTH = HG_HEADS * HG_DIM
HG_CHUNK = 128
HG_STEP_CHUNKS = 8
HG_UNROLL = 4
HG_FAST_BLOCK = 64
HG_FAST_MAX_LOG2 = 96.0

X_HEADS = 4
IN_ROW_PARTS = 2
MIX_ROW_PARTS = 4
CONV_W = 3
FF_CHUNK = 256

VMEM_LIMIT = 56 * 1024 * 1024
VMEM_LIMIT_WIDE = 61 * 1024 * 1024
SUBLANES = 8
BF16_SUBLANES = 16
LANES = 128

HALO = BF16_SUBLANES


def _params(*sem, vmem_limit=VMEM_LIMIT):
    return pltpu.CompilerParams(dimension_semantics=sem, vmem_limit_bytes=vmem_limit)


def _rms(x, g):
    ms = jnp.mean(x * x, axis=-1, keepdims=True)
    return x * lax.rsqrt(ms + EPS) * g


def _silu(x):
    return x / (1.0 + jnp.exp(-x))


def _dot(a, b):
    return jnp.dot(a, b, preferred_element_type=F32)


def _dot_nt(a, b):
    return lax.dot_general(a, b, (((1,), (1,)), ((), ())), preferred_element_type=F32)


def _dot_tn(a, b):
    return lax.dot_general(a, b, (((0,), (0,)), ((), ())), preferred_element_type=F32)


def _const_spec(shape):
    return pl.BlockSpec(shape, lambda *_: (0,) * len(shape))


def _resident_spec(shape):
    return pl.BlockSpec(shape, lambda *_: (0,) * len(shape), pipeline_mode=pl.Buffered(1))


def _head_norm_rope(a, seg, gain, cos, sin_signed):
    w = a.shape[-1]
    sq = a * a
    hi = sq.astype(BF16)
    lo = (sq - hi.astype(F32)).astype(BF16)
    ss = [_dot(jnp.concatenate([hi[:, j:j + LANES], lo[:, j:j + LANES]], axis=-1), seg) for j in range(0, w, LANES)]
    ss = jnp.concatenate(ss, axis=-1) if len(ss) > 1 else ss[0]
    an = a * lax.rsqrt(ss * (1.0 / ATT_HEAD_DIM) + EPS) * gain
    lane = lax.broadcasted_iota(jnp.int32, a.shape, 1)
    partner = jnp.where(lane % 2 == 0, pltpu.roll(an, w - 1, 1), pltpu.roll(an, 1, 1))
    return an * cos + partner * sin_signed


def _in_proj_kernel(x_ref, g_ref, w_ref, qg_ref, kg_ref, cos_ref, sin_ref, lb_ref, seg_ref,
                    mem_ref, mg_ref, wkv_ref,
                    qt_ref, k_ref, vt_ref, hq_ref, lff_ref, lfb_ref, hi_ref, hg_ref, kv_ref):
    part = x_ref.shape[1] // IN_ROW_PARTS

    kv_ref[0] = _dot(_rms(mem_ref[0], mg_ref[...]).astype(BF16), wkv_ref[...].astype(BF16)).astype(BF16)

    def rows_of(r0):
        rows = slice(r0, r0 + part)
        h = _rms(x_ref[0, rows, :], g_ref[...]).astype(BF16)

        def proj(lo, width):
            return _dot(h, w_ref[:, lo:lo + width].astype(BF16))

        def heads_out(ref, val):
            for i in range(HG_HEADS):
                ref[0, i, rows, :] = val[:, i * HG_DIM:(i + 1) * HG_DIM].astype(ref.dtype)

        cos = cos_ref[rows, :]
        sin = sin_ref[rows, :]
        c0 = 0
        q = _head_norm_rope(proj(c0, ATT_Q_DIM), seg_ref[...], qg_ref[...],
                            jnp.concatenate([cos] * (ATT_Q_DIM // LANES), axis=-1),
                            jnp.concatenate([sin] * (ATT_Q_DIM // LANES), axis=-1))
        q_t = jnp.transpose(q)
        for i in range(ATT_HEADS):
            qt_ref[0, i, :, rows] = q_t[i * ATT_HEAD_DIM:(i + 1) * ATT_HEAD_DIM, :].astype(BF16)
        c0 += ATT_Q_DIM
        yield
        k = _head_norm_rope(proj(c0, ATT_KV_DIM), seg_ref[...], kg_ref[...], cos, sin)
        c0 += ATT_KV_DIM
        v_t = jnp.transpose(proj(c0, ATT_KV_DIM))
        c0 += ATT_KV_DIM
        for i in range(ATT_KV_HEADS):
            k_ref[0, i, rows, :] = k[:, i * ATT_HEAD_DIM:(i + 1) * ATT_HEAD_DIM].astype(BF16)
            vt_ref[0, i, :, rows] = v_t[i * ATT_HEAD_DIM:(i + 1) * ATT_HEAD_DIM, :].astype(BF16)
        yield
        heads_out(hq_ref, _silu(proj(c0, HG_WIDTH)))
        c0 += HG_WIDTH
        yield
        for d, ref in enumerate((lff_ref, lfb_ref)):
            lb = lb_ref[d:d + 1, :]
            z = proj(c0, HG_WIDTH)
            heads_out(ref, jnp.log(lb + (1.0 - lb) / (1.0 + jnp.exp(-z))) * LOG2E)
            c0 += HG_WIDTH
            yield
        heads_out(hi_ref, proj(c0, HG_WIDTH))
        c0 += HG_WIDTH
        yield
        hg_ref[0, rows, :] = _silu(proj(c0, HG_WIDTH))

    for _ in itertools.zip_longest(*[rows_of(j * part) for j in range(IN_ROW_PARTS)]):
        pass


def _in_proj(x, g, w_in, qg, kg, cos, sin, lb, seg, mem, mem_g, w_xkv, tm):
    b, n, d = x.shape
    nt = n // tm
    n_in = w_in.shape[1]
    nm = mem.shape[1]
    kv_cols = w_xkv.shape[1] // nt
    assert w_xkv.shape[1] % nt == 0 and kv_cols % LANES == 0
    tok = lambda i, j: (i, j, 0)
    head = lambda i, j: (i, 0, j, 0)
    pos = lambda i, j: (j, 0)
    return pl.pallas_call(
        _in_proj_kernel,
        grid=(b, nt),
        in_specs=[pl.BlockSpec((1, tm, d), tok), _const_spec((1, d)), _resident_spec((d, n_in)),
                  _const_spec((1, ATT_Q_DIM)), _const_spec((1, ATT_KV_DIM)),
                  pl.BlockSpec((tm, LANES), pos), pl.BlockSpec((tm, LANES), pos),
                  _const_spec((2, HG_WIDTH)),
                  _const_spec((2 * LANES, LANES)),
                  pl.BlockSpec((1, nm, d), lambda i, j: (i, 0, 0)), _const_spec((1, d)),
                  pl.BlockSpec((d, kv_cols), lambda i, j: (0, j))],
        out_specs=[pl.BlockSpec((1, ATT_HEADS, ATT_HEAD_DIM, tm), lambda i, j: (i, 0, 0, j)),
                   pl.BlockSpec((1, ATT_KV_HEADS, tm, ATT_HEAD_DIM), head),
                   pl.BlockSpec((1, ATT_KV_HEADS, ATT_HEAD_DIM, tm), lambda i, j: (i, 0, 0, j)),
                   pl.BlockSpec((1, HG_HEADS, tm, HG_DIM), head),
                   pl.BlockSpec((1, HG_HEADS, tm, HG_DIM), head),
                   pl.BlockSpec((1, HG_HEADS, tm, HG_DIM), head),
                   pl.BlockSpec((1, HG_HEADS, tm, HG_DIM), head),
                   pl.BlockSpec((1, tm, HG_WIDTH), tok),
                   pl.BlockSpec((1, nm, kv_cols), lambda i, j: (i, 0, j))],
        out_shape=[jax.ShapeDtypeStruct((b, ATT_HEADS, ATT_HEAD_DIM, n), BF16),
                   jax.ShapeDtypeStruct((b, ATT_KV_HEADS, n, ATT_HEAD_DIM), BF16),
                   jax.ShapeDtypeStruct((b, ATT_KV_HEADS, ATT_HEAD_DIM, n), BF16),
                   jax.ShapeDtypeStruct((b, HG_HEADS, n, HG_DIM), BF16),
                   jax.ShapeDtypeStruct((b, HG_HEADS, n, HG_DIM), F32),
                   jax.ShapeDtypeStruct((b, HG_HEADS, n, HG_DIM), F32),
                   jax.ShapeDtypeStruct((b, HG_HEADS, n, HG_DIM), BF16),
                   jax.ShapeDtypeStruct((b, n, HG_WIDTH), F32),
                   jax.ShapeDtypeStruct((b, nm, w_xkv.shape[1]), BF16)],
        compiler_params=_params("parallel", "parallel"),
        name="in_proj",
    )(x, g, w_in, qg, kg, cos, sin, lb, seg, mem, mem_g, w_xkv)


def _attention_kernel(qt_ref, qtn_ref, k_ref, vt_ref, o_ref, s_even, s_odd):
    n = k_ref.shape[2]
    part = n // ATT_KEY_SPLITS
    parts = [slice(j * part, (j + 1) * part) for j in range(ATT_KEY_SPLITS)]

    def scores(i, s_scr, queries_t=qt_ref):
        for rows in parts:
            s_scr[rows, :] = _dot(k_ref[0, i // ATT_GROUP, rows, :], queries_t[0, i])

    def finish(i, s_scr):
        s = [s_scr[rows, :] for rows in parts]
        m = functools.reduce(jnp.maximum, [jnp.max(x, axis=0, keepdims=True) for x in s])
        p = [jnp.exp2(x - m) for x in s]
        l = sum(jnp.sum(x, axis=0, keepdims=True) for x in p)
        o_t = sum(_dot(vt_ref[0, i // ATT_GROUP, :, rows], x.astype(BF16)) for rows, x in zip(parts, p))
        o_ref[0, i] = (o_t / l).astype(BF16)

    @pl.when(pl.program_id(1) == 0)
    def _():
        scores(0, s_even)

    def pair(j, carry):
        scores(2 * j + 1, s_odd)
        finish(2 * j, s_even)
        scores(2 * j + 2, s_even)
        finish(2 * j + 1, s_odd)
        return carry

    lax.fori_loop(0, ATT_HEADS // 2 - 1, pair, 0)
    scores(ATT_HEADS - 1, s_odd)
    finish(ATT_HEADS - 2, s_even)
    scores(0, s_even, qtn_ref)
    finish(ATT_HEADS - 1, s_odd)


def _attention(q_t, k, v_t, tq):
    b, _, dh, n = q_t.shape
    last = n // tq - 1
    return pl.pallas_call(
        _attention_kernel,
        grid=(b, n // tq),
        in_specs=[pl.BlockSpec((1, ATT_HEADS, dh, tq), lambda i, j: (i, 0, 0, j)),
                  pl.BlockSpec((1, ATT_HEADS, dh, tq), lambda i, j: (i, 0, 0, jnp.minimum(j + 1, last))),
                  pl.BlockSpec((1, ATT_KV_HEADS, n, dh), lambda i, j: (i, 0, 0, 0)),
                  pl.BlockSpec((1, ATT_KV_HEADS, dh, n), lambda i, j: (i, 0, 0, 0))],
        out_specs=pl.BlockSpec((1, ATT_HEADS, dh, tq), lambda i, j: (i, 0, 0, j)),
        out_shape=jax.ShapeDtypeStruct((b, ATT_HEADS, dh, n), BF16),
        scratch_shapes=[pltpu.VMEM((n, tq), F32), pltpu.VMEM((n, tq), F32)],
        compiler_params=_params("parallel", "arbitrary"),
        name="attention",
    )(q_t, q_t, k, v_t)


def _hgrn_levels():
    m, out = 1, []
    while m < HG_CHUNK:
        out.append(m)
        m *= 2
    return out


def _hgrn_tables():
    c = HG_CHUNK
    t = np.arange(c)[:, None]
    u = np.arange(c)[None, :]
    mats, masks, fast = [], [], []
    for reverse in (False, True):
        blocks, pairs = [], [t == u]
        for m in _hgrn_levels():
            base = (t // (2 * m)) * (2 * m)
            if not reverse:
                r = base + m - 1
                block = np.where(t > r, (u > r) & (u <= t), (u > t) & (u <= r))
                roles = ((t // m) % 2 == 1) & ((u // m) % 2 == 0)
            else:
                r = base + m
                block = np.where(t < r, (u >= t) & (u < r), (u >= r) & (u < t))
                roles = ((t // m) % 2 == 0) & ((u // m) % 2 == 1)
            if m < SUBLANES:
                blocks.append(block)
            pairs.append(roles & (t // (2 * m) == u // (2 * m)))
        blocks.append(u >= t if reverse else u <= t)
        mats.append(np.tile(np.concatenate(blocks, axis=0), (1, 2)))
        masks.append(np.stack(pairs))
        fast.append((t // HG_FAST_BLOCK == u // HG_FAST_BLOCK) & (u >= t if reverse else u <= t))
    return tuple(np.stack(x).astype(np.float32) for x in (mats, masks, fast))


def _hgrn_sums(lf2, sum_mat):
    hi = lf2.astype(BF16)
    mid = (lf2 - hi.astype(F32)).astype(BF16)
    return _dot(sum_mat, jnp.concatenate([hi, mid], axis=0))


def _hgrn_chunk(q_ref, lf_ref, v_ref, o_ref, state_ref, sum_mat, pair_mask, fast_mask, b_scr, reverse, fast):
    c = HG_CHUNK
    levels = list(enumerate(_hgrn_levels()))
    n_small = sum(m < SUBLANES for _, m in levels)
    q, lf2, v, state = q_ref[...], lf_ref[...], v_ref[...], state_ref[...]
    qb = q.astype(BF16)
    kb = (1.0 - jnp.exp2(lf2)).astype(BF16)
    if fast:
        b = _hgrn_sums(lf2, sum_mat[n_small * c:, :])
        yield
        b_scr[...] = b
        parts = []
        for g in range(c // HG_FAST_BLOCK):
            lo, up = g * HG_FAST_BLOCK, (g + 1) * HG_FAST_BLOCK
            before = up if reverse else lo - 1
            parts.append(b[lo:up] - b_scr[before:before + 1, :] if 0 <= before < c else b[lo:up])
        loc = jnp.concatenate(parts, axis=0)
        a = _dot_nt(qb * jnp.exp2(loc).astype(BF16), kb * jnp.exp2(-loc).astype(BF16)) * fast_mask[...]
        yield
        levels = [(j, m) for j, m in levels if m >= HG_FAST_BLOCK]
    else:
        sums = _hgrn_sums(lf2, sum_mat[...])
        yield
        b = sums[n_small * c:]
        b_scr[...] = b
        a = _dot_nt(qb, kb) * pair_mask[0]
        yield
    for j, m in levels:
        if m < SUBLANES:
            e = sums[j * c:(j + 1) * c]
        else:
            parts = []
            for g in range(c // (2 * m)):
                lo, mid_row, up = 2 * m * g, 2 * m * g + m, 2 * m * (g + 1)
                r = mid_row if reverse else mid_row - 1
                b_r = b_scr[r:r + 1, :]
                parts += [b[lo:mid_row] - b_r, b_r - b[mid_row:up]] if reverse else \
                         [b_r - b[lo:mid_row], b[mid_row:up] - b_r]
            e = jnp.concatenate(parts, axis=0)
        fac = jnp.exp2(e).astype(BF16)
        a = a + _dot_nt(qb * fac, kb * fac) * pair_mask[j + 1]
        yield
    vb = v.astype(BF16)
    last = 0 if reverse else c - 1
    total = b_scr[last:last + 1, :]
    o_ref[...] = _dot(a.astype(BF16), vb) + _dot_nt(qb * jnp.exp2(b).astype(BF16), state.astype(BF16))
    yield
    k_dec = kb * jnp.exp2(total - b).astype(BF16)
    state_ref[...] = state * jnp.exp2(total) + _dot_tn(vb, k_dec)


def _hgrn_kernel(qf_ref, lff_ref, vf_ref, qb_ref, lfb_ref, vb_ref, sm_ref, pm_ref, fm_ref, *rest):
    n_cast = (len(rest) - 4) // 2
    cast_in, (of_ref, ob_ref), cast_out = rest[:n_cast], rest[n_cast:n_cast + 2], rest[n_cast + 2:2 * n_cast + 2]
    s_ref, b_scr = rest[2 * n_cast + 2:]

    @pl.when(pl.program_id(1) == 0)
    def _():
        s_ref[...] = jnp.zeros_like(s_ref)

    for src, dst in zip(cast_in, cast_out):
        dst[...] = src[...].astype(dst.dtype)

    c = HG_CHUNK
    rows_per_step = qf_ref.shape[2]
    chains = [(d, h) for h in range(HG_HEADS) for d in (0, 1)]
    q_refs, lf_refs, v_refs, o_refs = (qf_ref, qb_ref), (lff_ref, lfb_ref), (vf_ref, vb_ref), (of_ref, ob_ref)

    worst = jnp.zeros((1, HG_DIM), F32)
    for d, h in chains:
        for g in range(rows_per_step // HG_FAST_BLOCK):
            block = lf_refs[d][0, h, g * HG_FAST_BLOCK:(g + 1) * HG_FAST_BLOCK, :]
            worst = jnp.minimum(worst, jnp.sum(block, axis=0, keepdims=True))
    mild = jnp.min(worst) >= -HG_FAST_MAX_LOG2

    def run(fast):
        n_sub = rows_per_step // c
        unroll = min(HG_UNROLL, n_sub) if fast else 1

        def trip(t, carry):
            for u in range(unroll):
                sub = t * unroll + u
                work = []
                for d, h in chains:
                    k = n_sub - 1 - sub if d else sub
                    rows = pl.ds(pl.multiple_of(k * c, c), c)
                    work.append(_hgrn_chunk(q_refs[d].at[0, h, rows, :], lf_refs[d].at[0, h, rows, :],
                                            v_refs[d].at[0, h, rows, :], o_refs[d].at[0, h, rows, :],
                                            s_ref.at[d, h], sm_ref.at[d], pm_ref.at[d], fm_ref.at[d],
                                            b_scr.at[d, h], bool(d), fast))
                for _ in itertools.zip_longest(*work):
                    pass
            return carry

        lax.fori_loop(0, n_sub // unroll, trip, 0)

    pl.when(mild)(lambda: run(True))
    pl.when(jnp.logical_not(mild))(lambda: run(False))


def _hgrn(hq, lf_f, lf_b, hi, to_cast):
    b, nh, n, dk = hq.shape
    rows_per_step = HG_CHUNK * HG_STEP_CHUNKS if n % (HG_CHUNK * HG_STEP_CHUNKS) == 0 else HG_CHUNK
    nc = n // rows_per_step
    blk = (1, nh, rows_per_step, dk)
    fwd = pl.BlockSpec(blk, lambda i, j: (i, 0, j, 0))
    bwd = pl.BlockSpec(blk, lambda i, j: (i, 0, nc - 1 - j, 0))
    out = jax.ShapeDtypeStruct((b, nh, n, dk), F32)
    tables = _hgrn_tables()
    steps = b * nc

    def slab_spec(w):
        hold = next(h for h in range(1, steps + 1)
                    if steps % h == 0 and w.shape[0] % (steps // h * BF16_SUBLANES) == 0)
        return pl.BlockSpec((w.shape[0] // (steps // hold), w.shape[1]), lambda i, j: ((i * nc + j) // hold, 0))

    slabs = [slab_spec(w) for w in to_cast]
    res = pl.pallas_call(
        _hgrn_kernel,
        grid=(b, nc),
        in_specs=[fwd, fwd, fwd, bwd, bwd, bwd] + [_const_spec(t.shape) for t in tables] + slabs,
        out_specs=[fwd, bwd] + slabs,
        out_shape=[out, out] + [jax.ShapeDtypeStruct(w.shape, BF16) for w in to_cast],
        scratch_shapes=[pltpu.VMEM((2, nh, dk, dk), F32), pltpu.VMEM((2, nh, HG_CHUNK, dk), F32)],
        compiler_params=_params("parallel", "arbitrary"),
        name="hgrn",
    )(hq, lf_f, hi, hq, lf_b, hi, jnp.asarray(tables[0], BF16), jnp.asarray(tables[1]), jnp.asarray(tables[2]),
      *to_cast)
    return res[0], res[1], res[2:]


def _mix_cross_kernel(x_ref, att_ref, of_ref, ob_ref, sg_ref, og_ref, wo_ref, pmg_ref, pxg_ref,
                      wq_ref, km_ref, vm_ref, wxo_ref, poxg_ref, pfg_ref, x2_ref, h3_ref):
    tm = x_ref.shape[1]
    part = tm // MIX_ROW_PARTS

    def rows_of(lo):
        rows = slice(lo, lo + part)
        rec = []
        for i in range(HG_HEADS):
            o = _rms(of_ref[0, i, rows, :] + ob_ref[0, i, rows, :], og_ref[...])
            rec.append((o * sg_ref[0, rows, i * HG_DIM:(i + 1) * HG_DIM]).astype(BF16))
        att_t = att_ref[0, :, :, rows].reshape(ATT_Q_DIM, part)
        mixed = _dot_tn(att_t, wo_ref[:ATT_Q_DIM, :].astype(BF16)) + \
            _dot(jnp.concatenate(rec, axis=-1), wo_ref[ATT_Q_DIM:, :].astype(BF16))
        yield
        x1 = x_ref[0, rows, :] + _rms(mixed, pmg_ref[...])
        h2 = _rms(x1, pxg_ref[...]).astype(BF16)
        d = h2.shape[-1]
        dh = d // X_HEADS
        q = _dot(h2, wq_ref[...].astype(BF16)) * (dh ** -0.5)
        yield
        heads = []
        for i in range(X_HEADS):
            sl = slice(i * dh, (i + 1) * dh)
            s = _dot_nt(q[:, sl].astype(BF16), km_ref[0, :, sl])
            p = jnp.exp(s - jnp.max(s, axis=-1, keepdims=True))
            l = jnp.sum(p, axis=-1, keepdims=True)
            heads.append((_dot(p.astype(BF16), vm_ref[0, :, sl]) / l).astype(BF16))
        yield
        xo = _dot(jnp.concatenate(heads, axis=-1), wxo_ref[...].astype(BF16))
        yield
        x2 = x1 + _rms(xo, poxg_ref[...])
        x2_ref[0, rows, :] = x2
        h3_ref[0, rows, :] = _rms(x2, pfg_ref[...]).astype(BF16)

    for _ in itertools.zip_longest(*[rows_of(j * part) for j in range(MIX_ROW_PARTS)]):
        pass


def _mix_cross(x, att, o_f, o_b, sg, og, w_out, pmg, pxg, w_xq, kv_mem, w_xo, poxg, pfg, tm):
    b, n, d = x.shape
    nm = kv_mem.shape[1]
    tok = lambda i, j: (i, j, 0)
    head = lambda i, j: (i, 0, j, 0)
    batch = lambda i, j: (i, 0, 0)
    vec = _const_spec((1, d))
    mat = _resident_spec((d, d))
    return pl.pallas_call(
        _mix_cross_kernel,
        grid=(b, n // tm),
        in_specs=[pl.BlockSpec((1, tm, d), tok),
                  pl.BlockSpec((1, ATT_HEADS, ATT_HEAD_DIM, tm), lambda i, j: (i, 0, 0, j)),
                  pl.BlockSpec((1, HG_HEADS, tm, HG_DIM), head), pl.BlockSpec((1, HG_HEADS, tm, HG_DIM), head),
                  pl.BlockSpec((1, tm, HG_WIDTH), tok), _const_spec((1, HG_DIM)),
                  mat, vec, vec, mat,
                  pl.BlockSpec((1, nm, d), batch), pl.BlockSpec((1, nm, d), lambda i, j: (i, 0, 1)),
                  mat, vec, vec],
        out_specs=[pl.BlockSpec((1, tm, d), tok), pl.BlockSpec((1, tm, d), tok)],
        out_shape=[jax.ShapeDtypeStruct((b, n, d), F32), jax.ShapeDtypeStruct((b, n, d), BF16)],
        compiler_params=_params("parallel", "parallel", vmem_limit=VMEM_LIMIT_WIDE),
        name="mix_cross",
    )(x, att, o_f, o_b, sg, og, w_out, pmg, pxg, w_xq, kv_mem, kv_mem, w_xo, poxg, pfg)


def _conv_ffn_kernel(x_ref, h_ref, hp_ref, hn_ref, wu_ref, cw_ref, cb_ref, wd_ref, g_ref, o_ref, act_scr):
    j = pl.program_id(1)
    tm = h_ref.shape[1]
    d_ff = wd_ref.shape[0]
    ck = FF_CHUNK
    prev = jnp.where(j > 0, hp_ref[0], jnp.zeros_like(hp_ref[0]))
    nxt = jnp.where(j < pl.num_programs(1) - 1, hn_ref[0], jnp.zeros_like(hn_ref[0]))
    hext = jnp.concatenate([prev, h_ref[0], nxt], axis=0)

    def conv(col):
        u = _dot(hext, wu_ref[:, col:col + ck])
        taps = (pltpu.roll(u, 1, 0), u, pltpu.roll(u, u.shape[0] - 1, 0))
        out = cb_ref[:, col:col + ck]
        for t in range(CONV_W):
            out = out + taps[t][HALO:HALO + tm] * cw_ref[t:t + 1, col:col + ck]
        return out

    for c in range(d_ff // ck):
        act_scr[:, c * ck:(c + 1) * ck] = (_silu(conv(c * ck)) * conv(d_ff + c * ck)).astype(BF16)
    o_ref[0] = x_ref[0] + _rms(_dot(act_scr[...], wd_ref[...]), g_ref[...])


def _conv_ffn(x2, h3, w_up, conv_w, conv_b, wd, g, tm):
    b, n, d = x2.shape
    d_ff = wd.shape[0]
    ck = FF_CHUNK
    hb = tm // HALO
    last = n // HALO - 1
    tok = lambda i, j: (i, j, 0)
    full = _resident_spec
    return pl.pallas_call(
        _conv_ffn_kernel,
        grid=(b, n // tm),
        in_specs=[pl.BlockSpec((1, tm, d), tok), pl.BlockSpec((1, tm, d), tok),
                  pl.BlockSpec((1, HALO, d), lambda i, j: (i, jnp.maximum(j * hb - 1, 0), 0)),
                  pl.BlockSpec((1, HALO, d), lambda i, j: (i, jnp.minimum((j + 1) * hb, last), 0)),
                  full((d, 2 * d_ff)), full((CONV_W, 2 * d_ff)), full((1, 2 * d_ff)),
                  full((d_ff, d)), _const_spec((1, d))],
        out_specs=pl.BlockSpec((1, tm, d), tok),
        out_shape=jax.ShapeDtypeStruct((b, n, d), F32),
        scratch_shapes=[pltpu.VMEM((tm, d_ff), BF16)],
        compiler_params=_params("parallel", "parallel"),
        name="conv_ffn",
    )(x2, h3, h3, h3, w_up, conv_w, conv_b, wd, g)


def _rope_tables(n):
    pairs = ATT_HEAD_DIM // 4
    pos = np.arange(n)
    inv = np.power(np.float32(ROPE_THETA), -np.arange(pairs, dtype=np.float32) / np.float32(pairs))
    ang = np.concatenate([(pos // GRID_W).astype(np.float32)[:, None] * inv,
                          (pos % GRID_W).astype(np.float32)[:, None] * inv], axis=-1)
    cos = np.repeat(np.cos(ang), 2, axis=-1)
    sin = np.repeat(np.sin(ang), 2, axis=-1) * np.tile(np.array([-1.0, 1.0], np.float32), ATT_HEAD_DIM // 2)
    return jnp.asarray(np.tile(cos, (1, 2)), F32), jnp.asarray(np.tile(sin, (1, 2)), F32)


def _segment_ones():
    i = jnp.arange(LANES) // ATT_HEAD_DIM
    return jnp.tile((i[:, None] == i[None, :]).astype(BF16), (2, 1))


def _layer(x, mem, lb, pre_mix_g, w_in, q_norm_g, k_norm_g, hg_out_norm_g, w_out, post_mix_g, pre_x_g,
           mem_norm_g, w_xq, w_xkv, w_xo, post_x_g, pre_ffn_g, w_up, conv_w, conv_b, w_down, post_ffn_g):
    b, n, d = x.shape
    d_ff = w_down.shape[0]
    assert n % HG_CHUNK == 0 and n % GRID_W == 0 and d_ff % FF_CHUNK == 0
    tm = min(256, n)
    tf = min(512, n)
    tx = min(1024, n)
    row = lambda g: g.reshape(1, -1).astype(F32)

    cos, sin = _rope_tables(n)
    qg = jnp.tile(q_norm_g, ATT_HEADS).reshape(1, -1) * (ATT_HEAD_DIM ** -0.5 * LOG2E)
    kg = jnp.tile(k_norm_g, ATT_KV_HEADS).reshape(1, -1)

    q_t, k, v_t, hq, lf_f, lf_b, hi, sg, kv_mem = _in_proj(
        x, row(pre_mix_g), w_in, qg, kg, cos, sin, lb,
        _segment_ones(), mem, row(mem_norm_g), w_xkv, tf)
    att = _attention(q_t, k, v_t, tm)
    o_f, o_b, (w_up_bf16, w_down_bf16) = _hgrn(hq, lf_f, lf_b, hi, [w_up, w_down])
    x2, h3 = _mix_cross(x, att, o_f, o_b, sg, row(hg_out_norm_g), w_out, row(post_mix_g),
                        row(pre_x_g), w_xq, kv_mem, w_xo,
                        row(post_x_g), row(pre_ffn_g), tx)

    return _conv_ffn(x2, h3, w_up_bf16, conv_w, row(conv_b), w_down_bf16, row(post_ffn_g), tf)


def kernel(x, mem, pre_mix_g, w_in, q_norm_g, k_norm_g, hg_lb, hg_out_norm_g, w_out, post_mix_g, pre_x_g,
           mem_norm_g, w_xq, w_xkv, w_xo, post_x_g, pre_ffn_g, w_up, conv_w, conv_b, w_down, post_ffn_g):
    lb_all = jnp.cumsum(jax.nn.softmax(hg_lb.astype(F32), axis=1), axis=1)
    for l in range(w_in.shape[0]):
        x = _layer(x, mem, lb_all[:, l], pre_mix_g[l], w_in[l], q_norm_g[l], k_norm_g[l], hg_out_norm_g[l],
                   w_out[l], post_mix_g[l], pre_x_g[l], mem_norm_g[l], w_xq[l], w_xkv[l], w_xo[l],
                   post_x_g[l], pre_ffn_g[l], w_up[l], conv_w[l], conv_b[l], w_down[l], post_ffn_g[l])
    return x
```

```python
import functools
import itertools
import math

import numpy as np
import jax
import jax.numpy as jnp
from jax import lax
from jax.experimental import pallas as pl
from jax.experimental.pallas import tpu as pltpu

F32 = jnp.float32
BF16 = jnp.bfloat16

EPS = 1e-6
LOG2E = math.log2(math.e)
GRID_W = 64
ROPE_THETA = 10000.0

ATT_HEADS = 8
ATT_KV_HEADS = 2
ATT_GROUP = ATT_HEADS // ATT_KV_HEADS
ATT_HEAD_DIM = 64
ATT_Q_DIM = ATT_HEADS * ATT_HEAD_DIM
ATT_KV_DIM = ATT_KV_HEADS * ATT_HEAD_DIM
ATT_KEY_SPLITS = 2

HG_HEADS = 4
HG_DIM = 128
HG_WIDTH = HG_HEADS * HG_DIM
HG_CHUNK = 128
HG_STEP_CHUNKS = 8
HG_UNROLL = 4
HG_FAST_BLOCK = 64
HG_FAST_MAX_LOG2 = 96.0

X_HEADS = 4
IN_ROW_PARTS = 2
MIX_ROW_PARTS = 4
CONV_W = 3
FF_CHUNK = 256

VMEM_LIMIT = 56 * 1024 * 1024
VMEM_LIMIT_WIDE = 61 * 1024 * 1024
SUBLANES = 8
BF16_SUBLANES = 16
LANES = 128

HALO = BF16_SUBLANES


def _params(*sem, vmem_limit=VMEM_LIMIT):
    return pltpu.CompilerParams(dimension_semantics=sem, vmem_limit_bytes=vmem_limit)


def _rms(x, g):
    ms = jnp.mean(x * x, axis=-1, keepdims=True)
    return x * lax.rsqrt(ms + EPS) * g


def _silu(x):
    return x / (1.0 + jnp.exp(-x))


def _dot(a, b):
    return jnp.dot(a, b, preferred_element_type=F32)


def _dot_nt(a, b):
    return lax.dot_general(a, b, (((1,), (1,)), ((), ())), preferred_element_type=F32)


def _dot_tn(a, b):
    return lax.dot_general(a, b, (((0,), (0,)), ((), ())), preferred_element_type=F32)


def _const_spec(shape):
    return pl.BlockSpec(shape, lambda *_: (0,) * len(shape))


def _resident_spec(shape):
    return pl.BlockSpec(shape, lambda *_: (0,) * len(shape), pipeline_mode=pl.Buffered(1))


def _head_norm_rope(a, seg, gain, cos, sin_signed):
    w = a.shape[-1]
    sq = a * a
    hi = sq.astype(BF16)
    lo = (sq - hi.astype(F32)).astype(BF16)
    ss = [_dot(jnp.concatenate([hi[:, j:j + LANES], lo[:, j:j + LANES]], axis=-1), seg) for j in range(0, w, LANES)]
    ss = jnp.concatenate(ss, axis=-1) if len(ss) > 1 else ss[0]
    an = a * lax.rsqrt(ss * (1.0 / ATT_HEAD_DIM) + EPS) * gain
    lane = lax.broadcasted_iota(jnp.int32, a.shape, 1)
    partner = jnp.where(lane % 2 == 0, pltpu.roll(an, w - 1, 1), pltpu.roll(an, 1, 1))
    return an * cos + partner * sin_signed


def _in_proj_kernel(x_ref, g_ref, w_ref, qg_ref, kg_ref, cos_ref, sin_ref, lb_ref, seg_ref,
                    mem_ref, mg_ref, wkv_ref,
                    qt_ref, k_ref, vt_ref, hq_ref, lff_ref, lfb_ref, hi_ref, hg_ref, kv_ref):
    part = x_ref.shape[1] // IN_ROW_PARTS

    kv_ref[0] = _dot(_rms(mem_ref[0], mg_ref[...]).astype(BF16), wkv_ref[...].astype(BF16)).astype(BF16)

    def rows_of(r0):
        rows = slice(r0, r0 + part)
        h = _rms(x_ref[0, rows, :], g_ref[...]).astype(BF16)

        def proj(lo, width):
            return _dot(h, w_ref[:, lo:lo + width].astype(BF16))

        def heads_out(ref, val):
            for i in range(HG_HEADS):
                ref[0, i, rows, :] = val[:, i * HG_DIM:(i + 1) * HG_DIM].astype(ref.dtype)

        cos = cos_ref[rows, :]
        sin = sin_ref[rows, :]
        c0 = 0
        q = _head_norm_rope(proj(c0, ATT_Q_DIM), seg_ref[...], qg_ref[...],
                            jnp.concatenate([cos] * (ATT_Q_DIM // LANES), axis=-1),
                            jnp.concatenate([sin] * (ATT_Q_DIM // LANES), axis=-1))
        q_t = jnp.transpose(q)
        for i in range(ATT_HEADS):
            qt_ref[0, i, :, rows] = q_t[i * ATT_HEAD_DIM:(i + 1) * ATT_HEAD_DIM, :].astype(BF16)
        c0 += ATT_Q_DIM
        yield
        k = _head_norm_rope(proj(c0, ATT_KV_DIM), seg_ref[...], kg_ref[...], cos, sin)
        c0 += ATT_KV_DIM
        v_t = jnp.transpose(proj(c0, ATT_KV_DIM))
        c0 += ATT_KV_DIM
        for i in range(ATT_KV_HEADS):
            k_ref[0, i, rows, :] = k[:, i * ATT_HEAD_DIM:(i + 1) * ATT_HEAD_DIM].astype(BF16)
            vt_ref[0, i, :, rows] = v_t[i * ATT_HEAD_DIM:(i + 1) * ATT_HEAD_DIM, :].astype(BF16)
        yield
        heads_out(hq_ref, _silu(proj(c0, HG_WIDTH)))
        c0 += HG_WIDTH
        yield
        for d, ref in enumerate((lff_ref, lfb_ref)):
            lb = lb_ref[d:d + 1, :]
            z = proj(c0, HG_WIDTH)
            heads_out(ref, jnp.log(lb + (1.0 - lb) / (1.0 + jnp.exp(-z))) * LOG2E)
            c0 += HG_WIDTH
            yield
        heads_out(hi_ref, proj(c0, HG_WIDTH))
        c0 += HG_WIDTH
        yield
        hg_ref[0, rows, :] = _silu(proj(c0, HG_WIDTH))

    for _ in itertools.zip_longest(*[rows_of(j * part) for j in range(IN_ROW_PARTS)]):
        pass


def _in_proj(x, g, w_in, qg, kg, cos, sin, lb, seg, mem, mem_g, w_xkv, tm):
    b, n, d = x.shape
    nt = n // tm
    n_in = w_in.shape[1]
    nm = mem.shape[1]
    kv_cols = w_xkv.shape[1] // nt
    assert w_xkv.shape[1] % nt == 0 and kv_cols % LANES == 0
    tok = lambda i, j: (i, j, 0)
    head = lambda i, j: (i, 0, j, 0)
    pos = lambda i, j: (j, 0)
    return pl.pallas_call(
        _in_proj_kernel,
        grid=(b, nt),
        in_specs=[pl.BlockSpec((1, tm, d), tok), _const_spec((1, d)), _resident_spec((d, n_in)),
                  _const_spec((1, ATT_Q_DIM)), _const_spec((1, ATT_KV_DIM)),
                  pl.BlockSpec((tm, LANES), pos), pl.BlockSpec((tm, LANES), pos),
                  _const_spec((2, HG_WIDTH)),
                  _const_spec((2 * LANES, LANES)),
                  pl.BlockSpec((1, nm, d), lambda i, j: (i, 0, 0)), _const_spec((1, d)),
                  pl.BlockSpec((d, kv_cols), lambda i, j: (0, j))],
        out_specs=[pl.BlockSpec((1, ATT_HEADS, ATT_HEAD_DIM, tm), lambda i, j: (i, 0, 0, j)),
                   pl.BlockSpec((1, ATT_KV_HEADS, tm, ATT_HEAD_DIM), head),
                   pl.BlockSpec((1, ATT_KV_HEADS, ATT_HEAD_DIM, tm), lambda i, j: (i, 0, 0, j)),
                   pl.BlockSpec((1, HG_HEADS, tm, HG_DIM), head),
                   pl.BlockSpec((1, HG_HEADS, tm, HG_DIM), head),
                   pl.BlockSpec((1, HG_HEADS, tm, HG_DIM), head),
                   pl.BlockSpec((1, HG_HEADS, tm, HG_DIM), head),
                   pl.BlockSpec((1, tm, HG_WIDTH), tok),
                   pl.BlockSpec((1, nm, kv_cols), lambda i, j: (i, 0, j))],
        out_shape=[jax.ShapeDtypeStruct((b, ATT_HEADS, ATT_HEAD_DIM, n), BF16),
                   jax.ShapeDtypeStruct((b, ATT_KV_HEADS, n, ATT_HEAD_DIM), BF16),
                   jax.ShapeDtypeStruct((b, ATT_KV_HEADS, ATT_HEAD_DIM, n), BF16),
                   jax.ShapeDtypeStruct((b, HG_HEADS, n, HG_DIM), BF16),
                   jax.ShapeDtypeStruct((b, HG_HEADS, n, HG_DIM), F32),
                   jax.ShapeDtypeStruct((b, HG_HEADS, n, HG_DIM), F32),
                   jax.ShapeDtypeStruct((b, HG_HEADS, n, HG_DIM), BF16),
                   jax.ShapeDtypeStruct((b, n, HG_WIDTH), F32),
                   jax.ShapeDtypeStruct((b, nm, w_xkv.shape[1]), BF16)],
        compiler_params=_params("parallel", "parallel"),
        name="in_proj",
    )(x, g, w_in, qg, kg, cos, sin, lb, seg, mem, mem_g, w_xkv)


def _attention_kernel(qt_ref, qtn_ref, k_ref, vt_ref, o_ref, s_even, s_odd):
    n = k_ref.shape[2]
    part = n // ATT_KEY_SPLITS
    parts = [slice(j * part, (j + 1) * part) for j in range(ATT_KEY_SPLITS)]

    def scores(i, s_scr, queries_t=qt_ref):
        for rows in parts:
            s_scr[rows, :] = _dot(k_ref[0, i // ATT_GROUP, rows, :], queries_t[0, i])

    def finish(i, s_scr):
        s = [s_scr[rows, :] for rows in parts]
        m = functools.reduce(jnp.maximum, [jnp.max(x, axis=0, keepdims=True) for x in s])
        p = [jnp.exp2(x - m) for x in s]
        l = sum(jnp.sum(x, axis=0, keepdims=True) for x in p)
        o_t = sum(_dot(vt_ref[0, i // ATT_GROUP, :, rows], x.astype(BF16)) for rows, x in zip(parts, p))
        o_ref[0, i] = (o_t / l).astype(BF16)

    @pl.when(pl.program_id(1) == 0)
    def _():
        scores(0, s_even)

    def pair(j, carry):
        scores(2 * j + 1, s_odd)
        finish(2 * j, s_even)
        scores(2 * j + 2, s_even)
        finish(2 * j + 1, s_odd)
        return carry

    lax.fori_loop(0, ATT_HEADS // 2 - 1, pair, 0)
    scores(ATT_HEADS - 1, s_odd)
    finish(ATT_HEADS - 2, s_even)
    scores(0, s_even, qtn_ref)
    finish(ATT_HEADS - 1, s_odd)


def _attention(q_t, k, v_t, tq):
    b, _, dh, n = q_t.shape
    last = n // tq - 1
    return pl.pallas_call(
        _attention_kernel,
        grid=(b, n // tq),
        in_specs=[pl.BlockSpec((1, ATT_HEADS, dh, tq), lambda i, j: (i, 0, 0, j)),
                  pl.BlockSpec((1, ATT_HEADS, dh, tq), lambda i, j: (i, 0, 0, jnp.minimum(j + 1, last))),
                  pl.BlockSpec((1, ATT_KV_HEADS, n, dh), lambda i, j: (i, 0, 0, 0)),
                  pl.BlockSpec((1, ATT_KV_HEADS, dh, n), lambda i, j: (i, 0, 0, 0))],
        out_specs=pl.BlockSpec((1, ATT_HEADS, dh, tq), lambda i, j: (i, 0, 0, j)),
        out_shape=jax.ShapeDtypeStruct((b, ATT_HEADS, dh, n), BF16),
        scratch_shapes=[pltpu.VMEM((n, tq), F32), pltpu.VMEM((n, tq), F32)],
        compiler_params=_params("parallel", "arbitrary"),
        name="attention",
    )(q_t, q_t, k, v_t)


def _hgrn_levels():
    m, out = 1, []
    while m < HG_CHUNK:
        out.append(m)
        m *= 2
    return out


def _hgrn_tables():
    c = HG_CHUNK
    t = np.arange(c)[:, None]
    u = np.arange(c)[None, :]
    mats, masks, fast = [], [], []
    for reverse in (False, True):
        blocks, pairs = [], [t == u]
        for m in _hgrn_levels():
            base = (t // (2 * m)) * (2 * m)
            if not reverse:
                r = base + m - 1
                block = np.where(t > r, (u > r) & (u <= t), (u > t) & (u <= r))
                roles = ((t // m) % 2 == 1) & ((u // m) % 2 == 0)
            else:
                r = base + m
                block = np.where(t < r, (u >= t) & (u < r), (u >= r) & (u < t))
                roles = ((t // m) % 2 == 0) & ((u // m) % 2 == 1)
            if m < SUBLANES:
                blocks.append(block)
            pairs.append(roles & (t // (2 * m) == u // (2 * m)))
        blocks.append(u >= t if reverse else u <= t)
        mats.append(np.tile(np.concatenate(blocks, axis=0), (1, 2)))
        masks.append(np.stack(pairs))
        fast.append((t // HG_FAST_BLOCK == u // HG_FAST_BLOCK) & (u >= t if reverse else u <= t))
    return tuple(np.stack(x).astype(np.float32) for x in (mats, masks, fast))


def _hgrn_sums(lf2, sum_mat):
    hi = lf2.astype(BF16)
    mid = (lf2 - hi.astype(F32)).astype(BF16)
    return _dot(sum_mat, jnp.concatenate([hi, mid], axis=0))


def _hgrn_chunk(q_ref, lf_ref, v_ref, o_ref, state_ref, sum_mat, pair_mask, fast_mask, b_scr, reverse, fast):
    c = HG_CHUNK
    levels = list(enumerate(_hgrn_levels()))
    n_small = sum(m < SUBLANES for _, m in levels)
    q, lf2, v, state = q_ref[...], lf_ref[...], v_ref[...], state_ref[...]
    qb = q.astype(BF16)
    kb = (1.0 - jnp.exp2(lf2)).astype(BF16)
    if fast:
        b = _hgrn_sums(lf2, sum_mat[n_small * c:, :])
        yield
        b_scr[...] = b
        parts = []
        for g in range(c // HG_FAST_BLOCK):
            lo, up = g * HG_FAST_BLOCK, (g + 1) * HG_FAST_BLOCK
            before = up if reverse else lo - 1
            parts.append(b[lo:up] - b_scr[before:before + 1, :] if 0 <= before < c else b[lo:up])
        loc = jnp.concatenate(parts, axis=0)
        a = _dot_nt(qb * jnp.exp2(loc).astype(BF16), kb * jnp.exp2(-loc).astype(BF16)) * fast_mask[...]
        yield
        levels = [(j, m) for j, m in levels if m >= HG_FAST_BLOCK]
    else:
        sums = _hgrn_sums(lf2, sum_mat[...])
        yield
        b = sums[n_small * c:]
        b_scr[...] = b
        a = _dot_nt(qb, kb) * pair_mask[0]
        yield
    for j, m in levels:
        if m < SUBLANES:
            e = sums[j * c:(j + 1) * c]
        else:
            parts = []
            for g in range(c // (2 * m)):
                lo, mid_row, up = 2 * m * g, 2 * m * g + m, 2 * m * (g + 1)
                r = mid_row if reverse else mid_row - 1
                b_r = b_scr[r:r + 1, :]
                parts += [b[lo:mid_row] - b_r, b_r - b[mid_row:up]] if reverse else \
                         [b_r - b[lo:mid_row], b[mid_row:up] - b_r]
            e = jnp.concatenate(parts, axis=0)
        fac = jnp.exp2(e).astype(BF16)
        a = a + _dot_nt(qb * fac, kb * fac) * pair_mask[j + 1]
        yield
    vb = v.astype(BF16)
    last = 0 if reverse else c - 1
    total = b_scr[last:last + 1, :]
    o_ref[...] = _dot(a.astype(BF16), vb) + _dot(qb * jnp.exp2(b).astype(BF16), state.astype(BF16))
    yield
    k_dec = kb * jnp.exp2(total - b).astype(BF16)
    decay = jnp.transpose(jnp.broadcast_to(jnp.exp2(total), (HG_DIM, HG_DIM)))
    state_ref[...] = state * decay + _dot_tn(k_dec, vb)


def _hgrn_kernel(qf_ref, lff_ref, vf_ref, qb_ref, lfb_ref, vb_ref, sm_ref, pm_ref, fm_ref, *rest):
    n_cast = (len(rest) - 4) // 2
    cast_in, (of_ref, ob_ref), cast_out = rest[:n_cast], rest[n_cast:n_cast + 2], rest[n_cast + 2:2 * n_cast + 2]
    s_ref, b_scr = rest[2 * n_cast + 2:]

    @pl.when(pl.program_id(1) == 0)
    def _():
        s_ref[...] = jnp.zeros_like(s_ref)

    for src, dst in zip(cast_in, cast_out):
        dst[...] = src[...].astype(dst.dtype)

    c = HG_CHUNK
    rows_per_step = qf_ref.shape[2]
    chains = [(d, h) for h in range(HG_HEADS) for d in (0, 1)]
    q_refs, lf_refs, v_refs, o_refs = (qf_ref, qb_ref), (lff_ref, lfb_ref), (vf_ref, vb_ref), (of_ref, ob_ref)

    worst = jnp.zeros((1, HG_DIM), F32)
    for d, h in chains:
        for g in range(rows_per_step // HG_FAST_BLOCK):
            block = lf_refs[d][0, h, g * HG_FAST_BLOCK:(g + 1) * HG_FAST_BLOCK, :]
            worst = jnp.minimum(worst, jnp.sum(block, axis=0, keepdims=True))
    mild = jnp.min(worst) >= -HG_FAST_MAX_LOG2

    def run(fast):
        n_sub = rows_per_step // c
        unroll = min(HG_UNROLL, n_sub) if fast else 1

        def trip(t, carry):
            for u in range(unroll):
                sub = t * unroll + u
                work = []
                for d, h in chains:
                    k = n_sub - 1 - sub if d else sub
                    rows = pl.ds(pl.multiple_of(k * c, c), c)
                    work.append(_hgrn_chunk(q_refs[d].at[0, h, rows, :], lf_refs[d].at[0, h, rows, :],
                                            v_refs[d].at[0, h, rows, :], o_refs[d].at[0, h, rows, :],
                                            s_ref.at[d, h], sm_ref.at[d], pm_ref.at[d], fm_ref.at[d],
                                            b_scr.at[d, h], bool(d), fast))
                for _ in itertools.zip_longest(*work):
                    pass
            return carry

        lax.fori_loop(0, n_sub // unroll, trip, 0)

    pl.when(mild)(lambda: run(True))
    pl.when(jnp.logical_not(mild))(lambda: run(False))


def _hgrn(hq, lf_f, lf_b, hi, to_cast):
    b, nh, n, dk = hq.shape
    rows_per_step = HG_CHUNK * HG_STEP_CHUNKS if n % (HG_CHUNK * HG_STEP_CHUNKS) == 0 else HG_CHUNK
    nc = n // rows_per_step
    blk = (1, nh, rows_per_step, dk)
    fwd = pl.BlockSpec(blk, lambda i, j: (i, 0, j, 0))
    bwd = pl.BlockSpec(blk, lambda i, j: (i, 0, nc - 1 - j, 0))
    out = jax.ShapeDtypeStruct((b, nh, n, dk), F32)
    tables = _hgrn_tables()
    steps = b * nc

    def slab_spec(w):
        hold = next(h for h in range(1, steps + 1)
                    if steps % h == 0 and w.shape[0] % (steps // h * BF16_SUBLANES) == 0)
        return pl.BlockSpec((w.shape[0] // (steps // hold), w.shape[1]), lambda i, j: ((i * nc + j) // hold, 0))

    slabs = [slab_spec(w) for w in to_cast]
    res = pl.pallas_call(
        _hgrn_kernel,
        grid=(b, nc),
        in_specs=[fwd, fwd, fwd, bwd, bwd, bwd] + [_const_spec(t.shape) for t in tables] + slabs,
        out_specs=[fwd, bwd] + slabs,
        out_shape=[out, out] + [jax.ShapeDtypeStruct(w.shape, BF16) for w in to_cast],
        scratch_shapes=[pltpu.VMEM((2, nh, dk, dk), F32), pltpu.VMEM((2, nh, HG_CHUNK, dk), F32)],
        compiler_params=_params("parallel", "arbitrary"),
        name="hgrn",
    )(hq, lf_f, hi, hq, lf_b, hi, jnp.asarray(tables[0], BF16), jnp.asarray(tables[1]), jnp.asarray(tables[2]),
      *to_cast)
    return res[0], res[1], res[2:]


def _mix_cross_kernel(x_ref, att_ref, of_ref, ob_ref, sg_ref, og_ref, wo_ref, pmg_ref, pxg_ref,
                      wq_ref, km_ref, vm_ref, wxo_ref, poxg_ref, pfg_ref, x2_ref, h3_ref):
    tm = x_ref.shape[1]
    part = tm // MIX_ROW_PARTS

    def rows_of(lo):
        rows = slice(lo, lo + part)
        rec = []
        for i in range(HG_HEADS):
            o = _rms(of_ref[0, i, rows, :] + ob_ref[0, i, rows, :], og_ref[...])
            rec.append((o * sg_ref[0, rows, i * HG_DIM:(i + 1) * HG_DIM]).astype(BF16))
        att_t = att_ref[0, :, :, rows].reshape(ATT_Q_DIM, part)
        mixed = _dot_tn(att_t, wo_ref[:ATT_Q_DIM, :].astype(BF16)) + \
            _dot(jnp.concatenate(rec, axis=-1), wo_ref[ATT_Q_DIM:, :].astype(BF16))
        yield
        x1 = x_ref[0, rows, :] + _rms(mixed, pmg_ref[...])
        h2 = _rms(x1, pxg_ref[...]).astype(BF16)
        d = h2.shape[-1]
        dh = d // X_HEADS
        q = _dot(h2, wq_ref[...].astype(BF16)) * (dh ** -0.5)
        yield
        heads = []
        for i in range(X_HEADS):
            sl = slice(i * dh, (i + 1) * dh)
            s = _dot_nt(q[:, sl].astype(BF16), km_ref[0, :, sl])
            p = jnp.exp(s - jnp.max(s, axis=-1, keepdims=True))
            l = jnp.sum(p, axis=-1, keepdims=True)
            heads.append((_dot(p.astype(BF16), vm_ref[0, :, sl]) / l).astype(BF16))
        yield
        xo = _dot(jnp.concatenate(heads, axis=-1), wxo_ref[...].astype(BF16))
        yield
        x2 = x1 + _rms(xo, poxg_ref[...])
        x2_ref[0, rows, :] = x2
        h3_ref[0, rows, :] = _rms(x2, pfg_ref[...]).astype(BF16)

    for _ in itertools.zip_longest(*[rows_of(j * part) for j in range(MIX_ROW_PARTS)]):
        pass


def _mix_cross(x, att, o_f, o_b, sg, og, w_out, pmg, pxg, w_xq, kv_mem, w_xo, poxg, pfg, tm):
    b, n, d = x.shape
    nm = kv_mem.shape[1]
    tok = lambda i, j: (i, j, 0)
    head = lambda i, j: (i, 0, j, 0)
    batch = lambda i, j: (i, 0, 0)
    vec = _const_spec((1, d))
    mat = _resident_spec((d, d))
    return pl.pallas_call(
        _mix_cross_kernel,
        grid=(b, n // tm),
        in_specs=[pl.BlockSpec((1, tm, d), tok),
                  pl.BlockSpec((1, ATT_HEADS, ATT_HEAD_DIM, tm), lambda i, j: (i, 0, 0, j)),
                  pl.BlockSpec((1, HG_HEADS, tm, HG_DIM), head), pl.BlockSpec((1, HG_HEADS, tm, HG_DIM), head),
                  pl.BlockSpec((1, tm, HG_WIDTH), tok), _const_spec((1, HG_DIM)),
                  mat, vec, vec, mat,
                  pl.BlockSpec((1, nm, d), batch), pl.BlockSpec((1, nm, d), lambda i, j: (i, 0, 1)),
                  mat, vec, vec],
        out_specs=[pl.BlockSpec((1, tm, d), tok), pl.BlockSpec((1, tm, d), tok)],
        out_shape=[jax.ShapeDtypeStruct((b, n, d), F32), jax.ShapeDtypeStruct((b, n, d), BF16)],
        compiler_params=_params("parallel", "parallel", vmem_limit=VMEM_LIMIT_WIDE),
        name="mix_cross",
    )(x, att, o_f, o_b, sg, og, w_out, pmg, pxg, w_xq, kv_mem, kv_mem, w_xo, poxg, pfg)


def _conv_ffn_kernel(x_ref, h_ref, hp_ref, hn_ref, wu_ref, cw_ref, cb_ref, wd_ref, g_ref, o_ref, act_scr):
    j = pl.program_id(1)
    tm = h_ref.shape[1]
    d_ff = wd_ref.shape[0]
    ck = FF_CHUNK
    prev = jnp.where(j > 0, hp_ref[0], jnp.zeros_like(hp_ref[0]))
    nxt = jnp.where(j < pl.num_programs(1) - 1, hn_ref[0], jnp.zeros_like(hn_ref[0]))
    hext = jnp.concatenate([prev, h_ref[0], nxt], axis=0)

    def conv(col):
        u = _dot(hext, wu_ref[:, col:col + ck])
        taps = (pltpu.roll(u, 1, 0), u, pltpu.roll(u, u.shape[0] - 1, 0))
        out = cb_ref[:, col:col + ck]
        for t in range(CONV_W):
            out = out + taps[t][HALO:HALO + tm] * cw_ref[t:t + 1, col:col + ck]
        return out

    for c in range(d_ff // ck):
        act_scr[:, c * ck:(c + 1) * ck] = (_silu(conv(c * ck)) * conv(d_ff + c * ck)).astype(BF16)
    o_ref[0] = x_ref[0] + _rms(_dot(act_scr[...], wd_ref[...]), g_ref[...])


def _conv_ffn(x2, h3, w_up, conv_w, conv_b, wd, g, tm):
    b, n, d = x2.shape
    d_ff = wd.shape[0]
    ck = FF_CHUNK
    hb = tm // HALO
    last = n // HALO - 1
    tok = lambda i, j: (i, j, 0)
    full = _resident_spec
    return pl.pallas_call(
        _conv_ffn_kernel,
        grid=(b, n // tm),
        in_specs=[pl.BlockSpec((1, tm, d), tok), pl.BlockSpec((1, tm, d), tok),
                  pl.BlockSpec((1, HALO, d), lambda i, j: (i, jnp.maximum(j * hb - 1, 0), 0)),
                  pl.BlockSpec((1, HALO, d), lambda i, j: (i, jnp.minimum((j + 1) * hb, last), 0)),
                  full((d, 2 * d_ff)), full((CONV_W, 2 * d_ff)), full((1, 2 * d_ff)),
                  full((d_ff, d)), _const_spec((1, d))],
        out_specs=pl.BlockSpec((1, tm, d), tok),
        out_shape=jax.ShapeDtypeStruct((b, n, d), F32),
        scratch_shapes=[pltpu.VMEM((tm, d_ff), BF16)],
        compiler_params=_params("parallel", "parallel"),
        name="conv_ffn",
    )(x2, h3, h3, h3, w_up, conv_w, conv_b, wd, g)


def _rope_tables(n):
    pairs = ATT_HEAD_DIM // 4
    pos = np.arange(n)
    inv = np.power(np.float32(ROPE_THETA), -np.arange(pairs, dtype=np.float32) / np.float32(pairs))
    ang = np.concatenate([(pos // GRID_W).astype(np.float32)[:, None] * inv,
                          (pos % GRID_W).astype(np.float32)[:, None] * inv], axis=-1)
    cos = np.repeat(np.cos(ang), 2, axis=-1)
    sin = np.repeat(np.sin(ang), 2, axis=-1) * np.tile(np.array([-1.0, 1.0], np.float32), ATT_HEAD_DIM // 2)
    return jnp.asarray(np.tile(cos, (1, 2)), F32), jnp.asarray(np.tile(sin, (1, 2)), F32)


def _segment_ones():
    i = jnp.arange(LANES) // ATT_HEAD_DIM
    return jnp.tile((i[:, None] == i[None, :]).astype(BF16), (2, 1))


def _layer(x, mem, lb, pre_mix_g, w_in, q_norm_g, k_norm_g, hg_out_norm_g, w_out, post_mix_g, pre_x_g,
           mem_norm_g, w_xq, w_xkv, w_xo, post_x_g, pre_ffn_g, w_up, conv_w, conv_b, w_down, post_ffn_g):
    b, n, d = x.shape
    d_ff = w_down.shape[0]
    assert n % HG_CHUNK == 0 and n % GRID_W == 0 and d_ff % FF_CHUNK == 0
    tm = min(256, n)
    tf = min(512, n)
    tx = min(1024, n)
    row = lambda g: g.reshape(1, -1).astype(F32)

    cos, sin = _rope_tables(n)
    qg = jnp.tile(q_norm_g, ATT_HEADS).reshape(1, -1) * (ATT_HEAD_DIM ** -0.5 * LOG2E)
    kg = jnp.tile(k_norm_g, ATT_KV_HEADS).reshape(1, -1)

    q_t, k, v_t, hq, lf_f, lf_b, hi, sg, kv_mem = _in_proj(
        x, row(pre_mix_g), w_in, qg, kg, cos, sin, lb,
        _segment_ones(), mem, row(mem_norm_g), w_xkv, tf)
    att = _attention(q_t, k, v_t, tm)
    o_f, o_b, (w_up_bf16, w_down_bf16) = _hgrn(hq, lf_f, lf_b, hi, [w_up, w_down])
    x2, h3 = _mix_cross(x, att, o_f, o_b, sg, row(hg_out_norm_g), w_out, row(post_mix_g),
                        row(pre_x_g), w_xq, kv_mem, w_xo,
                        row(post_x_g), row(pre_ffn_g), tx)

    return _conv_ffn(x2, h3, w_up_bf16, conv_w, row(conv_b), w_down_bf16, row(post_ffn_g), tf)


def kernel(x, mem, pre_mix_g, w_in, q_norm_g, k_norm_g, hg_lb, hg_out_norm_g, w_out, post_mix_g, pre_x_g,
           mem_norm_g, w_xq, w_xkv, w_xo, post_x_g, pre_ffn_g, w_up, conv_w, conv_b, w_down, post_ffn_g):
    lb_all = jnp.cumsum(jax.nn.softmax(hg_lb.astype(F32), axis=1), axis=1)
    for l in range(w_in.shape[0]):
        x = _layer(x, mem, lb_all[:, l], pre_mix_g[l], w_in[l], q_norm_g[l], k_norm_g[l], hg_out_norm_g[l],
                   w_out[l], post_mix_g[l], pre_x_g[l], mem_norm_g[l], w_xq[l], w_xkv[l], w_xo[l],
                   post_x_g[l], pre_ffn_g[l], w_up[l], conv_w[l], conv_b[l], w_down[l], post_ffn_g[l])
    return x
```
